```python
import jax
import jax.numpy as jnp
from jax import lax
import numpy as np

D_MODEL = 2048
BATCH = 8
SEQ = 2048
DEPTH = 2

MLA_HEADS = 8
MLA_Q_RANK = 512
MLA_KV_RANK = 256
MLA_NOPE_DIM = 128
MLA_ROPE_DIM = 64
MLA_V_DIM = 128
MLA_WIDTH = MLA_HEADS * MLA_V_DIM

SGU_GROUPS = 8
SGU_GROUP_DIM = 128
SGU_WIDTH = SGU_GROUPS * SGU_GROUP_DIM
SGU_CHUNK = 128

MOBA_HEADS = 8
MOBA_HEAD_DIM = 128
MOBA_WIDTH = MOBA_HEADS * MOBA_HEAD_DIM
MOBA_BLOCK = 256
MOBA_TOPK = 3
MOBA_QCHUNK = 16

MEM_LEN = 256
MEM_HEADS = 4
MEM_HEAD_DIM = 128
MEM_WIDTH = MEM_HEADS * MEM_HEAD_DIM

N_BRANCH = 4
IN_COLS = (MLA_Q_RANK + MLA_KV_RANK + MLA_ROPE_DIM + 2 * SGU_WIDTH
           + 3 * MOBA_WIDTH + MEM_WIDTH + N_BRANCH * D_MODEL)

N_EXPERTS = 64
TOP_K = 6
D_EXPERT = 512
ROUTED_SCALE = 2.5
DISPATCH_BLOCK = 128

ATTN_QBLOCK = 128
ROPE_THETA = 10000.0
DEEPNORM_ALPHA = (2 * DEPTH) ** 0.25
DEEPNORM_BETA = (8 * DEPTH) ** -0.25

kernel_name = "hybrid_mla_sgu_moba_moe_block"


def _layer_norm(x, g, b, eps=1e-5):
    xf = x.astype(jnp.float32)
    mu = jnp.mean(xf, axis=-1, keepdims=True)
    xc = xf - mu
    var = jnp.mean(xc * xc, axis=-1, keepdims=True)
    return (xc * lax.rsqrt(var + eps) * g.astype(jnp.float32) + b.astype(jnp.float32)).astype(x.dtype)


def _rms_norm(x, g, eps=1e-6):
    xf = x.astype(jnp.float32)
    return (xf * lax.rsqrt(jnp.mean(xf * xf, axis=-1, keepdims=True) + eps) * g.astype(jnp.float32)).astype(x.dtype)


def _rope(x, positions):
    half = x.shape[-1] // 2
    inv_freq = ROPE_THETA ** (-jnp.arange(half, dtype=jnp.float32) / half)
    ang = positions.astype(jnp.float32)[:, None] * inv_freq[None, :]
    cos = jnp.cos(ang)[:, None, :]
    sin = jnp.sin(ang)[:, None, :]
    xf = x.astype(jnp.float32)
    x1, x2 = xf[..., :half], xf[..., half:]
    return jnp.concatenate([x1 * cos - x2 * sin, x2 * cos + x1 * sin], axis=-1).astype(x.dtype)


def _causal_attention(q, k, v, scale):
    bsz, seq, heads, dk = q.shape
    dv = v.shape[-1]
    nq = seq // ATTN_QBLOCK
    q_blocks = jnp.moveaxis(q.reshape(bsz, nq, ATTN_QBLOCK, heads, dk), 1, 0)
    kpos = jnp.arange(seq)

    def one_block(args):
        qb, bi = args
        s = jnp.einsum('bqhd,bkhd->bhqk', qb, k, preferred_element_type=jnp.float32) * scale
        qpos = bi * ATTN_QBLOCK + jnp.arange(ATTN_QBLOCK)
        s = jnp.where(kpos[None, :] <= qpos[:, None], s, -jnp.inf)
        p = jax.nn.softmax(s, axis=-1).astype(v.dtype)
        return jnp.einsum('bhqk,bkhd->bqhd', p, v)

    out = lax.map(one_block, (q_blocks, jnp.arange(nq)))
    return jnp.moveaxis(out, 0, 1).reshape(bsz, seq, heads, dv)


def _moba_attention(q, k, v):
    bsz, seq, heads, dh = q.shape
    n_blocks = -(-seq // MOBA_BLOCK)
    pad = n_blocks * MOBA_BLOCK - seq

    def to_blocks(t):
        t = jnp.pad(t, ((0, 0), (0, pad), (0, 0), (0, 0)))
        return t.reshape(bsz, n_blocks, MOBA_BLOCK, heads, dh).transpose(0, 3, 1, 2, 4)

    kb, vb = to_blocks(k), to_blocks(v)
    k_mean = jnp.mean(kb.astype(jnp.float32), axis=3)
    n_sel = min(MOBA_TOPK, n_blocks)
    nq = seq // MOBA_QCHUNK
    q_chunks = q.reshape(bsz, nq, MOBA_QCHUNK, heads, dh).transpose(1, 0, 3, 2, 4)
    b_idx = jnp.arange(bsz)[:, None, None, None]
    h_idx = jnp.arange(heads)[None, :, None, None]
    blk_ids = jnp.arange(n_blocks)
    offs = jnp.arange(MOBA_BLOCK)
    scale = dh ** -0.5

    def one_chunk(args):
        qc, ci = args
        qpos = ci * MOBA_QCHUNK + jnp.arange(MOBA_QCHUNK)
        own = (ci * MOBA_QCHUNK) // MOBA_BLOCK
        gate = jnp.einsum('bhqd,bhnd->bhqn', qc.astype(jnp.float32), k_mean)
        gate = jnp.where(blk_ids < own, gate, -jnp.inf)
        top_s, top_i = lax.top_k(gate, n_sel)
        valid = jnp.isfinite(top_s)
        k_sel = kb[b_idx, h_idx, top_i]
        v_sel = vb[b_idx, h_idx, top_i]
        s_sel = jnp.einsum('bhqd,bhqnkd->bhqnk', qc, k_sel, preferred_element_type=jnp.float32) * scale
        s_sel = jnp.where(valid[..., None], s_sel, -jnp.inf)
        s_sel = s_sel.reshape(bsz, heads, MOBA_QCHUNK, n_sel * MOBA_BLOCK)
        k_own = lax.dynamic_index_in_dim(kb, own, axis=2, keepdims=False)
        v_own = lax.dynamic_index_in_dim(vb, own, axis=2, keepdims=False)
        s_own = jnp.einsum('bhqd,bhkd->bhqk', qc, k_own, preferred_element_type=jnp.float32) * scale
        own_pos = own * MOBA_BLOCK + offs
        s_own = jnp.where(own_pos[None, :] <= qpos[:, None], s_own, -jnp.inf)
        p = jax.nn.softmax(jnp.concatenate([s_sel, s_own], axis=-1), axis=-1).astype(v.dtype)
        p_sel = p[..., :n_sel * MOBA_BLOCK].reshape(bsz, heads, MOBA_QCHUNK, n_sel, MOBA_BLOCK)
        p_own = p[..., n_sel * MOBA_BLOCK:]
        return (jnp.einsum('bhqnk,bhqnkd->bhqd', p_sel, v_sel)
                + jnp.einsum('bhqk,bhkd->bhqd', p_own, v_own))

    out = lax.map(one_chunk, (q_chunks, jnp.arange(nq)))
    return out.transpose(1, 0, 3, 2, 4).reshape(bsz, seq, heads * dh)


def _spatial_gating(z, ln_g, ln_b, w_s, b_s):
    bsz, seq, _ = z.shape
    u, v = jnp.split(jax.nn.gelu(z), 2, axis=-1)
    v = _layer_norm(v, ln_g, ln_b)
    nc = seq // SGU_CHUNK
    v = v.reshape(bsz, nc, SGU_CHUNK, SGU_GROUPS, SGU_GROUP_DIM)
    w = jnp.tril(w_s)
    v = jnp.einsum('gts,bnsgc->bntgc', w, v) + b_s.T[None, None, :, :, None]
    return u * v.reshape(bsz, seq, SGU_WIDTH)


def _memory_attention(q, mem, w_mem_kv):
    bsz, seq, _ = q.shape
    m_len = mem.shape[1]
    q = q.reshape(bsz, seq, MEM_HEADS, MEM_HEAD_DIM)
    k, v = jnp.split(mem @ w_mem_kv, 2, axis=-1)
    k = k.reshape(bsz, m_len, MEM_HEADS, MEM_HEAD_DIM)
    v = v.reshape(bsz, m_len, MEM_HEADS, MEM_HEAD_DIM)
    s = jnp.einsum('bshd,bmhd->bhsm', q, k, preferred_element_type=jnp.float32) * MEM_HEAD_DIM ** -0.5
    p = jax.nn.softmax(s, axis=-1).astype(v.dtype)
    return jnp.einsum('bhsm,bmhd->bshd', p, v).reshape(bsz, seq, MEM_WIDTH)


def _hybrid_mixer(x, mem, positions, w_in, g_qa, w_q_up, g_kva, w_kv_up, sgu_ln_g, sgu_ln_b,
                  w_spatial, b_spatial, w_mem_kv, w_br_a, w_br_b, w_br_c, w_br_m, w_o):
    bsz, seq, _ = x.shape
    h = x @ w_in
    sizes = (MLA_Q_RANK, MLA_KV_RANK + MLA_ROPE_DIM, 2 * SGU_WIDTH, 3 * MOBA_WIDTH, MEM_WIDTH)
    cuts, acc = [], 0
    for s in sizes:
        acc += s
        cuts.append(acc)
    a_q, a_kv, b_z, c_qkv, m_q, gate_pre = jnp.split(h, cuts, axis=-1)

    c_q = _rms_norm(a_q, g_qa)
    q = jnp.einsum('bsr,rhd->bshd', c_q, w_q_up)
    q_nope, q_pe = q[..., :MLA_NOPE_DIM], q[..., MLA_NOPE_DIM:]
    c_kv = _rms_norm(a_kv[..., :MLA_KV_RANK], g_kva)
    k_pe = _rope(a_kv[..., MLA_KV_RANK:][:, :, None, :], positions)
    kv = jnp.einsum('bsr,rhd->bshd', c_kv, w_kv_up)
    k_nope, v_a = kv[..., :MLA_NOPE_DIM], kv[..., MLA_NOPE_DIM:]
    q_a = jnp.concatenate([q_nope, _rope(q_pe, positions)], axis=-1)
    k_a = jnp.concatenate([k_nope, jnp.broadcast_to(k_pe, k_nope.shape[:-1] + (MLA_ROPE_DIM,))], axis=-1)
    o_a = _causal_attention(q_a, k_a, v_a, (MLA_NOPE_DIM + MLA_ROPE_DIM) ** -0.5).reshape(bsz, seq, MLA_WIDTH)

    o_b = _spatial_gating(b_z, sgu_ln_g, sgu_ln_b, w_spatial, b_spatial)

    q_c, k_c, v_c = [t.reshape(bsz, seq, MOBA_HEADS, MOBA_HEAD_DIM) for t in jnp.split(c_qkv, 3, axis=-1)]
    o_c = _moba_attention(_rope(q_c, positions), _rope(k_c, positions), v_c)

    o_m = _memory_attention(m_q, mem, w_mem_kv)

    g = jax.nn.sigmoid(gate_pre.astype(jnp.float32)).astype(x.dtype).reshape(bsz, seq, N_BRANCH, D_MODEL)
    y = (g[:, :, 0] * (o_a @ w_br_a) + g[:, :, 1] * (o_b @ w_br_b)
         + g[:, :, 2] * (o_c @ w_br_c) + g[:, :, 3] * (o_m @ w_br_m))
    return y @ w_o


def _moe(x, w_router, router_bias, w_exp_gate, w_exp_up, w_exp_down, w_sh_gate, w_sh_up, w_sh_down):
    bsz, seq, d = x.shape
    n_tok = bsz * seq
    xt = x.reshape(n_tok, d)
    scores = jax.nn.sigmoid(jnp.dot(xt, w_router, preferred_element_type=jnp.float32))
    _, top_i = lax.top_k(scores + router_bias.astype(jnp.float32), TOP_K)
    top_s = jnp.take_along_axis(scores, top_i, axis=-1)
    top_w = top_s / jnp.sum(top_s, axis=-1, keepdims=True) * ROUTED_SCALE

    n_slot = n_tok * TOP_K
    n_blk = -(-n_slot // DISPATCH_BLOCK) + N_EXPERTS
    flat_e = top_i.reshape(-1)
    order = jnp.argsort(flat_e)
    sorted_e = flat_e[order]
    counts = jnp.bincount(flat_e, length=N_EXPERTS)
    starts = jnp.cumsum(counts) - counts
    padded = (counts + DISPATCH_BLOCK - 1) // DISPATCH_BLOCK * DISPATCH_BLOCK
    pad_ends = jnp.cumsum(padded)
    pad_starts = pad_ends - padded
    dest = pad_starts[sorted_e] + jnp.arange(n_slot, dtype=jnp.int32) - starts[sorted_e]
    slot_tok = (jnp.arange(n_slot, dtype=jnp.int32) // TOP_K)[order]
    slot_w = top_w.reshape(-1)[order].astype(x.dtype)
    buf_tok = jnp.full((n_blk * DISPATCH_BLOCK,), n_tok, jnp.int32).at[dest].set(slot_tok)
    buf_w = jnp.zeros((n_blk * DISPATCH_BLOCK,), x.dtype).at[dest].set(slot_w)
    blk_e = jnp.minimum(jnp.searchsorted(pad_ends, jnp.arange(n_blk) * DISPATCH_BLOCK, side='right'),
                        N_EXPERTS - 1).astype(jnp.int32)
    x_pad = jnp.concatenate([xt, jnp.zeros((1, d), x.dtype)], axis=0)

    def expert_block(acc, args):
        tok, wgt, e = args
        xb = x_pad[tok]
        hb = jax.nn.silu(xb @ w_exp_gate[e]) * (xb @ w_exp_up[e])
        return acc.at[tok].add((hb @ w_exp_down[e]) * wgt[:, None]), None

    routed, _ = lax.scan(expert_block, jnp.zeros((n_tok + 1, d), x.dtype),
                         (buf_tok.reshape(n_blk, DISPATCH_BLOCK), buf_w.reshape(n_blk, DISPATCH_BLOCK), blk_e))
    shared = (jax.nn.silu(xt @ w_sh_gate) * (xt @ w_sh_up)) @ w_sh_down
    return (routed[:n_tok] + shared).reshape(bsz, seq, d)


def setup_inputs(seed: int = 0) -> dict:
    key = jax.random.key(seed)
    ks = jax.random.split(key, 32)
    f32 = jnp.float32
    L = DEPTH

    def nrm(k, shape, scale):
        return jax.random.normal(k, shape, f32) * scale

    return {
        "x": nrm(ks[0], (BATCH, SEQ, D_MODEL), 1.0),
        "mem": nrm(ks[1], (BATCH, MEM_LEN, D_MODEL), 1.0),
        "w_in": nrm(ks[2], (L, D_MODEL, IN_COLS), D_MODEL ** -0.5),
        "g_qa": 1.0 + nrm(ks[3], (L, MLA_Q_RANK), 0.05),
        "w_q_up": nrm(ks[4], (L, MLA_Q_RANK, MLA_HEADS, MLA_NOPE_DIM + MLA_ROPE_DIM), MLA_Q_RANK ** -0.5),
        "g_kva": 1.0 + nrm(ks[5], (L, MLA_KV_RANK), 0.05),
        "w_kv_up": nrm(ks[6], (L, MLA_KV_RANK, MLA_HEADS, MLA_NOPE_DIM + MLA_V_DIM), MLA_KV_RANK ** -0.5),
        "sgu_ln_g": 1.0 + nrm(ks[7], (L, SGU_WIDTH), 0.05),
        "sgu_ln_b": nrm(ks[8], (L, SGU_WIDTH), 0.02),
        "w_spatial": nrm(ks[9], (L, SGU_GROUPS, SGU_CHUNK, SGU_CHUNK), SGU_CHUNK ** -0.5),
        "b_spatial": 1.0 + nrm(ks[10], (L, SGU_GROUPS, SGU_CHUNK), 0.1),
        "w_mem_kv": nrm(ks[11], (L, D_MODEL, 2 * MEM_WIDTH), D_MODEL ** -0.5),
        "w_br_a": nrm(ks[12], (L, MLA_WIDTH, D_MODEL), MLA_WIDTH ** -0.5),
        "w_br_b": nrm(ks[13], (L, SGU_WIDTH, D_MODEL), SGU_WIDTH ** -0.5),
        "w_br_c": nrm(ks[14], (L, MOBA_WIDTH, D_MODEL), MOBA_WIDTH ** -0.5),
        "w_br_m": nrm(ks[15], (L, MEM_WIDTH, D_MODEL), MEM_WIDTH ** -0.5),
        "w_o": nrm(ks[16], (L, D_MODEL, D_MODEL), D_MODEL ** -0.5 * DEEPNORM_BETA),
        "ln1_g": 1.0 + nrm(ks[17], (L, D_MODEL), 0.05),
        "ln1_b": nrm(ks[18], (L, D_MODEL), 0.02),
        "w_router": nrm(ks[19], (L, D_MODEL, N_EXPERTS), D_MODEL ** -0.5),
        "router_bias": nrm(ks[20], (L, N_EXPERTS), 0.01),
        "w_exp_gate": nrm(ks[21], (L, N_EXPERTS, D_MODEL, D_EXPERT), D_MODEL ** -0.5),
        "w_exp_up": nrm(ks[22], (L, N_EXPERTS, D_MODEL, D_EXPERT), D_MODEL ** -0.5),
        "w_exp_down": nrm(ks[23], (L, N_EXPERTS, D_EXPERT, D_MODEL), D_EXPERT ** -0.5 * DEEPNORM_BETA),
        "w_sh_gate": nrm(ks[24], (L, D_MODEL, D_EXPERT), D_MODEL ** -0.5),
        "w_sh_up": nrm(ks[25], (L, D_MODEL, D_EXPERT), D_MODEL ** -0.5),
        "w_sh_down": nrm(ks[26], (L, D_EXPERT, D_MODEL), D_EXPERT ** -0.5 * DEEPNORM_BETA),
        "ln2_g": 1.0 + nrm(ks[27], (L, D_MODEL), 0.05),
        "ln2_b": nrm(ks[28], (L, D_MODEL), 0.02),
    }


def reference(x, mem, w_in, g_qa, w_q_up, g_kva, w_kv_up, sgu_ln_g, sgu_ln_b, w_spatial, b_spatial,
              w_mem_kv, w_br_a, w_br_b, w_br_c, w_br_m, w_o, ln1_g, ln1_b, w_router, router_bias,
              w_exp_gate, w_exp_up, w_exp_down, w_sh_gate, w_sh_up, w_sh_down, ln2_g, ln2_b):
    positions = jnp.arange(x.shape[1], dtype=jnp.int32)
    for l in range(DEPTH):
        y = _hybrid_mixer(x, mem, positions, w_in[l], g_qa[l], w_q_up[l], g_kva[l], w_kv_up[l],
                          sgu_ln_g[l], sgu_ln_b[l], w_spatial[l], b_spatial[l], w_mem_kv[l],
                          w_br_a[l], w_br_b[l], w_br_c[l], w_br_m[l], w_o[l])
        x = _layer_norm(DEEPNORM_ALPHA * x + y, ln1_g[l], ln1_b[l])
        y = _moe(x, w_router[l], router_bias[l], w_exp_gate[l], w_exp_up[l], w_exp_down[l],
                 w_sh_gate[l], w_sh_up[l], w_sh_down[l])
        x = _layer_norm(DEEPNORM_ALPHA * x + y, ln2_g[l], ln2_b[l])
    return x
```

```python
import functools

import jax
import jax.numpy as jnp
from jax import lax
from jax.experimental import pallas as pl
from jax.experimental.pallas import tpu as pltpu

F32 = jnp.float32
BF16 = jnp.bfloat16
I32 = jnp.int32

D_MODEL = 2048
DEPTH = 2
MLA_HEADS = 8
MLA_Q_RANK = 512
MLA_KV_RANK = 256
MLA_NOPE_DIM = 128
MLA_ROPE_DIM = 64
MLA_V_DIM = 128
SGU_GROUPS = 8
SGU_GROUP_DIM = 128
SGU_WIDTH = SGU_GROUPS * SGU_GROUP_DIM
SGU_CHUNK = 128
MOBA_HEADS = 8
MOBA_HEAD_DIM = 128
MOBA_WIDTH = MOBA_HEADS * MOBA_HEAD_DIM
MOBA_BLOCK = 256
MOBA_TOPK = 3
MEM_HEADS = 4
MEM_HEAD_DIM = 128
MEM_WIDTH = MEM_HEADS * MEM_HEAD_DIM
N_BRANCH = 4
N_EXPERTS = 64
TOP_K = 6
D_EXPERT = 512
ROUTED_SCALE = 2.5
ROPE_THETA = 10000.0
DEEPNORM_ALPHA = (2 * DEPTH) ** 0.25

LANES = 128
MOE_ROW_BLOCK = 256
SLOT_STRIDE = 8
NEG_BIG = -1e30
VMEM_LIMIT = 56 * 1024 * 1024


def _cparams(n_axes):
    return pltpu.CompilerParams(dimension_semantics=("arbitrary",) * n_axes,
                                vmem_limit_bytes=VMEM_LIMIT)


def _dot(a, b):
    return jnp.dot(a, b, preferred_element_type=F32)


def _dot_nt(a, b):
    return lax.dot_general(a, b, (((1,), (1,)), ((), ())), preferred_element_type=F32)


def _layer_norm_rows(v, g, b, eps=1e-5):
    mu = jnp.mean(v, axis=-1, keepdims=True)
    vc = v - mu
    var = jnp.mean(vc * vc, axis=-1, keepdims=True)
    return vc * lax.rsqrt(var + eps) * g + b


def _rms_norm_rows(v, g, eps=1e-6):
    return v * lax.rsqrt(jnp.mean(v * v, axis=-1, keepdims=True) + eps) * g


def _mm_kernel(x_ref, w_ref, o_ref):
    o_ref[...] = _dot(x_ref[...], w_ref[...]).astype(o_ref.dtype)


def _matmul(x, w, tm, tn, out_dtype, name):
    m, k = x.shape
    n = w.shape[1]
    return pl.pallas_call(
        _mm_kernel,
        grid=(n // tn, m // tm),
        in_specs=[pl.BlockSpec((tm, k), lambda j, i: (i, 0)),
                  pl.BlockSpec((k, tn), lambda j, i: (0, j))],
        out_specs=pl.BlockSpec((tm, tn), lambda j, i: (i, j)),
        out_shape=jax.ShapeDtypeStruct((m, n), out_dtype),
        compiler_params=_cparams(2),
        name=name,
    )(x, w)


def _mla_proj_kernel(x_ref, wa_ref, wkpe_ref, gq_ref, gkv_ref, wqn_ref, wqp_ref, wqpr_ref,
                     wkn_ref, wv_ref, cos_ref, sin_ref, q_ref, k_ref, v_ref, *, scale):
    xb = x_ref[...]
    a = _dot(xb, wa_ref[...])
    cq = _rms_norm_rows(a[:, :MLA_Q_RANK], gq_ref[...]).astype(BF16)
    ckv = _rms_norm_rows(a[:, MLA_Q_RANK:], gkv_ref[...]).astype(BF16)
    cos = cos_ref[...]
    sin = sin_ref[...]
    qn = _dot(cq, wqn_ref[...])
    qp = _dot(cq, wqp_ref[...])
    qpr = _dot(cq, wqpr_ref[...])
    kn = _dot(ckv, wkn_ref[...])
    v_ref[...] = _dot(ckv, wv_ref[...]).astype(v_ref.dtype)
    kp = _dot(xb, wkpe_ref[...])
    kpe = (kp[:, :LANES] * cos + kp[:, LANES:] * sin).astype(k_ref.dtype)
    for h in range(MLA_HEADS):
        lo, hi = h * LANES, (h + 1) * LANES
        q_ref[:, 2 * lo:2 * lo + LANES] = (qn[:, lo:hi] * scale).astype(q_ref.dtype)
        q_ref[:, 2 * lo + LANES:2 * hi] = ((qp[:, lo:hi] * cos + qpr[:, lo:hi] * sin) * scale).astype(q_ref.dtype)
        k_ref[:, 2 * lo:2 * lo + LANES] = kn[:, lo:hi].astype(k_ref.dtype)
        k_ref[:, 2 * lo + LANES:2 * hi] = kpe


def _mla_proj(x, wa, wkpe, gq, gkv, wqn, wqp, wqpr, wkn, wv, cos64, sin64, seq, tm=256):
    t = x.shape[0]
    full = lambda arr: pl.BlockSpec(arr.shape, lambda i: (0,) * arr.ndim)
    n_pos = seq // tm
    scale = (MLA_NOPE_DIM + MLA_ROPE_DIM) ** -0.5
    return pl.pallas_call(
        functools.partial(_mla_proj_kernel, scale=scale),
        grid=(t // tm,),
        in_specs=[pl.BlockSpec((tm, D_MODEL), lambda i: (i, 0)),
                  full(wa), full(wkpe), full(gq), full(gkv), full(wqn), full(wqp), full(wqpr),
                  full(wkn), full(wv),
                  pl.BlockSpec((tm, LANES), lambda i: (i % n_pos, 0)),
                  pl.BlockSpec((tm, LANES), lambda i: (i % n_pos, 0))],
        out_specs=[pl.BlockSpec((tm, 2 * MLA_HEADS * LANES), lambda i: (i, 0)),
                   pl.BlockSpec((tm, 2 * MLA_HEADS * LANES), lambda i: (i, 0)),
                   pl.BlockSpec((tm, MLA_HEADS * MLA_V_DIM), lambda i: (i, 0))],
        out_shape=[jax.ShapeDtypeStruct((t, 2 * MLA_HEADS * LANES), BF16),
                   jax.ShapeDtypeStruct((t, 2 * MLA_HEADS * LANES), BF16),
                   jax.ShapeDtypeStruct((t, MLA_HEADS * MLA_V_DIM), BF16)],
        compiler_params=_cparams(1),
        name="mla_proj",
    )(x, wa, wkpe, gq, gkv, wqn, wqp, wqpr, wkn, wv, cos64, sin64)


def _softmax_step(q, k, v, carry, mask):
    m_i, l_i, acc = carry
    s = _dot_nt(q, k)
    if mask is not None:
        s = jnp.where(mask, s, NEG_BIG)
    m_new = jnp.maximum(m_i, jnp.max(s, axis=1, keepdims=True))
    alpha = jnp.exp(m_i - m_new)
    p = jnp.exp(s - m_new)
    l_new = alpha * l_i + jnp.sum(p, axis=1, keepdims=True)
    acc_new = alpha * acc + _dot(p.astype(v.dtype), v)
    return m_new, l_new, acc_new


def _causal_attn_kernel(q_ref, k_ref, v_ref, o_ref, *, tq):
    qi = pl.program_id(2)
    q = q_ref[...]
    dv = v_ref.shape[1]
    row = lax.broadcasted_iota(I32, (tq, tq), 0)
    col = lax.broadcasted_iota(I32, (tq, tq), 1)
    start = pl.multiple_of(qi * tq, tq)
    init = (jnp.full((tq, 1), NEG_BIG, F32), jnp.zeros((tq, 1), F32), jnp.zeros((tq, dv), F32))
    carry = _softmax_step(q, k_ref[pl.ds(start, tq), :], v_ref[pl.ds(start, tq), :], init, col <= row)

    def body(kb, c):
        ks = pl.multiple_of(kb * tq, tq)
        return _softmax_step(q, k_ref[pl.ds(ks, tq), :], v_ref[pl.ds(ks, tq), :], c, None)

    _, l_i, acc = lax.fori_loop(0, qi, body, carry)
    o_ref[...] = (acc / l_i).astype(o_ref.dtype)


def _mla_attention(q_a, k_a, v, bsz, seq, tq=256):
    t = q_a.shape[0]
    nq = seq // tq
    dk = 2 * LANES
    return pl.pallas_call(
        functools.partial(_causal_attn_kernel, tq=tq),
        grid=(bsz, MLA_HEADS, nq),
        in_specs=[pl.BlockSpec((tq, dk), lambda b, h, i: (b * nq + i, h)),
                  pl.BlockSpec((seq, dk), lambda b, h, i: (b, h)),
                  pl.BlockSpec((seq, MLA_V_DIM), lambda b, h, i: (b, h))],
        out_specs=pl.BlockSpec((tq, MLA_V_DIM), lambda b, h, i: (b * nq + i, h)),
        out_shape=jax.ShapeDtypeStruct((t, MLA_HEADS * MLA_V_DIM), BF16),
        compiler_params=_cparams(3),
        name="mla_attn",
    )(q_a, k_a, v)


def _sgu_kernel(x_ref, wz_ref, g_ref, b_ref, ws_ref, bs_ref, o_ref, *, tm):
    z = jax.nn.gelu(_dot(x_ref[...], wz_ref[...]))
    u = z[:, :SGU_WIDTH]
    v = _layer_norm_rows(z[:, SGU_WIDTH:], g_ref[...], b_ref[...]).astype(BF16)
    row = lax.broadcasted_iota(I32, (SGU_CHUNK, SGU_CHUNK), 0)
    col = lax.broadcasted_iota(I32, (SGU_CHUNK, SGU_CHUNK), 1)
    bias = bs_ref[...]
    for g in range(SGU_GROUPS):
        w = jnp.where(col <= row, ws_ref[g], 0.0).astype(BF16)
        lo, hi = g * SGU_GROUP_DIM, (g + 1) * SGU_GROUP_DIM
        for c in range(tm // SGU_CHUNK):
            r0, r1 = c * SGU_CHUNK, (c + 1) * SGU_CHUNK
            mixed = _dot(w, v[r0:r1, lo:hi]) + bias[:, lo:hi]
            o_ref[r0:r1, lo:hi] = (u[r0:r1, lo:hi] * mixed).astype(o_ref.dtype)


def _sgu(x, wz, ln_g, ln_b, w_s, bias_full, tm=256):
    t = x.shape[0]
    full = lambda arr: pl.BlockSpec(arr.shape, lambda i: (0,) * arr.ndim)
    return pl.pallas_call(
        functools.partial(_sgu_kernel, tm=tm),
        grid=(t // tm,),
        in_specs=[pl.BlockSpec((tm, D_MODEL), lambda i: (i, 0)),
                  full(wz), full(ln_g), full(ln_b), full(w_s), full(bias_full)],
        out_specs=pl.BlockSpec((tm, SGU_WIDTH), lambda i: (i, 0)),
        out_shape=jax.ShapeDtypeStruct((t, SGU_WIDTH), BF16),
        compiler_params=_cparams(1),
        name="sgu",
    )(x, wz, ln_g, ln_b, w_s, bias_full)


def _moba_proj_kernel(x_ref, w_ref, cos_ref, sin_ref, o_ref, *, scale):
    j = pl.program_id(0)
    acc = _dot(x_ref[...], w_ref[...])

    @pl.when(j < 2)
    def _():
        cos = cos_ref[...]
        sin = sin_ref[...]
        mult = jnp.where(j == 0, scale, 1.0).astype(F32)
        for h in range(MOBA_HEADS):
            seg = acc[:, h * LANES:(h + 1) * LANES]
            rot = pltpu.roll(seg, MOBA_HEAD_DIM // 2, axis=1)
            o_ref[:, h * LANES:(h + 1) * LANES] = ((seg * cos + rot * sin) * mult).astype(o_ref.dtype)

    @pl.when(j == 2)
    def _():
        o_ref[...] = acc.astype(o_ref.dtype)


def _moba_proj(x, w_c, cos128, sin128, seq, tm=512):
    t = x.shape[0]
    n_pos = seq // tm
    return pl.pallas_call(
        functools.partial(_moba_proj_kernel, scale=MOBA_HEAD_DIM ** -0.5),
        grid=(3, t // tm),
        in_specs=[pl.BlockSpec((tm, D_MODEL), lambda j, i: (i, 0)),
                  pl.BlockSpec((D_MODEL, MOBA_WIDTH), lambda j, i: (0, j)),
                  pl.BlockSpec((tm, LANES), lambda j, i: (i % n_pos, 0)),
                  pl.BlockSpec((tm, LANES), lambda j, i: (i % n_pos, 0))],
        out_specs=pl.BlockSpec((tm, MOBA_WIDTH), lambda j, i: (i, j)),
        out_shape=jax.ShapeDtypeStruct((t, 3 * MOBA_WIDTH), BF16),
        compiler_params=_cparams(2),
        name="moba_proj",
    )(x, w_c, cos128, sin128)


def _moba_attn_kernel(q_ref, k_ref, v_ref, avg_ref, o_ref, kmean_ref):
    j = pl.program_id(2)
    blk = MOBA_BLOCK

    @pl.when(j == 0)
    def _():
        kmean_ref[...] = _dot(avg_ref[...], k_ref[...])

    q = q_ref[...]
    gate = lax.dot_general(q.astype(F32), kmean_ref[...], (((1,), (1,)), ((), ())),
                           precision=lax.Precision.HIGHEST, preferred_element_type=F32)
    lane = lax.broadcasted_iota(I32, (blk, LANES), 1)
    n_blocks = k_ref.shape[0] // blk
    rank = jnp.zeros((blk, LANES), I32)
    for m in range(n_blocks):
        gm = gate[:, m:m + 1]
        beats = (gm > gate) | ((gm == gate) & (m < lane))
        rank = rank + jnp.where(beats & (m < j), 1, 0)
    sel = jnp.where((rank < MOBA_TOPK) & (lane < j), 1.0, 0.0)

    row = lax.broadcasted_iota(I32, (blk, blk), 0)
    col = lax.broadcasted_iota(I32, (blk, blk), 1)
    start = pl.multiple_of(j * blk, blk)
    init = (jnp.full((blk, 1), NEG_BIG, F32), jnp.zeros((blk, 1), F32), jnp.zeros((blk, MOBA_HEAD_DIM), F32))
    carry = _softmax_step(q, k_ref[pl.ds(start, blk), :], v_ref[pl.ds(start, blk), :], init, col <= row)

    def body(n, c):
        ks = pl.multiple_of(n * blk, blk)
        sel_n = jnp.max(jnp.where(lane == n, sel, 0.0), axis=1, keepdims=True) > 0.5
        return _softmax_step(q, k_ref[pl.ds(ks, blk), :], v_ref[pl.ds(ks, blk), :], c, sel_n)

    _, l_i, acc = lax.fori_loop(0, j, body, carry)
    o_ref[...] = (acc / l_i).astype(o_ref.dtype)


def _moba_attention(qkv, avg, bsz, seq):
    t = qkv.shape[0]
    nq = seq // MOBA_BLOCK
    hd = MOBA_HEAD_DIM
    return pl.pallas_call(
        _moba_attn_kernel,
        grid=(bsz, MOBA_HEADS, nq),
        in_specs=[pl.BlockSpec((MOBA_BLOCK, hd), lambda b, h, i: (b * nq + i, h)),
                  pl.BlockSpec((seq, hd), lambda b, h, i: (b, MOBA_HEADS + h)),
                  pl.BlockSpec((seq, hd), lambda b, h, i: (b, 2 * MOBA_HEADS + h)),
                  pl.BlockSpec(avg.shape, lambda b, h, i: (0, 0))],
        out_specs=pl.BlockSpec((MOBA_BLOCK, hd), lambda b, h, i: (b * nq + i, h)),
        out_shape=jax.ShapeDtypeStruct((t, MOBA_WIDTH), BF16),
        scratch_shapes=[pltpu.VMEM((LANES, hd), F32)],
        compiler_params=_cparams(3),
        name="moba_attn",
    )(qkv, qkv, qkv, avg)


def _mem_attn_kernel(x_ref, wq_ref, kv_ref, o_ref, *, scale):
    q = (_dot(x_ref[...], wq_ref[...]) * scale).astype(BF16)
    for h in range(MEM_HEADS):
        lo, hi = h * MEM_HEAD_DIM, (h + 1) * MEM_HEAD_DIM
        s = _dot_nt(q[:, lo:hi], kv_ref[:, lo:hi])
        p = jnp.exp(s - jnp.max(s, axis=1, keepdims=True))
        o = _dot(p.astype(BF16), kv_ref[:, MEM_WIDTH + lo:MEM_WIDTH + hi])
        o_ref[:, lo:hi] = (o / jnp.sum(p, axis=1, keepdims=True)).astype(o_ref.dtype)


def _mem_attention(x, w_mq, kv_mem, bsz, seq, mem_len, tm=512):
    t = x.shape[0]
    ns = seq // tm
    return pl.pallas_call(
        functools.partial(_mem_attn_kernel, scale=MEM_HEAD_DIM ** -0.5),
        grid=(bsz, ns),
        in_specs=[pl.BlockSpec((tm, D_MODEL), lambda b, i: (b * ns + i, 0)),
                  pl.BlockSpec(w_mq.shape, lambda b, i: (0, 0)),
                  pl.BlockSpec((mem_len, 2 * MEM_WIDTH), lambda b, i: (b, 0))],
        out_specs=pl.BlockSpec((tm, MEM_WIDTH), lambda b, i: (b * ns + i, 0)),
        out_shape=jax.ShapeDtypeStruct((t, MEM_WIDTH), BF16),
        compiler_params=_cparams(2),
        name="mem_attn",
    )(x, w_mq, kv_mem)


def _gated_sum_kernel(x_ref, oa_ref, ob_ref, oc_ref, om_ref, g0_ref, g1_ref, g2_ref, g3_ref,
                      wa_ref, wb_ref, wc_ref, wm_ref, y_ref):
    xb = x_ref[...]
    acc = None
    for o_ref, g_ref, w_ref in ((oa_ref, g0_ref, wa_ref), (ob_ref, g1_ref, wb_ref),
                                (oc_ref, g2_ref, wc_ref), (om_ref, g3_ref, wm_ref)):
        term = jax.nn.sigmoid(_dot(xb, g_ref[...])) * _dot(o_ref[...], w_ref[...])
        acc = term if acc is None else acc + term
    y_ref[...] = acc.astype(y_ref.dtype)


def _gated_sum(x, o_a, o_b, o_c, o_m, w_gate, w_br_a, w_br_b, w_br_c, w_br_m, tm=512, tn=256):
    t = x.shape[0]
    nj = D_MODEL // tn
    rows = lambda width: pl.BlockSpec((tm, width), lambda j, i: (i, 0))
    gate = lambda b: pl.BlockSpec((D_MODEL, tn), lambda j, i: (0, b * nj + j))
    cols = lambda width: pl.BlockSpec((width, tn), lambda j, i: (0, j))
    return pl.pallas_call(
        _gated_sum_kernel,
        grid=(nj, t // tm),
        in_specs=[rows(D_MODEL), rows(o_a.shape[1]), rows(o_b.shape[1]), rows(o_c.shape[1]), rows(o_m.shape[1]),
                  gate(0), gate(1), gate(2), gate(3),
                  cols(w_br_a.shape[0]), cols(w_br_b.shape[0]), cols(w_br_c.shape[0]), cols(w_br_m.shape[0])],
        out_specs=pl.BlockSpec((tm, tn), lambda j, i: (i, j)),
        out_shape=jax.ShapeDtypeStruct((t, D_MODEL), BF16),
        compiler_params=_cparams(2),
        name="gated_sum",
    )(x, o_a, o_b, o_c, o_m, w_gate, w_gate, w_gate, w_gate, w_br_a, w_br_b, w_br_c, w_br_m)


def _proj_norm_kernel(y_ref, w_ref, res_ref, g_ref, b_ref, o32_ref, o16_ref):
    v = DEEPNORM_ALPHA * res_ref[...] + _dot(y_ref[...], w_ref[...])
    out = _layer_norm_rows(v, g_ref[...], b_ref[...])
    o32_ref[...] = out
    o16_ref[...] = out.astype(o16_ref.dtype)


def _proj_norm(y, w_o, res, g, b, tm=256):
    t = y.shape[0]
    full = lambda arr: pl.BlockSpec(arr.shape, lambda i: (0,) * arr.ndim)
    rows = pl.BlockSpec((tm, D_MODEL), lambda i: (i, 0))
    return pl.pallas_call(
        _proj_norm_kernel,
        grid=(t // tm,),
        in_specs=[rows, full(w_o), rows, full(g), full(b)],
        out_specs=[rows, rows],
        out_shape=[jax.ShapeDtypeStruct((t, D_MODEL), F32), jax.ShapeDtypeStruct((t, D_MODEL), BF16)],
        compiler_params=_cparams(1),
        name="proj_norm",
    )(y, w_o, res, g, b)


def _router_kernel(x_ref, w_ref, bias_ref, e_ref, wt_ref, rank_ref, cnt_ref, run_ref, *, tm):
    i = pl.program_id(0)

    @pl.when(i == 0)
    def _():
        run_ref[...] = jnp.zeros_like(run_ref)

    logits = jnp.dot(x_ref[...], w_ref[...], precision=lax.Precision.HIGHEST, preferred_element_type=F32)
    scores = jax.nn.sigmoid(logits)
    lane = lax.broadcasted_iota(I32, (tm, LANES), 1)
    biased = jnp.where(lane < N_EXPERTS, scores + bias_ref[...], NEG_BIG)
    picks = []
    chosen = jnp.zeros((tm, LANES), jnp.bool_)
    for _ in range(TOP_K):
        mx = jnp.max(biased, axis=1, keepdims=True)
        idx = jnp.min(jnp.where(biased == mx, lane, LANES), axis=1, keepdims=True)
        hit = lane == idx
        picks.append((idx, hit))
        chosen = chosen | hit
        biased = jnp.where(hit, 2 * NEG_BIG, biased)
    picked_scores = jnp.where(chosen, scores, 0.0)
    norm = ROUTED_SCALE / jnp.sum(picked_scores, axis=1, keepdims=True)

    r = lax.broadcasted_iota(I32, (tm, tm), 0)
    c = lax.broadcasted_iota(I32, (tm, tm), 1)
    strict_lower = jnp.where(c < r, 1.0, 0.0).astype(BF16)
    chosen_f = jnp.where(chosen, 1.0, 0.0)
    arrival = run_ref[0:1, :] + _dot(strict_lower, chosen_f.astype(BF16))
    run_ref[...] = run_ref[...] + jnp.sum(chosen_f, axis=0, keepdims=True)
    cnt_ref[...] = run_ref[...]

    e_out = jnp.zeros((tm, LANES), I32)
    w_out = jnp.zeros((tm, LANES), F32)
    r_out = jnp.zeros((tm, LANES), F32)
    for slot, (idx, hit) in enumerate(picks):
        here = lane == slot
        e_out = jnp.where(here, idx, e_out)
        w_out = jnp.where(here, jnp.sum(jnp.where(hit, scores, 0.0), axis=1, keepdims=True) * norm, w_out)
        r_out = jnp.where(here, jnp.sum(jnp.where(hit, arrival, 0.0), axis=1, keepdims=True), r_out)
    e_ref[...] = e_out
    wt_ref[...] = w_out
    rank_ref[...] = r_out.astype(I32)


def _router(x32, w_router_pad, bias_pad, tm=256):
    t = x32.shape[0]
    full = lambda arr: pl.BlockSpec(arr.shape, lambda i: (0,) * arr.ndim)
    rows = pl.BlockSpec((tm, LANES), lambda i: (i, 0))
    return pl.pallas_call(
        functools.partial(_router_kernel, tm=tm),
        grid=(t // tm,),
        in_specs=[pl.BlockSpec((tm, D_MODEL), lambda i: (i, 0)), full(w_router_pad), full(bias_pad)],
        out_specs=[rows, rows, rows, pl.BlockSpec((8, LANES), lambda i: (0, 0))],
        out_shape=[jax.ShapeDtypeStruct((t, LANES), I32), jax.ShapeDtypeStruct((t, LANES), F32),
                   jax.ShapeDtypeStruct((t, LANES), I32), jax.ShapeDtypeStruct((8, LANES), F32)],
        scratch_shapes=[pltpu.VMEM((8, LANES), F32)],
        compiler_params=_cparams(1),
        name="moe_router",
    )(x32, w_router_pad, bias_pad)


def _dispatch_kernel(dest_ref, x_hbm, buf_hbm, out_hbm, sem, *, tm):
    del buf_hbm
    base = pl.program_id(0) * tm

    def row_copy(src_row, dst_row):
        return pltpu.make_async_copy(x_hbm.at[pl.ds(src_row, 1), :], out_hbm.at[pl.ds(dst_row, 1), :], sem)

    def issue(tok, _):
        for k in range(TOP_K):
            row_copy(base + tok, dest_ref[tok * SLOT_STRIDE + k]).start()
        return 0

    lax.fori_loop(0, tm, issue, 0)
    pltpu.make_async_copy(x_hbm.at[pl.ds(0, tm * TOP_K), :], out_hbm.at[pl.ds(0, tm * TOP_K), :], sem).wait()


def _dispatch(x32, dest_flat, n_rows, tm=256):
    t = x32.shape[0]
    zeros = jnp.zeros((n_rows, D_MODEL), F32)
    return pl.pallas_call(
        functools.partial(_dispatch_kernel, tm=tm),
        grid=(t // tm,),
        in_specs=[pl.BlockSpec((tm * SLOT_STRIDE,), lambda i: (i,), memory_space=pltpu.SMEM),
                  pl.BlockSpec(memory_space=pl.ANY),
                  pl.BlockSpec(memory_space=pl.ANY)],
        out_specs=pl.BlockSpec(memory_space=pl.ANY),
        out_shape=jax.ShapeDtypeStruct((n_rows, D_MODEL), F32),
        scratch_shapes=[pltpu.SemaphoreType.DMA(())],
        input_output_aliases={2: 0},
        compiler_params=_cparams(1),
        name="moe_dispatch",
    )(dest_flat, x32, zeros)


def _expert_kernel(blk_e_ref, n_used_ref, xs_ref, wg_ref, wu_ref, wd_ref, ys_ref):
    del blk_e_ref

    @pl.when(pl.program_id(0) < n_used_ref[0])
    def _():
        xb = xs_ref[...].astype(BF16)
        hid = jax.nn.silu(_dot(xb, wg_ref[0])) * _dot(xb, wu_ref[0])
        ys_ref[...] = _dot(hid.astype(BF16), wd_ref[0]).astype(ys_ref.dtype)

    @pl.when(pl.program_id(0) >= n_used_ref[0])
    def _():
        ys_ref[...] = jnp.zeros_like(ys_ref)


def _expert_ffn(xs, blk_e, n_used, w_gate, w_up, w_down):
    n_rows = xs.shape[0]
    bm = MOE_ROW_BLOCK
    rows = pl.BlockSpec((bm, D_MODEL), lambda i, be, nu: (i, 0))
    grid_spec = pltpu.PrefetchScalarGridSpec(
        num_scalar_prefetch=2,
        grid=(n_rows // bm,),
        in_specs=[rows,
                  pl.BlockSpec((1, D_MODEL, D_EXPERT), lambda i, be, nu: (be[i], 0, 0)),
                  pl.BlockSpec((1, D_MODEL, D_EXPERT), lambda i, be, nu: (be[i], 0, 0)),
                  pl.BlockSpec((1, D_EXPERT, D_MODEL), lambda i, be, nu: (be[i], 0, 0))],
        out_specs=rows,
    )
    return pl.pallas_call(
        _expert_kernel,
        grid_spec=grid_spec,
        out_shape=jax.ShapeDtypeStruct((n_rows, D_MODEL), F32),
        compiler_params=_cparams(1),
        name="moe_experts",
    )(blk_e, n_used, xs, w_gate, w_up, w_down)


def _combine_kernel(dest_ref, x_ref, wt_ref, ys_hbm, wsg_ref, wsu_ref, wsd_ref, g_ref, b_ref,
                    o32_ref, o16_ref, buf_ref, sem, *, tm):
    def issue(tok, _):
        for k in range(TOP_K):
            pltpu.make_async_copy(ys_hbm.at[pl.ds(dest_ref[tok * SLOT_STRIDE + k], 1), :],
                                  buf_ref.at[k, pl.ds(tok, 1), :], sem).start()
        return 0

    lax.fori_loop(0, tm, issue, 0)
    x = x_ref[...]
    xb = x.astype(BF16)
    hid = jax.nn.silu(_dot(xb, wsg_ref[...])) * _dot(xb, wsu_ref[...])
    total = DEEPNORM_ALPHA * x + _dot(hid.astype(BF16), wsd_ref[...])
    for k in range(TOP_K):
        pltpu.make_async_copy(ys_hbm.at[pl.ds(0, tm), :], buf_ref.at[k], sem).wait()
    wt = wt_ref[...]
    for k in range(TOP_K):
        total = total + wt[:, k:k + 1] * buf_ref[k]
    out = _layer_norm_rows(total, g_ref[...], b_ref[...])
    o32_ref[...] = out
    o16_ref[...] = out.astype(o16_ref.dtype)


def _combine(x32, dest_flat, wts, ys, w_sg, w_su, w_sd, g, b, tm=128):
    t = x32.shape[0]
    full = lambda arr: pl.BlockSpec(arr.shape, lambda i: (0,) * arr.ndim)
    rows = pl.BlockSpec((tm, D_MODEL), lambda i: (i, 0))
    return pl.pallas_call(
        functools.partial(_combine_kernel, tm=tm),
        grid=(t // tm,),
        in_specs=[pl.BlockSpec((tm * SLOT_STRIDE,), lambda i: (i,), memory_space=pltpu.SMEM),
                  rows,
                  pl.BlockSpec((tm, LANES), lambda i: (i, 0)),
                  pl.BlockSpec(memory_space=pl.ANY),
                  full(w_sg), full(w_su), full(w_sd), full(g), full(b)],
        out_specs=[rows, rows],
        out_shape=[jax.ShapeDtypeStruct((t, D_MODEL), F32), jax.ShapeDtypeStruct((t, D_MODEL), BF16)],
        scratch_shapes=[pltpu.VMEM((TOP_K, tm, D_MODEL), F32), pltpu.SemaphoreType.DMA(())],
        compiler_params=_cparams(1),
        name="moe_combine",
    )(dest_flat, x32, wts, ys, w_sg, w_su, w_sd, g, b)


def _rot_half_cols(w):
    half = w.shape[-1] // 2
    return jnp.concatenate([-w[..., half:], w[..., :half]], axis=-1)


def _pad_cols(w, width):
    return jnp.pad(w, [(0, 0)] * (w.ndim - 1) + [(0, width - w.shape[-1])])


def _rope_tables(seq):
    pos = jnp.arange(seq, dtype=F32)[:, None]

    def table(dim):
        half = dim // 2
        inv_freq = ROPE_THETA ** (-jnp.arange(half, dtype=F32) / half)
        ang = pos * inv_freq[None, :]
        return jnp.cos(ang), jnp.sin(ang)

    c64, s64 = table(MLA_ROPE_DIM)
    cos64 = _pad_cols(jnp.concatenate([c64, c64], axis=1), LANES)
    sin64 = _pad_cols(jnp.concatenate([s64, s64], axis=1), LANES)
    c128, s128 = table(MOBA_HEAD_DIM)
    cos128 = jnp.concatenate([c128, c128], axis=1)
    sin128 = jnp.concatenate([-s128, s128], axis=1)
    return cos64, sin64, cos128, sin128


def _layer(x32, x16, mem16, tables, avg, bsz, seq, mem_len, p):
    (w_in, g_qa, w_q_up, g_kva, w_kv_up, sgu_ln_g, sgu_ln_b, w_spatial, b_spatial, w_mem_kv,
     w_br_a, w_br_b, w_br_c, w_br_m, w_o, ln1_g, ln1_b, w_router, router_bias,
     w_exp_gate, w_exp_up, w_exp_down, w_sh_gate, w_sh_up, w_sh_down, ln2_g, ln2_b) = p
    cos64, sin64, cos128, sin128 = tables
    t = x32.shape[0]
    row = lambda v: v.reshape(1, -1).astype(F32)

    c0 = MLA_Q_RANK + MLA_KV_RANK
    c1 = c0 + MLA_ROPE_DIM
    c2 = c1 + 2 * SGU_WIDTH
    c3 = c2 + 3 * MOBA_WIDTH
    c4 = c3 + MEM_WIDTH
    w_a = w_in[:, :c0].astype(BF16)
    w_kpe_raw = w_in[:, c0:c1]
    w_kpe = jnp.concatenate([_pad_cols(w_kpe_raw, LANES), _pad_cols(_rot_half_cols(w_kpe_raw), LANES)],
                            axis=1).astype(BF16)
    w_z = w_in[:, c1:c2].astype(BF16)
    w_c = w_in[:, c2:c3].astype(BF16)
    w_mq = w_in[:, c3:c4].astype(BF16)
    w_g = w_in[:, c4:].astype(BF16)

    w_qn = w_q_up[:, :, :MLA_NOPE_DIM].reshape(MLA_Q_RANK, -1).astype(BF16)
    w_qpe = w_q_up[:, :, MLA_NOPE_DIM:]
    w_qp = _pad_cols(w_qpe, LANES).reshape(MLA_Q_RANK, -1).astype(BF16)
    w_qpr = _pad_cols(_rot_half_cols(w_qpe), LANES).reshape(MLA_Q_RANK, -1).astype(BF16)
    w_kn = w_kv_up[:, :, :MLA_NOPE_DIM].reshape(MLA_KV_RANK, -1).astype(BF16)
    w_v = w_kv_up[:, :, MLA_NOPE_DIM:].reshape(MLA_KV_RANK, -1).astype(BF16)
    q_a, k_a, v_a = _mla_proj(x16, w_a, w_kpe, row(g_qa), row(g_kva), w_qn, w_qp, w_qpr, w_kn, w_v,
                              cos64, sin64, seq)
    o_a = _mla_attention(q_a, k_a, v_a, bsz, seq)

    bias_full = jnp.repeat(b_spatial.T.astype(F32), SGU_GROUP_DIM, axis=1)
    o_b = _sgu(x16, w_z, row(sgu_ln_g), row(sgu_ln_b), w_spatial.astype(F32), bias_full)

    qkv_c = _moba_proj(x16, w_c, cos128, sin128, seq)
    o_c = _moba_attention(qkv_c, avg, bsz, seq)

    kv_mem = _matmul(mem16, w_mem_kv.astype(BF16), mem_len, 2 * MEM_WIDTH, BF16, "mem_kv")
    o_m = _mem_attention(x16, w_mq, kv_mem, bsz, seq, mem_len)

    y = _gated_sum(x16, o_a, o_b, o_c, o_m, w_g, w_br_a.astype(BF16), w_br_b.astype(BF16),
                   w_br_c.astype(BF16), w_br_m.astype(BF16))
    x32, x16 = _proj_norm(y, w_o.astype(BF16), x32, row(ln1_g), row(ln1_b))

    e_idx, wts, rank, counts = _router(x32, _pad_cols(w_router.astype(F32), LANES),
                                       _pad_cols(row(router_bias), LANES))
    bm = MOE_ROW_BLOCK
    n_blk = (t * TOP_K) // bm + N_EXPERTS
    counts = counts[0, :N_EXPERTS].astype(I32)
    padded = (counts + bm - 1) // bm * bm
    pad_ends = jnp.cumsum(padded)
    pad_starts = pad_ends - padded
    dest = _pad_cols((pad_starts[e_idx[:, :TOP_K]] + rank[:, :TOP_K]).astype(I32), SLOT_STRIDE).reshape(-1)
    blk_e = jnp.minimum(jnp.searchsorted(pad_ends, jnp.arange(n_blk, dtype=I32) * bm, side='right'),
                        N_EXPERTS - 1).astype(I32)
    n_used = (pad_ends[-1:] // bm).astype(I32)
    xs = _dispatch(x32, dest, n_blk * bm)
    ys = _expert_ffn(xs, blk_e, n_used, w_exp_gate.astype(BF16), w_exp_up.astype(BF16),
                     w_exp_down.astype(BF16))
    return _combine(x32, dest, wts, ys, w_sh_gate.astype(BF16), w_sh_up.astype(BF16),
                    w_sh_down.astype(BF16), row(ln2_g), row(ln2_b))


def kernel(x, mem, w_in, g_qa, w_q_up, g_kva, w_kv_up, sgu_ln_g, sgu_ln_b, w_spatial, b_spatial, w_mem_kv, w_br_a, w_br_b, w_br_c, w_br_m, w_o, ln1_g, ln1_b, w_router, router_bias, w_exp_gate, w_exp_up, w_exp_down, w_sh_gate, w_sh_up, w_sh_down, ln2_g, ln2_b):
    bsz, seq, d = x.shape
    mem_len = mem.shape[1]
    params = (w_in, g_qa, w_q_up, g_kva, w_kv_up, sgu_ln_g, sgu_ln_b, w_spatial, b_spatial, w_mem_kv,
              w_br_a, w_br_b, w_br_c, w_br_m, w_o, ln1_g, ln1_b, w_router, router_bias,
              w_exp_gate, w_exp_up, w_exp_down, w_sh_gate, w_sh_up, w_sh_down, ln2_g, ln2_b)
    tables = _rope_tables(seq)
    n_blocks = seq // MOBA_BLOCK
    blk_of_pos = jnp.arange(seq, dtype=I32)[None, :] // MOBA_BLOCK
    avg = jnp.where(blk_of_pos == jnp.arange(LANES, dtype=I32)[:, None], 1.0 / MOBA_BLOCK, 0.0).astype(BF16)
    assert n_blocks <= LANES
    x32 = x.reshape(bsz * seq, d)
    x16 = x32.astype(BF16)
    mem16 = mem.reshape(bsz * mem_len, d).astype(BF16)
    for l in range(DEPTH):
        x32, x16 = _layer(x32, x16, mem16, tables, avg, bsz, seq, mem_len, tuple(w[l] for w in params))
    return x32.reshape(bsz, seq, d)
```

```python
import functools

import jax
import jax.numpy as jnp
from jax import lax
from jax.experimental import pallas as pl
from jax.experimental.pallas import tpu as pltpu

F32 = jnp.float32
BF16 = jnp.bfloat16
I32 = jnp.int32

D_MODEL = 2048
DEPTH = 2
MLA_HEADS = 8
MLA_Q_RANK = 512
MLA_KV_RANK = 256
MLA_NOPE_DIM = 128
MLA_ROPE_DIM = 64
MLA_V_DIM = 128
SGU_GROUPS = 8
SGU_GROUP_DIM = 128
SGU_WIDTH = SGU_GROUPS * SGU_GROUP_DIM
SGU_CHUNK = 128
MOBA_HEADS = 8
MOBA_HEAD_DIM = 128
MOBA_WIDTH = MOBA_HEADS * MOBA_HEAD_DIM
MOBA_BLOCK = 256
MOBA_TOPK = 3
MEM_HEADS = 4
MEM_HEAD_DIM = 128
MEM_WIDTH = MEM_HEADS * MEM_HEAD_DIM
N_BRANCH = 4
N_EXPERTS = 64
TOP_K = 6
D_EXPERT = 512
ROUTED_SCALE = 2.5
ROPE_THETA = 10000.0
DEEPNORM_ALPHA = (2 * DEPTH) ** 0.25

LANES = 128
MOE_ROW_BLOCK = 256
SLOT_STRIDE = 8
NEG_BIG = -1e30
VMEM_LIMIT = 56 * 1024 * 1024


def _cparams(n_axes):
    return pltpu.CompilerParams(dimension_semantics=("arbitrary",) * n_axes,
                                vmem_limit_bytes=VMEM_LIMIT)


def _dot(a, b):
    return jnp.dot(a, b, preferred_element_type=F32)


def _dot_nt(a, b):
    return lax.dot_general(a, b, (((1,), (1,)), ((), ())), preferred_element_type=F32)


def _layer_norm_rows(v, g, b, eps=1e-5):
    mu = jnp.mean(v, axis=-1, keepdims=True)
    vc = v - mu
    var = jnp.mean(vc * vc, axis=-1, keepdims=True)
    return vc * lax.rsqrt(var + eps) * g + b


def _rms_norm_rows(v, g, eps=1e-6):
    return v * lax.rsqrt(jnp.mean(v * v, axis=-1, keepdims=True) + eps) * g


def _mm_kernel(x_ref, w_ref, o_ref):
    o_ref[...] = _dot(x_ref[...], w_ref[...]).astype(o_ref.dtype)


def _matmul(x, w, tm, tn, out_dtype, name):
    m, k = x.shape
    n = w.shape[1]
    return pl.pallas_call(
        _mm_kernel,
        grid=(n // tn, m // tm),
        in_specs=[pl.BlockSpec((tm, k), lambda j, i: (i, 0)),
                  pl.BlockSpec((k, tn), lambda j, i: (0, j))],
        out_specs=pl.BlockSpec((tm, tn), lambda j, i: (i, j)),
        out_shape=jax.ShapeDtypeStruct((m, n), out_dtype),
        compiler_params=_cparams(2),
        name=name,
    )(x, w)


def _mla_proj_kernel(x_ref, wa_ref, wkpe_ref, gq_ref, gkv_ref, wqn_ref, wqp_ref, wqpr_ref,
                     wkn_ref, wv_ref, cos_ref, sin_ref, q_ref, k_ref, v_ref, *, scale):
    xb = x_ref[...]
    a = _dot(xb, wa_ref[...])
    cq = _rms_norm_rows(a[:, :MLA_Q_RANK], gq_ref[...]).astype(BF16)
    ckv = _rms_norm_rows(a[:, MLA_Q_RANK:], gkv_ref[...]).astype(BF16)
    cos = cos_ref[...]
    sin = sin_ref[...]
    qn = _dot(cq, wqn_ref[...])
    qp = _dot(cq, wqp_ref[...])
    qpr = _dot(cq, wqpr_ref[...])
    kn = _dot(ckv, wkn_ref[...])
    v_ref[...] = _dot(ckv, wv_ref[...]).astype(v_ref.dtype)
    kp = _dot(xb, wkpe_ref[...])
    kpe = (kp[:, :LANES] * cos + kp[:, LANES:] * sin).astype(k_ref.dtype)
    for h in range(MLA_HEADS):
        lo, hi = h * LANES, (h + 1) * LANES
        q_ref[:, 2 * lo:2 * lo + LANES] = (qn[:, lo:hi] * scale).astype(q_ref.dtype)
        q_ref[:, 2 * lo + LANES:2 * hi] = ((qp[:, lo:hi] * cos + qpr[:, lo:hi] * sin) * scale).astype(q_ref.dtype)
        k_ref[:, 2 * lo:2 * lo + LANES] = kn[:, lo:hi].astype(k_ref.dtype)
        k_ref[:, 2 * lo + LANES:2 * hi] = kpe


def _mla_proj(x, wa, wkpe, gq, gkv, wqn, wqp, wqpr, wkn, wv, cos64, sin64, seq, tm=256):
    t = x.shape[0]
    full = lambda arr: pl.BlockSpec(arr.shape, lambda i: (0,) * arr.ndim)
    n_pos = seq // tm
    scale = (MLA_NOPE_DIM + MLA_ROPE_DIM) ** -0.5
    return pl.pallas_call(
        functools.partial(_mla_proj_kernel, scale=scale),
        grid=(t // tm,),
        in_specs=[pl.BlockSpec((tm, D_MODEL), lambda i: (i, 0)),
                  full(wa), full(wkpe), full(gq), full(gkv), full(wqn), full(wqp), full(wqpr),
                  full(wkn), full(wv),
                  pl.BlockSpec((tm, LANES), lambda i: (i % n_pos, 0)),
                  pl.BlockSpec((tm, LANES), lambda i: (i % n_pos, 0))],
        out_specs=[pl.BlockSpec((tm, 2 * MLA_HEADS * LANES), lambda i: (i, 0)),
                   pl.BlockSpec((tm, 2 * MLA_HEADS * LANES), lambda i: (i, 0)),
                   pl.BlockSpec((tm, MLA_HEADS * MLA_V_DIM), lambda i: (i, 0))],
        out_shape=[jax.ShapeDtypeStruct((t, 2 * MLA_HEADS * LANES), BF16),
                   jax.ShapeDtypeStruct((t, 2 * MLA_HEADS * LANES), BF16),
                   jax.ShapeDtypeStruct((t, MLA_HEADS * MLA_V_DIM), BF16)],
        compiler_params=_cparams(1),
        name="mla_proj",
    )(x, wa, wkpe, gq, gkv, wqn, wqp, wqpr, wkn, wv, cos64, sin64)


def _softmax_step(q, k, v, carry, mask):
    m_i, l_i, acc = carry
    s = _dot_nt(q, k)
    if mask is not None:
        s = jnp.where(mask, s, NEG_BIG)
    m_new = jnp.maximum(m_i, jnp.max(s, axis=1, keepdims=True))
    alpha = jnp.exp(m_i - m_new)
    p = jnp.exp(s - m_new)
    l_new = alpha * l_i + jnp.sum(p, axis=1, keepdims=True)
    acc_new = alpha * acc + _dot(p.astype(v.dtype), v)
    return m_new, l_new, acc_new


def _causal_attn_kernel(q_ref, k_ref, v_ref, o_ref, *, tq, n_heads, dk, dv):
    qi = pl.program_id(2)
    row = lax.broadcasted_iota(I32, (tq, tq), 0)
    col = lax.broadcasted_iota(I32, (tq, tq), 1)
    qs = [q_ref[:, h * dk:(h + 1) * dk] for h in range(n_heads)]

    def step(ks, carries, mask):
        return tuple(_softmax_step(qs[h], k_ref[pl.ds(ks, tq), h * dk:(h + 1) * dk],
                                   v_ref[pl.ds(ks, tq), h * dv:(h + 1) * dv], carries[h], mask)
                     for h in range(n_heads))

    init = tuple((jnp.full((tq, 1), NEG_BIG, F32), jnp.zeros((tq, 1), F32), jnp.zeros((tq, dv), F32))
                 for _ in range(n_heads))
    carries = step(pl.multiple_of(qi * tq, tq), init, col <= row)
    carries = lax.fori_loop(0, qi, lambda kb, c: step(pl.multiple_of(kb * tq, tq), c, None), carries)
    for h, (_, l_i, acc) in enumerate(carries):
        o_ref[:, h * dv:(h + 1) * dv] = (acc / l_i).astype(o_ref.dtype)


def _mla_attention(q_a, k_a, v, bsz, seq, tq=256, n_heads=2):
    t = q_a.shape[0]
    nq = seq // tq
    dk = 2 * LANES
    dv = MLA_V_DIM
    return pl.pallas_call(
        functools.partial(_causal_attn_kernel, tq=tq, n_heads=n_heads, dk=dk, dv=dv),
        grid=(bsz, MLA_HEADS // n_heads, nq),
        in_specs=[pl.BlockSpec((tq, n_heads * dk), lambda b, h, i: (b * nq + i, h)),
                  pl.BlockSpec((seq, n_heads * dk), lambda b, h, i: (b, h)),
                  pl.BlockSpec((seq, n_heads * dv), lambda b, h, i: (b, h))],
        out_specs=pl.BlockSpec((tq, n_heads * dv), lambda b, h, i: (b * nq + i, h)),
        out_shape=jax.ShapeDtypeStruct((t, MLA_HEADS * dv), BF16),
        compiler_params=_cparams(3),
        name="mla_attn",
    )(q_a, k_a, v)


def _sgu_kernel(x_ref, wz_ref, g_ref, b_ref, ws_ref, bs_ref, o_ref, *, tm):
    z = jax.nn.gelu(_dot(x_ref[...], wz_ref[...]))
    u = z[:, :SGU_WIDTH]
    v = _layer_norm_rows(z[:, SGU_WIDTH:], g_ref[...], b_ref[...]).astype(BF16)
    row = lax.broadcasted_iota(I32, (SGU_CHUNK, SGU_CHUNK), 0)
    col = lax.broadcasted_iota(I32, (SGU_CHUNK, SGU_CHUNK), 1)
    bias = bs_ref[...]
    for g in range(SGU_GROUPS):
        w = jnp.where(col <= row, ws_ref[g], 0.0).astype(BF16)
        lo, hi = g * SGU_GROUP_DIM, (g + 1) * SGU_GROUP_DIM
        for c in range(tm // SGU_CHUNK):
            r0, r1 = c * SGU_CHUNK, (c + 1) * SGU_CHUNK
            mixed = _dot(w, v[r0:r1, lo:hi]) + bias[:, lo:hi]
            o_ref[r0:r1, lo:hi] = (u[r0:r1, lo:hi] * mixed).astype(o_ref.dtype)


def _sgu(x, wz, ln_g, ln_b, w_s, bias_full, tm=256):
    t = x.shape[0]
    full = lambda arr: pl.BlockSpec(arr.shape, lambda i: (0,) * arr.ndim)
    return pl.pallas_call(
        functools.partial(_sgu_kernel, tm=tm),
        grid=(t // tm,),
        in_specs=[pl.BlockSpec((tm, D_MODEL), lambda i: (i, 0)),
                  full(wz), full(ln_g), full(ln_b), full(w_s), full(bias_full)],
        out_specs=pl.BlockSpec((tm, SGU_WIDTH), lambda i: (i, 0)),
        out_shape=jax.ShapeDtypeStruct((t, SGU_WIDTH), BF16),
        compiler_params=_cparams(1),
        name="sgu",
    )(x, wz, ln_g, ln_b, w_s, bias_full)


def _moba_proj_kernel(x_ref, w_ref, cos_ref, sin_ref, o_ref, *, scale):
    j = pl.program_id(0)
    acc = _dot(x_ref[...], w_ref[...])

    @pl.when(j < 2)
    def _():
        cos = cos_ref[...]
        sin = sin_ref[...]
        mult = jnp.where(j == 0, scale, 1.0).astype(F32)
        for h in range(MOBA_HEADS):
            seg = acc[:, h * LANES:(h + 1) * LANES]
            rot = pltpu.roll(seg, MOBA_HEAD_DIM // 2, axis=1)
            o_ref[:, h * LANES:(h + 1) * LANES] = ((seg * cos + rot * sin) * mult).astype(o_ref.dtype)

    @pl.when(j == 2)
    def _():
        o_ref[...] = acc.astype(o_ref.dtype)


def _moba_proj(x, w_c, cos128, sin128, seq, tm=512):
    t = x.shape[0]
    n_pos = seq // tm
    return pl.pallas_call(
        functools.partial(_moba_proj_kernel, scale=MOBA_HEAD_DIM ** -0.5),
        grid=(3, t // tm),
        in_specs=[pl.BlockSpec((tm, D_MODEL), lambda j, i: (i, 0)),
                  pl.BlockSpec((D_MODEL, MOBA_WIDTH), lambda j, i: (0, j)),
                  pl.BlockSpec((tm, LANES), lambda j, i: (i % n_pos, 0)),
                  pl.BlockSpec((tm, LANES), lambda j, i: (i % n_pos, 0))],
        out_specs=pl.BlockSpec((tm, MOBA_WIDTH), lambda j, i: (i, j)),
        out_shape=jax.ShapeDtypeStruct((t, 3 * MOBA_WIDTH), BF16),
        compiler_params=_cparams(2),
        name="moba_proj",
    )(x, w_c, cos128, sin128)


def _moba_attn_kernel(q_ref, k_ref, v_ref, avg_ref, o_ref, kmean_ref, *, n_heads):
    j = pl.program_id(2)
    blk = MOBA_BLOCK
    hd = MOBA_HEAD_DIM

    @pl.when(j == 0)
    def _():
        kmean_ref[...] = _dot(avg_ref[...], k_ref[...])

    lane = lax.broadcasted_iota(I32, (blk, LANES), 1)
    n_blocks = k_ref.shape[0] // blk
    qs, sels = [], []
    for h in range(n_heads):
        q = q_ref[:, h * hd:(h + 1) * hd]
        gate = lax.dot_general(q.astype(F32), kmean_ref[:, h * hd:(h + 1) * hd], (((1,), (1,)), ((), ())),
                               precision=lax.Precision.HIGHEST, preferred_element_type=F32)
        rank = jnp.zeros((blk, LANES), I32)
        for m in range(n_blocks):
            gm = gate[:, m:m + 1]
            beats = (gm > gate) | ((gm == gate) & (m < lane))
            rank = rank + jnp.where(beats & (m < j), 1, 0)
        qs.append(q)
        sels.append(jnp.where((rank < MOBA_TOPK) & (lane < j), 1.0, 0.0))

    def step(ks, carries, masks):
        return tuple(_softmax_step(qs[h], k_ref[pl.ds(ks, blk), h * hd:(h + 1) * hd],
                                   v_ref[pl.ds(ks, blk), h * hd:(h + 1) * hd], carries[h], masks[h])
                     for h in range(n_heads))

    row = lax.broadcasted_iota(I32, (blk, blk), 0)
    col = lax.broadcasted_iota(I32, (blk, blk), 1)
    init = tuple((jnp.full((blk, 1), NEG_BIG, F32), jnp.zeros((blk, 1), F32), jnp.zeros((blk, hd), F32))
                 for _ in range(n_heads))
    carries = step(pl.multiple_of(j * blk, blk), init, [col <= row] * n_heads)

    def body(n, c):
        masks = [jnp.max(jnp.where(lane == n, sels[h], 0.0), axis=1, keepdims=True) > 0.5 for h in range(n_heads)]
        return step(pl.multiple_of(n * blk, blk), c, masks)

    carries = lax.fori_loop(0, j, body, carries)
    for h, (_, l_i, acc) in enumerate(carries):
        o_ref[:, h * hd:(h + 1) * hd] = (acc / l_i).astype(o_ref.dtype)


def _moba_attention(qkv, avg, bsz, seq, n_heads=2):
    t = qkv.shape[0]
    nq = seq // MOBA_BLOCK
    w = n_heads * MOBA_HEAD_DIM
    groups = MOBA_HEADS // n_heads
    return pl.pallas_call(
        functools.partial(_moba_attn_kernel, n_heads=n_heads),
        grid=(bsz, groups, nq),
        in_specs=[pl.BlockSpec((MOBA_BLOCK, w), lambda b, h, i: (b * nq + i, h)),
                  pl.BlockSpec((seq, w), lambda b, h, i: (b, groups + h)),
                  pl.BlockSpec((seq, w), lambda b, h, i: (b, 2 * groups + h)),
                  pl.BlockSpec(avg.shape, lambda b, h, i: (0, 0))],
        out_specs=pl.BlockSpec((MOBA_BLOCK, w), lambda b, h, i: (b * nq + i, h)),
        out_shape=jax.ShapeDtypeStruct((t, MOBA_WIDTH), BF16),
        scratch_shapes=[pltpu.VMEM((LANES, w), F32)],
        compiler_params=_cparams(3),
        name="moba_attn",
    )(qkv, qkv, qkv, avg)


def _mem_attn_kernel(x_ref, wq_ref, kv_ref, o_ref, *, scale):
    q = (_dot(x_ref[...], wq_ref[...]) * scale).astype(BF16)
    for h in range(MEM_HEADS):
        lo, hi = h * MEM_HEAD_DIM, (h + 1) * MEM_HEAD_DIM
        s = _dot_nt(q[:, lo:hi], kv_ref[:, lo:hi])
        p = jnp.exp(s - jnp.max(s, axis=1, keepdims=True))
        o = _dot(p.astype(BF16), kv_ref[:, MEM_WIDTH + lo:MEM_WIDTH + hi])
        o_ref[:, lo:hi] = (o / jnp.sum(p, axis=1, keepdims=True)).astype(o_ref.dtype)


def _mem_attention(x, w_mq, kv_mem, bsz, seq, mem_len, tm=512):
    t = x.shape[0]
    ns = seq // tm
    return pl.pallas_call(
        functools.partial(_mem_attn_kernel, scale=MEM_HEAD_DIM ** -0.5),
        grid=(bsz, ns),
        in_specs=[pl.BlockSpec((tm, D_MODEL), lambda b, i: (b * ns + i, 0)),
                  pl.BlockSpec(w_mq.shape, lambda b, i: (0, 0)),
                  pl.BlockSpec((mem_len, 2 * MEM_WIDTH), lambda b, i: (b, 0))],
        out_specs=pl.BlockSpec((tm, MEM_WIDTH), lambda b, i: (b * ns + i, 0)),
        out_shape=jax.ShapeDtypeStruct((t, MEM_WIDTH), BF16),
        compiler_params=_cparams(2),
        name="mem_attn",
    )(x, w_mq, kv_mem)


def _gated_sum_kernel(x_ref, oa_ref, ob_ref, oc_ref, om_ref, g0_ref, g1_ref, g2_ref, g3_ref,
                      wa_ref, wb_ref, wc_ref, wm_ref, y_ref):
    xb = x_ref[...]
    acc = None
    for o_ref, g_ref, w_ref in ((oa_ref, g0_ref, wa_ref), (ob_ref, g1_ref, wb_ref),
                                (oc_ref, g2_ref, wc_ref), (om_ref, g3_ref, wm_ref)):
        term = jax.nn.sigmoid(_dot(xb, g_ref[...])) * _dot(o_ref[...], w_ref[...])
        acc = term if acc is None else acc + term
    y_ref[...] = acc.astype(y_ref.dtype)


def _gated_sum(x, o_a, o_b, o_c, o_m, w_gate, w_br_a, w_br_b, w_br_c, w_br_m, tm=512, tn=256):
    t = x.shape[0]
    nj = D_MODEL // tn
    rows = lambda width: pl.BlockSpec((tm, width), lambda j, i: (i, 0))
    gate = lambda b: pl.BlockSpec((D_MODEL, tn), lambda j, i: (0, b * nj + j))
    cols = lambda width: pl.BlockSpec((width, tn), lambda j, i: (0, j))
    return pl.pallas_call(
        _gated_sum_kernel,
        grid=(nj, t // tm),
        in_specs=[rows(D_MODEL), rows(o_a.shape[1]), rows(o_b.shape[1]), rows(o_c.shape[1]), rows(o_m.shape[1]),
                  gate(0), gate(1), gate(2), gate(3),
                  cols(w_br_a.shape[0]), cols(w_br_b.shape[0]), cols(w_br_c.shape[0]), cols(w_br_m.shape[0])],
        out_specs=pl.BlockSpec((tm, tn), lambda j, i: (i, j)),
        out_shape=jax.ShapeDtypeStruct((t, D_MODEL), BF16),
        compiler_params=_cparams(2),
        name="gated_sum",
    )(x, o_a, o_b, o_c, o_m, w_gate, w_gate, w_gate, w_gate, w_br_a, w_br_b, w_br_c, w_br_m)


def _proj_norm_kernel(y_ref, w_ref, res_ref, g_ref, b_ref, o32_ref, o16_ref):
    v = DEEPNORM_ALPHA * res_ref[...] + _dot(y_ref[...], w_ref[...])
    out = _layer_norm_rows(v, g_ref[...], b_ref[...])
    o32_ref[...] = out
    o16_ref[...] = out.astype(o16_ref.dtype)


def _proj_norm(y, w_o, res, g, b, tm=256):
    t = y.shape[0]
    full = lambda arr: pl.BlockSpec(arr.shape, lambda i: (0,) * arr.ndim)
    rows = pl.BlockSpec((tm, D_MODEL), lambda i: (i, 0))
    return pl.pallas_call(
        _proj_norm_kernel,
        grid=(t // tm,),
        in_specs=[rows, full(w_o), rows, full(g), full(b)],
        out_specs=[rows, rows],
        out_shape=[jax.ShapeDtypeStruct((t, D_MODEL), F32), jax.ShapeDtypeStruct((t, D_MODEL), BF16)],
        compiler_params=_cparams(1),
        name="proj_norm",
    )(y, w_o, res, g, b)


def _router_kernel(x_ref, w_ref, bias_ref, e_ref, wt_ref, rank_ref, cnt_ref, run_ref, *, tm):
    i = pl.program_id(0)

    @pl.when(i == 0)
    def _():
        run_ref[...] = jnp.zeros_like(run_ref)

    logits = jnp.dot(x_ref[...], w_ref[...], precision=lax.Precision.HIGHEST, preferred_element_type=F32)
    scores = jax.nn.sigmoid(logits)
    lane = lax.broadcasted_iota(I32, (tm, LANES), 1)
    biased = jnp.where(lane < N_EXPERTS, scores + bias_ref[...], NEG_BIG)
    picks = []
    chosen = jnp.zeros((tm, LANES), jnp.bool_)
    for _ in range(TOP_K):
        mx = jnp.max(biased, axis=1, keepdims=True)
        idx = jnp.min(jnp.where(biased == mx, lane, LANES), axis=1, keepdims=True)
        hit = lane == idx
        picks.append((idx, hit))
        chosen = chosen | hit
        biased = jnp.where(hit, 2 * NEG_BIG, biased)
    picked_scores = jnp.where(chosen, scores, 0.0)
    norm = ROUTED_SCALE / jnp.sum(picked_scores, axis=1, keepdims=True)

    r = lax.broadcasted_iota(I32, (tm, tm), 0)
    c = lax.broadcasted_iota(I32, (tm, tm), 1)
    strict_lower = jnp.where(c < r, 1.0, 0.0).astype(BF16)
    chosen_f = jnp.where(chosen, 1.0, 0.0)
    arrival = run_ref[0:1, :] + _dot(strict_lower, chosen_f.astype(BF16))
    run_ref[...] = run_ref[...] + jnp.sum(chosen_f, axis=0, keepdims=True)
    cnt_ref[...] = run_ref[...]

    e_out = jnp.zeros((tm, LANES), I32)
    w_out = jnp.zeros((tm, LANES), F32)
    r_out = jnp.zeros((tm, LANES), F32)
    for slot, (idx, hit) in enumerate(picks):
        here = lane == slot
        e_out = jnp.where(here, idx, e_out)
        w_out = jnp.where(here, jnp.sum(jnp.where(hit, scores, 0.0), axis=1, keepdims=True) * norm, w_out)
        r_out = jnp.where(here, jnp.sum(jnp.where(hit, arrival, 0.0), axis=1, keepdims=True), r_out)
    e_ref[...] = e_out
    wt_ref[...] = w_out
    rank_ref[...] = r_out.astype(I32)


def _router(x32, w_router_pad, bias_pad, tm=256):
    t = x32.shape[0]
    full = lambda arr: pl.BlockSpec(arr.shape, lambda i: (0,) * arr.ndim)
    rows = pl.BlockSpec((tm, LANES), lambda i: (i, 0))
    return pl.pallas_call(
        functools.partial(_router_kernel, tm=tm),
        grid=(t // tm,),
        in_specs=[pl.BlockSpec((tm, D_MODEL), lambda i: (i, 0)), full(w_router_pad), full(bias_pad)],
        out_specs=[rows, rows, rows, pl.BlockSpec((8, LANES), lambda i: (0, 0))],
        out_shape=[jax.ShapeDtypeStruct((t, LANES), I32), jax.ShapeDtypeStruct((t, LANES), F32),
                   jax.ShapeDtypeStruct((t, LANES), I32), jax.ShapeDtypeStruct((8, LANES), F32)],
        scratch_shapes=[pltpu.VMEM((8, LANES), F32)],
        compiler_params=_cparams(1),
        name="moe_router",
    )(x32, w_router_pad, bias_pad)


def _dispatch_kernel(fill_start_ref, fill_n_ref, n_used_ref, dest_ref, x_ref, out_hbm, zero_ref, sem, fill_sem,
                     *, tm):
    @pl.when(pl.program_id(0) == 0)
    def _():
        zero_ref[...] = jnp.zeros_like(zero_ref)
        bm = zero_ref.shape[0]
        n_blk = out_hbm.shape[0] // bm

        def fill_copy(dst_row):
            return pltpu.make_async_copy(zero_ref.at[pl.ds(0, 1), :], out_hbm.at[pl.ds(dst_row, 1), :], fill_sem)

        def tail_copy(blk):
            return pltpu.make_async_copy(zero_ref, out_hbm.at[pl.ds(pl.multiple_of(blk * bm, bm), bm), :], fill_sem)

        def start_expert(e, _):
            first = fill_start_ref[e]
            lax.fori_loop(0, fill_n_ref[e], lambda r, c: (fill_copy(first + r).start(), c)[1], 0)
            return 0

        def wait_expert(e, _):
            lax.fori_loop(0, fill_n_ref[e], lambda r, c: (fill_copy(0).wait(), c)[1], 0)
            return 0

        lax.fori_loop(0, N_EXPERTS, start_expert, 0)
        lax.fori_loop(n_used_ref[0], n_blk, lambda b, c: (tail_copy(b).start(), c)[1], 0)
        lax.fori_loop(0, N_EXPERTS, wait_expert, 0)
        lax.fori_loop(n_used_ref[0], n_blk, lambda b, c: (tail_copy(0).wait(), c)[1], 0)

    def issue(tok, _):
        for k in range(TOP_K):
            pltpu.make_async_copy(x_ref.at[pl.ds(tok, 1), :],
                                  out_hbm.at[pl.ds(dest_ref[tok * SLOT_STRIDE + k], 1), :], sem).start()
        return 0

    lax.fori_loop(0, tm, issue, 0)
    for _ in range(TOP_K):
        pltpu.make_async_copy(x_ref, out_hbm.at[pl.ds(0, tm), :], sem).wait()


def _dispatch(x32, dest_flat, fill_start, fill_n, n_used, n_rows, tm=256):
    t = x32.shape[0]
    grid_spec = pltpu.PrefetchScalarGridSpec(
        num_scalar_prefetch=3,
        grid=(t // tm,),
        in_specs=[pl.BlockSpec((tm * SLOT_STRIDE,), lambda i, fs, fn, nu: (i,), memory_space=pltpu.SMEM),
                  pl.BlockSpec((tm, D_MODEL), lambda i, fs, fn, nu: (i, 0))],
        out_specs=pl.BlockSpec(memory_space=pl.ANY),
        scratch_shapes=[pltpu.VMEM((MOE_ROW_BLOCK, D_MODEL), F32), pltpu.SemaphoreType.DMA(()),
                        pltpu.SemaphoreType.DMA(())],
    )
    return pl.pallas_call(
        functools.partial(_dispatch_kernel, tm=tm),
        grid_spec=grid_spec,
        out_shape=jax.ShapeDtypeStruct((n_rows, D_MODEL), F32),
        compiler_params=_cparams(1),
        name="moe_dispatch",
    )(fill_start, fill_n, n_used, dest_flat, x32)


def _expert_kernel(blk_e_ref, n_used_ref, xs_ref, wg_ref, wu_ref, wd_ref, ys_ref, wg16_ref, wu16_ref, wd16_ref):
    i = pl.program_id(0)
    used = i < n_used_ref[0]
    new_expert = (i == 0) | (blk_e_ref[i] != blk_e_ref[jnp.maximum(i - 1, 0)])

    @pl.when(used & new_expert)
    def _():
        wg16_ref[...] = wg_ref[0, 0].astype(BF16)
        wu16_ref[...] = wu_ref[0, 0].astype(BF16)
        wd16_ref[...] = wd_ref[0, 0].astype(BF16)

    @pl.when(used)
    def _():
        xb = xs_ref[...].astype(BF16)
        hid = jax.nn.silu(_dot(xb, wg16_ref[...])) * _dot(xb, wu16_ref[...])
        ys_ref[...] = _dot(hid.astype(BF16), wd16_ref[...]).astype(ys_ref.dtype)

    @pl.when(jnp.logical_not(used))
    def _():
        ys_ref[...] = jnp.zeros_like(ys_ref)


def _expert_ffn(xs, blk_e, n_used, w_gate, w_up, w_down, layer):
    n_rows = xs.shape[0]
    bm = MOE_ROW_BLOCK
    grid_spec = pltpu.PrefetchScalarGridSpec(
        num_scalar_prefetch=2,
        grid=(n_rows // bm,),
        in_specs=[pl.BlockSpec((bm, D_MODEL), lambda i, be, nu: (jnp.minimum(i, nu[0] - 1), 0)),
                  pl.BlockSpec((1, 1, D_MODEL, D_EXPERT), lambda i, be, nu: (layer, be[i], 0, 0)),
                  pl.BlockSpec((1, 1, D_MODEL, D_EXPERT), lambda i, be, nu: (layer, be[i], 0, 0)),
                  pl.BlockSpec((1, 1, D_EXPERT, D_MODEL), lambda i, be, nu: (layer, be[i], 0, 0))],
        out_specs=pl.BlockSpec((bm, D_MODEL), lambda i, be, nu: (i, 0)),
        scratch_shapes=[pltpu.VMEM((D_MODEL, D_EXPERT), BF16), pltpu.VMEM((D_MODEL, D_EXPERT), BF16),
                        pltpu.VMEM((D_EXPERT, D_MODEL), BF16)],
    )
    return pl.pallas_call(
        _expert_kernel,
        grid_spec=grid_spec,
        out_shape=jax.ShapeDtypeStruct((n_rows, D_MODEL), F32),
        compiler_params=_cparams(1),
        name="moe_experts",
    )(blk_e, n_used, xs, w_gate, w_up, w_down)


def _combine_kernel(dest_ref, x_ref, wt_ref, ys_hbm, wsg_ref, wsu_ref, wsd_ref, g_ref, b_ref,
                    o32_ref, o16_ref, buf_ref, sem, *, tm):
    def issue(tok, _):
        for k in range(TOP_K):
            pltpu.make_async_copy(ys_hbm.at[pl.ds(dest_ref[tok * SLOT_STRIDE + k], 1), :],
                                  buf_ref.at[k, pl.ds(tok, 1), :], sem).start()
        return 0

    lax.fori_loop(0, tm, issue, 0)
    x = x_ref[...]
    xb = x.astype(BF16)
    hid = jax.nn.silu(_dot(xb, wsg_ref[...])) * _dot(xb, wsu_ref[...])
    total = DEEPNORM_ALPHA * x + _dot(hid.astype(BF16), wsd_ref[...])
    for k in range(TOP_K):
        pltpu.make_async_copy(ys_hbm.at[pl.ds(0, tm), :], buf_ref.at[k], sem).wait()
    wt = wt_ref[...]
    for k in range(TOP_K):
        total = total + wt[:, k:k + 1] * buf_ref[k]
    out = _layer_norm_rows(total, g_ref[...], b_ref[...])
    o32_ref[...] = out
    o16_ref[...] = out.astype(o16_ref.dtype)


def _combine(x32, dest_flat, wts, ys, w_sg, w_su, w_sd, g, b, tm=128):
    t = x32.shape[0]
    full = lambda arr: pl.BlockSpec(arr.shape, lambda i: (0,) * arr.ndim)
    rows = pl.BlockSpec((tm, D_MODEL), lambda i: (i, 0))
    return pl.pallas_call(
        functools.partial(_combine_kernel, tm=tm),
        grid=(t // tm,),
        in_specs=[pl.BlockSpec((tm * SLOT_STRIDE,), lambda i: (i,), memory_space=pltpu.SMEM),
                  rows,
                  pl.BlockSpec((tm, LANES), lambda i: (i, 0)),
                  pl.BlockSpec(memory_space=pl.ANY),
                  full(w_sg), full(w_su), full(w_sd), full(g), full(b)],
        out_specs=[rows, rows],
        out_shape=[jax.ShapeDtypeStruct((t, D_MODEL), F32), jax.ShapeDtypeStruct((t, D_MODEL), BF16)],
        scratch_shapes=[pltpu.VMEM((TOP_K, tm, D_MODEL), F32), pltpu.SemaphoreType.DMA(())],
        compiler_params=_cparams(1),
        name="moe_combine",
    )(dest_flat, x32, wts, ys, w_sg, w_su, w_sd, g, b)


def _rot_half_cols(w):
    half = w.shape[-1] // 2
    return jnp.concatenate([-w[..., half:], w[..., :half]], axis=-1)


def _pad_cols(w, width):
    return jnp.pad(w, [(0, 0)] * (w.ndim - 1) + [(0, width - w.shape[-1])])


def _rope_tables(seq):
    pos = jnp.arange(seq, dtype=F32)[:, None]

    def table(dim):
        half = dim // 2
        inv_freq = ROPE_THETA ** (-jnp.arange(half, dtype=F32) / half)
        ang = pos * inv_freq[None, :]
        return jnp.cos(ang), jnp.sin(ang)

    c64, s64 = table(MLA_ROPE_DIM)
    cos64 = _pad_cols(jnp.concatenate([c64, c64], axis=1), LANES)
    sin64 = _pad_cols(jnp.concatenate([s64, s64], axis=1), LANES)
    c128, s128 = table(MOBA_HEAD_DIM)
    cos128 = jnp.concatenate([c128, c128], axis=1)
    sin128 = jnp.concatenate([-s128, s128], axis=1)
    return cos64, sin64, cos128, sin128


def _layer(x32, x16, mem16, tables, avg, bsz, seq, mem_len, p, expert_weights, layer):
    (w_in, g_qa, w_q_up, g_kva, w_kv_up, sgu_ln_g, sgu_ln_b, w_spatial, b_spatial, w_mem_kv,
     w_br_a, w_br_b, w_br_c, w_br_m, w_o, ln1_g, ln1_b, w_router, router_bias,
     w_sh_gate, w_sh_up, w_sh_down, ln2_g, ln2_b) = p
    cos64, sin64, cos128, sin128 = tables
    t = x32.shape[0]
    row = lambda v: v.reshape(1, -1).astype(F32)

    c0 = MLA_Q_RANK + MLA_KV_RANK
    c1 = c0 + MLA_ROPE_DIM
    c2 = c1 + 2 * SGU_WIDTH
    c3 = c2 + 3 * MOBA_WIDTH
    c4 = c3 + MEM_WIDTH
    w_a = w_in[:, :c0].astype(BF16)
    w_kpe_raw = w_in[:, c0:c1]
    w_kpe = jnp.concatenate([_pad_cols(w_kpe_raw, LANES), _pad_cols(_rot_half_cols(w_kpe_raw), LANES)],
                            axis=1).astype(BF16)
    w_z = w_in[:, c1:c2].astype(BF16)
    w_c = w_in[:, c2:c3].astype(BF16)
    w_mq = w_in[:, c3:c4].astype(BF16)
    w_g = w_in[:, c4:].astype(BF16)

    w_qn = w_q_up[:, :, :MLA_NOPE_DIM].reshape(MLA_Q_RANK, -1).astype(BF16)
    w_qpe = w_q_up[:, :, MLA_NOPE_DIM:]
    w_qp = _pad_cols(w_qpe, LANES).reshape(MLA_Q_RANK, -1).astype(BF16)
    w_qpr = _pad_cols(_rot_half_cols(w_qpe), LANES).reshape(MLA_Q_RANK, -1).astype(BF16)
    w_kn = w_kv_up[:, :, :MLA_NOPE_DIM].reshape(MLA_KV_RANK, -1).astype(BF16)
    w_v = w_kv_up[:, :, MLA_NOPE_DIM:].reshape(MLA_KV_RANK, -1).astype(BF16)
    q_a, k_a, v_a = _mla_proj(x16, w_a, w_kpe, row(g_qa), row(g_kva), w_qn, w_qp, w_qpr, w_kn, w_v,
                              cos64, sin64, seq)
    o_a = _mla_attention(q_a, k_a, v_a, bsz, seq)

    bias_full = jnp.repeat(b_spatial.T.astype(F32), SGU_GROUP_DIM, axis=1)
    o_b = _sgu(x16, w_z, row(sgu_ln_g), row(sgu_ln_b), w_spatial.astype(F32), bias_full)

    qkv_c = _moba_proj(x16, w_c, cos128, sin128, seq)
    o_c = _moba_attention(qkv_c, avg, bsz, seq)

    kv_mem = _matmul(mem16, w_mem_kv.astype(BF16), mem_len, 2 * MEM_WIDTH, BF16, "mem_kv")
    o_m = _mem_attention(x16, w_mq, kv_mem, bsz, seq, mem_len)

    y = _gated_sum(x16, o_a, o_b, o_c, o_m, w_g, w_br_a.astype(BF16), w_br_b.astype(BF16),
                   w_br_c.astype(BF16), w_br_m.astype(BF16))
    x32, x16 = _proj_norm(y, w_o.astype(BF16), x32, row(ln1_g), row(ln1_b))

    e_idx, wts, rank, counts = _router(x32, _pad_cols(w_router.astype(F32), LANES),
                                       _pad_cols(row(router_bias), LANES))
    bm = MOE_ROW_BLOCK
    n_blk = (t * TOP_K) // bm + N_EXPERTS
    counts = counts[0, :N_EXPERTS].astype(I32)
    padded = (counts + bm - 1) // bm * bm
    pad_ends = jnp.cumsum(padded)
    pad_starts = pad_ends - padded
    dest = _pad_cols((pad_starts[e_idx[:, :TOP_K]] + rank[:, :TOP_K]).astype(I32), SLOT_STRIDE).reshape(-1)
    blk_first = jnp.arange(n_blk, dtype=I32)[:, None] * bm
    blk_e = jnp.minimum(jnp.sum((pad_ends[None, :] <= blk_first).astype(I32), axis=1), N_EXPERTS - 1)
    n_used = (pad_ends[-1:] // bm).astype(I32)
    xs = _dispatch(x32, dest, pad_starts + counts, padded - counts, n_used, n_blk * bm)
    ys = _expert_ffn(xs, blk_e, n_used, *expert_weights, layer)
    return _combine(x32, dest, wts, ys, w_sh_gate.astype(BF16), w_sh_up.astype(BF16),
                    w_sh_down.astype(BF16), row(ln2_g), row(ln2_b))


def kernel(x, mem, w_in, g_qa, w_q_up, g_kva, w_kv_up, sgu_ln_g, sgu_ln_b, w_spatial, b_spatial, w_mem_kv, w_br_a, w_br_b, w_br_c, w_br_m, w_o, ln1_g, ln1_b, w_router, router_bias, w_exp_gate, w_exp_up, w_exp_down, w_sh_gate, w_sh_up, w_sh_down, ln2_g, ln2_b):
    bsz, seq, d = x.shape
    mem_len = mem.shape[1]
    params = (w_in, g_qa, w_q_up, g_kva, w_kv_up, sgu_ln_g, sgu_ln_b, w_spatial, b_spatial, w_mem_kv,
              w_br_a, w_br_b, w_br_c, w_br_m, w_o, ln1_g, ln1_b, w_router, router_bias,
              w_sh_gate, w_sh_up, w_sh_down, ln2_g, ln2_b)
    expert_weights = (w_exp_gate, w_exp_up, w_exp_down)
    tables = _rope_tables(seq)
    n_blocks = seq // MOBA_BLOCK
    blk_of_pos = jnp.arange(seq, dtype=I32)[None, :] // MOBA_BLOCK
    avg = jnp.where(blk_of_pos == jnp.arange(LANES, dtype=I32)[:, None], 1.0 / MOBA_BLOCK, 0.0).astype(BF16)
    assert n_blocks <= LANES
    x32 = x.reshape(bsz * seq, d)
    x16 = x32.astype(BF16)
    mem16 = mem.reshape(bsz * mem_len, d).astype(BF16)
    for l in range(DEPTH):
        x32, x16 = _layer(x32, x16, mem16, tables, avg, bsz, seq, mem_len, tuple(w[l] for w in params),
                          expert_weights, l)
    return x32.reshape(bsz, seq, d)
```

```python
import functools

import jax
import jax.numpy as jnp
from jax import lax
from jax.experimental import pallas as pl
from jax.experimental.pallas import tpu as pltpu

F32 = jnp.float32
BF16 = jnp.bfloat16
I32 = jnp.int32

D_MODEL = 2048
DEPTH = 2
MLA_HEADS = 8
MLA_Q_RANK = 512
MLA_KV_RANK = 256
MLA_NOPE_DIM = 128
MLA_ROPE_DIM = 64
MLA_V_DIM = 128
SGU_GROUPS = 8
SGU_GROUP_DIM = 128
SGU_WIDTH = SGU_GROUPS * SGU_GROUP_DIM
SGU_CHUNK = 128
MOBA_HEADS = 8
MOBA_HEAD_DIM = 128
MOBA_WIDTH = MOBA_HEADS * MOBA_HEAD_DIM
MOBA_BLOCK = 256
MOBA_TOPK = 3
MEM_HEADS = 4
MEM_HEAD_DIM = 128
MEM_WIDTH = MEM_HEADS * MEM_HEAD_DIM
N_BRANCH = 4
N_EXPERTS = 64
TOP_K = 6
D_EXPERT = 512
ROUTED_SCALE = 2.5
ROPE_THETA = 10000.0
DEEPNORM_ALPHA = (2 * DEPTH) ** 0.25

LANES = 128
SUBLANES = 8
ATTN_BLOCK = 256
MOE_ROW_BLOCK = 256
SLOT_STRIDE = 8
NEG_BIG = -1e30
VMEM_LIMIT = 56 * 1024 * 1024


def _cparams(n_axes):
    return pltpu.CompilerParams(dimension_semantics=("arbitrary",) * n_axes,
                                vmem_limit_bytes=VMEM_LIMIT)


def _dot(a, b):
    return jnp.dot(a, b, preferred_element_type=F32)


def _dot_nt(a, b):
    return lax.dot_general(a, b, (((1,), (1,)), ((), ())), preferred_element_type=F32)


def _layer_norm_rows(v, g, b, eps=1e-5):
    mu = jnp.mean(v, axis=-1, keepdims=True)
    vc = v - mu
    var = jnp.mean(vc * vc, axis=-1, keepdims=True)
    return vc * lax.rsqrt(var + eps) * g + b


def _rms_norm_rows(v, g, eps=1e-6):
    return v * lax.rsqrt(jnp.mean(v * v, axis=-1, keepdims=True) + eps) * g


def _mm_kernel(x_ref, w_ref, o_ref):
    o_ref[...] = _dot(x_ref[...], w_ref[...]).astype(o_ref.dtype)


def _matmul(x, w, tm, tn, out_dtype, name):
    m, k = x.shape
    n = w.shape[1]
    return pl.pallas_call(
        _mm_kernel,
        grid=(n // tn, m // tm),
        in_specs=[pl.BlockSpec((tm, k), lambda j, i: (i, 0)),
                  pl.BlockSpec((k, tn), lambda j, i: (0, j))],
        out_specs=pl.BlockSpec((tm, tn), lambda j, i: (i, j)),
        out_shape=jax.ShapeDtypeStruct((m, n), out_dtype),
        compiler_params=_cparams(2),
        name=name,
    )(x, w)


def _mla_proj_kernel(x_ref, wa_ref, wkpe_ref, gq_ref, gkv_ref, wqnt_ref, wqpt_ref, wqprt_ref,
                     wkn_ref, wvt_ref, cos_ref, sin_ref, cost_ref, sint_ref, qt_ref, k_ref, vt_ref, *, scale):
    xb = x_ref[...]
    a = _dot(xb, wa_ref[...])
    cq = _rms_norm_rows(a[:, :MLA_Q_RANK], gq_ref[...]).astype(BF16)
    ckv = _rms_norm_rows(a[:, MLA_Q_RANK:], gkv_ref[...]).astype(BF16)
    qn_t = _dot_nt(wqnt_ref[...], cq)
    qp_t = _dot_nt(wqpt_ref[...], cq)
    qpr_t = _dot_nt(wqprt_ref[...], cq)
    v_t = _dot_nt(wvt_ref[...], ckv)
    cos_t = cost_ref[...]
    sin_t = sint_ref[...]
    for h in range(MLA_HEADS):
        lo, hi = h * LANES, (h + 1) * LANES
        qt_ref[0, h, 0, 0:LANES, :] = (qn_t[lo:hi] * scale).astype(qt_ref.dtype)
        qt_ref[0, h, 0, LANES:2 * LANES, :] = ((qp_t[lo:hi] * cos_t + qpr_t[lo:hi] * sin_t) * scale).astype(qt_ref.dtype)
        vt_ref[0, h, 0] = v_t[h * MLA_V_DIM:(h + 1) * MLA_V_DIM, :].astype(vt_ref.dtype)
    kn = _dot(ckv, wkn_ref[...])
    kp = _dot(xb, wkpe_ref[...])
    kpe = (kp[:, :LANES] * cos_ref[...] + kp[:, LANES:] * sin_ref[...]).astype(k_ref.dtype)
    for h in range(MLA_HEADS):
        lo, hi = h * LANES, (h + 1) * LANES
        k_ref[:, 2 * lo:2 * lo + LANES] = kn[:, lo:hi].astype(k_ref.dtype)
        k_ref[:, 2 * lo + LANES:2 * hi] = kpe


def _mla_proj(x, wa, wkpe, gq, gkv, wqnt, wqpt, wqprt, wkn, wvt, cos64, sin64, seq):
    t = x.shape[0]
    tm = ATTN_BLOCK
    full = lambda arr: pl.BlockSpec(arr.shape, lambda i: (0,) * arr.ndim)
    n_pos = seq // tm
    scale = (MLA_NOPE_DIM + MLA_ROPE_DIM) ** -0.5
    dk = 2 * LANES
    feat_major = lambda d: pl.BlockSpec((1, MLA_HEADS, 1, d, tm), lambda i: (i // n_pos, 0, i % n_pos, 0, 0))
    return pl.pallas_call(
        functools.partial(_mla_proj_kernel, scale=scale),
        grid=(t // tm,),
        in_specs=[pl.BlockSpec((tm, D_MODEL), lambda i: (i, 0)),
                  full(wa), full(wkpe), full(gq), full(gkv), full(wqnt), full(wqpt), full(wqprt),
                  full(wkn), full(wvt),
                  pl.BlockSpec((tm, LANES), lambda i: (i % n_pos, 0)),
                  pl.BlockSpec((tm, LANES), lambda i: (i % n_pos, 0)),
                  pl.BlockSpec((LANES, tm), lambda i: (0, i % n_pos)),
                  pl.BlockSpec((LANES, tm), lambda i: (0, i % n_pos))],
        out_specs=[feat_major(dk),
                   pl.BlockSpec((tm, MLA_HEADS * dk), lambda i: (i, 0)),
                   feat_major(MLA_V_DIM)],
        out_shape=[jax.ShapeDtypeStruct((t // seq, MLA_HEADS, n_pos, dk, tm), BF16),
                   jax.ShapeDtypeStruct((t, MLA_HEADS * dk), BF16),
                   jax.ShapeDtypeStruct((t // seq, MLA_HEADS, n_pos, MLA_V_DIM, tm), BF16)],
        compiler_params=_cparams(1),
        name="mla_proj",
    )(x, wa, wkpe, gq, gkv, wqnt, wqpt, wqprt, wkn, wvt, cos64, sin64, cos64.T, sin64.T)


def _softmax_steps(q_ts, ks, v_ts, carries, masks):
    heads = range(len(q_ts))
    s_ts = [_dot(ks[h], q_ts[h]) for h in heads]
    s_ts = [s_ts[h] if masks[h] is None else jnp.where(masks[h], s_ts[h], NEG_BIG) for h in heads]
    m_news = [jnp.maximum(carries[h][0], jnp.max(s_ts[h], axis=0, keepdims=True)) for h in heads]
    p_ts = [jnp.exp(s_ts[h] - m_news[h]) for h in heads]
    pvs = [_dot(v_ts[h], p_ts[h].astype(v_ts[h].dtype)) for h in heads]
    out = []
    for h in heads:
        m_i, l_i, acc_t = carries[h]
        alpha = jnp.exp(m_i - m_news[h])
        l_new = alpha * l_i + jnp.sum(p_ts[h], axis=0, keepdims=True)
        out.append((m_news[h], l_new, alpha * acc_t + pvs[h]))
    return tuple(out)


def _softmax_init(n_heads, tq, dv):
    return tuple((jnp.full((1, tq), NEG_BIG, F32), jnp.zeros((1, tq), F32), jnp.zeros((dv, tq), F32))
                 for _ in range(n_heads))


def _softmax_finish(o_ref, carries, dv):
    for h, (_, l_i, acc_t) in enumerate(carries):
        o_ref[:, h * dv:(h + 1) * dv] = (acc_t / l_i).T.astype(o_ref.dtype)


def _causal_attn_kernel(qt_ref, k_ref, vt_ref, o_ref, *, tq, n_heads, dk, dv):
    qi = pl.program_id(2)
    key = lax.broadcasted_iota(I32, (tq, tq), 0)
    qry = lax.broadcasted_iota(I32, (tq, tq), 1)
    qs = [qt_ref[0, h, 0] for h in range(n_heads)]

    def step(kb, carries, mask):
        ks = pl.multiple_of(kb * tq, tq)
        return _softmax_steps(qs, [k_ref[pl.ds(ks, tq), h * dk:(h + 1) * dk] for h in range(n_heads)],
                              [vt_ref[0, h, kb] for h in range(n_heads)], carries, [mask] * n_heads)

    carries = step(qi, _softmax_init(n_heads, tq, dv), key <= qry)
    carries = lax.fori_loop(0, qi, lambda kb, c: step(kb, c, None), carries)
    _softmax_finish(o_ref, carries, dv)


def _mla_attention(q_t, k_a, v_t, bsz, seq, n_heads=8):
    t = k_a.shape[0]
    tq = ATTN_BLOCK
    nq = seq // tq
    dk = 2 * LANES
    dv = MLA_V_DIM
    return pl.pallas_call(
        functools.partial(_causal_attn_kernel, tq=tq, n_heads=n_heads, dk=dk, dv=dv),
        grid=(bsz, MLA_HEADS // n_heads, nq),
        in_specs=[pl.BlockSpec((1, n_heads, 1, dk, tq), lambda b, h, i: (b, h, i, 0, 0)),
                  pl.BlockSpec((seq, n_heads * dk), lambda b, h, i: (b, h)),
                  pl.BlockSpec((1, n_heads, nq, dv, tq), lambda b, h, i: (b, h, 0, 0, 0))],
        out_specs=pl.BlockSpec((tq, n_heads * dv), lambda b, h, i: (b * nq + i, h)),
        out_shape=jax.ShapeDtypeStruct((t, MLA_HEADS * dv), BF16),
        compiler_params=_cparams(3),
        name="mla_attn",
    )(q_t, k_a, v_t)


def _sgu_kernel(x_ref, wz_ref, g_ref, b_ref, ws_ref, bs_ref, o_ref, *, tm):
    z = jax.nn.gelu(_dot(x_ref[...], wz_ref[...]))
    u = z[:, :SGU_WIDTH]
    v = _layer_norm_rows(z[:, SGU_WIDTH:], g_ref[...], b_ref[...]).astype(BF16)
    row = lax.broadcasted_iota(I32, (SGU_CHUNK, SGU_CHUNK), 0)
    col = lax.broadcasted_iota(I32, (SGU_CHUNK, SGU_CHUNK), 1)
    bias = bs_ref[...]
    for g in range(SGU_GROUPS):
        w = jnp.where(col <= row, ws_ref[g], 0.0).astype(BF16)
        lo, hi = g * SGU_GROUP_DIM, (g + 1) * SGU_GROUP_DIM
        for c in range(tm // SGU_CHUNK):
            r0, r1 = c * SGU_CHUNK, (c + 1) * SGU_CHUNK
            mixed = _dot(w, v[r0:r1, lo:hi]) + bias[:, lo:hi]
            o_ref[r0:r1, lo:hi] = (u[r0:r1, lo:hi] * mixed).astype(o_ref.dtype)


def _sgu(x, wz, ln_g, ln_b, w_s, bias_full, tm=256):
    t = x.shape[0]
    full = lambda arr: pl.BlockSpec(arr.shape, lambda i: (0,) * arr.ndim)
    return pl.pallas_call(
        functools.partial(_sgu_kernel, tm=tm),
        grid=(t // tm,),
        in_specs=[pl.BlockSpec((tm, D_MODEL), lambda i: (i, 0)),
                  full(wz), full(ln_g), full(ln_b), full(w_s), full(bias_full)],
        out_specs=pl.BlockSpec((tm, SGU_WIDTH), lambda i: (i, 0)),
        out_shape=jax.ShapeDtypeStruct((t, SGU_WIDTH), BF16),
        compiler_params=_cparams(1),
        name="sgu",
    )(x, wz, ln_g, ln_b, w_s, bias_full)


def _moba_k_proj_kernel(x_ref, w_ref, cos_ref, sin_ref, o_ref):
    acc = _dot(x_ref[...], w_ref[...])
    cos = cos_ref[...]
    sin = sin_ref[...]
    for h in range(MOBA_HEADS):
        seg = acc[:, h * LANES:(h + 1) * LANES]
        rot = pltpu.roll(seg, MOBA_HEAD_DIM // 2, axis=1)
        o_ref[:, h * LANES:(h + 1) * LANES] = (seg * cos + rot * sin).astype(o_ref.dtype)


def _moba_k_proj(x, w_k, cos128, sin128, seq, tm=512):
    t = x.shape[0]
    n_pos = seq // tm
    return pl.pallas_call(
        _moba_k_proj_kernel,
        grid=(t // tm,),
        in_specs=[pl.BlockSpec((tm, D_MODEL), lambda i: (i, 0)),
                  pl.BlockSpec(w_k.shape, lambda i: (0, 0)),
                  pl.BlockSpec((tm, LANES), lambda i: (i % n_pos, 0)),
                  pl.BlockSpec((tm, LANES), lambda i: (i % n_pos, 0))],
        out_specs=pl.BlockSpec((tm, MOBA_WIDTH), lambda i: (i, 0)),
        out_shape=jax.ShapeDtypeStruct((t, MOBA_WIDTH), BF16),
        compiler_params=_cparams(1),
        name="moba_k_proj",
    )(x, w_k, cos128, sin128)


def _moba_qv_proj_kernel(x_ref, wt_ref, cost_ref, sint_ref, o_ref, *, scale):
    y_t = _dot_nt(wt_ref[...], x_ref[...])
    hd = MOBA_HEAD_DIM

    @pl.when(pl.program_id(0) == 0)
    def _():
        cos_t = cost_ref[...]
        sin_t = sint_ref[...]
        for h in range(MOBA_HEADS):
            seg = y_t[h * hd:(h + 1) * hd]
            rot = jnp.concatenate([seg[hd // 2:], seg[:hd // 2]], axis=0)
            o_ref[0, h, 0] = ((seg * cos_t + rot * sin_t) * scale).astype(o_ref.dtype)

    @pl.when(pl.program_id(0) == 1)
    def _():
        for h in range(MOBA_HEADS):
            o_ref[0, h, 0] = y_t[h * hd:(h + 1) * hd].astype(o_ref.dtype)


def _moba_qv_proj(x, w_qv_t, cos128_t, sin128_t, seq):
    t = x.shape[0]
    tm = ATTN_BLOCK
    n_pos = seq // tm
    return pl.pallas_call(
        functools.partial(_moba_qv_proj_kernel, scale=MOBA_HEAD_DIM ** -0.5),
        grid=(2, t // tm),
        in_specs=[pl.BlockSpec((tm, D_MODEL), lambda j, i: (i, 0)),
                  pl.BlockSpec((MOBA_WIDTH, D_MODEL), lambda j, i: (j, 0)),
                  pl.BlockSpec((MOBA_HEAD_DIM, tm), lambda j, i: (0, i % n_pos)),
                  pl.BlockSpec((MOBA_HEAD_DIM, tm), lambda j, i: (0, i % n_pos))],
        out_specs=pl.BlockSpec((1, MOBA_HEADS, 1, MOBA_HEAD_DIM, tm),
                               lambda j, i: (i // n_pos, j, i % n_pos, 0, 0)),
        out_shape=jax.ShapeDtypeStruct((t // seq, 2 * MOBA_HEADS, n_pos, MOBA_HEAD_DIM, tm), BF16),
        compiler_params=_cparams(2),
        name="moba_qv_proj",
    )(x, w_qv_t, cos128_t, sin128_t)


def _moba_attn_kernel(qt_ref, k_ref, vt_ref, avg_ref, o_ref, kmean_ref, *, n_heads):
    j = pl.program_id(2)
    blk = MOBA_BLOCK
    hd = MOBA_HEAD_DIM

    @pl.when(j == 0)
    def _():
        kmean_ref[...] = _dot(avg_ref[...], k_ref[...])

    blk_id = lax.broadcasted_iota(I32, (SUBLANES, blk), 0)
    n_blocks = k_ref.shape[0] // blk
    qs, sels = [], []
    for h in range(n_heads):
        q = qt_ref[0, h, 0]
        gate = jnp.dot(kmean_ref[0:SUBLANES, h * hd:(h + 1) * hd], q.astype(F32),
                       precision=lax.Precision.HIGHEST, preferred_element_type=F32)
        rank = jnp.zeros((SUBLANES, blk), I32)
        for m in range(n_blocks):
            gm = gate[m:m + 1, :]
            beats = (gm > gate) | ((gm == gate) & (m < blk_id))
            rank = rank + jnp.where(beats & (m < j), 1, 0)
        qs.append(q)
        sels.append(jnp.where((rank < MOBA_TOPK) & (blk_id < j), 1.0, 0.0))

    def step(kb, carries, masks):
        ks = pl.multiple_of(kb * blk, blk)
        return _softmax_steps(qs, [k_ref[pl.ds(ks, blk), h * hd:(h + 1) * hd] for h in range(n_heads)],
                              [vt_ref[0, h, kb] for h in range(n_heads)], carries, masks)

    key = lax.broadcasted_iota(I32, (blk, blk), 0)
    qry = lax.broadcasted_iota(I32, (blk, blk), 1)
    carries = step(j, _softmax_init(n_heads, blk, hd), [key <= qry] * n_heads)

    def body(n, c):
        masks = [jnp.max(jnp.where(blk_id == n, sels[h], 0.0), axis=0, keepdims=True) > 0.5
                 for h in range(n_heads)]
        return step(n, c, masks)

    carries = lax.fori_loop(0, j, body, carries)
    _softmax_finish(o_ref, carries, hd)


def _moba_attention(qv_t, k, avg, bsz, seq, n_heads=8):
    t = k.shape[0]
    nq = seq // MOBA_BLOCK
    assert nq <= SUBLANES and MOBA_BLOCK == ATTN_BLOCK
    w = n_heads * MOBA_HEAD_DIM
    groups = MOBA_HEADS // n_heads
    return pl.pallas_call(
        functools.partial(_moba_attn_kernel, n_heads=n_heads),
        grid=(bsz, groups, nq),
        in_specs=[pl.BlockSpec((1, n_heads, 1, MOBA_HEAD_DIM, MOBA_BLOCK), lambda b, h, i: (b, h, i, 0, 0)),
                  pl.BlockSpec((seq, w), lambda b, h, i: (b, h)),
                  pl.BlockSpec((1, n_heads, nq, MOBA_HEAD_DIM, MOBA_BLOCK),
                               lambda b, h, i: (b, groups + h, 0, 0, 0)),
                  pl.BlockSpec(avg.shape, lambda b, h, i: (0, 0))],
        out_specs=pl.BlockSpec((MOBA_BLOCK, w), lambda b, h, i: (b * nq + i, h)),
        out_shape=jax.ShapeDtypeStruct((t, MOBA_WIDTH), BF16),
        scratch_shapes=[pltpu.VMEM((LANES, w), F32)],
        compiler_params=_cparams(3),
        name="moba_attn",
    )(qv_t, k, qv_t, avg)


def _mem_attn_kernel(x_ref, wq_ref, kv_ref, o_ref, *, scale):
    q = (_dot(x_ref[...], wq_ref[...]) * scale).astype(BF16)
    for h in range(MEM_HEADS):
        lo, hi = h * MEM_HEAD_DIM, (h + 1) * MEM_HEAD_DIM
        s = _dot_nt(q[:, lo:hi], kv_ref[:, lo:hi])
        p = jnp.exp(s - jnp.max(s, axis=1, keepdims=True))
        o = _dot(p.astype(BF16), kv_ref[:, MEM_WIDTH + lo:MEM_WIDTH + hi])
        o_ref[:, lo:hi] = (o / jnp.sum(p, axis=1, keepdims=True)).astype(o_ref.dtype)


def _mem_attention(x, w_mq, kv_mem, bsz, seq, mem_len, tm=512):
    t = x.shape[0]
    ns = seq // tm
    return pl.pallas_call(
        functools.partial(_mem_attn_kernel, scale=MEM_HEAD_DIM ** -0.5),
        grid=(bsz, ns),
        in_specs=[pl.BlockSpec((tm, D_MODEL), lambda b, i: (b * ns + i, 0)),
                  pl.BlockSpec(w_mq.shape, lambda b, i: (0, 0)),
                  pl.BlockSpec((mem_len, 2 * MEM_WIDTH), lambda b, i: (b, 0))],
        out_specs=pl.BlockSpec((tm, MEM_WIDTH), lambda b, i: (b * ns + i, 0)),
        out_shape=jax.ShapeDtypeStruct((t, MEM_WIDTH), BF16),
        compiler_params=_cparams(2),
        name="mem_attn",
    )(x, w_mq, kv_mem)


def _gated_sum_kernel(x_ref, oa_ref, ob_ref, oc_ref, om_ref, g0_ref, g1_ref, g2_ref, g3_ref,
                      wa_ref, wb_ref, wc_ref, wm_ref, y_ref):
    xb = x_ref[...]
    acc = None
    for o_ref, g_ref, w_ref in ((oa_ref, g0_ref, wa_ref), (ob_ref, g1_ref, wb_ref),
                                (oc_ref, g2_ref, wc_ref), (om_ref, g3_ref, wm_ref)):
        term = jax.nn.sigmoid(_dot(xb, g_ref[...])) * _dot(o_ref[...], w_ref[...])
        acc = term if acc is None else acc + term
    y_ref[...] = acc.astype(y_ref.dtype)


def _gated_sum(x, o_a, o_b, o_c, o_m, w_gate, w_br_a, w_br_b, w_br_c, w_br_m, tm=512, tn=256):
    t = x.shape[0]
    nj = D_MODEL // tn
    rows = lambda width: pl.BlockSpec((tm, width), lambda j, i: (i, 0))
    gate = lambda b: pl.BlockSpec((D_MODEL, tn), lambda j, i: (0, b * nj + j))
    cols = lambda width: pl.BlockSpec((width, tn), lambda j, i: (0, j))
    return pl.pallas_call(
        _gated_sum_kernel,
        grid=(nj, t // tm),
        in_specs=[rows(D_MODEL), rows(o_a.shape[1]), rows(o_b.shape[1]), rows(o_c.shape[1]), rows(o_m.shape[1]),
                  gate(0), gate(1), gate(2), gate(3),
                  cols(w_br_a.shape[0]), cols(w_br_b.shape[0]), cols(w_br_c.shape[0]), cols(w_br_m.shape[0])],
        out_specs=pl.BlockSpec((tm, tn), lambda j, i: (i, j)),
        out_shape=jax.ShapeDtypeStruct((t, D_MODEL), BF16),
        compiler_params=_cparams(2),
        name="gated_sum",
    )(x, o_a, o_b, o_c, o_m, w_gate, w_gate, w_gate, w_gate, w_br_a, w_br_b, w_br_c, w_br_m)


def _proj_norm_kernel(y_ref, w_ref, res_ref, g_ref, b_ref, o32_ref, o16_ref):
    v = DEEPNORM_ALPHA * res_ref[...] + _dot(y_ref[...], w_ref[...])
    out = _layer_norm_rows(v, g_ref[...], b_ref[...])
    o32_ref[...] = out
    o16_ref[...] = out.astype(o16_ref.dtype)


def _proj_norm(y, w_o, res, g, b, tm=256):
    t = y.shape[0]
    full = lambda arr: pl.BlockSpec(arr.shape, lambda i: (0,) * arr.ndim)
    rows = pl.BlockSpec((tm, D_MODEL), lambda i: (i, 0))
    return pl.pallas_call(
        _proj_norm_kernel,
        grid=(t // tm,),
        in_specs=[rows, full(w_o), rows, full(g), full(b)],
        out_specs=[rows, rows],
        out_shape=[jax.ShapeDtypeStruct((t, D_MODEL), F32), jax.ShapeDtypeStruct((t, D_MODEL), BF16)],
        compiler_params=_cparams(1),
        name="proj_norm",
    )(y, w_o, res, g, b)


def _router_kernel(x_ref, w_ref, bias_ref, e_ref, wt_ref, rank_ref, cnt_ref, run_ref, *, tm):
    i = pl.program_id(0)

    @pl.when(i == 0)
    def _():
        run_ref[...] = jnp.zeros_like(run_ref)

    logits = jnp.dot(x_ref[...], w_ref[...], precision=lax.Precision.HIGHEST, preferred_element_type=F32)
    scores = jax.nn.sigmoid(logits)
    lane = lax.broadcasted_iota(I32, (tm, LANES), 1)
    biased = jnp.where(lane < N_EXPERTS, scores + bias_ref[...], NEG_BIG)
    picks = []
    chosen = jnp.zeros((tm, LANES), jnp.bool_)
    for _ in range(TOP_K):
        mx = jnp.max(biased, axis=1, keepdims=True)
        idx = jnp.min(jnp.where(biased == mx, lane, LANES), axis=1, keepdims=True)
        hit = lane == idx
        picks.append((idx, hit))
        chosen = chosen | hit
        biased = jnp.where(hit, 2 * NEG_BIG, biased)
    picked_scores = jnp.where(chosen, scores, 0.0)
    norm = ROUTED_SCALE / jnp.sum(picked_scores, axis=1, keepdims=True)

    r = lax.broadcasted_iota(I32, (tm, tm), 0)
    c = lax.broadcasted_iota(I32, (tm, tm), 1)
    strict_lower = jnp.where(c < r, 1.0, 0.0).astype(BF16)
    chosen_f = jnp.where(chosen, 1.0, 0.0)
    arrival = run_ref[0:1, :] + _dot(strict_lower, chosen_f.astype(BF16))
    run_ref[...] = run_ref[...] + jnp.sum(chosen_f, axis=0, keepdims=True)
    cnt_ref[...] = run_ref[...]

    e_out = jnp.zeros((tm, LANES), I32)
    w_out = jnp.zeros((tm, LANES), F32)
    r_out = jnp.zeros((tm, LANES), F32)
    for slot, (idx, hit) in enumerate(picks):
        here = lane == slot
        e_out = jnp.where(here, idx, e_out)
        w_out = jnp.where(here, jnp.sum(jnp.where(hit, scores, 0.0), axis=1, keepdims=True) * norm, w_out)
        r_out = jnp.where(here, jnp.sum(jnp.where(hit, arrival, 0.0), axis=1, keepdims=True), r_out)
    e_ref[...] = e_out
    wt_ref[...] = w_out
    rank_ref[...] = r_out.astype(I32)


def _router(x32, w_router_pad, bias_pad, tm=256):
    t = x32.shape[0]
    full = lambda arr: pl.BlockSpec(arr.shape, lambda i: (0,) * arr.ndim)
    rows = pl.BlockSpec((tm, LANES), lambda i: (i, 0))
    return pl.pallas_call(
        functools.partial(_router_kernel, tm=tm),
        grid=(t // tm,),
        in_specs=[pl.BlockSpec((tm, D_MODEL), lambda i: (i, 0)), full(w_router_pad), full(bias_pad)],
        out_specs=[rows, rows, rows, pl.BlockSpec((8, LANES), lambda i: (0, 0))],
        out_shape=[jax.ShapeDtypeStruct((t, LANES), I32), jax.ShapeDtypeStruct((t, LANES), F32),
                   jax.ShapeDtypeStruct((t, LANES), I32), jax.ShapeDtypeStruct((8, LANES), F32)],
        scratch_shapes=[pltpu.VMEM((8, LANES), F32)],
        compiler_params=_cparams(1),
        name="moe_router",
    )(x32, w_router_pad, bias_pad)


def _dispatch_kernel(fill_start_ref, fill_n_ref, n_used_ref, dest_ref, x_ref, out_hbm, zero_ref, sem, fill_sem,
                     *, tm):
    @pl.when(pl.program_id(0) == 0)
    def _():
        zero_ref[...] = jnp.zeros_like(zero_ref)
        bm = zero_ref.shape[0]
        n_blk = out_hbm.shape[0] // bm

        def fill_copy(dst_row):
            return pltpu.make_async_copy(zero_ref.at[pl.ds(0, 1), :], out_hbm.at[pl.ds(dst_row, 1), :], fill_sem)

        def tail_copy(blk):
            return pltpu.make_async_copy(zero_ref, out_hbm.at[pl.ds(pl.multiple_of(blk * bm, bm), bm), :], fill_sem)

        def start_expert(e, _):
            first = fill_start_ref[e]
            lax.fori_loop(0, fill_n_ref[e], lambda r, c: (fill_copy(first + r).start(), c)[1], 0)
            return 0

        def wait_expert(e, _):
            lax.fori_loop(0, fill_n_ref[e], lambda r, c: (fill_copy(0).wait(), c)[1], 0)
            return 0

        lax.fori_loop(0, N_EXPERTS, start_expert, 0)
        lax.fori_loop(n_used_ref[0], n_blk, lambda b, c: (tail_copy(b).start(), c)[1], 0)
        lax.fori_loop(0, N_EXPERTS, wait_expert, 0)
        lax.fori_loop(n_used_ref[0], n_blk, lambda b, c: (tail_copy(0).wait(), c)[1], 0)

    def issue(tok, _):
        for k in range(TOP_K):
            pltpu.make_async_copy(x_ref.at[pl.ds(tok, 1), :],
                                  out_hbm.at[pl.ds(dest_ref[tok * SLOT_STRIDE + k], 1), :], sem).start()
        return 0

    lax.fori_loop(0, tm, issue, 0)
    for _ in range(TOP_K):
        pltpu.make_async_copy(x_ref, out_hbm.at[pl.ds(0, tm), :], sem).wait()


def _dispatch(x32, dest_flat, fill_start, fill_n, n_used, n_rows, tm=256):
    t = x32.shape[0]
    grid_spec = pltpu.PrefetchScalarGridSpec(
        num_scalar_prefetch=3,
        grid=(t // tm,),
        in_specs=[pl.BlockSpec((tm * SLOT_STRIDE,), lambda i, fs, fn, nu: (i,), memory_space=pltpu.SMEM),
                  pl.BlockSpec((tm, D_MODEL), lambda i, fs, fn, nu: (i, 0))],
        out_specs=pl.BlockSpec(memory_space=pl.ANY),
        scratch_shapes=[pltpu.VMEM((MOE_ROW_BLOCK, D_MODEL), F32), pltpu.SemaphoreType.DMA(()),
                        pltpu.SemaphoreType.DMA(())],
    )
    return pl.pallas_call(
        functools.partial(_dispatch_kernel, tm=tm),
        grid_spec=grid_spec,
        out_shape=jax.ShapeDtypeStruct((n_rows, D_MODEL), F32),
        compiler_params=_cparams(1),
        name="moe_dispatch",
    )(fill_start, fill_n, n_used, dest_flat, x32)


def _expert_kernel(blk_e_ref, n_used_ref, xs_ref, wg_ref, wu_ref, wd_ref, ys_ref, wg16_ref, wu16_ref, wd16_ref):
    i = pl.program_id(0)
    used = i < n_used_ref[0]
    new_expert = (i == 0) | (blk_e_ref[i] != blk_e_ref[jnp.maximum(i - 1, 0)])

    @pl.when(used & new_expert)
    def _():
        wg16_ref[...] = wg_ref[0, 0].astype(BF16)
        wu16_ref[...] = wu_ref[0, 0].astype(BF16)
        wd16_ref[...] = wd_ref[0, 0].astype(BF16)

    @pl.when(used)
    def _():
        xb = xs_ref[...].astype(BF16)
        hid = jax.nn.silu(_dot(xb, wg16_ref[...])) * _dot(xb, wu16_ref[...])
        ys_ref[...] = _dot(hid.astype(BF16), wd16_ref[...]).astype(ys_ref.dtype)

    @pl.when(jnp.logical_not(used))
    def _():
        ys_ref[...] = jnp.zeros_like(ys_ref)


def _expert_ffn(xs, blk_e, n_used, w_gate, w_up, w_down, layer):
    n_rows = xs.shape[0]
    bm = MOE_ROW_BLOCK
    grid_spec = pltpu.PrefetchScalarGridSpec(
        num_scalar_prefetch=2,
        grid=(n_rows // bm,),
        in_specs=[pl.BlockSpec((bm, D_MODEL), lambda i, be, nu: (jnp.minimum(i, nu[0] - 1), 0)),
                  pl.BlockSpec((1, 1, D_MODEL, D_EXPERT), lambda i, be, nu: (layer, be[i], 0, 0)),
                  pl.BlockSpec((1, 1, D_MODEL, D_EXPERT), lambda i, be, nu: (layer, be[i], 0, 0)),
                  pl.BlockSpec((1, 1, D_EXPERT, D_MODEL), lambda i, be, nu: (layer, be[i], 0, 0))],
        out_specs=pl.BlockSpec((bm, D_MODEL), lambda i, be, nu: (i, 0)),
        scratch_shapes=[pltpu.VMEM((D_MODEL, D_EXPERT), BF16), pltpu.VMEM((D_MODEL, D_EXPERT), BF16),
                        pltpu.VMEM((D_EXPERT, D_MODEL), BF16)],
    )
    return pl.pallas_call(
        _expert_kernel,
        grid_spec=grid_spec,
        out_shape=jax.ShapeDtypeStruct((n_rows, D_MODEL), F32),
        compiler_params=_cparams(1),
        name="moe_experts",
    )(blk_e, n_used, xs, w_gate, w_up, w_down)


def _combine_kernel(dest_ref, x_ref, wt_ref, ys_hbm, wsg_ref, wsu_ref, wsd_ref, g_ref, b_ref,
                    o32_ref, o16_ref, buf_ref, sem, *, tm):
    def issue(tok, _):
        for k in range(TOP_K):
            pltpu.make_async_copy(ys_hbm.at[pl.ds(dest_ref[tok * SLOT_STRIDE + k], 1), :],
                                  buf_ref.at[k, pl.ds(tok, 1), :], sem).start()
        return 0

    lax.fori_loop(0, tm, issue, 0)
    x = x_ref[...]
    xb = x.astype(BF16)
    hid = jax.nn.silu(_dot(xb, wsg_ref[...])) * _dot(xb, wsu_ref[...])
    total = DEEPNORM_ALPHA * x + _dot(hid.astype(BF16), wsd_ref[...])
    for k in range(TOP_K):
        pltpu.make_async_copy(ys_hbm.at[pl.ds(0, tm), :], buf_ref.at[k], sem).wait()
    wt = wt_ref[...]
    for k in range(TOP_K):
        total = total + wt[:, k:k + 1] * buf_ref[k]
    out = _layer_norm_rows(total, g_ref[...], b_ref[...])
    o32_ref[...] = out
    o16_ref[...] = out.astype(o16_ref.dtype)


def _combine(x32, dest_flat, wts, ys, w_sg, w_su, w_sd, g, b, tm=128):
    t = x32.shape[0]
    full = lambda arr: pl.BlockSpec(arr.shape, lambda i: (0,) * arr.ndim)
    rows = pl.BlockSpec((tm, D_MODEL), lambda i: (i, 0))
    return pl.pallas_call(
        functools.partial(_combine_kernel, tm=tm),
        grid=(t // tm,),
        in_specs=[pl.BlockSpec((tm * SLOT_STRIDE,), lambda i: (i,), memory_space=pltpu.SMEM),
                  rows,
                  pl.BlockSpec((tm, LANES), lambda i: (i, 0)),
                  pl.BlockSpec(memory_space=pl.ANY),
                  full(w_sg), full(w_su), full(w_sd), full(g), full(b)],
        out_specs=[rows, rows],
        out_shape=[jax.ShapeDtypeStruct((t, D_MODEL), F32), jax.ShapeDtypeStruct((t, D_MODEL), BF16)],
        scratch_shapes=[pltpu.VMEM((TOP_K, tm, D_MODEL), F32), pltpu.SemaphoreType.DMA(())],
        compiler_params=_cparams(1),
        name="moe_combine",
    )(dest_flat, x32, wts, ys, w_sg, w_su, w_sd, g, b)


def _rot_half_cols(w):
    half = w.shape[-1] // 2
    return jnp.concatenate([-w[..., half:], w[..., :half]], axis=-1)


def _pad_cols(w, width):
    return jnp.pad(w, [(0, 0)] * (w.ndim - 1) + [(0, width - w.shape[-1])])


def _rope_tables(seq):
    pos = jnp.arange(seq, dtype=F32)[:, None]

    def table(dim):
        half = dim // 2
        inv_freq = ROPE_THETA ** (-jnp.arange(half, dtype=F32) / half)
        ang = pos * inv_freq[None, :]
        return jnp.cos(ang), jnp.sin(ang)

    c64, s64 = table(MLA_ROPE_DIM)
    cos64 = _pad_cols(jnp.concatenate([c64, c64], axis=1), LANES)
    sin64 = _pad_cols(jnp.concatenate([s64, s64], axis=1), LANES)
    c128, s128 = table(MOBA_HEAD_DIM)
    cos128 = jnp.concatenate([c128, c128], axis=1)
    sin128 = jnp.concatenate([-s128, s128], axis=1)
    return cos64, sin64, cos128, sin128


def _layer(x32, x16, mem16, tables, avg, bsz, seq, mem_len, p, expert_weights, layer):
    (w_in, g_qa, w_q_up, g_kva, w_kv_up, sgu_ln_g, sgu_ln_b, w_spatial, b_spatial, w_mem_kv,
     w_br_a, w_br_b, w_br_c, w_br_m, w_o, ln1_g, ln1_b, w_router, router_bias,
     w_sh_gate, w_sh_up, w_sh_down, ln2_g, ln2_b) = p
    cos64, sin64, cos128, sin128 = tables
    t = x32.shape[0]
    row = lambda v: v.reshape(1, -1).astype(F32)

    c0 = MLA_Q_RANK + MLA_KV_RANK
    c1 = c0 + MLA_ROPE_DIM
    c2 = c1 + 2 * SGU_WIDTH
    c3 = c2 + 3 * MOBA_WIDTH
    c4 = c3 + MEM_WIDTH
    w_a = w_in[:, :c0].astype(BF16)
    w_kpe_raw = w_in[:, c0:c1]
    w_kpe = jnp.concatenate([_pad_cols(w_kpe_raw, LANES), _pad_cols(_rot_half_cols(w_kpe_raw), LANES)],
                            axis=1).astype(BF16)
    w_z = w_in[:, c1:c2].astype(BF16)
    w_c = w_in[:, c2:c3].astype(BF16)
    w_mq = w_in[:, c3:c4].astype(BF16)
    w_g = w_in[:, c4:].astype(BF16)

    w_qn = w_q_up[:, :, :MLA_NOPE_DIM].reshape(MLA_Q_RANK, -1).astype(BF16)
    w_qpe = w_q_up[:, :, MLA_NOPE_DIM:]
    w_qp = _pad_cols(w_qpe, LANES).reshape(MLA_Q_RANK, -1).astype(BF16)
    w_qpr = _pad_cols(_rot_half_cols(w_qpe), LANES).reshape(MLA_Q_RANK, -1).astype(BF16)
    w_kn = w_kv_up[:, :, :MLA_NOPE_DIM].reshape(MLA_KV_RANK, -1).astype(BF16)
    w_v = w_kv_up[:, :, MLA_NOPE_DIM:].reshape(MLA_KV_RANK, -1).T.astype(BF16)
    qt_a, k_a, vt_a = _mla_proj(x16, w_a, w_kpe, row(g_qa), row(g_kva), w_qn.T, w_qp.T, w_qpr.T, w_kn, w_v,
                                cos64, sin64, seq)
    o_a = _mla_attention(qt_a, k_a, vt_a, bsz, seq)

    bias_full = jnp.repeat(b_spatial.T.astype(F32), SGU_GROUP_DIM, axis=1)
    o_b = _sgu(x16, w_z, row(sgu_ln_g), row(sgu_ln_b), w_spatial.astype(F32), bias_full)

    k_c = _moba_k_proj(x16, w_c[:, MOBA_WIDTH:2 * MOBA_WIDTH], cos128, sin128, seq)
    w_qv_t = jnp.concatenate([w_c[:, :MOBA_WIDTH], w_c[:, 2 * MOBA_WIDTH:]], axis=1).T
    qvt_c = _moba_qv_proj(x16, w_qv_t, cos128.T, sin128.T, seq)
    o_c = _moba_attention(qvt_c, k_c, avg, bsz, seq)

    kv_mem = _matmul(mem16, w_mem_kv.astype(BF16), mem_len, 2 * MEM_WIDTH, BF16, "mem_kv")
    o_m = _mem_attention(x16, w_mq, kv_mem, bsz, seq, mem_len)

    y = _gated_sum(x16, o_a, o_b, o_c, o_m, w_g, w_br_a.astype(BF16), w_br_b.astype(BF16),
                   w_br_c.astype(BF16), w_br_m.astype(BF16))
    x32, x16 = _proj_norm(y, w_o.astype(BF16), x32, row(ln1_g), row(ln1_b))

    e_idx, wts, rank, counts = _router(x32, _pad_cols(w_router.astype(F32), LANES),
                                       _pad_cols(row(router_bias), LANES))
    bm = MOE_ROW_BLOCK
    n_blk = (t * TOP_K) // bm + N_EXPERTS
    counts = counts[0, :N_EXPERTS].astype(I32)
    padded = (counts + bm - 1) // bm * bm
    pad_ends = jnp.cumsum(padded)
    pad_starts = pad_ends - padded
    dest = _pad_cols((pad_starts[e_idx[:, :TOP_K]] + rank[:, :TOP_K]).astype(I32), SLOT_STRIDE).reshape(-1)
    blk_first = jnp.arange(n_blk, dtype=I32)[:, None] * bm
    blk_e = jnp.minimum(jnp.sum((pad_ends[None, :] <= blk_first).astype(I32), axis=1), N_EXPERTS - 1)
    n_used = (pad_ends[-1:] // bm).astype(I32)
    xs = _dispatch(x32, dest, pad_starts + counts, padded - counts, n_used, n_blk * bm)
    ys = _expert_ffn(xs, blk_e, n_used, *expert_weights, layer)
    return _combine(x32, dest, wts, ys, w_sh_gate.astype(BF16), w_sh_up.astype(BF16),
                    w_sh_down.astype(BF16), row(ln2_g), row(ln2_b))


def kernel(x, mem, w_in, g_qa, w_q_up, g_kva, w_kv_up, sgu_ln_g, sgu_ln_b, w_spatial, b_spatial, w_mem_kv, w_br_a, w_br_b, w_br_c, w_br_m, w_o, ln1_g, ln1_b, w_router, router_bias, w_exp_gate, w_exp_up, w_exp_down, w_sh_gate, w_sh_up, w_sh_down, ln2_g, ln2_b):
    bsz, seq, d = x.shape
    mem_len = mem.shape[1]
    params = (w_in, g_qa, w_q_up, g_kva, w_kv_up, sgu_ln_g, sgu_ln_b, w_spatial, b_spatial, w_mem_kv,
              w_br_a, w_br_b, w_br_c, w_br_m, w_o, ln1_g, ln1_b, w_router, router_bias,
              w_sh_gate, w_sh_up, w_sh_down, ln2_g, ln2_b)
    expert_weights = (w_exp_gate, w_exp_up, w_exp_down)
    tables = _rope_tables(seq)
    n_blocks = seq // MOBA_BLOCK
    blk_of_pos = jnp.arange(seq, dtype=I32)[None, :] // MOBA_BLOCK
    avg = jnp.where(blk_of_pos == jnp.arange(LANES, dtype=I32)[:, None], 1.0 / MOBA_BLOCK, 0.0).astype(BF16)
    assert n_blocks <= LANES
    x32 = x.reshape(bsz * seq, d)
    x16 = x32.astype(BF16)
    mem16 = mem.reshape(bsz * mem_len, d).astype(BF16)
    for l in range(DEPTH):
        x32, x16 = _layer(x32, x16, mem16, tables, avg, bsz, seq, mem_len, tuple(w[l] for w in params),
                          expert_weights, l)
    return x32.reshape(bsz, seq, d)
```

```python
import functools

import jax
import jax.numpy as jnp
from jax import lax
from jax.experimental import pallas as pl
from jax.experimental.pallas import tpu as pltpu

F32 = jnp.float32
BF16 = jnp.bfloat16
I32 = jnp.int32
U32 = jnp.uint32

D_MODEL = 2048
DEPTH = 2
MLA_HEADS = 8
MLA_Q_RANK = 512
MLA_KV_RANK = 256
MLA_NOPE_DIM = 128
MLA_ROPE_DIM = 64
MLA_V_DIM = 128
SGU_GROUPS = 8
SGU_GROUP_DIM = 128
SGU_WIDTH = SGU_GROUPS * SGU_GROUP_DIM
SGU_CHUNK = 128
MOBA_HEADS = 8
MOBA_HEAD_DIM = 128
MOBA_WIDTH = MOBA_HEADS * MOBA_HEAD_DIM
MOBA_BLOCK = 256
MOBA_TOPK = 3
MEM_HEADS = 4
MEM_HEAD_DIM = 128
MEM_WIDTH = MEM_HEADS * MEM_HEAD_DIM
N_BRANCH = 4
N_EXPERTS = 64
TOP_K = 6
D_EXPERT = 512
ROUTED_SCALE = 2.5
ROPE_THETA = 10000.0
DEEPNORM_ALPHA = (2 * DEPTH) ** 0.25

LANES = 128
SUBLANES = 8
ATTN_BLOCK = 256
MOE_ROW_BLOCK = 256
SLOT_STRIDE = 8
NEG_BIG = -1e30
VMEM_LIMIT = 56 * 1024 * 1024


def _cparams(n_axes):
    return pltpu.CompilerParams(dimension_semantics=("arbitrary",) * n_axes,
                                vmem_limit_bytes=VMEM_LIMIT)


def _dot(a, b):
    return jnp.dot(a, b, preferred_element_type=F32)


def _dot_nt(a, b):
    return lax.dot_general(a, b, (((1,), (1,)), ((), ())), preferred_element_type=F32)


def _layer_norm_rows(v, g, b, eps=1e-5):
    mu = jnp.mean(v, axis=-1, keepdims=True)
    vc = v - mu
    var = jnp.mean(vc * vc, axis=-1, keepdims=True)
    return vc * lax.rsqrt(var + eps) * g + b


def _rms_norm_rows(v, g, eps=1e-6):
    return v * lax.rsqrt(jnp.mean(v * v, axis=-1, keepdims=True) + eps) * g


def _mm_kernel(x_ref, w_ref, o_ref):
    o_ref[...] = _dot(x_ref[...], w_ref[...]).astype(o_ref.dtype)


def _matmul(x, w, tm, tn, out_dtype, name):
    m, k = x.shape
    n = w.shape[1]
    return pl.pallas_call(
        _mm_kernel,
        grid=(n // tn, m // tm),
        in_specs=[pl.BlockSpec((tm, k), lambda j, i: (i, 0)),
                  pl.BlockSpec((k, tn), lambda j, i: (0, j))],
        out_specs=pl.BlockSpec((tm, tn), lambda j, i: (i, j)),
        out_shape=jax.ShapeDtypeStruct((m, n), out_dtype),
        compiler_params=_cparams(2),
        name=name,
    )(x, w)


def _mla_proj_kernel(x_ref, wa_ref, wkpe_ref, gq_ref, gkv_ref, wqnt_ref, wqpt_ref, wqprt_ref,
                     wkn_ref, wvt_ref, cos_ref, sin_ref, cost_ref, sint_ref, qt_ref, k_ref, vt_ref, *, scale):
    xb = x_ref[...]
    a = _dot(xb, wa_ref[...])
    cq = _rms_norm_rows(a[:, :MLA_Q_RANK], gq_ref[...]).astype(BF16)
    ckv = _rms_norm_rows(a[:, MLA_Q_RANK:], gkv_ref[...]).astype(BF16)
    qn_t = _dot_nt(wqnt_ref[...], cq)
    qp_t = _dot_nt(wqpt_ref[...], cq)
    qpr_t = _dot_nt(wqprt_ref[...], cq)
    v_t = _dot_nt(wvt_ref[...], ckv)
    cos_t = cost_ref[...]
    sin_t = sint_ref[...]
    for h in range(MLA_HEADS):
        lo, hi = h * LANES, (h + 1) * LANES
        qt_ref[0, h, 0, 0:LANES, :] = (qn_t[lo:hi] * scale).astype(qt_ref.dtype)
        qt_ref[0, h, 0, LANES:2 * LANES, :] = ((qp_t[lo:hi] * cos_t + qpr_t[lo:hi] * sin_t) * scale).astype(qt_ref.dtype)
        vt_ref[0, h, 0] = v_t[h * MLA_V_DIM:(h + 1) * MLA_V_DIM, :].astype(vt_ref.dtype)
    kn = _dot(ckv, wkn_ref[...])
    kp = _dot(xb, wkpe_ref[...])
    kpe = (kp[:, :LANES] * cos_ref[...] + kp[:, LANES:] * sin_ref[...]).astype(k_ref.dtype)
    for h in range(MLA_HEADS):
        lo, hi = h * LANES, (h + 1) * LANES
        k_ref[:, 2 * lo:2 * lo + LANES] = kn[:, lo:hi].astype(k_ref.dtype)
        k_ref[:, 2 * lo + LANES:2 * hi] = kpe


def _mla_proj(x, wa, wkpe, gq, gkv, wqnt, wqpt, wqprt, wkn, wvt, cos64, sin64, seq):
    t = x.shape[0]
    tm = ATTN_BLOCK
    full = lambda arr: pl.BlockSpec(arr.shape, lambda i: (0,) * arr.ndim)
    n_pos = seq // tm
    scale = (MLA_NOPE_DIM + MLA_ROPE_DIM) ** -0.5
    dk = 2 * LANES
    feat_major = lambda d: pl.BlockSpec((1, MLA_HEADS, 1, d, tm), lambda i: (i // n_pos, 0, i % n_pos, 0, 0))
    return pl.pallas_call(
        functools.partial(_mla_proj_kernel, scale=scale),
        grid=(t // tm,),
        in_specs=[pl.BlockSpec((tm, D_MODEL), lambda i: (i, 0)),
                  full(wa), full(wkpe), full(gq), full(gkv), full(wqnt), full(wqpt), full(wqprt),
                  full(wkn), full(wvt),
                  pl.BlockSpec((tm, LANES), lambda i: (i % n_pos, 0)),
                  pl.BlockSpec((tm, LANES), lambda i: (i % n_pos, 0)),
                  pl.BlockSpec((LANES, tm), lambda i: (0, i % n_pos)),
                  pl.BlockSpec((LANES, tm), lambda i: (0, i % n_pos))],
        out_specs=[feat_major(dk),
                   pl.BlockSpec((tm, MLA_HEADS * dk), lambda i: (i, 0)),
                   feat_major(MLA_V_DIM)],
        out_shape=[jax.ShapeDtypeStruct((t // seq, MLA_HEADS, n_pos, dk, tm), BF16),
                   jax.ShapeDtypeStruct((t, MLA_HEADS * dk), BF16),
                   jax.ShapeDtypeStruct((t // seq, MLA_HEADS, n_pos, MLA_V_DIM, tm), BF16)],
        compiler_params=_cparams(1),
        name="mla_proj",
    )(x, wa, wkpe, gq, gkv, wqnt, wqpt, wqprt, wkn, wvt, cos64, sin64, cos64.T, sin64.T)


def _softmax_steps(q_ts, ks, v_ts, carries, masks):
    heads = range(len(q_ts))
    s_ts = [_dot(ks[h], q_ts[h]) for h in heads]
    s_ts = [s_ts[h] if masks[h] is None else jnp.where(masks[h], s_ts[h], NEG_BIG) for h in heads]
    m_news = [jnp.maximum(carries[h][0], jnp.max(s_ts[h], axis=0, keepdims=True)) for h in heads]
    p_ts = [jnp.exp(s_ts[h] - m_news[h]) for h in heads]
    pvs = [_dot(v_ts[h], p_ts[h].astype(v_ts[h].dtype)) for h in heads]
    out = []
    for h in heads:
        m_i, l_i, acc_t = carries[h]
        alpha = jnp.exp(m_i - m_news[h])
        l_new = alpha * l_i + jnp.sum(p_ts[h], axis=0, keepdims=True)
        out.append((m_news[h], l_new, alpha * acc_t + pvs[h]))
    return tuple(out)


def _softmax_init(n_heads, tq, dv):
    return tuple((jnp.full((1, tq), NEG_BIG, F32), jnp.zeros((1, tq), F32), jnp.zeros((dv, tq), F32))
                 for _ in range(n_heads))


def _softmax_finish(o_ref, carries, dv):
    for h, (_, l_i, acc_t) in enumerate(carries):
        o_ref[:, h * dv:(h + 1) * dv] = (acc_t / l_i).T.astype(o_ref.dtype)


def _causal_attn_kernel(qt_ref, k_ref, vt_ref, o_ref, *, tq, n_heads, dk, dv):
    qi = pl.program_id(2)
    key = lax.broadcasted_iota(I32, (tq, tq), 0)
    qry = lax.broadcasted_iota(I32, (tq, tq), 1)
    qs = [qt_ref[0, h, 0] for h in range(n_heads)]

    def step(kb, carries, mask):
        ks = pl.multiple_of(kb * tq, tq)
        return _softmax_steps(qs, [k_ref[pl.ds(ks, tq), h * dk:(h + 1) * dk] for h in range(n_heads)],
                              [vt_ref[0, h, kb] for h in range(n_heads)], carries, [mask] * n_heads)

    carries = step(qi, _softmax_init(n_heads, tq, dv), key <= qry)
    carries = lax.fori_loop(0, qi, lambda kb, c: step(kb, c, None), carries)
    _softmax_finish(o_ref, carries, dv)


def _mla_attention(q_t, k_a, v_t, bsz, seq, n_heads=8):
    t = k_a.shape[0]
    tq = ATTN_BLOCK
    nq = seq // tq
    dk = 2 * LANES
    dv = MLA_V_DIM
    return pl.pallas_call(
        functools.partial(_causal_attn_kernel, tq=tq, n_heads=n_heads, dk=dk, dv=dv),
        grid=(bsz, MLA_HEADS // n_heads, nq),
        in_specs=[pl.BlockSpec((1, n_heads, 1, dk, tq), lambda b, h, i: (b, h, i, 0, 0)),
                  pl.BlockSpec((seq, n_heads * dk), lambda b, h, i: (b, h)),
                  pl.BlockSpec((1, n_heads, nq, dv, tq), lambda b, h, i: (b, h, 0, 0, 0))],
        out_specs=pl.BlockSpec((tq, n_heads * dv), lambda b, h, i: (b * nq + i, h)),
        out_shape=jax.ShapeDtypeStruct((t, MLA_HEADS * dv), BF16),
        compiler_params=_cparams(3),
        name="mla_attn",
    )(q_t, k_a, v_t)


def _sgu_kernel(x_ref, wz_ref, g_ref, b_ref, ws_ref, bs_ref, o_ref, *, tm):
    z = jax.nn.gelu(_dot(x_ref[...], wz_ref[...]))
    u = z[:, :SGU_WIDTH]
    v = _layer_norm_rows(z[:, SGU_WIDTH:], g_ref[...], b_ref[...]).astype(BF16)
    row = lax.broadcasted_iota(I32, (SGU_CHUNK, SGU_CHUNK), 0)
    col = lax.broadcasted_iota(I32, (SGU_CHUNK, SGU_CHUNK), 1)
    bias = bs_ref[...]
    for g in range(SGU_GROUPS):
        w = jnp.where(col <= row, ws_ref[g], 0.0).astype(BF16)
        lo, hi = g * SGU_GROUP_DIM, (g + 1) * SGU_GROUP_DIM
        for c in range(tm // SGU_CHUNK):
            r0, r1 = c * SGU_CHUNK, (c + 1) * SGU_CHUNK
            mixed = _dot(w, v[r0:r1, lo:hi]) + bias[:, lo:hi]
            o_ref[r0:r1, lo:hi] = (u[r0:r1, lo:hi] * mixed).astype(o_ref.dtype)


def _sgu(x, wz, ln_g, ln_b, w_s, bias_full, tm=256):
    t = x.shape[0]
    full = lambda arr: pl.BlockSpec(arr.shape, lambda i: (0,) * arr.ndim)
    return pl.pallas_call(
        functools.partial(_sgu_kernel, tm=tm),
        grid=(t // tm,),
        in_specs=[pl.BlockSpec((tm, D_MODEL), lambda i: (i, 0)),
                  full(wz), full(ln_g), full(ln_b), full(w_s), full(bias_full)],
        out_specs=pl.BlockSpec((tm, SGU_WIDTH), lambda i: (i, 0)),
        out_shape=jax.ShapeDtypeStruct((t, SGU_WIDTH), BF16),
        compiler_params=_cparams(1),
        name="sgu",
    )(x, wz, ln_g, ln_b, w_s, bias_full)


def _moba_k_proj_kernel(x_ref, w_ref, cos_ref, sin_ref, o_ref):
    acc = _dot(x_ref[...], w_ref[...])
    cos = cos_ref[...]
    sin = sin_ref[...]
    for h in range(MOBA_HEADS):
        seg = acc[:, h * LANES:(h + 1) * LANES]
        rot = pltpu.roll(seg, MOBA_HEAD_DIM // 2, axis=1)
        o_ref[:, h * LANES:(h + 1) * LANES] = (seg * cos + rot * sin).astype(o_ref.dtype)


def _moba_k_proj(x, w_k, cos128, sin128, seq, tm=512):
    t = x.shape[0]
    n_pos = seq // tm
    return pl.pallas_call(
        _moba_k_proj_kernel,
        grid=(t // tm,),
        in_specs=[pl.BlockSpec((tm, D_MODEL), lambda i: (i, 0)),
                  pl.BlockSpec(w_k.shape, lambda i: (0, 0)),
                  pl.BlockSpec((tm, LANES), lambda i: (i % n_pos, 0)),
                  pl.BlockSpec((tm, LANES), lambda i: (i % n_pos, 0))],
        out_specs=pl.BlockSpec((tm, MOBA_WIDTH), lambda i: (i, 0)),
        out_shape=jax.ShapeDtypeStruct((t, MOBA_WIDTH), BF16),
        compiler_params=_cparams(1),
        name="moba_k_proj",
    )(x, w_k, cos128, sin128)


def _moba_qv_proj_kernel(x_ref, wt_ref, cost_ref, sint_ref, o_ref, *, scale):
    y_t = _dot_nt(wt_ref[...], x_ref[...])
    hd = MOBA_HEAD_DIM

    @pl.when(pl.program_id(0) == 0)
    def _():
        cos_t = cost_ref[...]
        sin_t = sint_ref[...]
        for h in range(MOBA_HEADS):
            seg = y_t[h * hd:(h + 1) * hd]
            rot = jnp.concatenate([seg[hd // 2:], seg[:hd // 2]], axis=0)
            o_ref[0, h, 0] = ((seg * cos_t + rot * sin_t) * scale).astype(o_ref.dtype)

    @pl.when(pl.program_id(0) == 1)
    def _():
        for h in range(MOBA_HEADS):
            o_ref[0, h, 0] = y_t[h * hd:(h + 1) * hd].astype(o_ref.dtype)


def _moba_qv_proj(x, w_qv_t, cos128_t, sin128_t, seq):
    t = x.shape[0]
    tm = ATTN_BLOCK
    n_pos = seq // tm
    return pl.pallas_call(
        functools.partial(_moba_qv_proj_kernel, scale=MOBA_HEAD_DIM ** -0.5),
        grid=(2, t // tm),
        in_specs=[pl.BlockSpec((tm, D_MODEL), lambda j, i: (i, 0)),
                  pl.BlockSpec((MOBA_WIDTH, D_MODEL), lambda j, i: (j, 0)),
                  pl.BlockSpec((MOBA_HEAD_DIM, tm), lambda j, i: (0, i % n_pos)),
                  pl.BlockSpec((MOBA_HEAD_DIM, tm), lambda j, i: (0, i % n_pos))],
        out_specs=pl.BlockSpec((1, MOBA_HEADS, 1, MOBA_HEAD_DIM, tm),
                               lambda j, i: (i // n_pos, j, i % n_pos, 0, 0)),
        out_shape=jax.ShapeDtypeStruct((t // seq, 2 * MOBA_HEADS, n_pos, MOBA_HEAD_DIM, tm), BF16),
        compiler_params=_cparams(2),
        name="moba_qv_proj",
    )(x, w_qv_t, cos128_t, sin128_t)


def _moba_attn_kernel(qt_ref, k_ref, vt_ref, avg_ref, o_ref, kmean_ref, *, n_heads):
    j = pl.program_id(2)
    blk = MOBA_BLOCK
    hd = MOBA_HEAD_DIM

    @pl.when(j == 0)
    def _():
        kmean_ref[...] = _dot(avg_ref[...], k_ref[...])

    blk_id = lax.broadcasted_iota(I32, (SUBLANES, blk), 0)
    n_blocks = k_ref.shape[0] // blk
    qs, sels = [], []
    for h in range(n_heads):
        q = qt_ref[0, h, 0]
        gate = jnp.dot(kmean_ref[0:SUBLANES, h * hd:(h + 1) * hd], q.astype(F32),
                       precision=lax.Precision.HIGHEST, preferred_element_type=F32)
        rank = jnp.zeros((SUBLANES, blk), I32)
        for m in range(n_blocks):
            gm = gate[m:m + 1, :]
            beats = (gm > gate) | ((gm == gate) & (m < blk_id))
            rank = rank + jnp.where(beats & (m < j), 1, 0)
        qs.append(q)
        sels.append(jnp.where((rank < MOBA_TOPK) & (blk_id < j), 1.0, 0.0))

    def step(kb, carries, masks):
        ks = pl.multiple_of(kb * blk, blk)
        return _softmax_steps(qs, [k_ref[pl.ds(ks, blk), h * hd:(h + 1) * hd] for h in range(n_heads)],
                              [vt_ref[0, h, kb] for h in range(n_heads)], carries, masks)

    key = lax.broadcasted_iota(I32, (blk, blk), 0)
    qry = lax.broadcasted_iota(I32, (blk, blk), 1)
    carries = step(j, _softmax_init(n_heads, blk, hd), [key <= qry] * n_heads)

    def body(n, c):
        masks = [jnp.max(jnp.where(blk_id == n, sels[h], 0.0), axis=0, keepdims=True) > 0.5
                 for h in range(n_heads)]
        return step(n, c, masks)

    carries = lax.fori_loop(0, j, body, carries)
    _softmax_finish(o_ref, carries, hd)


def _moba_attention(qv_t, k, avg, bsz, seq, n_heads=8):
    t = k.shape[0]
    nq = seq // MOBA_BLOCK
    assert nq <= SUBLANES and MOBA_BLOCK == ATTN_BLOCK
    w = n_heads * MOBA_HEAD_DIM
    groups = MOBA_HEADS // n_heads
    return pl.pallas_call(
        functools.partial(_moba_attn_kernel, n_heads=n_heads),
        grid=(bsz, groups, nq),
        in_specs=[pl.BlockSpec((1, n_heads, 1, MOBA_HEAD_DIM, MOBA_BLOCK), lambda b, h, i: (b, h, i, 0, 0)),
                  pl.BlockSpec((seq, w), lambda b, h, i: (b, h)),
                  pl.BlockSpec((1, n_heads, nq, MOBA_HEAD_DIM, MOBA_BLOCK),
                               lambda b, h, i: (b, groups + h, 0, 0, 0)),
                  pl.BlockSpec(avg.shape, lambda b, h, i: (0, 0))],
        out_specs=pl.BlockSpec((MOBA_BLOCK, w), lambda b, h, i: (b * nq + i, h)),
        out_shape=jax.ShapeDtypeStruct((t, MOBA_WIDTH), BF16),
        scratch_shapes=[pltpu.VMEM((LANES, w), F32)],
        compiler_params=_cparams(3),
        name="moba_attn",
    )(qv_t, k, qv_t, avg)


def _mem_attn_kernel(x_ref, wq_ref, kv_ref, o_ref, *, scale):
    q = (_dot(x_ref[...], wq_ref[...]) * scale).astype(BF16)
    for h in range(MEM_HEADS):
        lo, hi = h * MEM_HEAD_DIM, (h + 1) * MEM_HEAD_DIM
        s = _dot_nt(q[:, lo:hi], kv_ref[:, lo:hi])
        p = jnp.exp(s - jnp.max(s, axis=1, keepdims=True))
        o = _dot(p.astype(BF16), kv_ref[:, MEM_WIDTH + lo:MEM_WIDTH + hi])
        o_ref[:, lo:hi] = (o / jnp.sum(p, axis=1, keepdims=True)).astype(o_ref.dtype)


def _mem_attention(x, w_mq, kv_mem, bsz, seq, mem_len, tm=512):
    t = x.shape[0]
    ns = seq // tm
    return pl.pallas_call(
        functools.partial(_mem_attn_kernel, scale=MEM_HEAD_DIM ** -0.5),
        grid=(bsz, ns),
        in_specs=[pl.BlockSpec((tm, D_MODEL), lambda b, i: (b * ns + i, 0)),
                  pl.BlockSpec(w_mq.shape, lambda b, i: (0, 0)),
                  pl.BlockSpec((mem_len, 2 * MEM_WIDTH), lambda b, i: (b, 0))],
        out_specs=pl.BlockSpec((tm, MEM_WIDTH), lambda b, i: (b * ns + i, 0)),
        out_shape=jax.ShapeDtypeStruct((t, MEM_WIDTH), BF16),
        compiler_params=_cparams(2),
        name="mem_attn",
    )(x, w_mq, kv_mem)


def _gated_sum_kernel(x_ref, oa_ref, ob_ref, oc_ref, om_ref, g0_ref, g1_ref, g2_ref, g3_ref,
                      wa_ref, wb_ref, wc_ref, wm_ref, y_ref):
    xb = x_ref[...]
    acc = None
    for o_ref, g_ref, w_ref in ((oa_ref, g0_ref, wa_ref), (ob_ref, g1_ref, wb_ref),
                                (oc_ref, g2_ref, wc_ref), (om_ref, g3_ref, wm_ref)):
        term = jax.nn.sigmoid(_dot(xb, g_ref[...])) * _dot(o_ref[...], w_ref[...])
        acc = term if acc is None else acc + term
    y_ref[...] = acc.astype(y_ref.dtype)


def _gated_sum(x, o_a, o_b, o_c, o_m, w_gate, w_br_a, w_br_b, w_br_c, w_br_m, tm=512, tn=256):
    t = x.shape[0]
    nj = D_MODEL // tn
    rows = lambda width: pl.BlockSpec((tm, width), lambda j, i: (i, 0))
    gate = lambda b: pl.BlockSpec((D_MODEL, tn), lambda j, i: (0, b * nj + j))
    cols = lambda width: pl.BlockSpec((width, tn), lambda j, i: (0, j))
    return pl.pallas_call(
        _gated_sum_kernel,
        grid=(nj, t // tm),
        in_specs=[rows(D_MODEL), rows(o_a.shape[1]), rows(o_b.shape[1]), rows(o_c.shape[1]), rows(o_m.shape[1]),
                  gate(0), gate(1), gate(2), gate(3),
                  cols(w_br_a.shape[0]), cols(w_br_b.shape[0]), cols(w_br_c.shape[0]), cols(w_br_m.shape[0])],
        out_specs=pl.BlockSpec((tm, tn), lambda j, i: (i, j)),
        out_shape=jax.ShapeDtypeStruct((t, D_MODEL), BF16),
        compiler_params=_cparams(2),
        name="gated_sum",
    )(x, o_a, o_b, o_c, o_m, w_gate, w_gate, w_gate, w_gate, w_br_a, w_br_b, w_br_c, w_br_m)


def _pack_bf16_pairs(v):
    bits = lax.bitcast_convert_type(v.astype(BF16).astype(F32), U32)
    half = v.shape[1] // 2
    return (bits[:, :half] >> 16) | bits[:, half:]


def _unpack_bf16_pairs(words):
    lo = lax.bitcast_convert_type(words << 16, F32)
    hi = lax.bitcast_convert_type(words & jnp.uint32(0xFFFF0000), F32)
    return lo, hi


def _proj_norm_kernel(y_ref, w_ref, res_ref, g_ref, b_ref, o32_ref, o16_ref, opk_ref):
    v = DEEPNORM_ALPHA * res_ref[...] + _dot(y_ref[...], w_ref[...])
    out = _layer_norm_rows(v, g_ref[...], b_ref[...])
    o32_ref[...] = out
    o16_ref[...] = out.astype(o16_ref.dtype)
    opk_ref[...] = _pack_bf16_pairs(out)


def _proj_norm(y, w_o, res, g, b, tm=256):
    t = y.shape[0]
    full = lambda arr: pl.BlockSpec(arr.shape, lambda i: (0,) * arr.ndim)
    rows = pl.BlockSpec((tm, D_MODEL), lambda i: (i, 0))
    return pl.pallas_call(
        _proj_norm_kernel,
        grid=(t // tm,),
        in_specs=[rows, full(w_o), rows, full(g), full(b)],
        out_specs=[rows, rows, pl.BlockSpec((tm, D_MODEL // 2), lambda i: (i, 0))],
        out_shape=[jax.ShapeDtypeStruct((t, D_MODEL), F32), jax.ShapeDtypeStruct((t, D_MODEL), BF16),
                   jax.ShapeDtypeStruct((t, D_MODEL // 2), U32)],
        compiler_params=_cparams(1),
        name="proj_norm",
    )(y, w_o, res, g, b)


def _router_kernel(x_ref, w_ref, bias_ref, e_ref, wt_ref, rank_ref, cnt_ref, run_ref, *, tm):
    i = pl.program_id(0)

    @pl.when(i == 0)
    def _():
        run_ref[...] = jnp.zeros_like(run_ref)

    logits = jnp.dot(x_ref[...], w_ref[...], precision=lax.Precision.HIGHEST, preferred_element_type=F32)
    scores = jax.nn.sigmoid(logits)
    lane = lax.broadcasted_iota(I32, (tm, LANES), 1)
    biased = jnp.where(lane < N_EXPERTS, scores + bias_ref[...], NEG_BIG)
    picks = []
    chosen = jnp.zeros((tm, LANES), jnp.bool_)
    for _ in range(TOP_K):
        mx = jnp.max(biased, axis=1, keepdims=True)
        idx = jnp.min(jnp.where(biased == mx, lane, LANES), axis=1, keepdims=True)
        hit = lane == idx
        picks.append((idx, hit))
        chosen = chosen | hit
        biased = jnp.where(hit, 2 * NEG_BIG, biased)
    picked_scores = jnp.where(chosen, scores, 0.0)
    norm = ROUTED_SCALE / jnp.sum(picked_scores, axis=1, keepdims=True)

    r = lax.broadcasted_iota(I32, (tm, tm), 0)
    c = lax.broadcasted_iota(I32, (tm, tm), 1)
    strict_lower = jnp.where(c < r, 1.0, 0.0).astype(BF16)
    chosen_f = jnp.where(chosen, 1.0, 0.0)
    arrival = run_ref[0:1, :] + _dot(strict_lower, chosen_f.astype(BF16))
    run_ref[...] = run_ref[...] + jnp.sum(chosen_f, axis=0, keepdims=True)
    cnt_ref[...] = run_ref[...]

    e_out = jnp.zeros((tm, LANES), I32)
    w_out = jnp.zeros((tm, LANES), F32)
    r_out = jnp.zeros((tm, LANES), F32)
    for slot, (idx, hit) in enumerate(picks):
        here = lane == slot
        e_out = jnp.where(here, idx, e_out)
        w_out = jnp.where(here, jnp.sum(jnp.where(hit, scores, 0.0), axis=1, keepdims=True) * norm, w_out)
        r_out = jnp.where(here, jnp.sum(jnp.where(hit, arrival, 0.0), axis=1, keepdims=True), r_out)
    e_ref[...] = e_out
    wt_ref[...] = w_out
    rank_ref[...] = r_out.astype(I32)


def _router(x32, w_router_pad, bias_pad, tm=256):
    t = x32.shape[0]
    full = lambda arr: pl.BlockSpec(arr.shape, lambda i: (0,) * arr.ndim)
    rows = pl.BlockSpec((tm, LANES), lambda i: (i, 0))
    return pl.pallas_call(
        functools.partial(_router_kernel, tm=tm),
        grid=(t // tm,),
        in_specs=[pl.BlockSpec((tm, D_MODEL), lambda i: (i, 0)), full(w_router_pad), full(bias_pad)],
        out_specs=[rows, rows, rows, pl.BlockSpec((8, LANES), lambda i: (0, 0))],
        out_shape=[jax.ShapeDtypeStruct((t, LANES), I32), jax.ShapeDtypeStruct((t, LANES), F32),
                   jax.ShapeDtypeStruct((t, LANES), I32), jax.ShapeDtypeStruct((8, LANES), F32)],
        scratch_shapes=[pltpu.VMEM((8, LANES), F32)],
        compiler_params=_cparams(1),
        name="moe_router",
    )(x32, w_router_pad, bias_pad)


def _dispatch_kernel(fill_start_ref, fill_n_ref, n_used_ref, dest_ref, x_ref, out_hbm, zero_ref, sem, fill_sem,
                     *, tm):
    @pl.when(pl.program_id(0) == 0)
    def _():
        zero_ref[...] = jnp.zeros_like(zero_ref)
        bm = zero_ref.shape[0]
        n_blk = out_hbm.shape[0] // bm

        def fill_copy(dst_row):
            return pltpu.make_async_copy(zero_ref.at[pl.ds(0, 1), :], out_hbm.at[pl.ds(dst_row, 1), :], fill_sem)

        def tail_copy(blk):
            return pltpu.make_async_copy(zero_ref, out_hbm.at[pl.ds(pl.multiple_of(blk * bm, bm), bm), :], fill_sem)

        def start_expert(e, _):
            first = fill_start_ref[e]
            lax.fori_loop(0, fill_n_ref[e], lambda r, c: (fill_copy(first + r).start(), c)[1], 0)
            return 0

        def wait_expert(e, _):
            lax.fori_loop(0, fill_n_ref[e], lambda r, c: (fill_copy(0).wait(), c)[1], 0)
            return 0

        lax.fori_loop(0, N_EXPERTS, start_expert, 0)
        lax.fori_loop(n_used_ref[0], n_blk, lambda b, c: (tail_copy(b).start(), c)[1], 0)
        lax.fori_loop(0, N_EXPERTS, wait_expert, 0)
        lax.fori_loop(n_used_ref[0], n_blk, lambda b, c: (tail_copy(0).wait(), c)[1], 0)

    def issue(tok, _):
        for k in range(TOP_K):
            pltpu.make_async_copy(x_ref.at[pl.ds(tok, 1), :],
                                  out_hbm.at[pl.ds(dest_ref[tok * SLOT_STRIDE + k], 1), :], sem).start()
        return 0

    lax.fori_loop(0, tm, issue, 0)
    for _ in range(TOP_K):
        pltpu.make_async_copy(x_ref, out_hbm.at[pl.ds(0, tm), :], sem).wait()


def _dispatch(x_rows, dest_flat, fill_start, fill_n, n_used, n_rows, tm=256):
    t, width = x_rows.shape
    grid_spec = pltpu.PrefetchScalarGridSpec(
        num_scalar_prefetch=3,
        grid=(t // tm,),
        in_specs=[pl.BlockSpec((tm * SLOT_STRIDE,), lambda i, fs, fn, nu: (i,), memory_space=pltpu.SMEM),
                  pl.BlockSpec((tm, width), lambda i, fs, fn, nu: (i, 0))],
        out_specs=pl.BlockSpec(memory_space=pl.ANY),
        scratch_shapes=[pltpu.VMEM((MOE_ROW_BLOCK, width), x_rows.dtype), pltpu.SemaphoreType.DMA(()),
                        pltpu.SemaphoreType.DMA(())],
    )
    return pl.pallas_call(
        functools.partial(_dispatch_kernel, tm=tm),
        grid_spec=grid_spec,
        out_shape=jax.ShapeDtypeStruct((n_rows, width), x_rows.dtype),
        compiler_params=_cparams(1),
        name="moe_dispatch",
    )(fill_start, fill_n, n_used, dest_flat, x_rows)


def _expert_kernel(blk_e_ref, n_used_ref, xs_ref, wg_ref, wu_ref, wd_ref, ys_ref, wg16_ref, wu16_ref, wd16_ref):
    i = pl.program_id(0)
    used = i < n_used_ref[0]
    new_expert = (i == 0) | (blk_e_ref[i] != blk_e_ref[jnp.maximum(i - 1, 0)])

    @pl.when(used & new_expert)
    def _():
        wg16_ref[...] = wg_ref[0, 0].astype(BF16)
        wu16_ref[...] = wu_ref[0, 0].astype(BF16)
        wd16_ref[...] = wd_ref[0, 0].astype(BF16)

    @pl.when(used)
    def _():
        lo, hi = _unpack_bf16_pairs(xs_ref[...])
        xb = jnp.concatenate([lo.astype(BF16), hi.astype(BF16)], axis=1)
        hid = jax.nn.silu(_dot(xb, wg16_ref[...])) * _dot(xb, wu16_ref[...])
        ys_ref[...] = _pack_bf16_pairs(_dot(hid.astype(BF16), wd16_ref[...]))

    @pl.when(jnp.logical_not(used))
    def _():
        ys_ref[...] = jnp.zeros_like(ys_ref)


def _expert_ffn(xs, blk_e, n_used, w_gate, w_up, w_down, layer):
    n_rows, width = xs.shape
    bm = MOE_ROW_BLOCK
    grid_spec = pltpu.PrefetchScalarGridSpec(
        num_scalar_prefetch=2,
        grid=(n_rows // bm,),
        in_specs=[pl.BlockSpec((bm, width), lambda i, be, nu: (jnp.minimum(i, nu[0] - 1), 0)),
                  pl.BlockSpec((1, 1, D_MODEL, D_EXPERT), lambda i, be, nu: (layer, be[i], 0, 0)),
                  pl.BlockSpec((1, 1, D_MODEL, D_EXPERT), lambda i, be, nu: (layer, be[i], 0, 0)),
                  pl.BlockSpec((1, 1, D_EXPERT, D_MODEL), lambda i, be, nu: (layer, be[i], 0, 0))],
        out_specs=pl.BlockSpec((bm, width), lambda i, be, nu: (i, 0)),
        scratch_shapes=[pltpu.VMEM((D_MODEL, D_EXPERT), BF16), pltpu.VMEM((D_MODEL, D_EXPERT), BF16),
                        pltpu.VMEM((D_EXPERT, D_MODEL), BF16)],
    )
    return pl.pallas_call(
        _expert_kernel,
        grid_spec=grid_spec,
        out_shape=jax.ShapeDtypeStruct((n_rows, width), xs.dtype),
        compiler_params=_cparams(1),
        name="moe_experts",
    )(blk_e, n_used, xs, w_gate, w_up, w_down)


def _combine_kernel(dest_ref, x_ref, wt_ref, ys_hbm, wsg_ref, wsu_ref, wsd_ref, g_ref, b_ref,
                    o32_ref, o16_ref, buf_ref, sem, *, tm):
    def issue(tok, _):
        for k in range(TOP_K):
            pltpu.make_async_copy(ys_hbm.at[pl.ds(dest_ref[tok * SLOT_STRIDE + k], 1), :],
                                  buf_ref.at[k, pl.ds(tok, 1), :], sem).start()
        return 0

    lax.fori_loop(0, tm, issue, 0)
    x = x_ref[...]
    xb = x.astype(BF16)
    hid = jax.nn.silu(_dot(xb, wsg_ref[...])) * _dot(xb, wsu_ref[...])
    total = DEEPNORM_ALPHA * x + _dot(hid.astype(BF16), wsd_ref[...])
    for k in range(TOP_K):
        pltpu.make_async_copy(ys_hbm.at[pl.ds(0, tm), :], buf_ref.at[k], sem).wait()
    wt = wt_ref[...]
    routed_lo = routed_hi = None
    for k in range(TOP_K):
        lo, hi = _unpack_bf16_pairs(buf_ref[k])
        w_k = wt[:, k:k + 1]
        routed_lo = w_k * lo if routed_lo is None else routed_lo + w_k * lo
        routed_hi = w_k * hi if routed_hi is None else routed_hi + w_k * hi
    total = total + jnp.concatenate([routed_lo, routed_hi], axis=1)
    out = _layer_norm_rows(total, g_ref[...], b_ref[...])
    o32_ref[...] = out
    o16_ref[...] = out.astype(o16_ref.dtype)


def _combine(x32, dest_flat, wts, ys, w_sg, w_su, w_sd, g, b, tm=256):
    t = x32.shape[0]
    full = lambda arr: pl.BlockSpec(arr.shape, lambda i: (0,) * arr.ndim)
    rows = pl.BlockSpec((tm, D_MODEL), lambda i: (i, 0))
    return pl.pallas_call(
        functools.partial(_combine_kernel, tm=tm),
        grid=(t // tm,),
        in_specs=[pl.BlockSpec((tm * SLOT_STRIDE,), lambda i: (i,), memory_space=pltpu.SMEM),
                  rows,
                  pl.BlockSpec((tm, LANES), lambda i: (i, 0)),
                  pl.BlockSpec(memory_space=pl.ANY),
                  full(w_sg), full(w_su), full(w_sd), full(g), full(b)],
        out_specs=[rows, rows],
        out_shape=[jax.ShapeDtypeStruct((t, D_MODEL), F32), jax.ShapeDtypeStruct((t, D_MODEL), BF16)],
        scratch_shapes=[pltpu.VMEM((TOP_K, tm, ys.shape[1]), ys.dtype), pltpu.SemaphoreType.DMA(())],
        compiler_params=_cparams(1),
        name="moe_combine",
    )(dest_flat, x32, wts, ys, w_sg, w_su, w_sd, g, b)


def _rot_half_cols(w):
    half = w.shape[-1] // 2
    return jnp.concatenate([-w[..., half:], w[..., :half]], axis=-1)


def _pad_cols(w, width):
    return jnp.pad(w, [(0, 0)] * (w.ndim - 1) + [(0, width - w.shape[-1])])


def _rope_tables(seq):
    pos = jnp.arange(seq, dtype=F32)[:, None]

    def table(dim):
        half = dim // 2
        inv_freq = ROPE_THETA ** (-jnp.arange(half, dtype=F32) / half)
        ang = pos * inv_freq[None, :]
        return jnp.cos(ang), jnp.sin(ang)

    c64, s64 = table(MLA_ROPE_DIM)
    cos64 = _pad_cols(jnp.concatenate([c64, c64], axis=1), LANES)
    sin64 = _pad_cols(jnp.concatenate([s64, s64], axis=1), LANES)
    c128, s128 = table(MOBA_HEAD_DIM)
    cos128 = jnp.concatenate([c128, c128], axis=1)
    sin128 = jnp.concatenate([-s128, s128], axis=1)
    return cos64, sin64, cos128, sin128


def _layer(x32, x16, mem16, tables, avg, bsz, seq, mem_len, p, expert_weights, layer):
    (w_in, g_qa, w_q_up, g_kva, w_kv_up, sgu_ln_g, sgu_ln_b, w_spatial, b_spatial, w_mem_kv,
     w_br_a, w_br_b, w_br_c, w_br_m, w_o, ln1_g, ln1_b, w_router, router_bias,
     w_sh_gate, w_sh_up, w_sh_down, ln2_g, ln2_b) = p
    cos64, sin64, cos128, sin128 = tables
    t = x32.shape[0]
    row = lambda v: v.reshape(1, -1).astype(F32)

    c0 = MLA_Q_RANK + MLA_KV_RANK
    c1 = c0 + MLA_ROPE_DIM
    c2 = c1 + 2 * SGU_WIDTH
    c3 = c2 + 3 * MOBA_WIDTH
    c4 = c3 + MEM_WIDTH
    w_a = w_in[:, :c0].astype(BF16)
    w_kpe_raw = w_in[:, c0:c1]
    w_kpe = jnp.concatenate([_pad_cols(w_kpe_raw, LANES), _pad_cols(_rot_half_cols(w_kpe_raw), LANES)],
                            axis=1).astype(BF16)
    w_z = w_in[:, c1:c2].astype(BF16)
    w_c = w_in[:, c2:c3].astype(BF16)
    w_mq = w_in[:, c3:c4].astype(BF16)
    w_g = w_in[:, c4:].astype(BF16)

    w_qn = w_q_up[:, :, :MLA_NOPE_DIM].reshape(MLA_Q_RANK, -1).astype(BF16)
    w_qpe = w_q_up[:, :, MLA_NOPE_DIM:]
    w_qp = _pad_cols(w_qpe, LANES).reshape(MLA_Q_RANK, -1).astype(BF16)
    w_qpr = _pad_cols(_rot_half_cols(w_qpe), LANES).reshape(MLA_Q_RANK, -1).astype(BF16)
    w_kn = w_kv_up[:, :, :MLA_NOPE_DIM].reshape(MLA_KV_RANK, -1).astype(BF16)
    w_v = w_kv_up[:, :, MLA_NOPE_DIM:].reshape(MLA_KV_RANK, -1).T.astype(BF16)
    qt_a, k_a, vt_a = _mla_proj(x16, w_a, w_kpe, row(g_qa), row(g_kva), w_qn.T, w_qp.T, w_qpr.T, w_kn, w_v,
                                cos64, sin64, seq)
    o_a = _mla_attention(qt_a, k_a, vt_a, bsz, seq)

    bias_full = jnp.repeat(b_spatial.T.astype(F32), SGU_GROUP_DIM, axis=1)
    o_b = _sgu(x16, w_z, row(sgu_ln_g), row(sgu_ln_b), w_spatial.astype(F32), bias_full)

    k_c = _moba_k_proj(x16, w_c[:, MOBA_WIDTH:2 * MOBA_WIDTH], cos128, sin128, seq)
    w_qv_t = jnp.concatenate([w_c[:, :MOBA_WIDTH], w_c[:, 2 * MOBA_WIDTH:]], axis=1).T
    qvt_c = _moba_qv_proj(x16, w_qv_t, cos128.T, sin128.T, seq)
    o_c = _moba_attention(qvt_c, k_c, avg, bsz, seq)

    kv_mem = _matmul(mem16, w_mem_kv.astype(BF16), mem_len, 2 * MEM_WIDTH, BF16, "mem_kv")
    o_m = _mem_attention(x16, w_mq, kv_mem, bsz, seq, mem_len)

    y = _gated_sum(x16, o_a, o_b, o_c, o_m, w_g, w_br_a.astype(BF16), w_br_b.astype(BF16),
                   w_br_c.astype(BF16), w_br_m.astype(BF16))
    x32, x16, x_packed = _proj_norm(y, w_o.astype(BF16), x32, row(ln1_g), row(ln1_b))

    e_idx, wts, rank, counts = _router(x32, _pad_cols(w_router.astype(F32), LANES),
                                       _pad_cols(row(router_bias), LANES))
    bm = MOE_ROW_BLOCK
    n_blk = (t * TOP_K) // bm + N_EXPERTS
    counts = counts[0, :N_EXPERTS].astype(I32)
    padded = (counts + bm - 1) // bm * bm
    pad_ends = jnp.cumsum(padded)
    pad_starts = pad_ends - padded
    dest = _pad_cols((pad_starts[e_idx[:, :TOP_K]] + rank[:, :TOP_K]).astype(I32), SLOT_STRIDE).reshape(-1)
    blk_first = jnp.arange(n_blk, dtype=I32)[:, None] * bm
    blk_e = jnp.minimum(jnp.sum((pad_ends[None, :] <= blk_first).astype(I32), axis=1), N_EXPERTS - 1)
    n_used = (pad_ends[-1:] // bm).astype(I32)
    xs = _dispatch(x_packed, dest, pad_starts + counts, padded - counts, n_used, n_blk * bm)
    ys = _expert_ffn(xs, blk_e, n_used, *expert_weights, layer)
    return _combine(x32, dest, wts, ys, w_sh_gate.astype(BF16), w_sh_up.astype(BF16),
                    w_sh_down.astype(BF16), row(ln2_g), row(ln2_b))


def kernel(x, mem, w_in, g_qa, w_q_up, g_kva, w_kv_up, sgu_ln_g, sgu_ln_b, w_spatial, b_spatial, w_mem_kv, w_br_a, w_br_b, w_br_c, w_br_m, w_o, ln1_g, ln1_b, w_router, router_bias, w_exp_gate, w_exp_up, w_exp_down, w_sh_gate, w_sh_up, w_sh_down, ln2_g, ln2_b):
    bsz, seq, d = x.shape
    mem_len = mem.shape[1]
    params = (w_in, g_qa, w_q_up, g_kva, w_kv_up, sgu_ln_g, sgu_ln_b, w_spatial, b_spatial, w_mem_kv,
              w_br_a, w_br_b, w_br_c, w_br_m, w_o, ln1_g, ln1_b, w_router, router_bias,
              w_sh_gate, w_sh_up, w_sh_down, ln2_g, ln2_b)
    expert_weights = (w_exp_gate, w_exp_up, w_exp_down)
    tables = _rope_tables(seq)
    n_blocks = seq // MOBA_BLOCK
    blk_of_pos = jnp.arange(seq, dtype=I32)[None, :] // MOBA_BLOCK
    avg = jnp.where(blk_of_pos == jnp.arange(LANES, dtype=I32)[:, None], 1.0 / MOBA_BLOCK, 0.0).astype(BF16)
    assert n_blocks <= LANES
    x32 = x.reshape(bsz * seq, d)
    x16 = x32.astype(BF16)
    mem16 = mem.reshape(bsz * mem_len, d).astype(BF16)
    for l in range(DEPTH):
        x32, x16 = _layer(x32, x16, mem16, tables, avg, bsz, seq, mem_len, tuple(w[l] for w in params),
                          expert_weights, l)
    return x32.reshape(bsz, seq, d)
```

```python
import functools

import jax
import jax.numpy as jnp
from jax import lax
from jax.experimental import pallas as pl
from jax.experimental.pallas import tpu as pltpu

F32 = jnp.float32
BF16 = jnp.bfloat16
I32 = jnp.int32
U32 = jnp.uint32

D_MODEL = 2048
DEPTH = 2
MLA_HEADS = 8
MLA_Q_RANK = 512
MLA_KV_RANK = 256
MLA_NOPE_DIM = 128
MLA_ROPE_DIM = 64
MLA_V_DIM = 128
SGU_GROUPS = 8
SGU_GROUP_DIM = 128
SGU_WIDTH = SGU_GROUPS * SGU_GROUP_DIM
SGU_CHUNK = 128
MOBA_HEADS = 8
MOBA_HEAD_DIM = 128
MOBA_WIDTH = MOBA_HEADS * MOBA_HEAD_DIM
MOBA_BLOCK = 256
MOBA_TOPK = 3
MEM_HEADS = 4
MEM_HEAD_DIM = 128
MEM_WIDTH = MEM_HEADS * MEM_HEAD_DIM
N_BRANCH = 4
N_EXPERTS = 64
TOP_K = 6
D_EXPERT = 512
ROUTED_SCALE = 2.5
ROPE_THETA = 10000.0
DEEPNORM_ALPHA = (2 * DEPTH) ** 0.25

LANES = 128
SUBLANES = 8
ATTN_BLOCK = 256
MOE_ROW_BLOCK = 512
SLOT_STRIDE = 8
NEG_BIG = -1e30
VMEM_LIMIT = 56 * 1024 * 1024


def _cparams(n_axes):
    return pltpu.CompilerParams(dimension_semantics=("arbitrary",) * n_axes,
                                vmem_limit_bytes=VMEM_LIMIT)


def _dot(a, b):
    return jnp.dot(a, b, preferred_element_type=F32)


def _dot_nt(a, b):
    return lax.dot_general(a, b, (((1,), (1,)), ((), ())), preferred_element_type=F32)


def _layer_norm_rows(v, g, b, eps=1e-5):
    mu = jnp.mean(v, axis=-1, keepdims=True)
    vc = v - mu
    var = jnp.mean(vc * vc, axis=-1, keepdims=True)
    return vc * lax.rsqrt(var + eps) * g + b


def _rms_norm_rows(v, g, eps=1e-6):
    return v * lax.rsqrt(jnp.mean(v * v, axis=-1, keepdims=True) + eps) * g


def _mm_kernel(x_ref, w_ref, o_ref):
    o_ref[...] = _dot(x_ref[...], w_ref[...]).astype(o_ref.dtype)


def _matmul(x, w, tm, tn, out_dtype, name):
    m, k = x.shape
    n = w.shape[1]
    return pl.pallas_call(
        _mm_kernel,
        grid=(n // tn, m // tm),
        in_specs=[pl.BlockSpec((tm, k), lambda j, i: (i, 0)),
                  pl.BlockSpec((k, tn), lambda j, i: (0, j))],
        out_specs=pl.BlockSpec((tm, tn), lambda j, i: (i, j)),
        out_shape=jax.ShapeDtypeStruct((m, n), out_dtype),
        compiler_params=_cparams(2),
        name=name,
    )(x, w)


def _mla_proj_kernel(x_ref, wa_ref, wkpe_ref, gq_ref, gkv_ref, wqnt_ref, wqpt_ref, wqprt_ref,
                     wkn_ref, wvt_ref, cos_ref, sin_ref, cost_ref, sint_ref, qt_ref, k_ref, vt_ref, *, scale):
    xb = x_ref[...]
    a = _dot(xb, wa_ref[...])
    cq = _rms_norm_rows(a[:, :MLA_Q_RANK], gq_ref[...]).astype(BF16)
    ckv = _rms_norm_rows(a[:, MLA_Q_RANK:], gkv_ref[...]).astype(BF16)
    qn_t = _dot_nt(wqnt_ref[...], cq)
    qp_t = _dot_nt(wqpt_ref[...], cq)
    qpr_t = _dot_nt(wqprt_ref[...], cq)
    v_t = _dot_nt(wvt_ref[...], ckv)
    cos_t = cost_ref[...]
    sin_t = sint_ref[...]
    for h in range(MLA_HEADS):
        lo, hi = h * LANES, (h + 1) * LANES
        qt_ref[0, h, 0, 0:LANES, :] = (qn_t[lo:hi] * scale).astype(qt_ref.dtype)
        qt_ref[0, h, 0, LANES:2 * LANES, :] = ((qp_t[lo:hi] * cos_t + qpr_t[lo:hi] * sin_t) * scale).astype(qt_ref.dtype)
        vt_ref[0, h, 0] = v_t[h * MLA_V_DIM:(h + 1) * MLA_V_DIM, :].astype(vt_ref.dtype)
    kn = _dot(ckv, wkn_ref[...])
    kp = _dot(xb, wkpe_ref[...])
    kpe = (kp[:, :LANES] * cos_ref[...] + kp[:, LANES:] * sin_ref[...]).astype(k_ref.dtype)
    for h in range(MLA_HEADS):
        lo, hi = h * LANES, (h + 1) * LANES
        k_ref[:, 2 * lo:2 * lo + LANES] = kn[:, lo:hi].astype(k_ref.dtype)
        k_ref[:, 2 * lo + LANES:2 * hi] = kpe


def _mla_proj(x, wa, wkpe, gq, gkv, wqnt, wqpt, wqprt, wkn, wvt, cos64, sin64, seq):
    t = x.shape[0]
    tm = ATTN_BLOCK
    full = lambda arr: pl.BlockSpec(arr.shape, lambda i: (0,) * arr.ndim)
    n_pos = seq // tm
    scale = (MLA_NOPE_DIM + MLA_ROPE_DIM) ** -0.5
    dk = 2 * LANES
    feat_major = lambda d: pl.BlockSpec((1, MLA_HEADS, 1, d, tm), lambda i: (i // n_pos, 0, i % n_pos, 0, 0))
    return pl.pallas_call(
        functools.partial(_mla_proj_kernel, scale=scale),
        grid=(t // tm,),
        in_specs=[pl.BlockSpec((tm, D_MODEL), lambda i: (i, 0)),
                  full(wa), full(wkpe), full(gq), full(gkv), full(wqnt), full(wqpt), full(wqprt),
                  full(wkn), full(wvt),
                  pl.BlockSpec((tm, LANES), lambda i: (i % n_pos, 0)),
                  pl.BlockSpec((tm, LANES), lambda i: (i % n_pos, 0)),
                  pl.BlockSpec((LANES, tm), lambda i: (0, i % n_pos)),
                  pl.BlockSpec((LANES, tm), lambda i: (0, i % n_pos))],
        out_specs=[feat_major(dk),
                   pl.BlockSpec((tm, MLA_HEADS * dk), lambda i: (i, 0)),
                   feat_major(MLA_V_DIM)],
        out_shape=[jax.ShapeDtypeStruct((t // seq, MLA_HEADS, n_pos, dk, tm), BF16),
                   jax.ShapeDtypeStruct((t, MLA_HEADS * dk), BF16),
                   jax.ShapeDtypeStruct((t // seq, MLA_HEADS, n_pos, MLA_V_DIM, tm), BF16)],
        compiler_params=_cparams(1),
        name="mla_proj",
    )(x, wa, wkpe, gq, gkv, wqnt, wqpt, wqprt, wkn, wvt, cos64, sin64, cos64.T, sin64.T)


def _softmax_steps(q_ts, ks, v_ts, carries, masks):
    heads = range(len(q_ts))
    s_ts = [_dot(ks[h], q_ts[h]) for h in heads]
    s_ts = [s_ts[h] if masks[h] is None else jnp.where(masks[h], s_ts[h], NEG_BIG) for h in heads]
    m_news = [jnp.maximum(carries[h][0], jnp.max(s_ts[h], axis=0, keepdims=True)) for h in heads]
    p_ts = [jnp.exp(s_ts[h] - m_news[h]) for h in heads]
    pvs = [_dot(v_ts[h], p_ts[h].astype(v_ts[h].dtype)) for h in heads]
    out = []
    for h in heads:
        m_i, l_i, acc_t = carries[h]
        alpha = jnp.exp(m_i - m_news[h])
        l_new = alpha * l_i + jnp.sum(p_ts[h], axis=0, keepdims=True)
        out.append((m_news[h], l_new, alpha * acc_t + pvs[h]))
    return tuple(out)


def _softmax_init(n_heads, tq, dv):
    return tuple((jnp.full((1, tq), NEG_BIG, F32), jnp.zeros((1, tq), F32), jnp.zeros((dv, tq), F32))
                 for _ in range(n_heads))


def _softmax_finish(o_ref, carries, dv):
    for h, (_, l_i, acc_t) in enumerate(carries):
        o_ref[:, h * dv:(h + 1) * dv] = (acc_t / l_i).T.astype(o_ref.dtype)


def _causal_attn_kernel(qt_ref, k_ref, vt_ref, o_ref, *, tq, n_heads, dk, dv):
    qi = pl.program_id(2)
    key = lax.broadcasted_iota(I32, (tq, tq), 0)
    qry = lax.broadcasted_iota(I32, (tq, tq), 1)
    qs = [qt_ref[0, h, 0] for h in range(n_heads)]

    def step(kb, carries, mask):
        ks = pl.multiple_of(kb * tq, tq)
        return _softmax_steps(qs, [k_ref[pl.ds(ks, tq), h * dk:(h + 1) * dk] for h in range(n_heads)],
                              [vt_ref[0, h, kb] for h in range(n_heads)], carries, [mask] * n_heads)

    carries = step(qi, _softmax_init(n_heads, tq, dv), key <= qry)
    carries = lax.fori_loop(0, qi, lambda kb, c: step(kb, c, None), carries)
    _softmax_finish(o_ref, carries, dv)


def _mla_attention(q_t, k_a, v_t, bsz, seq, n_heads=8):
    t = k_a.shape[0]
    tq = ATTN_BLOCK
    nq = seq // tq
    dk = 2 * LANES
    dv = MLA_V_DIM
    return pl.pallas_call(
        functools.partial(_causal_attn_kernel, tq=tq, n_heads=n_heads, dk=dk, dv=dv),
        grid=(bsz, MLA_HEADS // n_heads, nq),
        in_specs=[pl.BlockSpec((1, n_heads, 1, dk, tq), lambda b, h, i: (b, h, i, 0, 0)),
                  pl.BlockSpec((seq, n_heads * dk), lambda b, h, i: (b, h)),
                  pl.BlockSpec((1, n_heads, nq, dv, tq), lambda b, h, i: (b, h, 0, 0, 0))],
        out_specs=pl.BlockSpec((tq, n_heads * dv), lambda b, h, i: (b * nq + i, h)),
        out_shape=jax.ShapeDtypeStruct((t, MLA_HEADS * dv), BF16),
        compiler_params=_cparams(3),
        name="mla_attn",
    )(q_t, k_a, v_t)


def _sgu_kernel(x_ref, wz_ref, g_ref, b_ref, ws_ref, bs_ref, o_ref, *, tm):
    z = jax.nn.gelu(_dot(x_ref[...], wz_ref[...]))
    u = z[:, :SGU_WIDTH]
    v = _layer_norm_rows(z[:, SGU_WIDTH:], g_ref[...], b_ref[...]).astype(BF16)
    row = lax.broadcasted_iota(I32, (SGU_CHUNK, SGU_CHUNK), 0)
    col = lax.broadcasted_iota(I32, (SGU_CHUNK, SGU_CHUNK), 1)
    bias = bs_ref[...]
    for g in range(SGU_GROUPS):
        w = jnp.where(col <= row, ws_ref[g], 0.0).astype(BF16)
        lo, hi = g * SGU_GROUP_DIM, (g + 1) * SGU_GROUP_DIM
        for c in range(tm // SGU_CHUNK):
            r0, r1 = c * SGU_CHUNK, (c + 1) * SGU_CHUNK
            mixed = _dot(w, v[r0:r1, lo:hi]) + bias[:, lo:hi]
            o_ref[r0:r1, lo:hi] = (u[r0:r1, lo:hi] * mixed).astype(o_ref.dtype)


def _sgu(x, wz, ln_g, ln_b, w_s, bias_full, tm=256):
    t = x.shape[0]
    full = lambda arr: pl.BlockSpec(arr.shape, lambda i: (0,) * arr.ndim)
    return pl.pallas_call(
        functools.partial(_sgu_kernel, tm=tm),
        grid=(t // tm,),
        in_specs=[pl.BlockSpec((tm, D_MODEL), lambda i: (i, 0)),
                  full(wz), full(ln_g), full(ln_b), full(w_s), full(bias_full)],
        out_specs=pl.BlockSpec((tm, SGU_WIDTH), lambda i: (i, 0)),
        out_shape=jax.ShapeDtypeStruct((t, SGU_WIDTH), BF16),
        compiler_params=_cparams(1),
        name="sgu",
    )(x, wz, ln_g, ln_b, w_s, bias_full)


def _moba_k_proj_kernel(x_ref, w_ref, cos_ref, sin_ref, o_ref):
    acc = _dot(x_ref[...], w_ref[...])
    cos = cos_ref[...]
    sin = sin_ref[...]
    for h in range(MOBA_HEADS):
        seg = acc[:, h * LANES:(h + 1) * LANES]
        rot = pltpu.roll(seg, MOBA_HEAD_DIM // 2, axis=1)
        o_ref[:, h * LANES:(h + 1) * LANES] = (seg * cos + rot * sin).astype(o_ref.dtype)


def _moba_k_proj(x, w_k, cos128, sin128, seq, tm=512):
    t = x.shape[0]
    n_pos = seq // tm
    return pl.pallas_call(
        _moba_k_proj_kernel,
        grid=(t // tm,),
        in_specs=[pl.BlockSpec((tm, D_MODEL), lambda i: (i, 0)),
                  pl.BlockSpec(w_k.shape, lambda i: (0, 0)),
                  pl.BlockSpec((tm, LANES), lambda i: (i % n_pos, 0)),
                  pl.BlockSpec((tm, LANES), lambda i: (i % n_pos, 0))],
        out_specs=pl.BlockSpec((tm, MOBA_WIDTH), lambda i: (i, 0)),
        out_shape=jax.ShapeDtypeStruct((t, MOBA_WIDTH), BF16),
        compiler_params=_cparams(1),
        name="moba_k_proj",
    )(x, w_k, cos128, sin128)


def _moba_qv_proj_kernel(x_ref, wt_ref, cost_ref, sint_ref, o_ref, *, scale):
    y_t = _dot_nt(wt_ref[...], x_ref[...])
    hd = MOBA_HEAD_DIM

    @pl.when(pl.program_id(0) == 0)
    def _():
        cos_t = cost_ref[...]
        sin_t = sint_ref[...]
        for h in range(MOBA_HEADS):
            seg = y_t[h * hd:(h + 1) * hd]
            rot = jnp.concatenate([seg[hd // 2:], seg[:hd // 2]], axis=0)
            o_ref[0, h, 0] = ((seg * cos_t + rot * sin_t) * scale).astype(o_ref.dtype)

    @pl.when(pl.program_id(0) == 1)
    def _():
        for h in range(MOBA_HEADS):
            o_ref[0, h, 0] = y_t[h * hd:(h + 1) * hd].astype(o_ref.dtype)


def _moba_qv_proj(x, w_qv_t, cos128_t, sin128_t, seq):
    t = x.shape[0]
    tm = ATTN_BLOCK
    n_pos = seq // tm
    return pl.pallas_call(
        functools.partial(_moba_qv_proj_kernel, scale=MOBA_HEAD_DIM ** -0.5),
        grid=(2, t // tm),
        in_specs=[pl.BlockSpec((tm, D_MODEL), lambda j, i: (i, 0)),
                  pl.BlockSpec((MOBA_WIDTH, D_MODEL), lambda j, i: (j, 0)),
                  pl.BlockSpec((MOBA_HEAD_DIM, tm), lambda j, i: (0, i % n_pos)),
                  pl.BlockSpec((MOBA_HEAD_DIM, tm), lambda j, i: (0, i % n_pos))],
        out_specs=pl.BlockSpec((1, MOBA_HEADS, 1, MOBA_HEAD_DIM, tm),
                               lambda j, i: (i // n_pos, j, i % n_pos, 0, 0)),
        out_shape=jax.ShapeDtypeStruct((t // seq, 2 * MOBA_HEADS, n_pos, MOBA_HEAD_DIM, tm), BF16),
        compiler_params=_cparams(2),
        name="moba_qv_proj",
    )(x, w_qv_t, cos128_t, sin128_t)


def _moba_attn_kernel(qt_ref, k_ref, vt_ref, avg_ref, o_ref, kmean_ref, *, n_heads):
    j = pl.program_id(2)
    blk = MOBA_BLOCK
    hd = MOBA_HEAD_DIM

    @pl.when(j == 0)
    def _():
        kmean_ref[...] = _dot(avg_ref[...], k_ref[...])

    blk_id = lax.broadcasted_iota(I32, (SUBLANES, blk), 0)
    n_blocks = k_ref.shape[0] // blk
    qs, sels = [], []
    for h in range(n_heads):
        q = qt_ref[0, h, 0]
        gate = jnp.dot(kmean_ref[0:SUBLANES, h * hd:(h + 1) * hd], q.astype(F32),
                       precision=lax.Precision.HIGHEST, preferred_element_type=F32)
        rank = jnp.zeros((SUBLANES, blk), I32)
        for m in range(n_blocks):
            gm = gate[m:m + 1, :]
            beats = (gm > gate) | ((gm == gate) & (m < blk_id))
            rank = rank + jnp.where(beats & (m < j), 1, 0)
        qs.append(q)
        sels.append(jnp.where((rank < MOBA_TOPK) & (blk_id < j), 1.0, 0.0))

    def step(kb, carries, masks):
        ks = pl.multiple_of(kb * blk, blk)
        return _softmax_steps(qs, [k_ref[pl.ds(ks, blk), h * hd:(h + 1) * hd] for h in range(n_heads)],
                              [vt_ref[0, h, kb] for h in range(n_heads)], carries, masks)

    key = lax.broadcasted_iota(I32, (blk, blk), 0)
    qry = lax.broadcasted_iota(I32, (blk, blk), 1)
    carries = step(j, _softmax_init(n_heads, blk, hd), [key <= qry] * n_heads)

    def body(n, c):
        masks = [jnp.max(jnp.where(blk_id == n, sels[h], 0.0), axis=0, keepdims=True) > 0.5
                 for h in range(n_heads)]
        return step(n, c, masks)

    carries = lax.fori_loop(0, j, body, carries)
    _softmax_finish(o_ref, carries, hd)


def _moba_attention(qv_t, k, avg, bsz, seq, n_heads=8):
    t = k.shape[0]
    nq = seq // MOBA_BLOCK
    assert nq <= SUBLANES and MOBA_BLOCK == ATTN_BLOCK
    w = n_heads * MOBA_HEAD_DIM
    groups = MOBA_HEADS // n_heads
    return pl.pallas_call(
        functools.partial(_moba_attn_kernel, n_heads=n_heads),
        grid=(bsz, groups, nq),
        in_specs=[pl.BlockSpec((1, n_heads, 1, MOBA_HEAD_DIM, MOBA_BLOCK), lambda b, h, i: (b, h, i, 0, 0)),
                  pl.BlockSpec((seq, w), lambda b, h, i: (b, h)),
                  pl.BlockSpec((1, n_heads, nq, MOBA_HEAD_DIM, MOBA_BLOCK),
                               lambda b, h, i: (b, groups + h, 0, 0, 0)),
                  pl.BlockSpec(avg.shape, lambda b, h, i: (0, 0))],
        out_specs=pl.BlockSpec((MOBA_BLOCK, w), lambda b, h, i: (b * nq + i, h)),
        out_shape=jax.ShapeDtypeStruct((t, MOBA_WIDTH), BF16),
        scratch_shapes=[pltpu.VMEM((LANES, w), F32)],
        compiler_params=_cparams(3),
        name="moba_attn",
    )(qv_t, k, qv_t, avg)


def _mem_attn_kernel(x_ref, wq_ref, kv_ref, o_ref, *, scale):
    q = (_dot(x_ref[...], wq_ref[...]) * scale).astype(BF16)
    for h in range(MEM_HEADS):
        lo, hi = h * MEM_HEAD_DIM, (h + 1) * MEM_HEAD_DIM
        s = _dot_nt(q[:, lo:hi], kv_ref[:, lo:hi])
        p = jnp.exp(s - jnp.max(s, axis=1, keepdims=True))
        o = _dot(p.astype(BF16), kv_ref[:, MEM_WIDTH + lo:MEM_WIDTH + hi])
        o_ref[:, lo:hi] = (o / jnp.sum(p, axis=1, keepdims=True)).astype(o_ref.dtype)


def _mem_attention(x, w_mq, kv_mem, bsz, seq, mem_len, tm=512):
    t = x.shape[0]
    ns = seq // tm
    return pl.pallas_call(
        functools.partial(_mem_attn_kernel, scale=MEM_HEAD_DIM ** -0.5),
        grid=(bsz, ns),
        in_specs=[pl.BlockSpec((tm, D_MODEL), lambda b, i: (b * ns + i, 0)),
                  pl.BlockSpec(w_mq.shape, lambda b, i: (0, 0)),
                  pl.BlockSpec((mem_len, 2 * MEM_WIDTH), lambda b, i: (b, 0))],
        out_specs=pl.BlockSpec((tm, MEM_WIDTH), lambda b, i: (b * ns + i, 0)),
        out_shape=jax.ShapeDtypeStruct((t, MEM_WIDTH), BF16),
        compiler_params=_cparams(2),
        name="mem_attn",
    )(x, w_mq, kv_mem)


def _gated_sum_kernel(x_ref, oa_ref, ob_ref, oc_ref, om_ref, g0_ref, g1_ref, g2_ref, g3_ref,
                      wa_ref, wb_ref, wc_ref, wm_ref, y_ref):
    xb = x_ref[...]
    acc = None
    for o_ref, g_ref, w_ref in ((oa_ref, g0_ref, wa_ref), (ob_ref, g1_ref, wb_ref),
                                (oc_ref, g2_ref, wc_ref), (om_ref, g3_ref, wm_ref)):
        term = jax.nn.sigmoid(_dot(xb, g_ref[...])) * _dot(o_ref[...], w_ref[...])
        acc = term if acc is None else acc + term
    y_ref[...] = acc.astype(y_ref.dtype)


def _gated_sum(x, o_a, o_b, o_c, o_m, w_gate, w_br_a, w_br_b, w_br_c, w_br_m, tm=1024, tn=256):
    t = x.shape[0]
    nj = D_MODEL // tn
    rows = lambda width: pl.BlockSpec((tm, width), lambda j, i: (i, 0))
    gate = lambda b: pl.BlockSpec((D_MODEL, tn), lambda j, i: (0, b * nj + j))
    cols = lambda width: pl.BlockSpec((width, tn), lambda j, i: (0, j))
    return pl.pallas_call(
        _gated_sum_kernel,
        grid=(nj, t // tm),
        in_specs=[rows(D_MODEL), rows(o_a.shape[1]), rows(o_b.shape[1]), rows(o_c.shape[1]), rows(o_m.shape[1]),
                  gate(0), gate(1), gate(2), gate(3),
                  cols(w_br_a.shape[0]), cols(w_br_b.shape[0]), cols(w_br_c.shape[0]), cols(w_br_m.shape[0])],
        out_specs=pl.BlockSpec((tm, tn), lambda j, i: (i, j)),
        out_shape=jax.ShapeDtypeStruct((t, D_MODEL), BF16),
        compiler_params=_cparams(2),
        name="gated_sum",
    )(x, o_a, o_b, o_c, o_m, w_gate, w_gate, w_gate, w_gate, w_br_a, w_br_b, w_br_c, w_br_m)


def _pack_bf16_pairs(v):
    bits = lax.bitcast_convert_type(v.astype(BF16).astype(F32), U32)
    half = v.shape[1] // 2
    return (bits[:, :half] >> 16) | bits[:, half:]


def _unpack_bf16_pairs(words):
    lo = lax.bitcast_convert_type(words << 16, F32)
    hi = lax.bitcast_convert_type(words & jnp.uint32(0xFFFF0000), F32)
    return lo, hi


def _proj_norm_kernel(y_ref, w_ref, res_ref, g_ref, b_ref, o32_ref, o16_ref, opk_ref):
    v = DEEPNORM_ALPHA * res_ref[...] + _dot(y_ref[...], w_ref[...])
    out = _layer_norm_rows(v, g_ref[...], b_ref[...])
    o32_ref[...] = out
    o16_ref[...] = out.astype(o16_ref.dtype)
    opk_ref[...] = _pack_bf16_pairs(out)


def _proj_norm(y, w_o, res, g, b, tm=256):
    t = y.shape[0]
    full = lambda arr: pl.BlockSpec(arr.shape, lambda i: (0,) * arr.ndim)
    rows = pl.BlockSpec((tm, D_MODEL), lambda i: (i, 0))
    return pl.pallas_call(
        _proj_norm_kernel,
        grid=(t // tm,),
        in_specs=[rows, full(w_o), rows, full(g), full(b)],
        out_specs=[rows, rows, pl.BlockSpec((tm, D_MODEL // 2), lambda i: (i, 0))],
        out_shape=[jax.ShapeDtypeStruct((t, D_MODEL), F32), jax.ShapeDtypeStruct((t, D_MODEL), BF16),
                   jax.ShapeDtypeStruct((t, D_MODEL // 2), U32)],
        compiler_params=_cparams(1),
        name="proj_norm",
    )(y, w_o, res, g, b)


def _router_kernel(x_ref, w_ref, bias_ref, e_ref, wt_ref, rank_ref, cnt_ref, run_ref, *, tm):
    i = pl.program_id(0)

    @pl.when(i == 0)
    def _():
        run_ref[...] = jnp.zeros_like(run_ref)

    logits = jnp.dot(x_ref[...], w_ref[...], precision=lax.Precision.HIGHEST, preferred_element_type=F32)
    scores = jax.nn.sigmoid(logits)
    lane = lax.broadcasted_iota(I32, (tm, LANES), 1)
    biased = jnp.where(lane < N_EXPERTS, scores + bias_ref[...], NEG_BIG)
    picks = []
    chosen = jnp.zeros((tm, LANES), jnp.bool_)
    for _ in range(TOP_K):
        mx = jnp.max(biased, axis=1, keepdims=True)
        idx = jnp.min(jnp.where(biased == mx, lane, LANES), axis=1, keepdims=True)
        hit = lane == idx
        picks.append((idx, hit))
        chosen = chosen | hit
        biased = jnp.where(hit, 2 * NEG_BIG, biased)
    picked_scores = jnp.where(chosen, scores, 0.0)
    norm = ROUTED_SCALE / jnp.sum(picked_scores, axis=1, keepdims=True)

    r = lax.broadcasted_iota(I32, (tm, tm), 0)
    c = lax.broadcasted_iota(I32, (tm, tm), 1)
    strict_lower = jnp.where(c < r, 1.0, 0.0).astype(BF16)
    chosen_f = jnp.where(chosen, 1.0, 0.0)
    arrival = run_ref[0:1, :] + _dot(strict_lower, chosen_f.astype(BF16))
    run_ref[...] = run_ref[...] + jnp.sum(chosen_f, axis=0, keepdims=True)
    cnt_ref[...] = run_ref[...]

    e_out = jnp.zeros((tm, LANES), I32)
    w_out = jnp.zeros((tm, LANES), F32)
    r_out = jnp.zeros((tm, LANES), F32)
    for slot, (idx, hit) in enumerate(picks):
        here = lane == slot
        e_out = jnp.where(here, idx, e_out)
        w_out = jnp.where(here, jnp.sum(jnp.where(hit, scores, 0.0), axis=1, keepdims=True) * norm, w_out)
        r_out = jnp.where(here, jnp.sum(jnp.where(hit, arrival, 0.0), axis=1, keepdims=True), r_out)
    e_ref[...] = e_out
    wt_ref[...] = w_out
    rank_ref[...] = r_out.astype(I32)


def _router(x32, w_router_pad, bias_pad, tm=256):
    t = x32.shape[0]
    full = lambda arr: pl.BlockSpec(arr.shape, lambda i: (0,) * arr.ndim)
    rows = pl.BlockSpec((tm, LANES), lambda i: (i, 0))
    return pl.pallas_call(
        functools.partial(_router_kernel, tm=tm),
        grid=(t // tm,),
        in_specs=[pl.BlockSpec((tm, D_MODEL), lambda i: (i, 0)), full(w_router_pad), full(bias_pad)],
        out_specs=[rows, rows, rows, pl.BlockSpec((8, LANES), lambda i: (0, 0))],
        out_shape=[jax.ShapeDtypeStruct((t, LANES), I32), jax.ShapeDtypeStruct((t, LANES), F32),
                   jax.ShapeDtypeStruct((t, LANES), I32), jax.ShapeDtypeStruct((8, LANES), F32)],
        scratch_shapes=[pltpu.VMEM((8, LANES), F32)],
        compiler_params=_cparams(1),
        name="moe_router",
    )(x32, w_router_pad, bias_pad)


def _dispatch_kernel(fill_start_ref, fill_n_ref, n_used_ref, dest_ref, x_ref, out_hbm, zero_ref, sem, fill_sem,
                     *, tm):
    @pl.when(pl.program_id(0) == 0)
    def _():
        zero_ref[...] = jnp.zeros_like(zero_ref)
        bm = zero_ref.shape[0]
        n_blk = out_hbm.shape[0] // bm

        def fill_copy(dst_row):
            return pltpu.make_async_copy(zero_ref.at[pl.ds(0, 1), :], out_hbm.at[pl.ds(dst_row, 1), :], fill_sem)

        def tail_copy(blk):
            return pltpu.make_async_copy(zero_ref, out_hbm.at[pl.ds(pl.multiple_of(blk * bm, bm), bm), :], fill_sem)

        def start_expert(e, _):
            first = fill_start_ref[e]
            lax.fori_loop(0, fill_n_ref[e], lambda r, c: (fill_copy(first + r).start(), c)[1], 0)
            return 0

        def wait_expert(e, _):
            lax.fori_loop(0, fill_n_ref[e], lambda r, c: (fill_copy(0).wait(), c)[1], 0)
            return 0

        lax.fori_loop(0, N_EXPERTS, start_expert, 0)
        lax.fori_loop(n_used_ref[0], n_blk, lambda b, c: (tail_copy(b).start(), c)[1], 0)
        lax.fori_loop(0, N_EXPERTS, wait_expert, 0)
        lax.fori_loop(n_used_ref[0], n_blk, lambda b, c: (tail_copy(0).wait(), c)[1], 0)

    def issue(tok, _):
        for k in range(TOP_K):
            pltpu.make_async_copy(x_ref.at[pl.ds(tok, 1), :],
                                  out_hbm.at[pl.ds(dest_ref[tok * SLOT_STRIDE + k], 1), :],
                                  sem).start(priority=k % 2)
        return 0

    lax.fori_loop(0, tm, issue, 0)
    for _ in range(TOP_K):
        pltpu.make_async_copy(x_ref, out_hbm.at[pl.ds(0, tm), :], sem).wait()


def _dispatch(x_rows, dest_flat, fill_start, fill_n, n_used, n_rows, tm=256):
    t, width = x_rows.shape
    grid_spec = pltpu.PrefetchScalarGridSpec(
        num_scalar_prefetch=3,
        grid=(t // tm,),
        in_specs=[pl.BlockSpec((tm * SLOT_STRIDE,), lambda i, fs, fn, nu: (i,), memory_space=pltpu.SMEM),
                  pl.BlockSpec((tm, width), lambda i, fs, fn, nu: (i, 0))],
        out_specs=pl.BlockSpec(memory_space=pl.ANY),
        scratch_shapes=[pltpu.VMEM((MOE_ROW_BLOCK, width), x_rows.dtype), pltpu.SemaphoreType.DMA(()),
                        pltpu.SemaphoreType.DMA(())],
    )
    return pl.pallas_call(
        functools.partial(_dispatch_kernel, tm=tm),
        grid_spec=grid_spec,
        out_shape=jax.ShapeDtypeStruct((n_rows, width), x_rows.dtype),
        compiler_params=_cparams(1),
        name="moe_dispatch",
    )(fill_start, fill_n, n_used, dest_flat, x_rows)


def _expert_kernel(blk_e_ref, n_used_ref, xs_ref, wg_ref, wu_ref, wd_ref, ys_ref, wg16_ref, wu16_ref, wd16_ref):
    i = pl.program_id(0)
    used = i < n_used_ref[0]
    new_expert = (i == 0) | (blk_e_ref[i] != blk_e_ref[jnp.maximum(i - 1, 0)])

    @pl.when(used & new_expert)
    def _():
        wg16_ref[...] = wg_ref[0, 0].astype(BF16)
        wu16_ref[...] = wu_ref[0, 0].astype(BF16)
        wd16_ref[...] = wd_ref[0, 0].astype(BF16)

    @pl.when(used)
    def _():
        lo, hi = _unpack_bf16_pairs(xs_ref[...])
        xb = jnp.concatenate([lo.astype(BF16), hi.astype(BF16)], axis=1)
        hid = jax.nn.silu(_dot(xb, wg16_ref[...])) * _dot(xb, wu16_ref[...])
        ys_ref[...] = _pack_bf16_pairs(_dot(hid.astype(BF16), wd16_ref[...]))

    @pl.when(jnp.logical_not(used))
    def _():
        ys_ref[...] = jnp.zeros_like(ys_ref)


def _expert_ffn(xs, blk_e, n_used, w_gate, w_up, w_down, layer):
    n_rows, width = xs.shape
    bm = MOE_ROW_BLOCK
    grid_spec = pltpu.PrefetchScalarGridSpec(
        num_scalar_prefetch=2,
        grid=(n_rows // bm,),
        in_specs=[pl.BlockSpec((bm, width), lambda i, be, nu: (jnp.minimum(i, nu[0] - 1), 0)),
                  pl.BlockSpec((1, 1, D_MODEL, D_EXPERT), lambda i, be, nu: (layer, be[i], 0, 0)),
                  pl.BlockSpec((1, 1, D_MODEL, D_EXPERT), lambda i, be, nu: (layer, be[i], 0, 0)),
                  pl.BlockSpec((1, 1, D_EXPERT, D_MODEL), lambda i, be, nu: (layer, be[i], 0, 0))],
        out_specs=pl.BlockSpec((bm, width), lambda i, be, nu: (i, 0)),
        scratch_shapes=[pltpu.VMEM((D_MODEL, D_EXPERT), BF16), pltpu.VMEM((D_MODEL, D_EXPERT), BF16),
                        pltpu.VMEM((D_EXPERT, D_MODEL), BF16)],
    )
    return pl.pallas_call(
        _expert_kernel,
        grid_spec=grid_spec,
        out_shape=jax.ShapeDtypeStruct((n_rows, width), xs.dtype),
        compiler_params=_cparams(1),
        name="moe_experts",
    )(blk_e, n_used, xs, w_gate, w_up, w_down)


def _combine_kernel(dest_ref, x_ref, wt_ref, ys_hbm, wsg_ref, wsu_ref, wsd_ref, g_ref, b_ref,
                    o32_ref, o16_ref, buf_ref, sem, *, tm):
    def issue(tok, _):
        for k in range(TOP_K):
            pltpu.make_async_copy(ys_hbm.at[pl.ds(dest_ref[tok * SLOT_STRIDE + k], 1), :],
                                  buf_ref.at[k, pl.ds(tok, 1), :], sem).start(priority=k % 2)
        return 0

    lax.fori_loop(0, tm, issue, 0)
    x = x_ref[...]
    xb = x.astype(BF16)
    hid = jax.nn.silu(_dot(xb, wsg_ref[...])) * _dot(xb, wsu_ref[...])
    total = DEEPNORM_ALPHA * x + _dot(hid.astype(BF16), wsd_ref[...])
    for k in range(TOP_K):
        pltpu.make_async_copy(ys_hbm.at[pl.ds(0, tm), :], buf_ref.at[k], sem).wait()
    wt = wt_ref[...]
    routed_lo = routed_hi = None
    for k in range(TOP_K):
        lo, hi = _unpack_bf16_pairs(buf_ref[k])
        w_k = wt[:, k:k + 1]
        routed_lo = w_k * lo if routed_lo is None else routed_lo + w_k * lo
        routed_hi = w_k * hi if routed_hi is None else routed_hi + w_k * hi
    total = total + jnp.concatenate([routed_lo, routed_hi], axis=1)
    out = _layer_norm_rows(total, g_ref[...], b_ref[...])
    o32_ref[...] = out
    o16_ref[...] = out.astype(o16_ref.dtype)


def _combine(x32, dest_flat, wts, ys, w_sg, w_su, w_sd, g, b, tm=256):
    t = x32.shape[0]
    full = lambda arr: pl.BlockSpec(arr.shape, lambda i: (0,) * arr.ndim)
    rows = pl.BlockSpec((tm, D_MODEL), lambda i: (i, 0))
    return pl.pallas_call(
        functools.partial(_combine_kernel, tm=tm),
        grid=(t // tm,),
        in_specs=[pl.BlockSpec((tm * SLOT_STRIDE,), lambda i: (i,), memory_space=pltpu.SMEM),
                  rows,
                  pl.BlockSpec((tm, LANES), lambda i: (i, 0)),
                  pl.BlockSpec(memory_space=pl.ANY),
                  full(w_sg), full(w_su), full(w_sd), full(g), full(b)],
        out_specs=[rows, rows],
        out_shape=[jax.ShapeDtypeStruct((t, D_MODEL), F32), jax.ShapeDtypeStruct((t, D_MODEL), BF16)],
        scratch_shapes=[pltpu.VMEM((TOP_K, tm, ys.shape[1]), ys.dtype), pltpu.SemaphoreType.DMA(())],
        compiler_params=_cparams(1),
        name="moe_combine",
    )(dest_flat, x32, wts, ys, w_sg, w_su, w_sd, g, b)


def _rot_half_cols(w):
    half = w.shape[-1] // 2
    return jnp.concatenate([-w[..., half:], w[..., :half]], axis=-1)


def _pad_cols(w, width):
    return jnp.pad(w, [(0, 0)] * (w.ndim - 1) + [(0, width - w.shape[-1])])


def _rope_tables(seq):
    pos = jnp.arange(seq, dtype=F32)[:, None]

    def table(dim):
        half = dim // 2
        inv_freq = ROPE_THETA ** (-jnp.arange(half, dtype=F32) / half)
        ang = pos * inv_freq[None, :]
        return jnp.cos(ang), jnp.sin(ang)

    c64, s64 = table(MLA_ROPE_DIM)
    cos64 = _pad_cols(jnp.concatenate([c64, c64], axis=1), LANES)
    sin64 = _pad_cols(jnp.concatenate([s64, s64], axis=1), LANES)
    c128, s128 = table(MOBA_HEAD_DIM)
    cos128 = jnp.concatenate([c128, c128], axis=1)
    sin128 = jnp.concatenate([-s128, s128], axis=1)
    return cos64, sin64, cos128, sin128


def _layer(x32, x16, mem16, tables, avg, bsz, seq, mem_len, p, expert_weights, layer):
    (w_in, g_qa, w_q_up, g_kva, w_kv_up, sgu_ln_g, sgu_ln_b, w_spatial, b_spatial, w_mem_kv,
     w_br_a, w_br_b, w_br_c, w_br_m, w_o, ln1_g, ln1_b, w_router, router_bias,
     w_sh_gate, w_sh_up, w_sh_down, ln2_g, ln2_b) = p
    cos64, sin64, cos128, sin128 = tables
    t = x32.shape[0]
    row = lambda v: v.reshape(1, -1).astype(F32)

    c0 = MLA_Q_RANK + MLA_KV_RANK
    c1 = c0 + MLA_ROPE_DIM
    c2 = c1 + 2 * SGU_WIDTH
    c3 = c2 + 3 * MOBA_WIDTH
    c4 = c3 + MEM_WIDTH
    w_a = w_in[:, :c0].astype(BF16)
    w_kpe_raw = w_in[:, c0:c1]
    w_kpe = jnp.concatenate([_pad_cols(w_kpe_raw, LANES), _pad_cols(_rot_half_cols(w_kpe_raw), LANES)],
                            axis=1).astype(BF16)
    w_z = w_in[:, c1:c2].astype(BF16)
    w_c = w_in[:, c2:c3].astype(BF16)
    w_mq = w_in[:, c3:c4].astype(BF16)
    w_g = w_in[:, c4:].astype(BF16)

    w_qn = w_q_up[:, :, :MLA_NOPE_DIM].reshape(MLA_Q_RANK, -1).astype(BF16)
    w_qpe = w_q_up[:, :, MLA_NOPE_DIM:]
    w_qp = _pad_cols(w_qpe, LANES).reshape(MLA_Q_RANK, -1).astype(BF16)
    w_qpr = _pad_cols(_rot_half_cols(w_qpe), LANES).reshape(MLA_Q_RANK, -1).astype(BF16)
    w_kn = w_kv_up[:, :, :MLA_NOPE_DIM].reshape(MLA_KV_RANK, -1).astype(BF16)
    w_v = w_kv_up[:, :, MLA_NOPE_DIM:].reshape(MLA_KV_RANK, -1).T.astype(BF16)
    qt_a, k_a, vt_a = _mla_proj(x16, w_a, w_kpe, row(g_qa), row(g_kva), w_qn.T, w_qp.T, w_qpr.T, w_kn, w_v,
                                cos64, sin64, seq)
    o_a = _mla_attention(qt_a, k_a, vt_a, bsz, seq)

    bias_full = jnp.repeat(b_spatial.T.astype(F32), SGU_GROUP_DIM, axis=1)
    o_b = _sgu(x16, w_z, row(sgu_ln_g), row(sgu_ln_b), w_spatial.astype(F32), bias_full)

    k_c = _moba_k_proj(x16, w_c[:, MOBA_WIDTH:2 * MOBA_WIDTH], cos128, sin128, seq)
    w_qv_t = jnp.concatenate([w_c[:, :MOBA_WIDTH], w_c[:, 2 * MOBA_WIDTH:]], axis=1).T
    qvt_c = _moba_qv_proj(x16, w_qv_t, cos128.T, sin128.T, seq)
    o_c = _moba_attention(qvt_c, k_c, avg, bsz, seq)

    kv_mem = _matmul(mem16, w_mem_kv.astype(BF16), mem_len, 2 * MEM_WIDTH, BF16, "mem_kv")
    o_m = _mem_attention(x16, w_mq, kv_mem, bsz, seq, mem_len)

    y = _gated_sum(x16, o_a, o_b, o_c, o_m, w_g, w_br_a.astype(BF16), w_br_b.astype(BF16),
                   w_br_c.astype(BF16), w_br_m.astype(BF16))
    x32, x16, x_packed = _proj_norm(y, w_o.astype(BF16), x32, row(ln1_g), row(ln1_b))

    e_idx, wts, rank, counts = _router(x32, _pad_cols(w_router.astype(F32), LANES),
                                       _pad_cols(row(router_bias), LANES))
    bm = MOE_ROW_BLOCK
    n_blk = (t * TOP_K) // bm + N_EXPERTS
    counts = counts[0, :N_EXPERTS].astype(I32)
    padded = (counts + bm - 1) // bm * bm
    pad_ends = jnp.cumsum(padded)
    pad_starts = pad_ends - padded
    dest = _pad_cols((pad_starts[e_idx[:, :TOP_K]] + rank[:, :TOP_K]).astype(I32), SLOT_STRIDE).reshape(-1)
    blk_first = jnp.arange(n_blk, dtype=I32)[:, None] * bm
    blk_e = jnp.minimum(jnp.sum((pad_ends[None, :] <= blk_first).astype(I32), axis=1), N_EXPERTS - 1)
    n_used = (pad_ends[-1:] // bm).astype(I32)
    xs = _dispatch(x_packed, dest, pad_starts + counts, padded - counts, n_used, n_blk * bm)
    ys = _expert_ffn(xs, blk_e, n_used, *expert_weights, layer)
    return _combine(x32, dest, wts, ys, w_sh_gate.astype(BF16), w_sh_up.astype(BF16),
                    w_sh_down.astype(BF16), row(ln2_g), row(ln2_b))


def kernel(x, mem, w_in, g_qa, w_q_up, g_kva, w_kv_up, sgu_ln_g, sgu_ln_b, w_spatial, b_spatial, w_mem_kv, w_br_a, w_br_b, w_br_c, w_br_m, w_o, ln1_g, ln1_b, w_router, router_bias, w_exp_gate, w_exp_up, w_exp_down, w_sh_gate, w_sh_up, w_sh_down, ln2_g, ln2_b):
    bsz, seq, d = x.shape
    mem_len = mem.shape[1]
    params = (w_in, g_qa, w_q_up, g_kva, w_kv_up, sgu_ln_g, sgu_ln_b, w_spatial, b_spatial, w_mem_kv,
              w_br_a, w_br_b, w_br_c, w_br_m, w_o, ln1_g, ln1_b, w_router, router_bias,
              w_sh_gate, w_sh_up, w_sh_down, ln2_g, ln2_b)
    expert_weights = (w_exp_gate, w_exp_up, w_exp_down)
    tables = _rope_tables(seq)
    n_blocks = seq // MOBA_BLOCK
    blk_of_pos = jnp.arange(seq, dtype=I32)[None, :] // MOBA_BLOCK
    avg = jnp.where(blk_of_pos == jnp.arange(LANES, dtype=I32)[:, None], 1.0 / MOBA_BLOCK, 0.0).astype(BF16)
    assert n_blocks <= LANES
    x32 = x.reshape(bsz * seq, d)
    x16 = x32.astype(BF16)
    mem16 = mem.reshape(bsz * mem_len, d).astype(BF16)
    for l in range(DEPTH):
        x32, x16 = _layer(x32, x16, mem16, tables, avg, bsz, seq, mem_len, tuple(w[l] for w in params),
                          expert_weights, l)
    return x32.reshape(bsz, seq, d)
```

```python
import functools

import jax
import jax.numpy as jnp
from jax import lax
from jax.experimental import pallas as pl
from jax.experimental.pallas import tpu as pltpu

F32 = jnp.float32
BF16 = jnp.bfloat16
I32 = jnp.int32
U32 = jnp.uint32

D_MODEL = 2048
DEPTH = 2
MLA_HEADS = 8
MLA_Q_RANK = 512
MLA_KV_RANK = 256
MLA_NOPE_DIM = 128
MLA_ROPE_DIM = 64
MLA_V_DIM = 128
SGU_GROUPS = 8
SGU_GROUP_DIM = 128
SGU_WIDTH = SGU_GROUPS * SGU_GROUP_DIM
SGU_CHUNK = 128
MOBA_HEADS = 8
MOBA_HEAD_DIM = 128
MOBA_WIDTH = MOBA_HEADS * MOBA_HEAD_DIM
MOBA_BLOCK = 256
MOBA_TOPK = 3
MEM_HEADS = 4
MEM_HEAD_DIM = 128
MEM_WIDTH = MEM_HEADS * MEM_HEAD_DIM
N_BRANCH = 4
N_EXPERTS = 64
TOP_K = 6
D_EXPERT = 512
ROUTED_SCALE = 2.5
ROPE_THETA = 10000.0
DEEPNORM_ALPHA = (2 * DEPTH) ** 0.25

LANES = 128
SUBLANES = 8
SUBLANE_SHIFT = 3
ATTN_BLOCK = 256
MOE_ROW_BLOCK = 512
SLOT_STRIDE = 8
ROW_WORDS = D_MODEL // 2
ROW_SLAB = ROW_WORDS // LANES
NEG_BIG = -1e30
VMEM_LIMIT = 56 * 1024 * 1024


def _cparams(n_axes):
    return pltpu.CompilerParams(dimension_semantics=("arbitrary",) * n_axes,
                                vmem_limit_bytes=VMEM_LIMIT)


def _dot(a, b):
    return jnp.dot(a, b, preferred_element_type=F32)


def _dot_nt(a, b):
    return lax.dot_general(a, b, (((1,), (1,)), ((), ())), preferred_element_type=F32)


def _layer_norm_rows(v, g, b, eps=1e-5):
    mu = jnp.mean(v, axis=-1, keepdims=True)
    vc = v - mu
    var = jnp.mean(vc * vc, axis=-1, keepdims=True)
    return vc * lax.rsqrt(var + eps) * g + b


def _rms_norm_rows(v, g, eps=1e-6):
    return v * lax.rsqrt(jnp.mean(v * v, axis=-1, keepdims=True) + eps) * g


def _mm_kernel(x_ref, w_ref, o_ref):
    o_ref[...] = _dot(x_ref[...], w_ref[...]).astype(o_ref.dtype)


def _matmul(x, w, tm, tn, out_dtype, name):
    m, k = x.shape
    n = w.shape[1]
    return pl.pallas_call(
        _mm_kernel,
        grid=(n // tn, m // tm),
        in_specs=[pl.BlockSpec((tm, k), lambda j, i: (i, 0)),
                  pl.BlockSpec((k, tn), lambda j, i: (0, j))],
        out_specs=pl.BlockSpec((tm, tn), lambda j, i: (i, j)),
        out_shape=jax.ShapeDtypeStruct((m, n), out_dtype),
        compiler_params=_cparams(2),
        name=name,
    )(x, w)


def _mla_proj_kernel(x_ref, wa_ref, wkpe_ref, gq_ref, gkv_ref, wqnt_ref, wqpt_ref, wqprt_ref,
                     wkn_ref, wvt_ref, cos_ref, sin_ref, cost_ref, sint_ref, qt_ref, k_ref, vt_ref, *, scale):
    xb = x_ref[...]
    a = _dot(xb, wa_ref[...])
    cq = _rms_norm_rows(a[:, :MLA_Q_RANK], gq_ref[...]).astype(BF16)
    ckv = _rms_norm_rows(a[:, MLA_Q_RANK:], gkv_ref[...]).astype(BF16)
    qn_t = _dot_nt(wqnt_ref[...], cq)
    qp_t = _dot_nt(wqpt_ref[...], cq)
    qpr_t = _dot_nt(wqprt_ref[...], cq)
    v_t = _dot_nt(wvt_ref[...], ckv)
    cos_t = cost_ref[...]
    sin_t = sint_ref[...]
    for h in range(MLA_HEADS):
        lo, hi = h * LANES, (h + 1) * LANES
        qt_ref[0, h, 0, 0:LANES, :] = (qn_t[lo:hi] * scale).astype(qt_ref.dtype)
        qt_ref[0, h, 0, LANES:2 * LANES, :] = ((qp_t[lo:hi] * cos_t + qpr_t[lo:hi] * sin_t) * scale).astype(qt_ref.dtype)
        vt_ref[0, h, 0] = v_t[h * MLA_V_DIM:(h + 1) * MLA_V_DIM, :].astype(vt_ref.dtype)
    kn = _dot(ckv, wkn_ref[...])
    kp = _dot(xb, wkpe_ref[...])
    kpe = (kp[:, :LANES] * cos_ref[...] + kp[:, LANES:] * sin_ref[...]).astype(k_ref.dtype)
    for h in range(MLA_HEADS):
        lo, hi = h * LANES, (h + 1) * LANES
        k_ref[:, 2 * lo:2 * lo + LANES] = kn[:, lo:hi].astype(k_ref.dtype)
        k_ref[:, 2 * lo + LANES:2 * hi] = kpe


def _mla_proj(x, wa, wkpe, gq, gkv, wqnt, wqpt, wqprt, wkn, wvt, cos64, sin64, seq):
    t = x.shape[0]
    tm = ATTN_BLOCK
    full = lambda arr: pl.BlockSpec(arr.shape, lambda i: (0,) * arr.ndim)
    n_pos = seq // tm
    scale = (MLA_NOPE_DIM + MLA_ROPE_DIM) ** -0.5
    dk = 2 * LANES
    feat_major = lambda d: pl.BlockSpec((1, MLA_HEADS, 1, d, tm), lambda i: (i // n_pos, 0, i % n_pos, 0, 0))
    return pl.pallas_call(
        functools.partial(_mla_proj_kernel, scale=scale),
        grid=(t // tm,),
        in_specs=[pl.BlockSpec((tm, D_MODEL), lambda i: (i, 0)),
                  full(wa), full(wkpe), full(gq), full(gkv), full(wqnt), full(wqpt), full(wqprt),
                  full(wkn), full(wvt),
                  pl.BlockSpec((tm, LANES), lambda i: (i % n_pos, 0)),
                  pl.BlockSpec((tm, LANES), lambda i: (i % n_pos, 0)),
                  pl.BlockSpec((LANES, tm), lambda i: (0, i % n_pos)),
                  pl.BlockSpec((LANES, tm), lambda i: (0, i % n_pos))],
        out_specs=[feat_major(dk),
                   pl.BlockSpec((tm, MLA_HEADS * dk), lambda i: (i, 0)),
                   feat_major(MLA_V_DIM)],
        out_shape=[jax.ShapeDtypeStruct((t // seq, MLA_HEADS, n_pos, dk, tm), BF16),
                   jax.ShapeDtypeStruct((t, MLA_HEADS * dk), BF16),
                   jax.ShapeDtypeStruct((t // seq, MLA_HEADS, n_pos, MLA_V_DIM, tm), BF16)],
        compiler_params=_cparams(1),
        name="mla_proj",
    )(x, wa, wkpe, gq, gkv, wqnt, wqpt, wqprt, wkn, wvt, cos64, sin64, cos64.T, sin64.T)


def _softmax_steps(q_ts, ks, v_ts, carries, masks):
    heads = range(len(q_ts))
    s_ts = [_dot(ks[h], q_ts[h]) for h in heads]
    s_ts = [s_ts[h] if masks[h] is None else jnp.where(masks[h], s_ts[h], NEG_BIG) for h in heads]
    m_news = [jnp.maximum(carries[h][0], jnp.max(s_ts[h], axis=0, keepdims=True)) for h in heads]
    p_ts = [jnp.exp(s_ts[h] - m_news[h]) for h in heads]
    pvs = [_dot(v_ts[h], p_ts[h].astype(v_ts[h].dtype)) for h in heads]
    out = []
    for h in heads:
        m_i, l_i, acc_t = carries[h]
        alpha = jnp.exp(m_i - m_news[h])
        l_new = alpha * l_i + jnp.sum(p_ts[h], axis=0, keepdims=True)
        out.append((m_news[h], l_new, alpha * acc_t + pvs[h]))
    return tuple(out)


def _softmax_init(n_heads, tq, dv):
    return tuple((jnp.full((1, tq), NEG_BIG, F32), jnp.zeros((1, tq), F32), jnp.zeros((dv, tq), F32))
                 for _ in range(n_heads))


def _softmax_finish(o_ref, carries, dv):
    for h, (_, l_i, acc_t) in enumerate(carries):
        o_ref[:, h * dv:(h + 1) * dv] = (acc_t / l_i).T.astype(o_ref.dtype)


def _causal_attn_kernel(qt_ref, k_ref, vt_ref, o_ref, *, tq, n_heads, dk, dv):
    qi = pl.program_id(2)
    key = lax.broadcasted_iota(I32, (tq, tq), 0)
    qry = lax.broadcasted_iota(I32, (tq, tq), 1)
    qs = [qt_ref[0, h, 0] for h in range(n_heads)]

    def step(kb, carries, mask):
        ks = pl.multiple_of(kb * tq, tq)
        return _softmax_steps(qs, [k_ref[pl.ds(ks, tq), h * dk:(h + 1) * dk] for h in range(n_heads)],
                              [vt_ref[0, h, kb] for h in range(n_heads)], carries, [mask] * n_heads)

    carries = step(qi, _softmax_init(n_heads, tq, dv), key <= qry)
    carries = lax.fori_loop(0, qi, lambda kb, c: step(kb, c, None), carries)
    _softmax_finish(o_ref, carries, dv)


def _mla_attention(q_t, k_a, v_t, bsz, seq, n_heads=8):
    t = k_a.shape[0]
    tq = ATTN_BLOCK
    nq = seq // tq
    dk = 2 * LANES
    dv = MLA_V_DIM
    return pl.pallas_call(
        functools.partial(_causal_attn_kernel, tq=tq, n_heads=n_heads, dk=dk, dv=dv),
        grid=(bsz, MLA_HEADS // n_heads, nq),
        in_specs=[pl.BlockSpec((1, n_heads, 1, dk, tq), lambda b, h, i: (b, h, i, 0, 0)),
                  pl.BlockSpec((seq, n_heads * dk), lambda b, h, i: (b, h)),
                  pl.BlockSpec((1, n_heads, nq, dv, tq), lambda b, h, i: (b, h, 0, 0, 0))],
        out_specs=pl.BlockSpec((tq, n_heads * dv), lambda b, h, i: (b * nq + i, h)),
        out_shape=jax.ShapeDtypeStruct((t, MLA_HEADS * dv), BF16),
        compiler_params=_cparams(3),
        name="mla_attn",
    )(q_t, k_a, v_t)


def _sgu_kernel(x_ref, wz_ref, g_ref, b_ref, ws_ref, bs_ref, o_ref, *, tm):
    z = jax.nn.gelu(_dot(x_ref[...], wz_ref[...]))
    u = z[:, :SGU_WIDTH]
    v = _layer_norm_rows(z[:, SGU_WIDTH:], g_ref[...], b_ref[...]).astype(BF16)
    row = lax.broadcasted_iota(I32, (SGU_CHUNK, SGU_CHUNK), 0)
    col = lax.broadcasted_iota(I32, (SGU_CHUNK, SGU_CHUNK), 1)
    bias = bs_ref[...]
    for g in range(SGU_GROUPS):
        w = jnp.where(col <= row, ws_ref[g], 0.0).astype(BF16)
        lo, hi = g * SGU_GROUP_DIM, (g + 1) * SGU_GROUP_DIM
        for c in range(tm // SGU_CHUNK):
            r0, r1 = c * SGU_CHUNK, (c + 1) * SGU_CHUNK
            mixed = _dot(w, v[r0:r1, lo:hi]) + bias[:, lo:hi]
            o_ref[r0:r1, lo:hi] = (u[r0:r1, lo:hi] * mixed).astype(o_ref.dtype)


def _sgu(x, wz, ln_g, ln_b, w_s, bias_full, tm=256):
    t = x.shape[0]
    full = lambda arr: pl.BlockSpec(arr.shape, lambda i: (0,) * arr.ndim)
    return pl.pallas_call(
        functools.partial(_sgu_kernel, tm=tm),
        grid=(t // tm,),
        in_specs=[pl.BlockSpec((tm, D_MODEL), lambda i: (i, 0)),
                  full(wz), full(ln_g), full(ln_b), full(w_s), full(bias_full)],
        out_specs=pl.BlockSpec((tm, SGU_WIDTH), lambda i: (i, 0)),
        out_shape=jax.ShapeDtypeStruct((t, SGU_WIDTH), BF16),
        compiler_params=_cparams(1),
        name="sgu",
    )(x, wz, ln_g, ln_b, w_s, bias_full)


def _moba_k_proj_kernel(x_ref, w_ref, cos_ref, sin_ref, o_ref):
    acc = _dot(x_ref[...], w_ref[...])
    cos = cos_ref[...]
    sin = sin_ref[...]
    for h in range(MOBA_HEADS):
        seg = acc[:, h * LANES:(h + 1) * LANES]
        rot = pltpu.roll(seg, MOBA_HEAD_DIM // 2, axis=1)
        o_ref[:, h * LANES:(h + 1) * LANES] = (seg * cos + rot * sin).astype(o_ref.dtype)


def _moba_k_proj(x, w_k, cos128, sin128, seq, tm=512):
    t = x.shape[0]
    n_pos = seq // tm
    return pl.pallas_call(
        _moba_k_proj_kernel,
        grid=(t // tm,),
        in_specs=[pl.BlockSpec((tm, D_MODEL), lambda i: (i, 0)),
                  pl.BlockSpec(w_k.shape, lambda i: (0, 0)),
                  pl.BlockSpec((tm, LANES), lambda i: (i % n_pos, 0)),
                  pl.BlockSpec((tm, LANES), lambda i: (i % n_pos, 0))],
        out_specs=pl.BlockSpec((tm, MOBA_WIDTH), lambda i: (i, 0)),
        out_shape=jax.ShapeDtypeStruct((t, MOBA_WIDTH), BF16),
        compiler_params=_cparams(1),
        name="moba_k_proj",
    )(x, w_k, cos128, sin128)


def _moba_qv_proj_kernel(x_ref, wt_ref, cost_ref, sint_ref, o_ref, *, scale):
    y_t = _dot_nt(wt_ref[...], x_ref[...])
    hd = MOBA_HEAD_DIM

    @pl.when(pl.program_id(0) == 0)
    def _():
        cos_t = cost_ref[...]
        sin_t = sint_ref[...]
        for h in range(MOBA_HEADS):
            seg = y_t[h * hd:(h + 1) * hd]
            rot = jnp.concatenate([seg[hd // 2:], seg[:hd // 2]], axis=0)
            o_ref[0, h, 0] = ((seg * cos_t + rot * sin_t) * scale).astype(o_ref.dtype)

    @pl.when(pl.program_id(0) == 1)
    def _():
        for h in range(MOBA_HEADS):
            o_ref[0, h, 0] = y_t[h * hd:(h + 1) * hd].astype(o_ref.dtype)


def _moba_qv_proj(x, w_qv_t, cos128_t, sin128_t, seq):
    t = x.shape[0]
    tm = ATTN_BLOCK
    n_pos = seq // tm
    return pl.pallas_call(
        functools.partial(_moba_qv_proj_kernel, scale=MOBA_HEAD_DIM ** -0.5),
        grid=(2, t // tm),
        in_specs=[pl.BlockSpec((tm, D_MODEL), lambda j, i: (i, 0)),
                  pl.BlockSpec((MOBA_WIDTH, D_MODEL), lambda j, i: (j, 0)),
                  pl.BlockSpec((MOBA_HEAD_DIM, tm), lambda j, i: (0, i % n_pos)),
                  pl.BlockSpec((MOBA_HEAD_DIM, tm), lambda j, i: (0, i % n_pos))],
        out_specs=pl.BlockSpec((1, MOBA_HEADS, 1, MOBA_HEAD_DIM, tm),
                               lambda j, i: (i // n_pos, j, i % n_pos, 0, 0)),
        out_shape=jax.ShapeDtypeStruct((t // seq, 2 * MOBA_HEADS, n_pos, MOBA_HEAD_DIM, tm), BF16),
        compiler_params=_cparams(2),
        name="moba_qv_proj",
    )(x, w_qv_t, cos128_t, sin128_t)


def _moba_attn_kernel(qt_ref, k_ref, vt_ref, avg_ref, o_ref, kmean_ref, *, n_heads):
    j = pl.program_id(2)
    blk = MOBA_BLOCK
    hd = MOBA_HEAD_DIM

    @pl.when(j == 0)
    def _():
        kmean_ref[...] = _dot(avg_ref[...], k_ref[...])

    blk_id = lax.broadcasted_iota(I32, (SUBLANES, blk), 0)
    n_blocks = k_ref.shape[0] // blk
    qs, sels = [], []
    for h in range(n_heads):
        q = qt_ref[0, h, 0]
        gate = jnp.dot(kmean_ref[0:SUBLANES, h * hd:(h + 1) * hd], q.astype(F32),
                       precision=lax.Precision.HIGHEST, preferred_element_type=F32)
        rank = jnp.zeros((SUBLANES, blk), I32)
        for m in range(n_blocks):
            gm = gate[m:m + 1, :]
            beats = (gm > gate) | ((gm == gate) & (m < blk_id))
            rank = rank + jnp.where(beats & (m < j), 1, 0)
        qs.append(q)
        sels.append(jnp.where((rank < MOBA_TOPK) & (blk_id < j), 1.0, 0.0))

    def step(kb, carries, masks):
        ks = pl.multiple_of(kb * blk, blk)
        return _softmax_steps(qs, [k_ref[pl.ds(ks, blk), h * hd:(h + 1) * hd] for h in range(n_heads)],
                              [vt_ref[0, h, kb] for h in range(n_heads)], carries, masks)

    key = lax.broadcasted_iota(I32, (blk, blk), 0)
    qry = lax.broadcasted_iota(I32, (blk, blk), 1)
    carries = step(j, _softmax_init(n_heads, blk, hd), [key <= qry] * n_heads)

    def body(n, c):
        masks = [jnp.max(jnp.where(blk_id == n, sels[h], 0.0), axis=0, keepdims=True) > 0.5
                 for h in range(n_heads)]
        return step(n, c, masks)

    carries = lax.fori_loop(0, j, body, carries)
    _softmax_finish(o_ref, carries, hd)


def _moba_attention(qv_t, k, avg, bsz, seq, n_heads=8):
    t = k.shape[0]
    nq = seq // MOBA_BLOCK
    assert nq <= SUBLANES and MOBA_BLOCK == ATTN_BLOCK
    w = n_heads * MOBA_HEAD_DIM
    groups = MOBA_HEADS // n_heads
    return pl.pallas_call(
        functools.partial(_moba_attn_kernel, n_heads=n_heads),
        grid=(bsz, groups, nq),
        in_specs=[pl.BlockSpec((1, n_heads, 1, MOBA_HEAD_DIM, MOBA_BLOCK), lambda b, h, i: (b, h, i, 0, 0)),
                  pl.BlockSpec((seq, w), lambda b, h, i: (b, h)),
                  pl.BlockSpec((1, n_heads, nq, MOBA_HEAD_DIM, MOBA_BLOCK),
                               lambda b, h, i: (b, groups + h, 0, 0, 0)),
                  pl.BlockSpec(avg.shape, lambda b, h, i: (0, 0))],
        out_specs=pl.BlockSpec((MOBA_BLOCK, w), lambda b, h, i: (b * nq + i, h)),
        out_shape=jax.ShapeDtypeStruct((t, MOBA_WIDTH), BF16),
        scratch_shapes=[pltpu.VMEM((LANES, w), F32)],
        compiler_params=_cparams(3),
        name="moba_attn",
    )(qv_t, k, qv_t, avg)


def _mem_attn_kernel(x_ref, wq_ref, kv_ref, o_ref, *, scale):
    q = (_dot(x_ref[...], wq_ref[...]) * scale).astype(BF16)
    for h in range(MEM_HEADS):
        lo, hi = h * MEM_HEAD_DIM, (h + 1) * MEM_HEAD_DIM
        s = _dot_nt(q[:, lo:hi], kv_ref[:, lo:hi])
        p = jnp.exp(s - jnp.max(s, axis=1, keepdims=True))
        o = _dot(p.astype(BF16), kv_ref[:, MEM_WIDTH + lo:MEM_WIDTH + hi])
        o_ref[:, lo:hi] = (o / jnp.sum(p, axis=1, keepdims=True)).astype(o_ref.dtype)


def _mem_attention(x, w_mq, kv_mem, bsz, seq, mem_len, tm=512):
    t = x.shape[0]
    ns = seq // tm
    return pl.pallas_call(
        functools.partial(_mem_attn_kernel, scale=MEM_HEAD_DIM ** -0.5),
        grid=(bsz, ns),
        in_specs=[pl.BlockSpec((tm, D_MODEL), lambda b, i: (b * ns + i, 0)),
                  pl.BlockSpec(w_mq.shape, lambda b, i: (0, 0)),
                  pl.BlockSpec((mem_len, 2 * MEM_WIDTH), lambda b, i: (b, 0))],
        out_specs=pl.BlockSpec((tm, MEM_WIDTH), lambda b, i: (b * ns + i, 0)),
        out_shape=jax.ShapeDtypeStruct((t, MEM_WIDTH), BF16),
        compiler_params=_cparams(2),
        name="mem_attn",
    )(x, w_mq, kv_mem)


def _gated_sum_kernel(x_ref, oa_ref, ob_ref, oc_ref, om_ref, g0_ref, g1_ref, g2_ref, g3_ref,
                      wa_ref, wb_ref, wc_ref, wm_ref, y_ref):
    xb = x_ref[...]
    acc = None
    for o_ref, g_ref, w_ref in ((oa_ref, g0_ref, wa_ref), (ob_ref, g1_ref, wb_ref),
                                (oc_ref, g2_ref, wc_ref), (om_ref, g3_ref, wm_ref)):
        term = jax.nn.sigmoid(_dot(xb, g_ref[...])) * _dot(o_ref[...], w_ref[...])
        acc = term if acc is None else acc + term
    y_ref[...] = acc.astype(y_ref.dtype)


def _gated_sum(x, o_a, o_b, o_c, o_m, w_gate, w_br_a, w_br_b, w_br_c, w_br_m, tm=1024, tn=256):
    t = x.shape[0]
    nj = D_MODEL // tn
    rows = lambda width: pl.BlockSpec((tm, width), lambda j, i: (i, 0))
    gate = lambda b: pl.BlockSpec((D_MODEL, tn), lambda j, i: (0, b * nj + j))
    cols = lambda width: pl.BlockSpec((width, tn), lambda j, i: (0, j))
    return pl.pallas_call(
        _gated_sum_kernel,
        grid=(nj, t // tm),
        in_specs=[rows(D_MODEL), rows(o_a.shape[1]), rows(o_b.shape[1]), rows(o_c.shape[1]), rows(o_m.shape[1]),
                  gate(0), gate(1), gate(2), gate(3),
                  cols(w_br_a.shape[0]), cols(w_br_b.shape[0]), cols(w_br_c.shape[0]), cols(w_br_m.shape[0])],
        out_specs=pl.BlockSpec((tm, tn), lambda j, i: (i, j)),
        out_shape=jax.ShapeDtypeStruct((t, D_MODEL), BF16),
        compiler_params=_cparams(2),
        name="gated_sum",
    )(x, o_a, o_b, o_c, o_m, w_gate, w_gate, w_gate, w_gate, w_br_a, w_br_b, w_br_c, w_br_m)


def _pack_bf16_pairs(v):
    bits = lax.bitcast_convert_type(v.astype(BF16).astype(F32), U32)
    half = v.shape[1] // 2
    return (bits[:, :half] >> 16) | bits[:, half:]


def _unpack_bf16_pairs(words):
    lo = lax.bitcast_convert_type(words << 16, F32)
    hi = lax.bitcast_convert_type(words & jnp.uint32(0xFFFF0000), F32)
    return lo, hi


def _store_row_slabs(ref, first_row, words):
    for c in range(ROW_SLAB):
        ref[pl.ds(first_row * ROW_SLAB + c, words.shape[0], stride=ROW_SLAB), :] = words[:, c * LANES:(c + 1) * LANES]


def _load_row_slabs(ref, first_row, n_rows):
    return jnp.concatenate([ref[pl.ds(first_row * ROW_SLAB + c, n_rows, stride=ROW_SLAB), :]
                            for c in range(ROW_SLAB)], axis=1)


def _proj_norm_kernel(y_ref, w_ref, res_ref, g_ref, b_ref, o32_ref, o16_ref, opk_ref):
    v = DEEPNORM_ALPHA * res_ref[...] + _dot(y_ref[...], w_ref[...])
    out = _layer_norm_rows(v, g_ref[...], b_ref[...])
    o32_ref[...] = out
    o16_ref[...] = out.astype(o16_ref.dtype)
    _store_row_slabs(opk_ref, 0, _pack_bf16_pairs(out))


def _proj_norm(y, w_o, res, g, b, tm=256):
    t = y.shape[0]
    full = lambda arr: pl.BlockSpec(arr.shape, lambda i: (0,) * arr.ndim)
    rows = pl.BlockSpec((tm, D_MODEL), lambda i: (i, 0))
    return pl.pallas_call(
        _proj_norm_kernel,
        grid=(t // tm,),
        in_specs=[rows, full(w_o), rows, full(g), full(b)],
        out_specs=[rows, rows, pl.BlockSpec((tm * ROW_SLAB, LANES), lambda i: (i, 0))],
        out_shape=[jax.ShapeDtypeStruct((t, D_MODEL), F32), jax.ShapeDtypeStruct((t, D_MODEL), BF16),
                   jax.ShapeDtypeStruct((t * ROW_SLAB, LANES), U32)],
        compiler_params=_cparams(1),
        name="proj_norm",
    )(y, w_o, res, g, b)


def _router_kernel(x_ref, w_ref, bias_ref, e_ref, wt_ref, rank_ref, cnt_ref, run_ref, *, tm):
    i = pl.program_id(0)

    @pl.when(i == 0)
    def _():
        run_ref[...] = jnp.zeros_like(run_ref)

    logits = jnp.dot(x_ref[...], w_ref[...], precision=lax.Precision.HIGHEST, preferred_element_type=F32)
    scores = jax.nn.sigmoid(logits)
    lane = lax.broadcasted_iota(I32, (tm, LANES), 1)
    biased = jnp.where(lane < N_EXPERTS, scores + bias_ref[...], NEG_BIG)
    picks = []
    chosen = jnp.zeros((tm, LANES), jnp.bool_)
    for _ in range(TOP_K):
        mx = jnp.max(biased, axis=1, keepdims=True)
        idx = jnp.min(jnp.where(biased == mx, lane, LANES), axis=1, keepdims=True)
        hit = lane == idx
        picks.append((idx, hit))
        chosen = chosen | hit
        biased = jnp.where(hit, 2 * NEG_BIG, biased)
    picked_scores = jnp.where(chosen, scores, 0.0)
    norm = ROUTED_SCALE / jnp.sum(picked_scores, axis=1, keepdims=True)

    r = lax.broadcasted_iota(I32, (tm, tm), 0)
    c = lax.broadcasted_iota(I32, (tm, tm), 1)
    strict_lower = jnp.where(c < r, 1.0, 0.0).astype(BF16)
    chosen_f = jnp.where(chosen, 1.0, 0.0)
    arrival = run_ref[0:1, :] + _dot(strict_lower, chosen_f.astype(BF16))
    run_ref[...] = run_ref[...] + jnp.sum(chosen_f, axis=0, keepdims=True)
    cnt_ref[...] = run_ref[...]

    e_out = jnp.zeros((tm, LANES), I32)
    w_out = jnp.zeros((tm, LANES), F32)
    r_out = jnp.zeros((tm, LANES), F32)
    for slot, (idx, hit) in enumerate(picks):
        here = lane == slot
        e_out = jnp.where(here, idx, e_out)
        w_out = jnp.where(here, jnp.sum(jnp.where(hit, scores, 0.0), axis=1, keepdims=True) * norm, w_out)
        r_out = jnp.where(here, jnp.sum(jnp.where(hit, arrival, 0.0), axis=1, keepdims=True), r_out)
    e_ref[...] = e_out
    wt_ref[...] = w_out
    rank_ref[...] = r_out.astype(I32)


def _router(x32, w_router_pad, bias_pad, tm=256):
    t = x32.shape[0]
    full = lambda arr: pl.BlockSpec(arr.shape, lambda i: (0,) * arr.ndim)
    rows = pl.BlockSpec((tm, LANES), lambda i: (i, 0))
    return pl.pallas_call(
        functools.partial(_router_kernel, tm=tm),
        grid=(t // tm,),
        in_specs=[pl.BlockSpec((tm, D_MODEL), lambda i: (i, 0)), full(w_router_pad), full(bias_pad)],
        out_specs=[rows, rows, rows, pl.BlockSpec((8, LANES), lambda i: (0, 0))],
        out_shape=[jax.ShapeDtypeStruct((t, LANES), I32), jax.ShapeDtypeStruct((t, LANES), F32),
                   jax.ShapeDtypeStruct((t, LANES), I32), jax.ShapeDtypeStruct((8, LANES), F32)],
        scratch_shapes=[pltpu.VMEM((8, LANES), F32)],
        compiler_params=_cparams(1),
        name="moe_router",
    )(x32, w_router_pad, bias_pad)


def _experts_kernel(blk_e_ref, blk_off_ref, blk_nvalid_ref, n_used_ref, x_hbm, src_hbm, dst_hbm, wg_ref, wu_ref,
                    wd_ref, out_hbm, wg16_ref, wu16_ref, wd16_ref, src_ref, dst_ref, xbuf_ref, ybuf_ref,
                    idx_sem, in_sem, out_sem, *, bm, n_tok):
    i = pl.program_id(0)
    n_used = n_used_ref[0]
    win = bm + LANES
    dump_row = TOP_K * n_tok
    unroll = 8

    def slab(ref, r):
        return ref.at[pl.ds(pl.multiple_of(r * ROW_SLAB, ROW_SLAB), ROW_SLAB), :]

    def slabs(ref, n_rows):
        return ref.at[pl.ds(0, n_rows * ROW_SLAB), :]

    def idx_copies(j):
        first = pl.multiple_of((blk_off_ref[j] // LANES) * LANES, LANES)
        slot = pl.multiple_of((j % 3) * win, LANES)
        return [pltpu.make_async_copy(hbm.at[pl.ds(first, win)], smem.at[pl.ds(slot, win)], idx_sem.at[j % 3])
                for hbm, smem in ((src_hbm, src_ref), (dst_hbm, dst_ref))]

    def row_copies(n_real, copy_real, copy_pad):
        groups = n_real // unroll

        def group(g, c):
            for u in range(unroll):
                copy_real(g * unroll + u)
            return c

        lax.fori_loop(0, groups, group, 0)
        lax.fori_loop(groups * unroll, n_real, lambda r, c: (copy_real(r), c)[1], 0)
        lax.fori_loop(n_real, bm, lambda r, c: (copy_pad(r), c)[1], 0)

    def gather_start(j):
        lst = (j % 3) * win + blk_off_ref[j] % LANES
        buf = (j % 2) * bm

        def copy_from(tok, r):
            pltpu.make_async_copy(slab(x_hbm, tok), slab(xbuf_ref, buf + r), in_sem.at[j % 2]).start()

        row_copies(blk_nvalid_ref[j], lambda r: copy_from(src_ref[lst + r], r), lambda r: copy_from(0, r))

    def gather_wait(j):
        pltpu.make_async_copy(slabs(x_hbm, bm), slabs(xbuf_ref, bm), in_sem.at[j % 2]).wait()

    def scatter_start(j):
        lst = (j % 3) * win + blk_off_ref[j] % LANES
        buf = (j % 2) * bm

        def copy_to(dst, r):
            pltpu.make_async_copy(slab(ybuf_ref, buf + r), slab(out_hbm, dst), out_sem.at[j % 2]).start()

        row_copies(blk_nvalid_ref[j], lambda r: copy_to(dst_ref[lst + r], r),
                   lambda r: copy_to(dump_row + buf + r, r))

    def scatter_wait(j):
        pltpu.make_async_copy(slabs(ybuf_ref, bm), slabs(out_hbm, bm), out_sem.at[j % 2]).wait()

    @pl.when(i == 0)
    def _():
        ybuf_ref[...] = jnp.zeros_like(ybuf_ref)
        zero_fill = pltpu.make_async_copy(ybuf_ref, out_hbm.at[pl.ds(dump_row * ROW_SLAB, 2 * bm * ROW_SLAB), :],
                                          out_sem.at[0])
        zero_fill.start()
        zero_fill.wait()
        for c in idx_copies(0):
            c.start()
        for c in idx_copies(0):
            c.wait()
        gather_start(0)

        @pl.when(n_used > 1)
        def _():
            for c in idx_copies(1):
                c.start()

    @pl.when(i < n_used)
    def _():
        @pl.when(i + 1 < n_used)
        def _():
            for c in idx_copies(i + 1):
                c.wait()
            gather_start(i + 1)

        @pl.when(i + 2 < n_used)
        def _():
            for c in idx_copies(i + 2):
                c.start()

        @pl.when((i == 0) | (blk_e_ref[i] != blk_e_ref[jnp.maximum(i - 1, 0)]))
        def _():
            wg16_ref[...] = wg_ref[0, 0].astype(BF16)
            wu16_ref[...] = wu_ref[0, 0].astype(BF16)
            wd16_ref[...] = wd_ref[0, 0].astype(BF16)

        gather_wait(i)

        @pl.when(i >= 2)
        def _():
            scatter_wait(i - 2)

        first = (i % 2) * bm
        lo, hi = _unpack_bf16_pairs(_load_row_slabs(xbuf_ref, first, bm))
        xb = jnp.concatenate([lo.astype(BF16), hi.astype(BF16)], axis=1)
        hid = jax.nn.silu(_dot(xb, wg16_ref[...])) * _dot(xb, wu16_ref[...])
        _store_row_slabs(ybuf_ref, first, _pack_bf16_pairs(_dot(hid.astype(BF16), wd16_ref[...])))
        scatter_start(i)

        @pl.when(i == n_used - 1)
        def _():
            @pl.when(i >= 1)
            def _():
                scatter_wait(i - 1)
            scatter_wait(i)


def _experts(x_rows, src_tok, dst_row, blk_e, blk_off, blk_nvalid, n_used, w_gate, w_up, w_down, layer):
    n_tok = x_rows.shape[0] // ROW_SLAB
    bm = MOE_ROW_BLOCK
    n_blk = blk_e.shape[0]
    grid_spec = pltpu.PrefetchScalarGridSpec(
        num_scalar_prefetch=4,
        grid=(n_blk,),
        in_specs=[pl.BlockSpec(memory_space=pl.ANY),
                  pl.BlockSpec(memory_space=pl.ANY),
                  pl.BlockSpec(memory_space=pl.ANY),
                  pl.BlockSpec((1, 1, D_MODEL, D_EXPERT), lambda i, be, bo, bv, nu: (layer, be[i], 0, 0)),
                  pl.BlockSpec((1, 1, D_MODEL, D_EXPERT), lambda i, be, bo, bv, nu: (layer, be[i], 0, 0)),
                  pl.BlockSpec((1, 1, D_EXPERT, D_MODEL), lambda i, be, bo, bv, nu: (layer, be[i], 0, 0))],
        out_specs=pl.BlockSpec(memory_space=pl.ANY),
        scratch_shapes=[pltpu.VMEM((D_MODEL, D_EXPERT), BF16), pltpu.VMEM((D_MODEL, D_EXPERT), BF16),
                        pltpu.VMEM((D_EXPERT, D_MODEL), BF16),
                        pltpu.SMEM((3 * (bm + LANES),), I32), pltpu.SMEM((3 * (bm + LANES),), I32),
                        pltpu.VMEM((2 * bm * ROW_SLAB, LANES), x_rows.dtype),
                        pltpu.VMEM((2 * bm * ROW_SLAB, LANES), x_rows.dtype),
                        pltpu.SemaphoreType.DMA((3,)), pltpu.SemaphoreType.DMA((2,)),
                        pltpu.SemaphoreType.DMA((2,))],
    )
    return pl.pallas_call(
        functools.partial(_experts_kernel, bm=bm, n_tok=n_tok),
        grid_spec=grid_spec,
        out_shape=jax.ShapeDtypeStruct(((TOP_K * n_tok + 2 * bm) * ROW_SLAB, LANES), x_rows.dtype),
        compiler_params=_cparams(1),
        name="moe_experts",
    )(blk_e, blk_off, blk_nvalid, n_used, x_rows, src_tok, dst_row, w_gate, w_up, w_down)


def _combine_kernel(x_ref, wt_ref, *refs):
    ys_refs = refs[:TOP_K]
    wsg_ref, wsu_ref, wsd_ref, g_ref, b_ref, o32_ref, o16_ref = refs[TOP_K:]
    x = x_ref[...]
    xb = x.astype(BF16)
    hid = jax.nn.silu(_dot(xb, wsg_ref[...])) * _dot(xb, wsu_ref[...])
    total = DEEPNORM_ALPHA * x + _dot(hid.astype(BF16), wsd_ref[...])
    wt = wt_ref[...]
    routed_lo = routed_hi = None
    for k in range(TOP_K):
        lo, hi = _unpack_bf16_pairs(_load_row_slabs(ys_refs[k], 0, x.shape[0]))
        w_k = wt[:, k:k + 1]
        routed_lo = w_k * lo if routed_lo is None else routed_lo + w_k * lo
        routed_hi = w_k * hi if routed_hi is None else routed_hi + w_k * hi
    total = total + jnp.concatenate([routed_lo, routed_hi], axis=1)
    out = _layer_norm_rows(total, g_ref[...], b_ref[...])
    o32_ref[...] = out
    o16_ref[...] = out.astype(o16_ref.dtype)


def _combine(x32, wts, ys, w_sg, w_su, w_sd, g, b, tm=256):
    t = x32.shape[0]
    full = lambda arr: pl.BlockSpec(arr.shape, lambda i: (0,) * arr.ndim)
    rows = pl.BlockSpec((tm, D_MODEL), lambda i: (i, 0))
    slot_rows = lambda k: pl.BlockSpec((tm * ROW_SLAB, LANES), lambda i: (k * (t // tm) + i, 0))
    return pl.pallas_call(
        _combine_kernel,
        grid=(t // tm,),
        in_specs=[rows, pl.BlockSpec((tm, LANES), lambda i: (i, 0))]
                 + [slot_rows(k) for k in range(TOP_K)]
                 + [full(w_sg), full(w_su), full(w_sd), full(g), full(b)],
        out_specs=[rows, rows],
        out_shape=[jax.ShapeDtypeStruct((t, D_MODEL), F32), jax.ShapeDtypeStruct((t, D_MODEL), BF16)],
        compiler_params=_cparams(1),
        name="moe_combine",
    )(x32, wts, *([ys] * TOP_K), w_sg, w_su, w_sd, g, b)


def _rot_half_cols(w):
    half = w.shape[-1] // 2
    return jnp.concatenate([-w[..., half:], w[..., :half]], axis=-1)


def _pad_cols(w, width):
    return jnp.pad(w, [(0, 0)] * (w.ndim - 1) + [(0, width - w.shape[-1])])


def _rope_tables(seq):
    pos = jnp.arange(seq, dtype=F32)[:, None]

    def table(dim):
        half = dim // 2
        inv_freq = ROPE_THETA ** (-jnp.arange(half, dtype=F32) / half)
        ang = pos * inv_freq[None, :]
        return jnp.cos(ang), jnp.sin(ang)

    c64, s64 = table(MLA_ROPE_DIM)
    cos64 = _pad_cols(jnp.concatenate([c64, c64], axis=1), LANES)
    sin64 = _pad_cols(jnp.concatenate([s64, s64], axis=1), LANES)
    c128, s128 = table(MOBA_HEAD_DIM)
    cos128 = jnp.concatenate([c128, c128], axis=1)
    sin128 = jnp.concatenate([-s128, s128], axis=1)
    return cos64, sin64, cos128, sin128


def _layer(x32, x16, mem16, tables, avg, bsz, seq, mem_len, p, expert_weights, layer):
    (w_in, g_qa, w_q_up, g_kva, w_kv_up, sgu_ln_g, sgu_ln_b, w_spatial, b_spatial, w_mem_kv,
     w_br_a, w_br_b, w_br_c, w_br_m, w_o, ln1_g, ln1_b, w_router, router_bias,
     w_sh_gate, w_sh_up, w_sh_down, ln2_g, ln2_b) = p
    cos64, sin64, cos128, sin128 = tables
    t = x32.shape[0]
    row = lambda v: v.reshape(1, -1).astype(F32)

    c0 = MLA_Q_RANK + MLA_KV_RANK
    c1 = c0 + MLA_ROPE_DIM
    c2 = c1 + 2 * SGU_WIDTH
    c3 = c2 + 3 * MOBA_WIDTH
    c4 = c3 + MEM_WIDTH
    w_a = w_in[:, :c0].astype(BF16)
    w_kpe_raw = w_in[:, c0:c1]
    w_kpe = jnp.concatenate([_pad_cols(w_kpe_raw, LANES), _pad_cols(_rot_half_cols(w_kpe_raw), LANES)],
                            axis=1).astype(BF16)
    w_z = w_in[:, c1:c2].astype(BF16)
    w_c = w_in[:, c2:c3].astype(BF16)
    w_mq = w_in[:, c3:c4].astype(BF16)
    w_g = w_in[:, c4:].astype(BF16)

    w_qn = w_q_up[:, :, :MLA_NOPE_DIM].reshape(MLA_Q_RANK, -1).astype(BF16)
    w_qpe = w_q_up[:, :, MLA_NOPE_DIM:]
    w_qp = _pad_cols(w_qpe, LANES).reshape(MLA_Q_RANK, -1).astype(BF16)
    w_qpr = _pad_cols(_rot_half_cols(w_qpe), LANES).reshape(MLA_Q_RANK, -1).astype(BF16)
    w_kn = w_kv_up[:, :, :MLA_NOPE_DIM].reshape(MLA_KV_RANK, -1).astype(BF16)
    w_v = w_kv_up[:, :, MLA_NOPE_DIM:].reshape(MLA_KV_RANK, -1).T.astype(BF16)
    qt_a, k_a, vt_a = _mla_proj(x16, w_a, w_kpe, row(g_qa), row(g_kva), w_qn.T, w_qp.T, w_qpr.T, w_kn, w_v,
                                cos64, sin64, seq)
    o_a = _mla_attention(qt_a, k_a, vt_a, bsz, seq)

    bias_full = jnp.repeat(b_spatial.T.astype(F32), SGU_GROUP_DIM, axis=1)
    o_b = _sgu(x16, w_z, row(sgu_ln_g), row(sgu_ln_b), w_spatial.astype(F32), bias_full)

    k_c = _moba_k_proj(x16, w_c[:, MOBA_WIDTH:2 * MOBA_WIDTH], cos128, sin128, seq)
    w_qv_t = jnp.concatenate([w_c[:, :MOBA_WIDTH], w_c[:, 2 * MOBA_WIDTH:]], axis=1).T
    qvt_c = _moba_qv_proj(x16, w_qv_t, cos128.T, sin128.T, seq)
    o_c = _moba_attention(qvt_c, k_c, avg, bsz, seq)

    kv_mem = _matmul(mem16, w_mem_kv.astype(BF16), mem_len, 2 * MEM_WIDTH, BF16, "mem_kv")
    o_m = _mem_attention(x16, w_mq, kv_mem, bsz, seq, mem_len)

    y = _gated_sum(x16, o_a, o_b, o_c, o_m, w_g, w_br_a.astype(BF16), w_br_b.astype(BF16),
                   w_br_c.astype(BF16), w_br_m.astype(BF16))
    x32, x16, x_packed = _proj_norm(y, w_o.astype(BF16), x32, row(ln1_g), row(ln1_b))

    e_idx, wts, rank, counts = _router(x32, _pad_cols(w_router.astype(F32), LANES),
                                       _pad_cols(row(router_bias), LANES))
    bm = MOE_ROW_BLOCK
    n_blk = (t * TOP_K) // bm + N_EXPERTS
    counts = counts[0, :N_EXPERTS].astype(I32)
    padded = (counts + bm - 1) // bm * bm
    pad_ends = jnp.cumsum(padded)
    pad_starts = pad_ends - padded
    dest = (pad_starts[e_idx[:, :TOP_K]] + rank[:, :TOP_K]).astype(I32)
    keys = jnp.pad(dest, ((0, 0), (0, SLOT_STRIDE - TOP_K)), constant_values=jnp.iinfo(jnp.int32).max).reshape(-1)
    _, order = lax.sort((keys, jnp.arange(t * SLOT_STRIDE, dtype=I32)), num_keys=1)
    blk_first = jnp.arange(n_blk, dtype=I32) * bm
    blk_e = jnp.minimum(jnp.sum((pad_ends[None, :] <= blk_first[:, None]).astype(I32), axis=1), N_EXPERTS - 1)
    n_used = (pad_ends[-1:] // bm).astype(I32)
    compact_starts = jnp.cumsum(counts) - counts
    blk_off = jnp.clip(blk_first - (pad_starts - compact_starts)[blk_e], 0, t * SLOT_STRIDE - bm - LANES)
    blk_nvalid = jnp.clip((pad_starts + counts)[blk_e] - blk_first, 0, bm)
    src_tok = order // SLOT_STRIDE
    dst_row = (order % SLOT_STRIDE) * t + src_tok
    ys = _experts(x_packed, src_tok, dst_row, blk_e, blk_off.astype(I32), blk_nvalid.astype(I32), n_used,
                  *expert_weights, layer)
    return _combine(x32, wts, ys, w_sh_gate.astype(BF16), w_sh_up.astype(BF16),
                    w_sh_down.astype(BF16), row(ln2_g), row(ln2_b))


def kernel(x, mem, w_in, g_qa, w_q_up, g_kva, w_kv_up, sgu_ln_g, sgu_ln_b, w_spatial, b_spatial, w_mem_kv, w_br_a, w_br_b, w_br_c, w_br_m, w_o, ln1_g, ln1_b, w_router, router_bias, w_exp_gate, w_exp_up, w_exp_down, w_sh_gate, w_sh_up, w_sh_down, ln2_g, ln2_b):
    bsz, seq, d = x.shape
    mem_len = mem.shape[1]
    params = (w_in, g_qa, w_q_up, g_kva, w_kv_up, sgu_ln_g, sgu_ln_b, w_spatial, b_spatial, w_mem_kv,
              w_br_a, w_br_b, w_br_c, w_br_m, w_o, ln1_g, ln1_b, w_router, router_bias,
              w_sh_gate, w_sh_up, w_sh_down, ln2_g, ln2_b)
    expert_weights = (w_exp_gate, w_exp_up, w_exp_down)
    tables = _rope_tables(seq)
    n_blocks = seq // MOBA_BLOCK
    blk_of_pos = jnp.arange(seq, dtype=I32)[None, :] // MOBA_BLOCK
    avg = jnp.where(blk_of_pos == jnp.arange(LANES, dtype=I32)[:, None], 1.0 / MOBA_BLOCK, 0.0).astype(BF16)
    assert n_blocks <= LANES
    x32 = x.reshape(bsz * seq, d)
    x16 = x32.astype(BF16)
    mem16 = mem.reshape(bsz * mem_len, d).astype(BF16)
    for l in range(DEPTH):
        x32, x16 = _layer(x32, x16, mem16, tables, avg, bsz, seq, mem_len, tuple(w[l] for w in params),
                          expert_weights, l)
    return x32.reshape(bsz, seq, d)
```

```python
import functools

import jax
import jax.numpy as jnp
from jax import lax
from jax.experimental import pallas as pl
from jax.experimental.pallas import tpu as pltpu

F32 = jnp.float32
BF16 = jnp.bfloat16
I32 = jnp.int32
U32 = jnp.uint32

D_MODEL = 2048
DEPTH = 2
MLA_HEADS = 8
MLA_Q_RANK = 512
MLA_KV_RANK = 256
MLA_NOPE_DIM = 128
MLA_ROPE_DIM = 64
MLA_V_DIM = 128
SGU_GROUPS = 8
SGU_GROUP_DIM = 128
SGU_WIDTH = SGU_GROUPS * SGU_GROUP_DIM
SGU_CHUNK = 128
MOBA_HEADS = 8
MOBA_HEAD_DIM = 128
MOBA_WIDTH = MOBA_HEADS * MOBA_HEAD_DIM
MOBA_BLOCK = 256
MOBA_TOPK = 3
MEM_HEADS = 4
MEM_HEAD_DIM = 128
MEM_WIDTH = MEM_HEADS * MEM_HEAD_DIM
N_BRANCH = 4
N_EXPERTS = 64
TOP_K = 6
D_EXPERT = 512
ROUTED_SCALE = 2.5
ROPE_THETA = 10000.0
DEEPNORM_ALPHA = (2 * DEPTH) ** 0.25

LANES = 128
SUBLANES = 8
ATTN_BLOCK = 256
MOE_ROW_BLOCK = 512
SLOT_STRIDE = 8
NEG_BIG = -1e30
VMEM_LIMIT = 56 * 1024 * 1024


def _cparams(n_axes):
    return pltpu.CompilerParams(dimension_semantics=("arbitrary",) * n_axes,
                                vmem_limit_bytes=VMEM_LIMIT)


def _dot(a, b):
    return jnp.dot(a, b, preferred_element_type=F32)


def _dot_nt(a, b):
    return lax.dot_general(a, b, (((1,), (1,)), ((), ())), preferred_element_type=F32)


def _layer_norm_rows(v, g, b, eps=1e-5):
    mu = jnp.mean(v, axis=-1, keepdims=True)
    vc = v - mu
    var = jnp.mean(vc * vc, axis=-1, keepdims=True)
    return vc * lax.rsqrt(var + eps) * g + b


def _rms_norm_rows(v, g, eps=1e-6):
    return v * lax.rsqrt(jnp.mean(v * v, axis=-1, keepdims=True) + eps) * g


def _mm_kernel(x_ref, w_ref, o_ref):
    o_ref[...] = _dot(x_ref[...], w_ref[...]).astype(o_ref.dtype)


def _matmul(x, w, tm, tn, out_dtype, name):
    m, k = x.shape
    n = w.shape[1]
    return pl.pallas_call(
        _mm_kernel,
        grid=(n // tn, m // tm),
        in_specs=[pl.BlockSpec((tm, k), lambda j, i: (i, 0)),
                  pl.BlockSpec((k, tn), lambda j, i: (0, j))],
        out_specs=pl.BlockSpec((tm, tn), lambda j, i: (i, j)),
        out_shape=jax.ShapeDtypeStruct((m, n), out_dtype),
        compiler_params=_cparams(2),
        name=name,
    )(x, w)


def _mla_proj_kernel(x_ref, wa_ref, wkpe_ref, gq_ref, gkv_ref, wqnt_ref, wqpt_ref, wqprt_ref,
                     wkn_ref, wvt_ref, cos_ref, sin_ref, cost_ref, sint_ref, qt_ref, k_ref, vt_ref, *, scale):
    xb = x_ref[...]
    a = _dot(xb, wa_ref[...])
    cq = _rms_norm_rows(a[:, :MLA_Q_RANK], gq_ref[...]).astype(BF16)
    ckv = _rms_norm_rows(a[:, MLA_Q_RANK:], gkv_ref[...]).astype(BF16)
    qn_t = _dot_nt(wqnt_ref[...], cq)
    qp_t = _dot_nt(wqpt_ref[...], cq)
    qpr_t = _dot_nt(wqprt_ref[...], cq)
    v_t = _dot_nt(wvt_ref[...], ckv)
    cos_t = cost_ref[...]
    sin_t = sint_ref[...]
    for h in range(MLA_HEADS):
        lo, hi = h * LANES, (h + 1) * LANES
        qt_ref[0, h, 0, 0:LANES, :] = (qn_t[lo:hi] * scale).astype(qt_ref.dtype)
        qt_ref[0, h, 0, LANES:2 * LANES, :] = ((qp_t[lo:hi] * cos_t + qpr_t[lo:hi] * sin_t) * scale).astype(qt_ref.dtype)
        vt_ref[0, h, 0] = v_t[h * MLA_V_DIM:(h + 1) * MLA_V_DIM, :].astype(vt_ref.dtype)
    kn = _dot(ckv, wkn_ref[...])
    kp = _dot(xb, wkpe_ref[...])
    kpe = (kp[:, :LANES] * cos_ref[...] + kp[:, LANES:] * sin_ref[...]).astype(k_ref.dtype)
    for h in range(MLA_HEADS):
        lo, hi = h * LANES, (h + 1) * LANES
        k_ref[:, 2 * lo:2 * lo + LANES] = kn[:, lo:hi].astype(k_ref.dtype)
        k_ref[:, 2 * lo + LANES:2 * hi] = kpe


def _mla_proj(x, wa, wkpe, gq, gkv, wqnt, wqpt, wqprt, wkn, wvt, cos64, sin64, seq):
    t = x.shape[0]
    tm = ATTN_BLOCK
    full = lambda arr: pl.BlockSpec(arr.shape, lambda i: (0,) * arr.ndim)
    n_pos = seq // tm
    scale = (MLA_NOPE_DIM + MLA_ROPE_DIM) ** -0.5
    dk = 2 * LANES
    feat_major = lambda d: pl.BlockSpec((1, MLA_HEADS, 1, d, tm), lambda i: (i // n_pos, 0, i % n_pos, 0, 0))
    return pl.pallas_call(
        functools.partial(_mla_proj_kernel, scale=scale),
        grid=(t // tm,),
        in_specs=[pl.BlockSpec((tm, D_MODEL), lambda i: (i, 0)),
                  full(wa), full(wkpe), full(gq), full(gkv), full(wqnt), full(wqpt), full(wqprt),
                  full(wkn), full(wvt),
                  pl.BlockSpec((tm, LANES), lambda i: (i % n_pos, 0)),
                  pl.BlockSpec((tm, LANES), lambda i: (i % n_pos, 0)),
                  pl.BlockSpec((LANES, tm), lambda i: (0, i % n_pos)),
                  pl.BlockSpec((LANES, tm), lambda i: (0, i % n_pos))],
        out_specs=[feat_major(dk),
                   pl.BlockSpec((tm, MLA_HEADS * dk), lambda i: (i, 0)),
                   feat_major(MLA_V_DIM)],
        out_shape=[jax.ShapeDtypeStruct((t // seq, MLA_HEADS, n_pos, dk, tm), BF16),
                   jax.ShapeDtypeStruct((t, MLA_HEADS * dk), BF16),
                   jax.ShapeDtypeStruct((t // seq, MLA_HEADS, n_pos, MLA_V_DIM, tm), BF16)],
        compiler_params=_cparams(1),
        name="mla_proj",
    )(x, wa, wkpe, gq, gkv, wqnt, wqpt, wqprt, wkn, wvt, cos64, sin64, cos64.T, sin64.T)


def _softmax_steps(q_ts, ks, v_ts, carries, masks):
    heads = range(len(q_ts))
    s_ts = [_dot(ks[h], q_ts[h]) for h in heads]
    s_ts = [s_ts[h] if masks[h] is None else jnp.where(masks[h], s_ts[h], NEG_BIG) for h in heads]
    m_news = [jnp.maximum(carries[h][0], jnp.max(s_ts[h], axis=0, keepdims=True)) for h in heads]
    p_ts = [jnp.exp(s_ts[h] - m_news[h]) for h in heads]
    pvs = [_dot(v_ts[h], p_ts[h].astype(v_ts[h].dtype)) for h in heads]
    out = []
    for h in heads:
        m_i, l_i, acc_t = carries[h]
        alpha = jnp.exp(m_i - m_news[h])
        l_new = alpha * l_i + jnp.sum(p_ts[h], axis=0, keepdims=True)
        out.append((m_news[h], l_new, alpha * acc_t + pvs[h]))
    return tuple(out)


def _softmax_init(n_heads, tq, dv):
    return tuple((jnp.full((1, tq), NEG_BIG, F32), jnp.zeros((1, tq), F32), jnp.zeros((dv, tq), F32))
                 for _ in range(n_heads))


def _softmax_finish(o_ref, carries, dv):
    for h, (_, l_i, acc_t) in enumerate(carries):
        o_ref[:, h * dv:(h + 1) * dv] = (acc_t / l_i).T.astype(o_ref.dtype)


def _causal_attn_kernel(qt_ref, k_ref, vt_ref, o_ref, *, tq, n_heads, dk, dv):
    qi = pl.program_id(2)
    key = lax.broadcasted_iota(I32, (tq, tq), 0)
    qry = lax.broadcasted_iota(I32, (tq, tq), 1)
    qs = [qt_ref[0, h, 0] for h in range(n_heads)]

    def step(kb, carries, mask):
        ks = pl.multiple_of(kb * tq, tq)
        return _softmax_steps(qs, [k_ref[pl.ds(ks, tq), h * dk:(h + 1) * dk] for h in range(n_heads)],
                              [vt_ref[0, h, kb] for h in range(n_heads)], carries, [mask] * n_heads)

    carries = step(qi, _softmax_init(n_heads, tq, dv), key <= qry)
    carries = lax.fori_loop(0, qi, lambda kb, c: step(kb, c, None), carries)
    _softmax_finish(o_ref, carries, dv)


def _mla_attention(q_t, k_a, v_t, bsz, seq, n_heads=8):
    t = k_a.shape[0]
    tq = ATTN_BLOCK
    nq = seq // tq
    dk = 2 * LANES
    dv = MLA_V_DIM
    return pl.pallas_call(
        functools.partial(_causal_attn_kernel, tq=tq, n_heads=n_heads, dk=dk, dv=dv),
        grid=(bsz, MLA_HEADS // n_heads, nq),
        in_specs=[pl.BlockSpec((1, n_heads, 1, dk, tq), lambda b, h, i: (b, h, i, 0, 0)),
                  pl.BlockSpec((seq, n_heads * dk), lambda b, h, i: (b, h)),
                  pl.BlockSpec((1, n_heads, nq, dv, tq), lambda b, h, i: (b, h, 0, 0, 0))],
        out_specs=pl.BlockSpec((tq, n_heads * dv), lambda b, h, i: (b * nq + i, h)),
        out_shape=jax.ShapeDtypeStruct((t, MLA_HEADS * dv), BF16),
        compiler_params=_cparams(3),
        name="mla_attn",
    )(q_t, k_a, v_t)


def _sgu_kernel(x_ref, wz_ref, g_ref, b_ref, ws_ref, bs_ref, o_ref, *, tm):
    z = jax.nn.gelu(_dot(x_ref[...], wz_ref[...]))
    u = z[:, :SGU_WIDTH]
    v = _layer_norm_rows(z[:, SGU_WIDTH:], g_ref[...], b_ref[...]).astype(BF16)
    row = lax.broadcasted_iota(I32, (SGU_CHUNK, SGU_CHUNK), 0)
    col = lax.broadcasted_iota(I32, (SGU_CHUNK, SGU_CHUNK), 1)
    bias = bs_ref[...]
    for g in range(SGU_GROUPS):
        w = jnp.where(col <= row, ws_ref[g], 0.0).astype(BF16)
        lo, hi = g * SGU_GROUP_DIM, (g + 1) * SGU_GROUP_DIM
        for c in range(tm // SGU_CHUNK):
            r0, r1 = c * SGU_CHUNK, (c + 1) * SGU_CHUNK
            mixed = _dot(w, v[r0:r1, lo:hi]) + bias[:, lo:hi]
            o_ref[r0:r1, lo:hi] = (u[r0:r1, lo:hi] * mixed).astype(o_ref.dtype)


def _sgu(x, wz, ln_g, ln_b, w_s, bias_full, tm=256):
    t = x.shape[0]
    full = lambda arr: pl.BlockSpec(arr.shape, lambda i: (0,) * arr.ndim)
    return pl.pallas_call(
        functools.partial(_sgu_kernel, tm=tm),
        grid=(t // tm,),
        in_specs=[pl.BlockSpec((tm, D_MODEL), lambda i: (i, 0)),
                  full(wz), full(ln_g), full(ln_b), full(w_s), full(bias_full)],
        out_specs=pl.BlockSpec((tm, SGU_WIDTH), lambda i: (i, 0)),
        out_shape=jax.ShapeDtypeStruct((t, SGU_WIDTH), BF16),
        compiler_params=_cparams(1),
        name="sgu",
    )(x, wz, ln_g, ln_b, w_s, bias_full)


def _moba_k_proj_kernel(x_ref, w_ref, cos_ref, sin_ref, o_ref):
    acc = _dot(x_ref[...], w_ref[...])
    cos = cos_ref[...]
    sin = sin_ref[...]
    for h in range(MOBA_HEADS):
        seg = acc[:, h * LANES:(h + 1) * LANES]
        rot = pltpu.roll(seg, MOBA_HEAD_DIM // 2, axis=1)
        o_ref[:, h * LANES:(h + 1) * LANES] = (seg * cos + rot * sin).astype(o_ref.dtype)


def _moba_k_proj(x, w_k, cos128, sin128, seq, tm=512):
    t = x.shape[0]
    n_pos = seq // tm
    return pl.pallas_call(
        _moba_k_proj_kernel,
        grid=(t // tm,),
        in_specs=[pl.BlockSpec((tm, D_MODEL), lambda i: (i, 0)),
                  pl.BlockSpec(w_k.shape, lambda i: (0, 0)),
                  pl.BlockSpec((tm, LANES), lambda i: (i % n_pos, 0)),
                  pl.BlockSpec((tm, LANES), lambda i: (i % n_pos, 0))],
        out_specs=pl.BlockSpec((tm, MOBA_WIDTH), lambda i: (i, 0)),
        out_shape=jax.ShapeDtypeStruct((t, MOBA_WIDTH), BF16),
        compiler_params=_cparams(1),
        name="moba_k_proj",
    )(x, w_k, cos128, sin128)


def _moba_qv_proj_kernel(x_ref, wt_ref, cost_ref, sint_ref, o_ref, *, scale):
    y_t = _dot_nt(wt_ref[...], x_ref[...])
    hd = MOBA_HEAD_DIM

    @pl.when(pl.program_id(0) == 0)
    def _():
        cos_t = cost_ref[...]
        sin_t = sint_ref[...]
        for h in range(MOBA_HEADS):
            seg = y_t[h * hd:(h + 1) * hd]
            rot = jnp.concatenate([seg[hd // 2:], seg[:hd // 2]], axis=0)
            o_ref[0, h, 0] = ((seg * cos_t + rot * sin_t) * scale).astype(o_ref.dtype)

    @pl.when(pl.program_id(0) == 1)
    def _():
        for h in range(MOBA_HEADS):
            o_ref[0, h, 0] = y_t[h * hd:(h + 1) * hd].astype(o_ref.dtype)


def _moba_qv_proj(x, w_qv_t, cos128_t, sin128_t, seq):
    t = x.shape[0]
    tm = ATTN_BLOCK
    n_pos = seq // tm
    return pl.pallas_call(
        functools.partial(_moba_qv_proj_kernel, scale=MOBA_HEAD_DIM ** -0.5),
        grid=(2, t // tm),
        in_specs=[pl.BlockSpec((tm, D_MODEL), lambda j, i: (i, 0)),
                  pl.BlockSpec((MOBA_WIDTH, D_MODEL), lambda j, i: (j, 0)),
                  pl.BlockSpec((MOBA_HEAD_DIM, tm), lambda j, i: (0, i % n_pos)),
                  pl.BlockSpec((MOBA_HEAD_DIM, tm), lambda j, i: (0, i % n_pos))],
        out_specs=pl.BlockSpec((1, MOBA_HEADS, 1, MOBA_HEAD_DIM, tm),
                               lambda j, i: (i // n_pos, j, i % n_pos, 0, 0)),
        out_shape=jax.ShapeDtypeStruct((t // seq, 2 * MOBA_HEADS, n_pos, MOBA_HEAD_DIM, tm), BF16),
        compiler_params=_cparams(2),
        name="moba_qv_proj",
    )(x, w_qv_t, cos128_t, sin128_t)


def _moba_attn_kernel(qt_ref, k_ref, vt_ref, avg_ref, o_ref, kmean_ref, *, n_heads):
    j = pl.program_id(2)
    blk = MOBA_BLOCK
    hd = MOBA_HEAD_DIM

    @pl.when(j == 0)
    def _():
        kmean_ref[...] = _dot(avg_ref[...], k_ref[...])

    blk_id = lax.broadcasted_iota(I32, (SUBLANES, blk), 0)
    n_blocks = k_ref.shape[0] // blk
    qs, sels = [], []
    for h in range(n_heads):
        q = qt_ref[0, h, 0]
        gate = jnp.dot(kmean_ref[0:SUBLANES, h * hd:(h + 1) * hd], q.astype(F32),
                       precision=lax.Precision.HIGHEST, preferred_element_type=F32)
        rank = jnp.zeros((SUBLANES, blk), I32)
        for m in range(n_blocks):
            gm = gate[m:m + 1, :]
            beats = (gm > gate) | ((gm == gate) & (m < blk_id))
            rank = rank + jnp.where(beats & (m < j), 1, 0)
        qs.append(q)
        sels.append(jnp.where((rank < MOBA_TOPK) & (blk_id < j), 1.0, 0.0))

    def step(kb, carries, masks):
        ks = pl.multiple_of(kb * blk, blk)
        return _softmax_steps(qs, [k_ref[pl.ds(ks, blk), h * hd:(h + 1) * hd] for h in range(n_heads)],
                              [vt_ref[0, h, kb] for h in range(n_heads)], carries, masks)

    key = lax.broadcasted_iota(I32, (blk, blk), 0)
    qry = lax.broadcasted_iota(I32, (blk, blk), 1)
    carries = step(j, _softmax_init(n_heads, blk, hd), [key <= qry] * n_heads)

    def body(n, c):
        masks = [jnp.max(jnp.where(blk_id == n, sels[h], 0.0), axis=0, keepdims=True) > 0.5
                 for h in range(n_heads)]
        return step(n, c, masks)

    carries = lax.fori_loop(0, j, body, carries)
    _softmax_finish(o_ref, carries, hd)


def _moba_attention(qv_t, k, avg, bsz, seq, n_heads=8):
    t = k.shape[0]
    nq = seq // MOBA_BLOCK
    assert nq <= SUBLANES and MOBA_BLOCK == ATTN_BLOCK
    w = n_heads * MOBA_HEAD_DIM
    groups = MOBA_HEADS // n_heads
    return pl.pallas_call(
        functools.partial(_moba_attn_kernel, n_heads=n_heads),
        grid=(bsz, groups, nq),
        in_specs=[pl.BlockSpec((1, n_heads, 1, MOBA_HEAD_DIM, MOBA_BLOCK), lambda b, h, i: (b, h, i, 0, 0)),
                  pl.BlockSpec((seq, w), lambda b, h, i: (b, h)),
                  pl.BlockSpec((1, n_heads, nq, MOBA_HEAD_DIM, MOBA_BLOCK),
                               lambda b, h, i: (b, groups + h, 0, 0, 0)),
                  pl.BlockSpec(avg.shape, lambda b, h, i: (0, 0))],
        out_specs=pl.BlockSpec((MOBA_BLOCK, w), lambda b, h, i: (b * nq + i, h)),
        out_shape=jax.ShapeDtypeStruct((t, MOBA_WIDTH), BF16),
        scratch_shapes=[pltpu.VMEM((LANES, w), F32)],
        compiler_params=_cparams(3),
        name="moba_attn",
    )(qv_t, k, qv_t, avg)


def _mem_attn_kernel(x_ref, wq_ref, kv_ref, o_ref, *, scale):
    q = (_dot(x_ref[...], wq_ref[...]) * scale).astype(BF16)
    for h in range(MEM_HEADS):
        lo, hi = h * MEM_HEAD_DIM, (h + 1) * MEM_HEAD_DIM
        s = _dot_nt(q[:, lo:hi], kv_ref[:, lo:hi])
        p = jnp.exp(s - jnp.max(s, axis=1, keepdims=True))
        o = _dot(p.astype(BF16), kv_ref[:, MEM_WIDTH + lo:MEM_WIDTH + hi])
        o_ref[:, lo:hi] = (o / jnp.sum(p, axis=1, keepdims=True)).astype(o_ref.dtype)


def _mem_attention(x, w_mq, kv_mem, bsz, seq, mem_len, tm=512):
    t = x.shape[0]
    ns = seq // tm
    return pl.pallas_call(
        functools.partial(_mem_attn_kernel, scale=MEM_HEAD_DIM ** -0.5),
        grid=(bsz, ns),
        in_specs=[pl.BlockSpec((tm, D_MODEL), lambda b, i: (b * ns + i, 0)),
                  pl.BlockSpec(w_mq.shape, lambda b, i: (0, 0)),
                  pl.BlockSpec((mem_len, 2 * MEM_WIDTH), lambda b, i: (b, 0))],
        out_specs=pl.BlockSpec((tm, MEM_WIDTH), lambda b, i: (b * ns + i, 0)),
        out_shape=jax.ShapeDtypeStruct((t, MEM_WIDTH), BF16),
        compiler_params=_cparams(2),
        name="mem_attn",
    )(x, w_mq, kv_mem)


def _gated_sum_kernel(x_ref, oa_ref, ob_ref, oc_ref, om_ref, g0_ref, g1_ref, g2_ref, g3_ref,
                      wa_ref, wb_ref, wc_ref, wm_ref, y_ref):
    xb = x_ref[...]
    acc = None
    for o_ref, g_ref, w_ref in ((oa_ref, g0_ref, wa_ref), (ob_ref, g1_ref, wb_ref),
                                (oc_ref, g2_ref, wc_ref), (om_ref, g3_ref, wm_ref)):
        term = jax.nn.sigmoid(_dot(xb, g_ref[...])) * _dot(o_ref[...], w_ref[...])
        acc = term if acc is None else acc + term
    y_ref[...] = acc.astype(y_ref.dtype)


def _gated_sum(x, o_a, o_b, o_c, o_m, w_gate, w_br_a, w_br_b, w_br_c, w_br_m, tm=1024, tn=256):
    t = x.shape[0]
    nj = D_MODEL // tn
    rows = lambda width: pl.BlockSpec((tm, width), lambda j, i: (i, 0))
    gate = lambda b: pl.BlockSpec((D_MODEL, tn), lambda j, i: (0, b * nj + j))
    cols = lambda width: pl.BlockSpec((width, tn), lambda j, i: (0, j))
    return pl.pallas_call(
        _gated_sum_kernel,
        grid=(nj, t // tm),
        in_specs=[rows(D_MODEL), rows(o_a.shape[1]), rows(o_b.shape[1]), rows(o_c.shape[1]), rows(o_m.shape[1]),
                  gate(0), gate(1), gate(2), gate(3),
                  cols(w_br_a.shape[0]), cols(w_br_b.shape[0]), cols(w_br_c.shape[0]), cols(w_br_m.shape[0])],
        out_specs=pl.BlockSpec((tm, tn), lambda j, i: (i, j)),
        out_shape=jax.ShapeDtypeStruct((t, D_MODEL), BF16),
        compiler_params=_cparams(2),
        name="gated_sum",
    )(x, o_a, o_b, o_c, o_m, w_gate, w_gate, w_gate, w_gate, w_br_a, w_br_b, w_br_c, w_br_m)


def _pack_bf16_pairs(v):
    bits = lax.bitcast_convert_type(v.astype(BF16).astype(F32), U32)
    half = v.shape[1] // 2
    return (bits[:, :half] >> 16) | bits[:, half:]


def _unpack_bf16_pairs(words):
    lo = lax.bitcast_convert_type(words << 16, F32)
    hi = lax.bitcast_convert_type(words & jnp.uint32(0xFFFF0000), F32)
    return lo, hi


def _proj_norm_kernel(y_ref, w_ref, res_ref, g_ref, b_ref, o32_ref, o16_ref, opk_ref):
    v = DEEPNORM_ALPHA * res_ref[...] + _dot(y_ref[...], w_ref[...])
    out = _layer_norm_rows(v, g_ref[...], b_ref[...])
    o32_ref[...] = out
    o16_ref[...] = out.astype(o16_ref.dtype)
    opk_ref[...] = _pack_bf16_pairs(out)


def _proj_norm(y, w_o, res, g, b, tm=256):
    t = y.shape[0]
    full = lambda arr: pl.BlockSpec(arr.shape, lambda i: (0,) * arr.ndim)
    rows = pl.BlockSpec((tm, D_MODEL), lambda i: (i, 0))
    return pl.pallas_call(
        _proj_norm_kernel,
        grid=(t // tm,),
        in_specs=[rows, full(w_o), rows, full(g), full(b)],
        out_specs=[rows, rows, pl.BlockSpec((tm, D_MODEL // 2), lambda i: (i, 0))],
        out_shape=[jax.ShapeDtypeStruct((t, D_MODEL), F32), jax.ShapeDtypeStruct((t, D_MODEL), BF16),
                   jax.ShapeDtypeStruct((t, D_MODEL // 2), U32)],
        compiler_params=_cparams(1),
        name="proj_norm",
    )(y, w_o, res, g, b)


def _router_kernel(x_ref, w_ref, bias_ref, e_ref, wt_ref, rank_ref, cnt_ref, run_ref, *, tm):
    i = pl.program_id(0)

    @pl.when(i == 0)
    def _():
        run_ref[...] = jnp.zeros_like(run_ref)

    logits = jnp.dot(x_ref[...], w_ref[...], precision=lax.Precision.HIGHEST, preferred_element_type=F32)
    scores = jax.nn.sigmoid(logits)
    lane = lax.broadcasted_iota(I32, (tm, LANES), 1)
    biased = jnp.where(lane < N_EXPERTS, scores + bias_ref[...], NEG_BIG)
    picks = []
    chosen = jnp.zeros((tm, LANES), jnp.bool_)
    for _ in range(TOP_K):
        mx = jnp.max(biased, axis=1, keepdims=True)
        idx = jnp.min(jnp.where(biased == mx, lane, LANES), axis=1, keepdims=True)
        hit = lane == idx
        picks.append((idx, hit))
        chosen = chosen | hit
        biased = jnp.where(hit, 2 * NEG_BIG, biased)
    picked_scores = jnp.where(chosen, scores, 0.0)
    norm = ROUTED_SCALE / jnp.sum(picked_scores, axis=1, keepdims=True)

    r = lax.broadcasted_iota(I32, (tm, tm), 0)
    c = lax.broadcasted_iota(I32, (tm, tm), 1)
    strict_lower = jnp.where(c < r, 1.0, 0.0).astype(BF16)
    chosen_f = jnp.where(chosen, 1.0, 0.0)
    arrival = run_ref[0:1, :] + _dot(strict_lower, chosen_f.astype(BF16))
    run_ref[...] = run_ref[...] + jnp.sum(chosen_f, axis=0, keepdims=True)
    cnt_ref[...] = run_ref[...]

    e_out = jnp.zeros((tm, LANES), I32)
    w_out = jnp.zeros((tm, LANES), F32)
    r_out = jnp.zeros((tm, LANES), F32)
    for slot, (idx, hit) in enumerate(picks):
        here = lane == slot
        e_out = jnp.where(here, idx, e_out)
        w_out = jnp.where(here, jnp.sum(jnp.where(hit, scores, 0.0), axis=1, keepdims=True) * norm, w_out)
        r_out = jnp.where(here, jnp.sum(jnp.where(hit, arrival, 0.0), axis=1, keepdims=True), r_out)
    e_ref[...] = e_out
    wt_ref[...] = w_out
    rank_ref[...] = r_out.astype(I32)


def _router(x32, w_router_pad, bias_pad, tm=256):
    t = x32.shape[0]
    full = lambda arr: pl.BlockSpec(arr.shape, lambda i: (0,) * arr.ndim)
    rows = pl.BlockSpec((tm, LANES), lambda i: (i, 0))
    return pl.pallas_call(
        functools.partial(_router_kernel, tm=tm),
        grid=(t // tm,),
        in_specs=[pl.BlockSpec((tm, D_MODEL), lambda i: (i, 0)), full(w_router_pad), full(bias_pad)],
        out_specs=[rows, rows, rows, pl.BlockSpec((8, LANES), lambda i: (0, 0))],
        out_shape=[jax.ShapeDtypeStruct((t, LANES), I32), jax.ShapeDtypeStruct((t, LANES), F32),
                   jax.ShapeDtypeStruct((t, LANES), I32), jax.ShapeDtypeStruct((8, LANES), F32)],
        scratch_shapes=[pltpu.VMEM((8, LANES), F32)],
        compiler_params=_cparams(1),
        name="moe_router",
    )(x32, w_router_pad, bias_pad)


def _dispatch_kernel(fill_start_ref, fill_n_ref, n_used_ref, dest_ref, x_ref, out_hbm, zero_ref, sem, fill_sem,
                     *, tm):
    @pl.when(pl.program_id(0) == 0)
    def _():
        zero_ref[...] = jnp.zeros_like(zero_ref)
        bm = zero_ref.shape[0]
        n_blk = out_hbm.shape[0] // bm

        def fill_expert(e, wait):
            def run(copy):
                copy.wait() if wait else copy.start()

            n = fill_n_ref[e]
            first = fill_start_ref[e]
            head = (-first) & (SUBLANES - 1)
            lax.fori_loop(0, head, lambda r, c: (run(pltpu.make_async_copy(
                zero_ref.at[pl.ds(0, 1), :], out_hbm.at[pl.ds(first + r, 1), :], fill_sem)), c)[1], 0)
            row = first + head
            rest = n - head
            for bit in range(SUBLANES.bit_length() - 1, bm.bit_length() - 1):
                size = 1 << bit

                @pl.when((rest & size) != 0)
                def _():
                    run(pltpu.make_async_copy(zero_ref.at[pl.ds(0, size), :],
                                              out_hbm.at[pl.ds(pl.multiple_of(row, SUBLANES), size), :], fill_sem))

                row = row + (rest & size)

        def tail_copy(blk):
            return pltpu.make_async_copy(zero_ref, out_hbm.at[pl.ds(pl.multiple_of(blk * bm, bm), bm), :], fill_sem)

        lax.fori_loop(0, N_EXPERTS, lambda e, c: (fill_expert(e, False), c)[1], 0)
        lax.fori_loop(n_used_ref[0], n_blk, lambda b, c: (tail_copy(b).start(), c)[1], 0)
        lax.fori_loop(0, N_EXPERTS, lambda e, c: (fill_expert(e, True), c)[1], 0)
        lax.fori_loop(n_used_ref[0], n_blk, lambda b, c: (tail_copy(0).wait(), c)[1], 0)

    def issue(tok, _):
        for k in range(TOP_K):
            pltpu.make_async_copy(x_ref.at[pl.ds(tok, 1), :],
                                  out_hbm.at[pl.ds(dest_ref[tok * SLOT_STRIDE + k], 1), :], sem).start()
        return 0

    lax.fori_loop(0, tm, issue, 0)
    for _ in range(TOP_K):
        pltpu.make_async_copy(x_ref, out_hbm.at[pl.ds(0, tm), :], sem).wait()


def _dispatch(x_rows, dest_flat, fill_start, fill_n, n_used, n_rows, tm=256):
    t, width = x_rows.shape
    grid_spec = pltpu.PrefetchScalarGridSpec(
        num_scalar_prefetch=3,
        grid=(t // tm,),
        in_specs=[pl.BlockSpec((tm * SLOT_STRIDE,), lambda i, fs, fn, nu: (i,), memory_space=pltpu.SMEM),
                  pl.BlockSpec((tm, width), lambda i, fs, fn, nu: (i, 0))],
        out_specs=pl.BlockSpec(memory_space=pl.ANY),
        scratch_shapes=[pltpu.VMEM((MOE_ROW_BLOCK, width), x_rows.dtype), pltpu.SemaphoreType.DMA(()),
                        pltpu.SemaphoreType.DMA(())],
    )
    return pl.pallas_call(
        functools.partial(_dispatch_kernel, tm=tm),
        grid_spec=grid_spec,
        out_shape=jax.ShapeDtypeStruct((n_rows, width), x_rows.dtype),
        compiler_params=_cparams(1),
        name="moe_dispatch",
    )(fill_start, fill_n, n_used, dest_flat, x_rows)


def _expert_kernel(blk_e_ref, n_used_ref, xs_ref, wg_ref, wu_ref, wd_ref, ys_ref, wg16_ref, wu16_ref, wd16_ref):
    i = pl.program_id(0)
    used = i < n_used_ref[0]
    new_expert = (i == 0) | (blk_e_ref[i] != blk_e_ref[jnp.maximum(i - 1, 0)])

    @pl.when(used & new_expert)
    def _():
        wg16_ref[...] = wg_ref[0, 0].astype(BF16)
        wu16_ref[...] = wu_ref[0, 0].astype(BF16)
        wd16_ref[...] = wd_ref[0, 0].astype(BF16)

    @pl.when(used)
    def _():
        lo, hi = _unpack_bf16_pairs(xs_ref[...])
        xb = jnp.concatenate([lo.astype(BF16), hi.astype(BF16)], axis=1)
        hid = jax.nn.silu(_dot(xb, wg16_ref[...])) * _dot(xb, wu16_ref[...])
        ys_ref[...] = _pack_bf16_pairs(_dot(hid.astype(BF16), wd16_ref[...]))

    @pl.when(jnp.logical_not(used))
    def _():
        ys_ref[...] = jnp.zeros_like(ys_ref)


def _expert_ffn(xs, blk_e, n_used, w_gate, w_up, w_down, layer):
    n_rows, width = xs.shape
    bm = MOE_ROW_BLOCK
    grid_spec = pltpu.PrefetchScalarGridSpec(
        num_scalar_prefetch=2,
        grid=(n_rows // bm,),
        in_specs=[pl.BlockSpec((bm, width), lambda i, be, nu: (jnp.minimum(i, nu[0] - 1), 0)),
                  pl.BlockSpec((1, 1, D_MODEL, D_EXPERT), lambda i, be, nu: (layer, be[i], 0, 0)),
                  pl.BlockSpec((1, 1, D_MODEL, D_EXPERT), lambda i, be, nu: (layer, be[i], 0, 0)),
                  pl.BlockSpec((1, 1, D_EXPERT, D_MODEL), lambda i, be, nu: (layer, be[i], 0, 0))],
        out_specs=pl.BlockSpec((bm, width), lambda i, be, nu: (i, 0)),
        scratch_shapes=[pltpu.VMEM((D_MODEL, D_EXPERT), BF16), pltpu.VMEM((D_MODEL, D_EXPERT), BF16),
                        pltpu.VMEM((D_EXPERT, D_MODEL), BF16)],
    )
    return pl.pallas_call(
        _expert_kernel,
        grid_spec=grid_spec,
        out_shape=jax.ShapeDtypeStruct((n_rows, width), xs.dtype),
        compiler_params=_cparams(1),
        name="moe_experts",
    )(blk_e, n_used, xs, w_gate, w_up, w_down)


def _combine_kernel(dest_ref, next_dest_ref, x_ref, wt_ref, ys_hbm, wsg_ref, wsu_ref, wsd_ref, g_ref, b_ref,
                    o32_ref, o16_ref, buf_ref, sem, *, tm):
    i = pl.program_id(0)

    def gather(table_ref, half):
        def issue(tok, _):
            for k in range(TOP_K):
                pltpu.make_async_copy(ys_hbm.at[pl.ds(table_ref[tok * SLOT_STRIDE + k], 1), :],
                                      buf_ref.at[half, k, pl.ds(tok, 1), :], sem.at[half]).start()
            return 0

        lax.fori_loop(0, tm, issue, 0)

    @pl.when(i == 0)
    def _():
        gather(dest_ref, 0)

    for half in range(2):
        @pl.when((i + 1 < pl.num_programs(0)) & ((i + 1) % 2 == half))
        def _():
            gather(next_dest_ref, half)

    x = x_ref[...]
    xb = x.astype(BF16)
    hid = jax.nn.silu(_dot(xb, wsg_ref[...])) * _dot(xb, wsu_ref[...])
    total = DEEPNORM_ALPHA * x + _dot(hid.astype(BF16), wsd_ref[...])
    for k in range(TOP_K):
        pltpu.make_async_copy(ys_hbm.at[pl.ds(0, tm), :], buf_ref.at[0, k], sem.at[i % 2]).wait()
    wt = wt_ref[...]
    routed_lo = routed_hi = None
    for k in range(TOP_K):
        lo, hi = _unpack_bf16_pairs(buf_ref[i % 2, k])
        w_k = wt[:, k:k + 1]
        routed_lo = w_k * lo if routed_lo is None else routed_lo + w_k * lo
        routed_hi = w_k * hi if routed_hi is None else routed_hi + w_k * hi
    total = total + jnp.concatenate([routed_lo, routed_hi], axis=1)
    out = _layer_norm_rows(total, g_ref[...], b_ref[...])
    o32_ref[...] = out
    o16_ref[...] = out.astype(o16_ref.dtype)


def _combine(x32, dest_flat, wts, ys, w_sg, w_su, w_sd, g, b, tm=256):
    t = x32.shape[0]
    n_steps = t // tm
    full = lambda arr: pl.BlockSpec(arr.shape, lambda i: (0,) * arr.ndim)
    rows = pl.BlockSpec((tm, D_MODEL), lambda i: (i, 0))
    return pl.pallas_call(
        functools.partial(_combine_kernel, tm=tm),
        grid=(n_steps,),
        in_specs=[pl.BlockSpec((tm * SLOT_STRIDE,), lambda i: (i,), memory_space=pltpu.SMEM),
                  pl.BlockSpec((tm * SLOT_STRIDE,), lambda i: (jnp.minimum(i + 1, n_steps - 1),),
                               memory_space=pltpu.SMEM),
                  rows,
                  pl.BlockSpec((tm, LANES), lambda i: (i, 0)),
                  pl.BlockSpec(memory_space=pl.ANY),
                  full(w_sg), full(w_su), full(w_sd), full(g), full(b)],
        out_specs=[rows, rows],
        out_shape=[jax.ShapeDtypeStruct((t, D_MODEL), F32), jax.ShapeDtypeStruct((t, D_MODEL), BF16)],
        scratch_shapes=[pltpu.VMEM((2, TOP_K, tm, ys.shape[1]), ys.dtype), pltpu.SemaphoreType.DMA((2,))],
        compiler_params=_cparams(1),
        name="moe_combine",
    )(dest_flat, dest_flat, x32, wts, ys, w_sg, w_su, w_sd, g, b)


def _rot_half_cols(w):
    half = w.shape[-1] // 2
    return jnp.concatenate([-w[..., half:], w[..., :half]], axis=-1)


def _pad_cols(w, width):
    return jnp.pad(w, [(0, 0)] * (w.ndim - 1) + [(0, width - w.shape[-1])])


def _rope_tables(seq):
    pos = jnp.arange(seq, dtype=F32)[:, None]

    def table(dim):
        half = dim // 2
        inv_freq = ROPE_THETA ** (-jnp.arange(half, dtype=F32) / half)
        ang = pos * inv_freq[None, :]
        return jnp.cos(ang), jnp.sin(ang)

    c64, s64 = table(MLA_ROPE_DIM)
    cos64 = _pad_cols(jnp.concatenate([c64, c64], axis=1), LANES)
    sin64 = _pad_cols(jnp.concatenate([s64, s64], axis=1), LANES)
    c128, s128 = table(MOBA_HEAD_DIM)
    cos128 = jnp.concatenate([c128, c128], axis=1)
    sin128 = jnp.concatenate([-s128, s128], axis=1)
    return cos64, sin64, cos128, sin128


def _layer(x32, x16, mem16, tables, avg, bsz, seq, mem_len, p, expert_weights, layer):
    (w_in, g_qa, w_q_up, g_kva, w_kv_up, sgu_ln_g, sgu_ln_b, w_spatial, b_spatial, w_mem_kv,
     w_br_a, w_br_b, w_br_c, w_br_m, w_o, ln1_g, ln1_b, w_router, router_bias,
     w_sh_gate, w_sh_up, w_sh_down, ln2_g, ln2_b) = p
    cos64, sin64, cos128, sin128 = tables
    t = x32.shape[0]
    row = lambda v: v.reshape(1, -1).astype(F32)

    c0 = MLA_Q_RANK + MLA_KV_RANK
    c1 = c0 + MLA_ROPE_DIM
    c2 = c1 + 2 * SGU_WIDTH
    c3 = c2 + 3 * MOBA_WIDTH
    c4 = c3 + MEM_WIDTH
    w_a = w_in[:, :c0].astype(BF16)
    w_kpe_raw = w_in[:, c0:c1]
    w_kpe = jnp.concatenate([_pad_cols(w_kpe_raw, LANES), _pad_cols(_rot_half_cols(w_kpe_raw), LANES)],
                            axis=1).astype(BF16)
    w_z = w_in[:, c1:c2].astype(BF16)
    w_c = w_in[:, c2:c3].astype(BF16)
    w_mq = w_in[:, c3:c4].astype(BF16)
    w_g = w_in[:, c4:].astype(BF16)

    w_qn = w_q_up[:, :, :MLA_NOPE_DIM].reshape(MLA_Q_RANK, -1).astype(BF16)
    w_qpe = w_q_up[:, :, MLA_NOPE_DIM:]
    w_qp = _pad_cols(w_qpe, LANES).reshape(MLA_Q_RANK, -1).astype(BF16)
    w_qpr = _pad_cols(_rot_half_cols(w_qpe), LANES).reshape(MLA_Q_RANK, -1).astype(BF16)
    w_kn = w_kv_up[:, :, :MLA_NOPE_DIM].reshape(MLA_KV_RANK, -1).astype(BF16)
    w_v = w_kv_up[:, :, MLA_NOPE_DIM:].reshape(MLA_KV_RANK, -1).T.astype(BF16)
    qt_a, k_a, vt_a = _mla_proj(x16, w_a, w_kpe, row(g_qa), row(g_kva), w_qn.T, w_qp.T, w_qpr.T, w_kn, w_v,
                                cos64, sin64, seq)
    o_a = _mla_attention(qt_a, k_a, vt_a, bsz, seq)

    bias_full = jnp.repeat(b_spatial.T.astype(F32), SGU_GROUP_DIM, axis=1)
    o_b = _sgu(x16, w_z, row(sgu_ln_g), row(sgu_ln_b), w_spatial.astype(F32), bias_full)

    k_c = _moba_k_proj(x16, w_c[:, MOBA_WIDTH:2 * MOBA_WIDTH], cos128, sin128, seq)
    w_qv_t = jnp.concatenate([w_c[:, :MOBA_WIDTH], w_c[:, 2 * MOBA_WIDTH:]], axis=1).T
    qvt_c = _moba_qv_proj(x16, w_qv_t, cos128.T, sin128.T, seq)
    o_c = _moba_attention(qvt_c, k_c, avg, bsz, seq)

    kv_mem = _matmul(mem16, w_mem_kv.astype(BF16), mem_len, 2 * MEM_WIDTH, BF16, "mem_kv")
    o_m = _mem_attention(x16, w_mq, kv_mem, bsz, seq, mem_len)

    y = _gated_sum(x16, o_a, o_b, o_c, o_m, w_g, w_br_a.astype(BF16), w_br_b.astype(BF16),
                   w_br_c.astype(BF16), w_br_m.astype(BF16))
    x32, x16, x_packed = _proj_norm(y, w_o.astype(BF16), x32, row(ln1_g), row(ln1_b))

    e_idx, wts, rank, counts = _router(x32, _pad_cols(w_router.astype(F32), LANES),
                                       _pad_cols(row(router_bias), LANES))
    bm = MOE_ROW_BLOCK
    n_blk = (t * TOP_K) // bm + N_EXPERTS
    counts = counts[0, :N_EXPERTS].astype(I32)
    padded = (counts + bm - 1) // bm * bm
    pad_ends = jnp.cumsum(padded)
    pad_starts = pad_ends - padded
    dest = _pad_cols((pad_starts[e_idx[:, :TOP_K]] + rank[:, :TOP_K]).astype(I32), SLOT_STRIDE).reshape(-1)
    blk_first = jnp.arange(n_blk, dtype=I32)[:, None] * bm
    blk_e = jnp.minimum(jnp.sum((pad_ends[None, :] <= blk_first).astype(I32), axis=1), N_EXPERTS - 1)
    n_used = (pad_ends[-1:] // bm).astype(I32)
    xs = _dispatch(x_packed, dest, pad_starts + counts, padded - counts, n_used, n_blk * bm)
    ys = _expert_ffn(xs, blk_e, n_used, *expert_weights, layer)
    return _combine(x32, dest, wts, ys, w_sh_gate.astype(BF16), w_sh_up.astype(BF16),
                    w_sh_down.astype(BF16), row(ln2_g), row(ln2_b))


def kernel(x, mem, w_in, g_qa, w_q_up, g_kva, w_kv_up, sgu_ln_g, sgu_ln_b, w_spatial, b_spatial, w_mem_kv, w_br_a, w_br_b, w_br_c, w_br_m, w_o, ln1_g, ln1_b, w_router, router_bias, w_exp_gate, w_exp_up, w_exp_down, w_sh_gate, w_sh_up, w_sh_down, ln2_g, ln2_b):
    bsz, seq, d = x.shape
    mem_len = mem.shape[1]
    params = (w_in, g_qa, w_q_up, g_kva, w_kv_up, sgu_ln_g, sgu_ln_b, w_spatial, b_spatial, w_mem_kv,
              w_br_a, w_br_b, w_br_c, w_br_m, w_o, ln1_g, ln1_b, w_router, router_bias,
              w_sh_gate, w_sh_up, w_sh_down, ln2_g, ln2_b)
    expert_weights = (w_exp_gate, w_exp_up, w_exp_down)
    tables = _rope_tables(seq)
    n_blocks = seq // MOBA_BLOCK
    blk_of_pos = jnp.arange(seq, dtype=I32)[None, :] // MOBA_BLOCK
    avg = jnp.where(blk_of_pos == jnp.arange(LANES, dtype=I32)[:, None], 1.0 / MOBA_BLOCK, 0.0).astype(BF16)
    assert n_blocks <= LANES
    x32 = x.reshape(bsz * seq, d)
    x16 = x32.astype(BF16)
    mem16 = mem.reshape(bsz * mem_len, d).astype(BF16)
    for l in range(DEPTH):
        x32, x16 = _layer(x32, x16, mem16, tables, avg, bsz, seq, mem_len, tuple(w[l] for w in params),
                          expert_weights, l)
    return x32.reshape(bsz, seq, d)
```

```python
import functools

import jax
import jax.numpy as jnp
from jax import lax
from jax.experimental import pallas as pl
from jax.experimental.pallas import tpu as pltpu

F32 = jnp.float32
BF16 = jnp.bfloat16
I32 = jnp.int32
U32 = jnp.uint32

D_MODEL = 2048
DEPTH = 2
MLA_HEADS = 8
MLA_Q_RANK = 512
MLA_KV_RANK = 256
MLA_NOPE_DIM = 128
MLA_ROPE_DIM = 64
MLA_V_DIM = 128
SGU_GROUPS = 8
SGU_GROUP_DIM = 128
SGU_WIDTH = SGU_GROUPS * SGU_GROUP_DIM
SGU_CHUNK = 128
MOBA_HEADS = 8
MOBA_HEAD_DIM = 128
MOBA_WIDTH = MOBA_HEADS * MOBA_HEAD_DIM
MOBA_BLOCK = 256
MOBA_TOPK = 3
MEM_HEADS = 4
MEM_HEAD_DIM = 128
MEM_WIDTH = MEM_HEADS * MEM_HEAD_DIM
N_BRANCH = 4
N_EXPERTS = 64
TOP_K = 6
D_EXPERT = 512
ROUTED_SCALE = 2.5
ROPE_THETA = 10000.0
DEEPNORM_ALPHA = (2 * DEPTH) ** 0.25

LANES = 128
SUBLANES = 8
ATTN_BLOCK = 256
MOE_ROW_BLOCK = 512
SLOT_STRIDE = 8
ROW_WORDS = D_MODEL // 2
ROW_SLAB = ROW_WORDS // LANES
NEG_BIG = -1e30
VMEM_LIMIT = 56 * 1024 * 1024


def _cparams(n_axes):
    return pltpu.CompilerParams(dimension_semantics=("arbitrary",) * n_axes,
                                vmem_limit_bytes=VMEM_LIMIT)


def _dot(a, b):
    return jnp.dot(a, b, preferred_element_type=F32)


def _dot_nt(a, b):
    return lax.dot_general(a, b, (((1,), (1,)), ((), ())), preferred_element_type=F32)


def _layer_norm_rows(v, g, b, eps=1e-5):
    mu = jnp.mean(v, axis=-1, keepdims=True)
    vc = v - mu
    var = jnp.mean(vc * vc, axis=-1, keepdims=True)
    return vc * lax.rsqrt(var + eps) * g + b


def _rms_norm_rows(v, g, eps=1e-6):
    return v * lax.rsqrt(jnp.mean(v * v, axis=-1, keepdims=True) + eps) * g


def _mm_kernel(x_ref, w_ref, o_ref):
    o_ref[...] = _dot(x_ref[...], w_ref[...]).astype(o_ref.dtype)


def _matmul(x, w, tm, tn, out_dtype, name):
    m, k = x.shape
    n = w.shape[1]
    return pl.pallas_call(
        _mm_kernel,
        grid=(n // tn, m // tm),
        in_specs=[pl.BlockSpec((tm, k), lambda j, i: (i, 0)),
                  pl.BlockSpec((k, tn), lambda j, i: (0, j))],
        out_specs=pl.BlockSpec((tm, tn), lambda j, i: (i, j)),
        out_shape=jax.ShapeDtypeStruct((m, n), out_dtype),
        compiler_params=_cparams(2),
        name=name,
    )(x, w)


def _mla_proj_kernel(x_ref, wa_ref, wkpe_ref, gq_ref, gkv_ref, wqnt_ref, wqpt_ref, wqprt_ref,
                     wkn_ref, wvt_ref, cos_ref, sin_ref, cost_ref, sint_ref, qt_ref, k_ref, vt_ref, *, scale):
    xb = x_ref[...]
    a = _dot(xb, wa_ref[...])
    cq = _rms_norm_rows(a[:, :MLA_Q_RANK], gq_ref[...]).astype(BF16)
    ckv = _rms_norm_rows(a[:, MLA_Q_RANK:], gkv_ref[...]).astype(BF16)
    qn_t = _dot_nt(wqnt_ref[...], cq)
    qp_t = _dot_nt(wqpt_ref[...], cq)
    qpr_t = _dot_nt(wqprt_ref[...], cq)
    v_t = _dot_nt(wvt_ref[...], ckv)
    cos_t = cost_ref[...]
    sin_t = sint_ref[...]
    for h in range(MLA_HEADS):
        lo, hi = h * LANES, (h + 1) * LANES
        qt_ref[0, h, 0, 0:LANES, :] = (qn_t[lo:hi] * scale).astype(qt_ref.dtype)
        qt_ref[0, h, 0, LANES:2 * LANES, :] = ((qp_t[lo:hi] * cos_t + qpr_t[lo:hi] * sin_t) * scale).astype(qt_ref.dtype)
        vt_ref[0, h, 0] = v_t[h * MLA_V_DIM:(h + 1) * MLA_V_DIM, :].astype(vt_ref.dtype)
    kn = _dot(ckv, wkn_ref[...])
    kp = _dot(xb, wkpe_ref[...])
    kpe = (kp[:, :LANES] * cos_ref[...] + kp[:, LANES:] * sin_ref[...]).astype(k_ref.dtype)
    for h in range(MLA_HEADS):
        lo, hi = h * LANES, (h + 1) * LANES
        k_ref[:, 2 * lo:2 * lo + LANES] = kn[:, lo:hi].astype(k_ref.dtype)
        k_ref[:, 2 * lo + LANES:2 * hi] = kpe


def _mla_proj(x, wa, wkpe, gq, gkv, wqnt, wqpt, wqprt, wkn, wvt, cos64, sin64, seq):
    t = x.shape[0]
    tm = ATTN_BLOCK
    full = lambda arr: pl.BlockSpec(arr.shape, lambda i: (0,) * arr.ndim)
    n_pos = seq // tm
    scale = (MLA_NOPE_DIM + MLA_ROPE_DIM) ** -0.5
    dk = 2 * LANES
    feat_major = lambda d: pl.BlockSpec((1, MLA_HEADS, 1, d, tm), lambda i: (i // n_pos, 0, i % n_pos, 0, 0))
    return pl.pallas_call(
        functools.partial(_mla_proj_kernel, scale=scale),
        grid=(t // tm,),
        in_specs=[pl.BlockSpec((tm, D_MODEL), lambda i: (i, 0)),
                  full(wa), full(wkpe), full(gq), full(gkv), full(wqnt), full(wqpt), full(wqprt),
                  full(wkn), full(wvt),
                  pl.BlockSpec((tm, LANES), lambda i: (i % n_pos, 0)),
                  pl.BlockSpec((tm, LANES), lambda i: (i % n_pos, 0)),
                  pl.BlockSpec((LANES, tm), lambda i: (0, i % n_pos)),
                  pl.BlockSpec((LANES, tm), lambda i: (0, i % n_pos))],
        out_specs=[feat_major(dk),
                   pl.BlockSpec((tm, MLA_HEADS * dk), lambda i: (i, 0)),
                   feat_major(MLA_V_DIM)],
        out_shape=[jax.ShapeDtypeStruct((t // seq, MLA_HEADS, n_pos, dk, tm), BF16),
                   jax.ShapeDtypeStruct((t, MLA_HEADS * dk), BF16),
                   jax.ShapeDtypeStruct((t // seq, MLA_HEADS, n_pos, MLA_V_DIM, tm), BF16)],
        compiler_params=_cparams(1),
        name="mla_proj",
    )(x, wa, wkpe, gq, gkv, wqnt, wqpt, wqprt, wkn, wvt, cos64, sin64, cos64.T, sin64.T)


def _softmax_steps(q_ts, ks, v_ts, carries, masks):
    heads = range(len(q_ts))
    s_ts = [_dot(ks[h], q_ts[h]) for h in heads]
    s_ts = [s_ts[h] if masks[h] is None else jnp.where(masks[h], s_ts[h], NEG_BIG) for h in heads]
    m_news = [jnp.maximum(carries[h][0], jnp.max(s_ts[h], axis=0, keepdims=True)) for h in heads]
    p_ts = [jnp.exp(s_ts[h] - m_news[h]) for h in heads]
    pvs = [_dot(v_ts[h], p_ts[h].astype(v_ts[h].dtype)) for h in heads]
    out = []
    for h in heads:
        m_i, l_i, acc_t = carries[h]
        alpha = jnp.exp(m_i - m_news[h])
        l_new = alpha * l_i + jnp.sum(p_ts[h], axis=0, keepdims=True)
        out.append((m_news[h], l_new, alpha * acc_t + pvs[h]))
    return tuple(out)


def _softmax_init(n_heads, tq, dv):
    return tuple((jnp.full((1, tq), NEG_BIG, F32), jnp.zeros((1, tq), F32), jnp.zeros((dv, tq), F32))
                 for _ in range(n_heads))


def _softmax_finish(o_ref, carries, dv):
    for h, (_, l_i, acc_t) in enumerate(carries):
        o_ref[:, h * dv:(h + 1) * dv] = (acc_t / l_i).T.astype(o_ref.dtype)


def _causal_attn_kernel(qt_ref, k_ref, vt_ref, o_ref, *, tq, n_heads, dk, dv):
    qi = pl.program_id(2)
    key = lax.broadcasted_iota(I32, (tq, tq), 0)
    qry = lax.broadcasted_iota(I32, (tq, tq), 1)
    qs = [qt_ref[0, h, 0] for h in range(n_heads)]

    def step(kb, carries, mask):
        ks = pl.multiple_of(kb * tq, tq)
        return _softmax_steps(qs, [k_ref[pl.ds(ks, tq), h * dk:(h + 1) * dk] for h in range(n_heads)],
                              [vt_ref[0, h, kb] for h in range(n_heads)], carries, [mask] * n_heads)

    carries = step(qi, _softmax_init(n_heads, tq, dv), key <= qry)
    carries = lax.fori_loop(0, qi, lambda kb, c: step(kb, c, None), carries)
    _softmax_finish(o_ref, carries, dv)


def _mla_attention(q_t, k_a, v_t, bsz, seq, n_heads=8):
    t = k_a.shape[0]
    tq = ATTN_BLOCK
    nq = seq // tq
    dk = 2 * LANES
    dv = MLA_V_DIM
    return pl.pallas_call(
        functools.partial(_causal_attn_kernel, tq=tq, n_heads=n_heads, dk=dk, dv=dv),
        grid=(bsz, MLA_HEADS // n_heads, nq),
        in_specs=[pl.BlockSpec((1, n_heads, 1, dk, tq), lambda b, h, i: (b, h, i, 0, 0)),
                  pl.BlockSpec((seq, n_heads * dk), lambda b, h, i: (b, h)),
                  pl.BlockSpec((1, n_heads, nq, dv, tq), lambda b, h, i: (b, h, 0, 0, 0))],
        out_specs=pl.BlockSpec((tq, n_heads * dv), lambda b, h, i: (b * nq + i, h)),
        out_shape=jax.ShapeDtypeStruct((t, MLA_HEADS * dv), BF16),
        compiler_params=_cparams(3),
        name="mla_attn",
    )(q_t, k_a, v_t)


def _sgu_kernel(x_ref, wz_ref, g_ref, b_ref, ws_ref, bs_ref, o_ref, *, tm):
    z = jax.nn.gelu(_dot(x_ref[...], wz_ref[...]))
    u = z[:, :SGU_WIDTH]
    v = _layer_norm_rows(z[:, SGU_WIDTH:], g_ref[...], b_ref[...]).astype(BF16)
    row = lax.broadcasted_iota(I32, (SGU_CHUNK, SGU_CHUNK), 0)
    col = lax.broadcasted_iota(I32, (SGU_CHUNK, SGU_CHUNK), 1)
    bias = bs_ref[...]
    for g in range(SGU_GROUPS):
        w = jnp.where(col <= row, ws_ref[g], 0.0).astype(BF16)
        lo, hi = g * SGU_GROUP_DIM, (g + 1) * SGU_GROUP_DIM
        for c in range(tm // SGU_CHUNK):
            r0, r1 = c * SGU_CHUNK, (c + 1) * SGU_CHUNK
            mixed = _dot(w, v[r0:r1, lo:hi]) + bias[:, lo:hi]
            o_ref[r0:r1, lo:hi] = (u[r0:r1, lo:hi] * mixed).astype(o_ref.dtype)


def _sgu(x, wz, ln_g, ln_b, w_s, bias_full, tm=256):
    t = x.shape[0]
    full = lambda arr: pl.BlockSpec(arr.shape, lambda i: (0,) * arr.ndim)
    return pl.pallas_call(
        functools.partial(_sgu_kernel, tm=tm),
        grid=(t // tm,),
        in_specs=[pl.BlockSpec((tm, D_MODEL), lambda i: (i, 0)),
                  full(wz), full(ln_g), full(ln_b), full(w_s), full(bias_full)],
        out_specs=pl.BlockSpec((tm, SGU_WIDTH), lambda i: (i, 0)),
        out_shape=jax.ShapeDtypeStruct((t, SGU_WIDTH), BF16),
        compiler_params=_cparams(1),
        name="sgu",
    )(x, wz, ln_g, ln_b, w_s, bias_full)


def _moba_k_proj_kernel(x_ref, w_ref, cos_ref, sin_ref, o_ref):
    acc = _dot(x_ref[...], w_ref[...])
    cos = cos_ref[...]
    sin = sin_ref[...]
    for h in range(MOBA_HEADS):
        seg = acc[:, h * LANES:(h + 1) * LANES]
        rot = pltpu.roll(seg, MOBA_HEAD_DIM // 2, axis=1)
        o_ref[:, h * LANES:(h + 1) * LANES] = (seg * cos + rot * sin).astype(o_ref.dtype)


def _moba_k_proj(x, w_k, cos128, sin128, seq, tm=512):
    t = x.shape[0]
    n_pos = seq // tm
    return pl.pallas_call(
        _moba_k_proj_kernel,
        grid=(t // tm,),
        in_specs=[pl.BlockSpec((tm, D_MODEL), lambda i: (i, 0)),
                  pl.BlockSpec(w_k.shape, lambda i: (0, 0)),
                  pl.BlockSpec((tm, LANES), lambda i: (i % n_pos, 0)),
                  pl.BlockSpec((tm, LANES), lambda i: (i % n_pos, 0))],
        out_specs=pl.BlockSpec((tm, MOBA_WIDTH), lambda i: (i, 0)),
        out_shape=jax.ShapeDtypeStruct((t, MOBA_WIDTH), BF16),
        compiler_params=_cparams(1),
        name="moba_k_proj",
    )(x, w_k, cos128, sin128)


def _moba_qv_proj_kernel(x_ref, wt_ref, cost_ref, sint_ref, o_ref, *, scale):
    y_t = _dot_nt(wt_ref[...], x_ref[...])
    hd = MOBA_HEAD_DIM

    @pl.when(pl.program_id(0) == 0)
    def _():
        cos_t = cost_ref[...]
        sin_t = sint_ref[...]
        for h in range(MOBA_HEADS):
            seg = y_t[h * hd:(h + 1) * hd]
            rot = jnp.concatenate([seg[hd // 2:], seg[:hd // 2]], axis=0)
            o_ref[0, h, 0] = ((seg * cos_t + rot * sin_t) * scale).astype(o_ref.dtype)

    @pl.when(pl.program_id(0) == 1)
    def _():
        for h in range(MOBA_HEADS):
            o_ref[0, h, 0] = y_t[h * hd:(h + 1) * hd].astype(o_ref.dtype)


def _moba_qv_proj(x, w_qv_t, cos128_t, sin128_t, seq):
    t = x.shape[0]
    tm = ATTN_BLOCK
    n_pos = seq // tm
    return pl.pallas_call(
        functools.partial(_moba_qv_proj_kernel, scale=MOBA_HEAD_DIM ** -0.5),
        grid=(2, t // tm),
        in_specs=[pl.BlockSpec((tm, D_MODEL), lambda j, i: (i, 0)),
                  pl.BlockSpec((MOBA_WIDTH, D_MODEL), lambda j, i: (j, 0)),
                  pl.BlockSpec((MOBA_HEAD_DIM, tm), lambda j, i: (0, i % n_pos)),
                  pl.BlockSpec((MOBA_HEAD_DIM, tm), lambda j, i: (0, i % n_pos))],
        out_specs=pl.BlockSpec((1, MOBA_HEADS, 1, MOBA_HEAD_DIM, tm),
                               lambda j, i: (i // n_pos, j, i % n_pos, 0, 0)),
        out_shape=jax.ShapeDtypeStruct((t // seq, 2 * MOBA_HEADS, n_pos, MOBA_HEAD_DIM, tm), BF16),
        compiler_params=_cparams(2),
        name="moba_qv_proj",
    )(x, w_qv_t, cos128_t, sin128_t)


def _moba_attn_kernel(qt_ref, k_ref, vt_ref, avg_ref, o_ref, kmean_ref, *, n_heads):
    j = pl.program_id(2)
    blk = MOBA_BLOCK
    hd = MOBA_HEAD_DIM

    @pl.when(j == 0)
    def _():
        kmean_ref[...] = _dot(avg_ref[...], k_ref[...])

    blk_id = lax.broadcasted_iota(I32, (SUBLANES, blk), 0)
    n_blocks = k_ref.shape[0] // blk
    qs, sels = [], []
    for h in range(n_heads):
        q = qt_ref[0, h, 0]
        gate = jnp.dot(kmean_ref[0:SUBLANES, h * hd:(h + 1) * hd], q.astype(F32),
                       precision=lax.Precision.HIGHEST, preferred_element_type=F32)
        rank = jnp.zeros((SUBLANES, blk), I32)
        for m in range(n_blocks):
            gm = gate[m:m + 1, :]
            beats = (gm > gate) | ((gm == gate) & (m < blk_id))
            rank = rank + jnp.where(beats & (m < j), 1, 0)
        qs.append(q)
        sels.append(jnp.where((rank < MOBA_TOPK) & (blk_id < j), 1.0, 0.0))

    def step(kb, carries, masks):
        ks = pl.multiple_of(kb * blk, blk)
        return _softmax_steps(qs, [k_ref[pl.ds(ks, blk), h * hd:(h + 1) * hd] for h in range(n_heads)],
                              [vt_ref[0, h, kb] for h in range(n_heads)], carries, masks)

    key = lax.broadcasted_iota(I32, (blk, blk), 0)
    qry = lax.broadcasted_iota(I32, (blk, blk), 1)
    carries = step(j, _softmax_init(n_heads, blk, hd), [key <= qry] * n_heads)

    def body(n, c):
        masks = [jnp.max(jnp.where(blk_id == n, sels[h], 0.0), axis=0, keepdims=True) > 0.5
                 for h in range(n_heads)]
        return step(n, c, masks)

    carries = lax.fori_loop(0, j, body, carries)
    _softmax_finish(o_ref, carries, hd)


def _moba_attention(qv_t, k, avg, bsz, seq, n_heads=8):
    t = k.shape[0]
    nq = seq // MOBA_BLOCK
    assert nq <= SUBLANES and MOBA_BLOCK == ATTN_BLOCK
    w = n_heads * MOBA_HEAD_DIM
    groups = MOBA_HEADS // n_heads
    return pl.pallas_call(
        functools.partial(_moba_attn_kernel, n_heads=n_heads),
        grid=(bsz, groups, nq),
        in_specs=[pl.BlockSpec((1, n_heads, 1, MOBA_HEAD_DIM, MOBA_BLOCK), lambda b, h, i: (b, h, i, 0, 0)),
                  pl.BlockSpec((seq, w), lambda b, h, i: (b, h)),
                  pl.BlockSpec((1, n_heads, nq, MOBA_HEAD_DIM, MOBA_BLOCK),
                               lambda b, h, i: (b, groups + h, 0, 0, 0)),
                  pl.BlockSpec(avg.shape, lambda b, h, i: (0, 0))],
        out_specs=pl.BlockSpec((MOBA_BLOCK, w), lambda b, h, i: (b * nq + i, h)),
        out_shape=jax.ShapeDtypeStruct((t, MOBA_WIDTH), BF16),
        scratch_shapes=[pltpu.VMEM((LANES, w), F32)],
        compiler_params=_cparams(3),
        name="moba_attn",
    )(qv_t, k, qv_t, avg)


def _mem_attn_kernel(x_ref, wq_ref, kv_ref, o_ref, *, scale):
    q = (_dot(x_ref[...], wq_ref[...]) * scale).astype(BF16)
    for h in range(MEM_HEADS):
        lo, hi = h * MEM_HEAD_DIM, (h + 1) * MEM_HEAD_DIM
        s = _dot_nt(q[:, lo:hi], kv_ref[:, lo:hi])
        p = jnp.exp(s - jnp.max(s, axis=1, keepdims=True))
        o = _dot(p.astype(BF16), kv_ref[:, MEM_WIDTH + lo:MEM_WIDTH + hi])
        o_ref[:, lo:hi] = (o / jnp.sum(p, axis=1, keepdims=True)).astype(o_ref.dtype)


def _mem_attention(x, w_mq, kv_mem, bsz, seq, mem_len, tm=512):
    t = x.shape[0]
    ns = seq // tm
    return pl.pallas_call(
        functools.partial(_mem_attn_kernel, scale=MEM_HEAD_DIM ** -0.5),
        grid=(bsz, ns),
        in_specs=[pl.BlockSpec((tm, D_MODEL), lambda b, i: (b * ns + i, 0)),
                  pl.BlockSpec(w_mq.shape, lambda b, i: (0, 0)),
                  pl.BlockSpec((mem_len, 2 * MEM_WIDTH), lambda b, i: (b, 0))],
        out_specs=pl.BlockSpec((tm, MEM_WIDTH), lambda b, i: (b * ns + i, 0)),
        out_shape=jax.ShapeDtypeStruct((t, MEM_WIDTH), BF16),
        compiler_params=_cparams(2),
        name="mem_attn",
    )(x, w_mq, kv_mem)


def _gated_sum_kernel(x_ref, oa_ref, ob_ref, oc_ref, om_ref, g0_ref, g1_ref, g2_ref, g3_ref,
                      wa_ref, wb_ref, wc_ref, wm_ref, y_ref):
    xb = x_ref[...]
    acc = None
    for o_ref, g_ref, w_ref in ((oa_ref, g0_ref, wa_ref), (ob_ref, g1_ref, wb_ref),
                                (oc_ref, g2_ref, wc_ref), (om_ref, g3_ref, wm_ref)):
        term = jax.nn.sigmoid(_dot(xb, g_ref[...])) * _dot(o_ref[...], w_ref[...])
        acc = term if acc is None else acc + term
    y_ref[...] = acc.astype(y_ref.dtype)


def _gated_sum(x, o_a, o_b, o_c, o_m, w_gate, w_br_a, w_br_b, w_br_c, w_br_m, tm=1024, tn=256):
    t = x.shape[0]
    nj = D_MODEL // tn
    rows = lambda width: pl.BlockSpec((tm, width), lambda j, i: (i, 0))
    gate = lambda b: pl.BlockSpec((D_MODEL, tn), lambda j, i: (0, b * nj + j))
    cols = lambda width: pl.BlockSpec((width, tn), lambda j, i: (0, j))
    return pl.pallas_call(
        _gated_sum_kernel,
        grid=(nj, t // tm),
        in_specs=[rows(D_MODEL), rows(o_a.shape[1]), rows(o_b.shape[1]), rows(o_c.shape[1]), rows(o_m.shape[1]),
                  gate(0), gate(1), gate(2), gate(3),
                  cols(w_br_a.shape[0]), cols(w_br_b.shape[0]), cols(w_br_c.shape[0]), cols(w_br_m.shape[0])],
        out_specs=pl.BlockSpec((tm, tn), lambda j, i: (i, j)),
        out_shape=jax.ShapeDtypeStruct((t, D_MODEL), BF16),
        compiler_params=_cparams(2),
        name="gated_sum",
    )(x, o_a, o_b, o_c, o_m, w_gate, w_gate, w_gate, w_gate, w_br_a, w_br_b, w_br_c, w_br_m)


def _pack_bf16_pairs(v):
    bits = lax.bitcast_convert_type(v.astype(BF16).astype(F32), U32)
    half = v.shape[1] // 2
    return (bits[:, :half] >> 16) | bits[:, half:]


def _unpack_bf16_pairs(words):
    lo = lax.bitcast_convert_type(words << 16, F32)
    hi = lax.bitcast_convert_type(words & jnp.uint32(0xFFFF0000), F32)
    return lo, hi


def _store_row_slabs(ref, words):
    for c in range(ROW_SLAB):
        ref[pl.ds(c, words.shape[0], stride=ROW_SLAB), :] = words[:, c * LANES:(c + 1) * LANES]


def _load_row_slabs(ref):
    n_rows = ref.shape[0] // ROW_SLAB
    return jnp.concatenate([ref[pl.ds(c, n_rows, stride=ROW_SLAB), :] for c in range(ROW_SLAB)], axis=1)


def _proj_norm_kernel(y_ref, w_ref, res_ref, g_ref, b_ref, o32_ref, o16_ref, opk_ref):
    v = DEEPNORM_ALPHA * res_ref[...] + _dot(y_ref[...], w_ref[...])
    out = _layer_norm_rows(v, g_ref[...], b_ref[...])
    o32_ref[...] = out
    o16_ref[...] = out.astype(o16_ref.dtype)
    opk_ref[...] = _pack_bf16_pairs(out)


def _proj_norm(y, w_o, res, g, b, tm=256):
    t = y.shape[0]
    full = lambda arr: pl.BlockSpec(arr.shape, lambda i: (0,) * arr.ndim)
    rows = pl.BlockSpec((tm, D_MODEL), lambda i: (i, 0))
    return pl.pallas_call(
        _proj_norm_kernel,
        grid=(t // tm,),
        in_specs=[rows, full(w_o), rows, full(g), full(b)],
        out_specs=[rows, rows, pl.BlockSpec((tm, D_MODEL // 2), lambda i: (i, 0))],
        out_shape=[jax.ShapeDtypeStruct((t, D_MODEL), F32), jax.ShapeDtypeStruct((t, D_MODEL), BF16),
                   jax.ShapeDtypeStruct((t, D_MODEL // 2), U32)],
        compiler_params=_cparams(1),
        name="proj_norm",
    )(y, w_o, res, g, b)


def _router_kernel(x_ref, w_ref, bias_ref, e_ref, wt_ref, rank_ref, cnt_ref, run_ref, *, tm):
    i = pl.program_id(0)

    @pl.when(i == 0)
    def _():
        run_ref[...] = jnp.zeros_like(run_ref)

    logits = jnp.dot(x_ref[...], w_ref[...], precision=lax.Precision.HIGHEST, preferred_element_type=F32)
    scores = jax.nn.sigmoid(logits)
    lane = lax.broadcasted_iota(I32, (tm, LANES), 1)
    biased = jnp.where(lane < N_EXPERTS, scores + bias_ref[...], NEG_BIG)
    picks = []
    chosen = jnp.zeros((tm, LANES), jnp.bool_)
    for _ in range(TOP_K):
        mx = jnp.max(biased, axis=1, keepdims=True)
        idx = jnp.min(jnp.where(biased == mx, lane, LANES), axis=1, keepdims=True)
        hit = lane == idx
        picks.append((idx, hit))
        chosen = chosen | hit
        biased = jnp.where(hit, 2 * NEG_BIG, biased)
    picked_scores = jnp.where(chosen, scores, 0.0)
    norm = ROUTED_SCALE / jnp.sum(picked_scores, axis=1, keepdims=True)

    r = lax.broadcasted_iota(I32, (tm, tm), 0)
    c = lax.broadcasted_iota(I32, (tm, tm), 1)
    strict_lower = jnp.where(c < r, 1.0, 0.0).astype(BF16)
    chosen_f = jnp.where(chosen, 1.0, 0.0)
    arrival = run_ref[0:1, :] + _dot(strict_lower, chosen_f.astype(BF16))
    run_ref[...] = run_ref[...] + jnp.sum(chosen_f, axis=0, keepdims=True)
    cnt_ref[...] = run_ref[...]

    e_out = jnp.zeros((tm, LANES), I32)
    w_out = jnp.zeros((tm, LANES), F32)
    r_out = jnp.zeros((tm, LANES), F32)
    for slot, (idx, hit) in enumerate(picks):
        here = lane == slot
        e_out = jnp.where(here, idx, e_out)
        w_out = jnp.where(here, jnp.sum(jnp.where(hit, scores, 0.0), axis=1, keepdims=True) * norm, w_out)
        r_out = jnp.where(here, jnp.sum(jnp.where(hit, arrival, 0.0), axis=1, keepdims=True), r_out)
    e_ref[...] = e_out
    wt_ref[...] = w_out
    rank_ref[...] = r_out.astype(I32)


def _router(x32, w_router_pad, bias_pad, tm=256):
    t = x32.shape[0]
    full = lambda arr: pl.BlockSpec(arr.shape, lambda i: (0,) * arr.ndim)
    rows = pl.BlockSpec((tm, LANES), lambda i: (i, 0))
    return pl.pallas_call(
        functools.partial(_router_kernel, tm=tm),
        grid=(t // tm,),
        in_specs=[pl.BlockSpec((tm, D_MODEL), lambda i: (i, 0)), full(w_router_pad), full(bias_pad)],
        out_specs=[rows, rows, rows, pl.BlockSpec((8, LANES), lambda i: (0, 0))],
        out_shape=[jax.ShapeDtypeStruct((t, LANES), I32), jax.ShapeDtypeStruct((t, LANES), F32),
                   jax.ShapeDtypeStruct((t, LANES), I32), jax.ShapeDtypeStruct((8, LANES), F32)],
        scratch_shapes=[pltpu.VMEM((8, LANES), F32)],
        compiler_params=_cparams(1),
        name="moe_router",
    )(x32, w_router_pad, bias_pad)


def _dispatch_kernel(fill_start_ref, fill_n_ref, n_used_ref, dest_ref, x_ref, out_hbm, zero_ref, sem, fill_sem,
                     *, tm):
    @pl.when(pl.program_id(0) == 0)
    def _():
        zero_ref[...] = jnp.zeros_like(zero_ref)
        bm = zero_ref.shape[0]
        n_blk = out_hbm.shape[0] // bm

        def fill_expert(e, wait):
            def run(copy):
                copy.wait() if wait else copy.start()

            n = fill_n_ref[e]
            first = fill_start_ref[e]
            head = (-first) & (SUBLANES - 1)
            lax.fori_loop(0, head, lambda r, c: (run(pltpu.make_async_copy(
                zero_ref.at[pl.ds(0, 1), :], out_hbm.at[pl.ds(first + r, 1), :], fill_sem)), c)[1], 0)
            row = first + head
            rest = n - head
            for bit in range(SUBLANES.bit_length() - 1, bm.bit_length() - 1):
                size = 1 << bit

                @pl.when((rest & size) != 0)
                def _():
                    run(pltpu.make_async_copy(zero_ref.at[pl.ds(0, size), :],
                                              out_hbm.at[pl.ds(pl.multiple_of(row, SUBLANES), size), :], fill_sem))

                row = row + (rest & size)

        def tail_copy(blk):
            return pltpu.make_async_copy(zero_ref, out_hbm.at[pl.ds(pl.multiple_of(blk * bm, bm), bm), :], fill_sem)

        lax.fori_loop(0, N_EXPERTS, lambda e, c: (fill_expert(e, False), c)[1], 0)
        lax.fori_loop(n_used_ref[0], n_blk, lambda b, c: (tail_copy(b).start(), c)[1], 0)
        lax.fori_loop(0, N_EXPERTS, lambda e, c: (fill_expert(e, True), c)[1], 0)
        lax.fori_loop(n_used_ref[0], n_blk, lambda b, c: (tail_copy(0).wait(), c)[1], 0)

    def issue(tok, _):
        for k in range(TOP_K):
            pltpu.make_async_copy(x_ref.at[pl.ds(tok, 1), :],
                                  out_hbm.at[pl.ds(dest_ref[tok * SLOT_STRIDE + k], 1), :], sem).start()
        return 0

    lax.fori_loop(0, tm, issue, 0)
    for _ in range(TOP_K):
        pltpu.make_async_copy(x_ref, out_hbm.at[pl.ds(0, tm), :], sem).wait()


def _dispatch(x_rows, dest_flat, fill_start, fill_n, n_used, n_rows, tm=256):
    t, width = x_rows.shape
    grid_spec = pltpu.PrefetchScalarGridSpec(
        num_scalar_prefetch=3,
        grid=(t // tm,),
        in_specs=[pl.BlockSpec((tm * SLOT_STRIDE,), lambda i, fs, fn, nu: (i,), memory_space=pltpu.SMEM),
                  pl.BlockSpec((tm, width), lambda i, fs, fn, nu: (i, 0))],
        out_specs=pl.BlockSpec(memory_space=pl.ANY),
        scratch_shapes=[pltpu.VMEM((MOE_ROW_BLOCK, width), x_rows.dtype), pltpu.SemaphoreType.DMA(()),
                        pltpu.SemaphoreType.DMA(())],
    )
    return pl.pallas_call(
        functools.partial(_dispatch_kernel, tm=tm),
        grid_spec=grid_spec,
        out_shape=jax.ShapeDtypeStruct((n_rows, width), x_rows.dtype),
        compiler_params=_cparams(1),
        name="moe_dispatch",
    )(fill_start, fill_n, n_used, dest_flat, x_rows)


def _expert_kernel(blk_e_ref, n_used_ref, xs_ref, wg_ref, wu_ref, wd_ref, ys_ref, wg16_ref, wu16_ref, wd16_ref):
    i = pl.program_id(0)
    used = i < n_used_ref[0]
    new_expert = (i == 0) | (blk_e_ref[i] != blk_e_ref[jnp.maximum(i - 1, 0)])

    @pl.when(used & new_expert)
    def _():
        wg16_ref[...] = wg_ref[0, 0].astype(BF16)
        wu16_ref[...] = wu_ref[0, 0].astype(BF16)
        wd16_ref[...] = wd_ref[0, 0].astype(BF16)

    @pl.when(used)
    def _():
        lo, hi = _unpack_bf16_pairs(xs_ref[...])
        xb = jnp.concatenate([lo.astype(BF16), hi.astype(BF16)], axis=1)
        hid = jax.nn.silu(_dot(xb, wg16_ref[...])) * _dot(xb, wu16_ref[...])
        _store_row_slabs(ys_ref, _pack_bf16_pairs(_dot(hid.astype(BF16), wd16_ref[...])))

    @pl.when(jnp.logical_not(used))
    def _():
        ys_ref[...] = jnp.zeros_like(ys_ref)


def _expert_ffn(xs, blk_e, n_used, w_gate, w_up, w_down, layer):
    n_rows, width = xs.shape
    bm = MOE_ROW_BLOCK
    grid_spec = pltpu.PrefetchScalarGridSpec(
        num_scalar_prefetch=2,
        grid=(n_rows // bm,),
        in_specs=[pl.BlockSpec((bm, width), lambda i, be, nu: (jnp.minimum(i, nu[0] - 1), 0)),
                  pl.BlockSpec((1, 1, D_MODEL, D_EXPERT), lambda i, be, nu: (layer, be[i], 0, 0)),
                  pl.BlockSpec((1, 1, D_MODEL, D_EXPERT), lambda i, be, nu: (layer, be[i], 0, 0)),
                  pl.BlockSpec((1, 1, D_EXPERT, D_MODEL), lambda i, be, nu: (layer, be[i], 0, 0))],
        out_specs=pl.BlockSpec((bm * ROW_SLAB, LANES), lambda i, be, nu: (i, 0)),
        scratch_shapes=[pltpu.VMEM((D_MODEL, D_EXPERT), BF16), pltpu.VMEM((D_MODEL, D_EXPERT), BF16),
                        pltpu.VMEM((D_EXPERT, D_MODEL), BF16)],
    )
    return pl.pallas_call(
        _expert_kernel,
        grid_spec=grid_spec,
        out_shape=jax.ShapeDtypeStruct((n_rows * ROW_SLAB, LANES), xs.dtype),
        compiler_params=_cparams(1),
        name="moe_experts",
    )(blk_e, n_used, xs, w_gate, w_up, w_down)


def _combine_kernel(dest_ref, next_dest_ref, x_ref, wt_ref, ys_hbm, wsg_ref, wsu_ref, wsd_ref, g_ref, b_ref,
                    o32_ref, o16_ref, buf_ref, sem, *, tm):
    i = pl.program_id(0)

    def tile(ref, r):
        return ref.at[pl.ds(pl.multiple_of(r * ROW_SLAB, ROW_SLAB), ROW_SLAB), :]

    def gather(table_ref, half):
        def issue(tok, _):
            for k in range(TOP_K):
                pltpu.make_async_copy(tile(ys_hbm, table_ref[tok * SLOT_STRIDE + k]),
                                      tile(buf_ref.at[half, k], tok), sem.at[half]).start()
            return 0

        lax.fori_loop(0, tm, issue, 0)

    @pl.when(i == 0)
    def _():
        gather(dest_ref, 0)

    for half in range(2):
        @pl.when((i + 1 < pl.num_programs(0)) & ((i + 1) % 2 == half))
        def _():
            gather(next_dest_ref, half)

    x = x_ref[...]
    xb = x.astype(BF16)
    hid = jax.nn.silu(_dot(xb, wsg_ref[...])) * _dot(xb, wsu_ref[...])
    total = DEEPNORM_ALPHA * x + _dot(hid.astype(BF16), wsd_ref[...])
    for k in range(TOP_K):
        pltpu.make_async_copy(ys_hbm.at[pl.ds(0, tm * ROW_SLAB), :], buf_ref.at[0, k], sem.at[i % 2]).wait()
    wt = wt_ref[...]
    routed_lo = routed_hi = None
    for k in range(TOP_K):
        lo, hi = _unpack_bf16_pairs(_load_row_slabs(buf_ref.at[i % 2, k]))
        w_k = wt[:, k:k + 1]
        routed_lo = w_k * lo if routed_lo is None else routed_lo + w_k * lo
        routed_hi = w_k * hi if routed_hi is None else routed_hi + w_k * hi
    total = total + jnp.concatenate([routed_lo, routed_hi], axis=1)
    out = _layer_norm_rows(total, g_ref[...], b_ref[...])
    o32_ref[...] = out
    o16_ref[...] = out.astype(o16_ref.dtype)


def _combine(x32, dest_flat, wts, ys, w_sg, w_su, w_sd, g, b, tm=256):
    t = x32.shape[0]
    n_steps = t // tm
    full = lambda arr: pl.BlockSpec(arr.shape, lambda i: (0,) * arr.ndim)
    rows = pl.BlockSpec((tm, D_MODEL), lambda i: (i, 0))
    return pl.pallas_call(
        functools.partial(_combine_kernel, tm=tm),
        grid=(n_steps,),
        in_specs=[pl.BlockSpec((tm * SLOT_STRIDE,), lambda i: (i,), memory_space=pltpu.SMEM),
                  pl.BlockSpec((tm * SLOT_STRIDE,), lambda i: (jnp.minimum(i + 1, n_steps - 1),),
                               memory_space=pltpu.SMEM),
                  rows,
                  pl.BlockSpec((tm, LANES), lambda i: (i, 0)),
                  pl.BlockSpec(memory_space=pl.ANY),
                  full(w_sg), full(w_su), full(w_sd), full(g), full(b)],
        out_specs=[rows, rows],
        out_shape=[jax.ShapeDtypeStruct((t, D_MODEL), F32), jax.ShapeDtypeStruct((t, D_MODEL), BF16)],
        scratch_shapes=[pltpu.VMEM((2, TOP_K, tm * ROW_SLAB, LANES), ys.dtype), pltpu.SemaphoreType.DMA((2,))],
        compiler_params=_cparams(1),
        name="moe_combine",
    )(dest_flat, dest_flat, x32, wts, ys, w_sg, w_su, w_sd, g, b)


def _rot_half_cols(w):
    half = w.shape[-1] // 2
    return jnp.concatenate([-w[..., half:], w[..., :half]], axis=-1)


def _pad_cols(w, width):
    return jnp.pad(w, [(0, 0)] * (w.ndim - 1) + [(0, width - w.shape[-1])])


def _rope_tables(seq):
    pos = jnp.arange(seq, dtype=F32)[:, None]

    def table(dim):
        half = dim // 2
        inv_freq = ROPE_THETA ** (-jnp.arange(half, dtype=F32) / half)
        ang = pos * inv_freq[None, :]
        return jnp.cos(ang), jnp.sin(ang)

    c64, s64 = table(MLA_ROPE_DIM)
    cos64 = _pad_cols(jnp.concatenate([c64, c64], axis=1), LANES)
    sin64 = _pad_cols(jnp.concatenate([s64, s64], axis=1), LANES)
    c128, s128 = table(MOBA_HEAD_DIM)
    cos128 = jnp.concatenate([c128, c128], axis=1)
    sin128 = jnp.concatenate([-s128, s128], axis=1)
    return cos64, sin64, cos128, sin128


def _layer(x32, x16, mem16, tables, avg, bsz, seq, mem_len, p, expert_weights, layer):
    (w_in, g_qa, w_q_up, g_kva, w_kv_up, sgu_ln_g, sgu_ln_b, w_spatial, b_spatial, w_mem_kv,
     w_br_a, w_br_b, w_br_c, w_br_m, w_o, ln1_g, ln1_b, w_router, router_bias,
     w_sh_gate, w_sh_up, w_sh_down, ln2_g, ln2_b) = p
    cos64, sin64, cos128, sin128 = tables
    t = x32.shape[0]
    row = lambda v: v.reshape(1, -1).astype(F32)

    c0 = MLA_Q_RANK + MLA_KV_RANK
    c1 = c0 + MLA_ROPE_DIM
    c2 = c1 + 2 * SGU_WIDTH
    c3 = c2 + 3 * MOBA_WIDTH
    c4 = c3 + MEM_WIDTH
    w_a = w_in[:, :c0].astype(BF16)
    w_kpe_raw = w_in[:, c0:c1]
    w_kpe = jnp.concatenate([_pad_cols(w_kpe_raw, LANES), _pad_cols(_rot_half_cols(w_kpe_raw), LANES)],
                            axis=1).astype(BF16)
    w_z = w_in[:, c1:c2].astype(BF16)
    w_c = w_in[:, c2:c3].astype(BF16)
    w_mq = w_in[:, c3:c4].astype(BF16)
    w_g = w_in[:, c4:].astype(BF16)

    w_qn = w_q_up[:, :, :MLA_NOPE_DIM].reshape(MLA_Q_RANK, -1).astype(BF16)
    w_qpe = w_q_up[:, :, MLA_NOPE_DIM:]
    w_qp = _pad_cols(w_qpe, LANES).reshape(MLA_Q_RANK, -1).astype(BF16)
    w_qpr = _pad_cols(_rot_half_cols(w_qpe), LANES).reshape(MLA_Q_RANK, -1).astype(BF16)
    w_kn = w_kv_up[:, :, :MLA_NOPE_DIM].reshape(MLA_KV_RANK, -1).astype(BF16)
    w_v = w_kv_up[:, :, MLA_NOPE_DIM:].reshape(MLA_KV_RANK, -1).T.astype(BF16)
    qt_a, k_a, vt_a = _mla_proj(x16, w_a, w_kpe, row(g_qa), row(g_kva), w_qn.T, w_qp.T, w_qpr.T, w_kn, w_v,
                                cos64, sin64, seq)
    o_a = _mla_attention(qt_a, k_a, vt_a, bsz, seq)

    bias_full = jnp.repeat(b_spatial.T.astype(F32), SGU_GROUP_DIM, axis=1)
    o_b = _sgu(x16, w_z, row(sgu_ln_g), row(sgu_ln_b), w_spatial.astype(F32), bias_full)

    k_c = _moba_k_proj(x16, w_c[:, MOBA_WIDTH:2 * MOBA_WIDTH], cos128, sin128, seq)
    w_qv_t = jnp.concatenate([w_c[:, :MOBA_WIDTH], w_c[:, 2 * MOBA_WIDTH:]], axis=1).T
    qvt_c = _moba_qv_proj(x16, w_qv_t, cos128.T, sin128.T, seq)
    o_c = _moba_attention(qvt_c, k_c, avg, bsz, seq)

    kv_mem = _matmul(mem16, w_mem_kv.astype(BF16), mem_len, 2 * MEM_WIDTH, BF16, "mem_kv")
    o_m = _mem_attention(x16, w_mq, kv_mem, bsz, seq, mem_len)

    y = _gated_sum(x16, o_a, o_b, o_c, o_m, w_g, w_br_a.astype(BF16), w_br_b.astype(BF16),
                   w_br_c.astype(BF16), w_br_m.astype(BF16))
    x32, x16, x_packed = _proj_norm(y, w_o.astype(BF16), x32, row(ln1_g), row(ln1_b))

    e_idx, wts, rank, counts = _router(x32, _pad_cols(w_router.astype(F32), LANES),
                                       _pad_cols(row(router_bias), LANES))
    bm = MOE_ROW_BLOCK
    n_blk = (t * TOP_K) // bm + N_EXPERTS
    counts = counts[0, :N_EXPERTS].astype(I32)
    padded = (counts + bm - 1) // bm * bm
    pad_ends = jnp.cumsum(padded)
    pad_starts = pad_ends - padded
    dest = _pad_cols((pad_starts[e_idx[:, :TOP_K]] + rank[:, :TOP_K]).astype(I32), SLOT_STRIDE).reshape(-1)
    blk_first = jnp.arange(n_blk, dtype=I32)[:, None] * bm
    blk_e = jnp.minimum(jnp.sum((pad_ends[None, :] <= blk_first).astype(I32), axis=1), N_EXPERTS - 1)
    n_used = (pad_ends[-1:] // bm).astype(I32)
    xs = _dispatch(x_packed, dest, pad_starts + counts, padded - counts, n_used, n_blk * bm)
    ys = _expert_ffn(xs, blk_e, n_used, *expert_weights, layer)
    return _combine(x32, dest, wts, ys, w_sh_gate.astype(BF16), w_sh_up.astype(BF16),
                    w_sh_down.astype(BF16), row(ln2_g), row(ln2_b))


def kernel(x, mem, w_in, g_qa, w_q_up, g_kva, w_kv_up, sgu_ln_g, sgu_ln_b, w_spatial, b_spatial, w_mem_kv, w_br_a, w_br_b, w_br_c, w_br_m, w_o, ln1_g, ln1_b, w_router, router_bias, w_exp_gate, w_exp_up, w_exp_down, w_sh_gate, w_sh_up, w_sh_down, ln2_g, ln2_b):
    bsz, seq, d = x.shape
    mem_len = mem.shape[1]
    params = (w_in, g_qa, w_q_up, g_kva, w_kv_up, sgu_ln_g, sgu_ln_b, w_spatial, b_spatial, w_mem_kv,
              w_br_a, w_br_b, w_br_c, w_br_m, w_o, ln1_g, ln1_b, w_router, router_bias,
              w_sh_gate, w_sh_up, w_sh_down, ln2_g, ln2_b)
    expert_weights = (w_exp_gate, w_exp_up, w_exp_down)
    tables = _rope_tables(seq)
    n_blocks = seq // MOBA_BLOCK
    blk_of_pos = jnp.arange(seq, dtype=I32)[None, :] // MOBA_BLOCK
    avg = jnp.where(blk_of_pos == jnp.arange(LANES, dtype=I32)[:, None], 1.0 / MOBA_BLOCK, 0.0).astype(BF16)
    assert n_blocks <= LANES
    x32 = x.reshape(bsz * seq, d)
    x16 = x32.astype(BF16)
    mem16 = mem.reshape(bsz * mem_len, d).astype(BF16)
    for l in range(DEPTH):
        x32, x16 = _layer(x32, x16, mem16, tables, avg, bsz, seq, mem_len, tuple(w[l] for w in params),
                          expert_weights, l)
    return x32.reshape(bsz, seq, d)
```

```python
import functools

import jax
import jax.numpy as jnp
from jax import lax
from jax.experimental import pallas as pl
from jax.experimental.pallas import tpu as pltpu

F32 = jnp.float32
BF16 = jnp.bfloat16
I32 = jnp.int32
U32 = jnp.uint32

D_MODEL = 2048
DEPTH = 2
MLA_HEADS = 8
MLA_Q_RANK = 512
MLA_KV_RANK = 256
MLA_NOPE_DIM = 128
MLA_ROPE_DIM = 64
MLA_V_DIM = 128
SGU_GROUPS = 8
SGU_GROUP_DIM = 128
SGU_WIDTH = SGU_GROUPS * SGU_GROUP_DIM
SGU_CHUNK = 128
MOBA_HEADS = 8
MOBA_HEAD_DIM = 128
MOBA_WIDTH = MOBA_HEADS * MOBA_HEAD_DIM
MOBA_BLOCK = 256
MOBA_TOPK = 3
MEM_HEADS = 4
MEM_HEAD_DIM = 128
MEM_WIDTH = MEM_HEADS * MEM_HEAD_DIM
N_BRANCH = 4
N_EXPERTS = 64
TOP_K = 6
D_EXPERT = 512
ROUTED_SCALE = 2.5
ROPE_THETA = 10000.0
DEEPNORM_ALPHA = (2 * DEPTH) ** 0.25

LANES = 128
SUBLANES = 8
ATTN_BLOCK = 256
MOE_ROW_BLOCK = 512
SLOT_STRIDE = 8
ROW_WORDS = D_MODEL // 2
ROW_SLAB = ROW_WORDS // LANES
NEG_BIG = -1e30
VMEM_LIMIT = 56 * 1024 * 1024


def _cparams(n_axes):
    return pltpu.CompilerParams(dimension_semantics=("arbitrary",) * n_axes,
                                vmem_limit_bytes=VMEM_LIMIT)


def _dot(a, b):
    return jnp.dot(a, b, preferred_element_type=F32)


def _dot_nt(a, b):
    return lax.dot_general(a, b, (((1,), (1,)), ((), ())), preferred_element_type=F32)


def _layer_norm_rows(v, g, b, eps=1e-5):
    mu = jnp.mean(v, axis=-1, keepdims=True)
    vc = v - mu
    var = jnp.mean(vc * vc, axis=-1, keepdims=True)
    return vc * lax.rsqrt(var + eps) * g + b


def _rms_norm_rows(v, g, eps=1e-6):
    return v * lax.rsqrt(jnp.mean(v * v, axis=-1, keepdims=True) + eps) * g


def _mm_kernel(x_ref, w_ref, o_ref):
    o_ref[...] = _dot(x_ref[...], w_ref[...]).astype(o_ref.dtype)


def _matmul(x, w, tm, tn, out_dtype, name):
    m, k = x.shape
    n = w.shape[1]
    return pl.pallas_call(
        _mm_kernel,
        grid=(n // tn, m // tm),
        in_specs=[pl.BlockSpec((tm, k), lambda j, i: (i, 0)),
                  pl.BlockSpec((k, tn), lambda j, i: (0, j))],
        out_specs=pl.BlockSpec((tm, tn), lambda j, i: (i, j)),
        out_shape=jax.ShapeDtypeStruct((m, n), out_dtype),
        compiler_params=_cparams(2),
        name=name,
    )(x, w)


def _mla_proj_kernel(x_ref, wa_ref, wkpe_ref, gq_ref, gkv_ref, wqnt_ref, wqpt_ref, wqprt_ref,
                     wkn_ref, wvt_ref, cos_ref, sin_ref, cost_ref, sint_ref, qt_ref, k_ref, vt_ref, *, scale):
    xb = x_ref[...]
    a = _dot(xb, wa_ref[...])
    cq = _rms_norm_rows(a[:, :MLA_Q_RANK], gq_ref[...]).astype(BF16)
    ckv = _rms_norm_rows(a[:, MLA_Q_RANK:], gkv_ref[...]).astype(BF16)
    qn_t = _dot_nt(wqnt_ref[...], cq)
    qp_t = _dot_nt(wqpt_ref[...], cq)
    qpr_t = _dot_nt(wqprt_ref[...], cq)
    v_t = _dot_nt(wvt_ref[...], ckv)
    cos_t = cost_ref[...]
    sin_t = sint_ref[...]
    for h in range(MLA_HEADS):
        lo, hi = h * LANES, (h + 1) * LANES
        qt_ref[0, h, 0, 0:LANES, :] = (qn_t[lo:hi] * scale).astype(qt_ref.dtype)
        qt_ref[0, h, 0, LANES:2 * LANES, :] = ((qp_t[lo:hi] * cos_t + qpr_t[lo:hi] * sin_t) * scale).astype(qt_ref.dtype)
        vt_ref[0, h, 0] = v_t[h * MLA_V_DIM:(h + 1) * MLA_V_DIM, :].astype(vt_ref.dtype)
    kn = _dot(ckv, wkn_ref[...])
    kp = _dot(xb, wkpe_ref[...])
    kpe = (kp[:, :LANES] * cos_ref[...] + kp[:, LANES:] * sin_ref[...]).astype(k_ref.dtype)
    for h in range(MLA_HEADS):
        lo, hi = h * LANES, (h + 1) * LANES
        k_ref[:, 2 * lo:2 * lo + LANES] = kn[:, lo:hi].astype(k_ref.dtype)
        k_ref[:, 2 * lo + LANES:2 * hi] = kpe


def _mla_proj(x, wa, wkpe, gq, gkv, wqnt, wqpt, wqprt, wkn, wvt, cos64, sin64, seq):
    t = x.shape[0]
    tm = ATTN_BLOCK
    full = lambda arr: pl.BlockSpec(arr.shape, lambda i: (0,) * arr.ndim)
    n_pos = seq // tm
    scale = (MLA_NOPE_DIM + MLA_ROPE_DIM) ** -0.5
    dk = 2 * LANES
    feat_major = lambda d: pl.BlockSpec((1, MLA_HEADS, 1, d, tm), lambda i: (i // n_pos, 0, i % n_pos, 0, 0))
    return pl.pallas_call(
        functools.partial(_mla_proj_kernel, scale=scale),
        grid=(t // tm,),
        in_specs=[pl.BlockSpec((tm, D_MODEL), lambda i: (i, 0)),
                  full(wa), full(wkpe), full(gq), full(gkv), full(wqnt), full(wqpt), full(wqprt),
                  full(wkn), full(wvt),
                  pl.BlockSpec((tm, LANES), lambda i: (i % n_pos, 0)),
                  pl.BlockSpec((tm, LANES), lambda i: (i % n_pos, 0)),
                  pl.BlockSpec((LANES, tm), lambda i: (0, i % n_pos)),
                  pl.BlockSpec((LANES, tm), lambda i: (0, i % n_pos))],
        out_specs=[feat_major(dk),
                   pl.BlockSpec((tm, MLA_HEADS * dk), lambda i: (i, 0)),
                   feat_major(MLA_V_DIM)],
        out_shape=[jax.ShapeDtypeStruct((t // seq, MLA_HEADS, n_pos, dk, tm), BF16),
                   jax.ShapeDtypeStruct((t, MLA_HEADS * dk), BF16),
                   jax.ShapeDtypeStruct((t // seq, MLA_HEADS, n_pos, MLA_V_DIM, tm), BF16)],
        compiler_params=_cparams(1),
        name="mla_proj",
    )(x, wa, wkpe, gq, gkv, wqnt, wqpt, wqprt, wkn, wvt, cos64, sin64, cos64.T, sin64.T)


def _softmax_steps(q_ts, ks, v_ts, carries, masks):
    heads = range(len(q_ts))
    s_ts = [_dot(ks[h], q_ts[h]) for h in heads]
    s_ts = [s_ts[h] if masks[h] is None else jnp.where(masks[h], s_ts[h], NEG_BIG) for h in heads]
    m_news = [jnp.maximum(carries[h][0], jnp.max(s_ts[h], axis=0, keepdims=True)) for h in heads]
    p_ts = [jnp.exp(s_ts[h] - m_news[h]) for h in heads]
    pvs = [_dot(v_ts[h], p_ts[h].astype(v_ts[h].dtype)) for h in heads]
    out = []
    for h in heads:
        m_i, l_i, acc_t = carries[h]
        alpha = jnp.exp(m_i - m_news[h])
        l_new = alpha * l_i + jnp.sum(p_ts[h], axis=0, keepdims=True)
        out.append((m_news[h], l_new, alpha * acc_t + pvs[h]))
    return tuple(out)


def _softmax_init(n_heads, tq, dv):
    return tuple((jnp.full((1, tq), NEG_BIG, F32), jnp.zeros((1, tq), F32), jnp.zeros((dv, tq), F32))
                 for _ in range(n_heads))


def _softmax_finish(o_ref, carries, dv):
    for h, (_, l_i, acc_t) in enumerate(carries):
        o_ref[:, h * dv:(h + 1) * dv] = (acc_t / l_i).T.astype(o_ref.dtype)


def _causal_attn_kernel(qt_ref, k_ref, vt_ref, o_ref, *, tq, n_heads, dk, dv):
    qi = pl.program_id(2)
    key = lax.broadcasted_iota(I32, (tq, tq), 0)
    qry = lax.broadcasted_iota(I32, (tq, tq), 1)
    qs = [qt_ref[0, h, 0] for h in range(n_heads)]

    def step(kb, carries, mask):
        ks = pl.multiple_of(kb * tq, tq)
        return _softmax_steps(qs, [k_ref[pl.ds(ks, tq), h * dk:(h + 1) * dk] for h in range(n_heads)],
                              [vt_ref[0, h, kb] for h in range(n_heads)], carries, [mask] * n_heads)

    carries = step(qi, _softmax_init(n_heads, tq, dv), key <= qry)
    carries = lax.fori_loop(0, qi, lambda kb, c: step(kb, c, None), carries)
    _softmax_finish(o_ref, carries, dv)


def _mla_attention(q_t, k_a, v_t, bsz, seq, n_heads=8):
    t = k_a.shape[0]
    tq = ATTN_BLOCK
    nq = seq // tq
    dk = 2 * LANES
    dv = MLA_V_DIM
    return pl.pallas_call(
        functools.partial(_causal_attn_kernel, tq=tq, n_heads=n_heads, dk=dk, dv=dv),
        grid=(bsz, MLA_HEADS // n_heads, nq),
        in_specs=[pl.BlockSpec((1, n_heads, 1, dk, tq), lambda b, h, i: (b, h, i, 0, 0)),
                  pl.BlockSpec((seq, n_heads * dk), lambda b, h, i: (b, h)),
                  pl.BlockSpec((1, n_heads, nq, dv, tq), lambda b, h, i: (b, h, 0, 0, 0))],
        out_specs=pl.BlockSpec((tq, n_heads * dv), lambda b, h, i: (b * nq + i, h)),
        out_shape=jax.ShapeDtypeStruct((t, MLA_HEADS * dv), BF16),
        compiler_params=_cparams(3),
        name="mla_attn",
    )(q_t, k_a, v_t)


def _sgu_kernel(x_ref, wz_ref, g_ref, b_ref, ws_ref, bs_ref, o_ref, *, tm):
    z = jax.nn.gelu(_dot(x_ref[...], wz_ref[...]))
    u = z[:, :SGU_WIDTH]
    v = _layer_norm_rows(z[:, SGU_WIDTH:], g_ref[...], b_ref[...]).astype(BF16)
    row = lax.broadcasted_iota(I32, (SGU_CHUNK, SGU_CHUNK), 0)
    col = lax.broadcasted_iota(I32, (SGU_CHUNK, SGU_CHUNK), 1)
    bias = bs_ref[...]
    for g in range(SGU_GROUPS):
        w = jnp.where(col <= row, ws_ref[g], 0.0).astype(BF16)
        lo, hi = g * SGU_GROUP_DIM, (g + 1) * SGU_GROUP_DIM
        for c in range(tm // SGU_CHUNK):
            r0, r1 = c * SGU_CHUNK, (c + 1) * SGU_CHUNK
            mixed = _dot(w, v[r0:r1, lo:hi]) + bias[:, lo:hi]
            o_ref[r0:r1, lo:hi] = (u[r0:r1, lo:hi] * mixed).astype(o_ref.dtype)


def _sgu(x, wz, ln_g, ln_b, w_s, bias_full, tm=256):
    t = x.shape[0]
    full = lambda arr: pl.BlockSpec(arr.shape, lambda i: (0,) * arr.ndim)
    return pl.pallas_call(
        functools.partial(_sgu_kernel, tm=tm),
        grid=(t // tm,),
        in_specs=[pl.BlockSpec((tm, D_MODEL), lambda i: (i, 0)),
                  full(wz), full(ln_g), full(ln_b), full(w_s), full(bias_full)],
        out_specs=pl.BlockSpec((tm, SGU_WIDTH), lambda i: (i, 0)),
        out_shape=jax.ShapeDtypeStruct((t, SGU_WIDTH), BF16),
        compiler_params=_cparams(1),
        name="sgu",
    )(x, wz, ln_g, ln_b, w_s, bias_full)


def _moba_k_proj_kernel(x_ref, w_ref, cos_ref, sin_ref, o_ref):
    acc = _dot(x_ref[...], w_ref[...])
    cos = cos_ref[...]
    sin = sin_ref[...]
    for h in range(MOBA_HEADS):
        seg = acc[:, h * LANES:(h + 1) * LANES]
        rot = pltpu.roll(seg, MOBA_HEAD_DIM // 2, axis=1)
        o_ref[:, h * LANES:(h + 1) * LANES] = (seg * cos + rot * sin).astype(o_ref.dtype)


def _moba_k_proj(x, w_k, cos128, sin128, seq, tm=512):
    t = x.shape[0]
    n_pos = seq // tm
    return pl.pallas_call(
        _moba_k_proj_kernel,
        grid=(t // tm,),
        in_specs=[pl.BlockSpec((tm, D_MODEL), lambda i: (i, 0)),
                  pl.BlockSpec(w_k.shape, lambda i: (0, 0)),
                  pl.BlockSpec((tm, LANES), lambda i: (i % n_pos, 0)),
                  pl.BlockSpec((tm, LANES), lambda i: (i % n_pos, 0))],
        out_specs=pl.BlockSpec((tm, MOBA_WIDTH), lambda i: (i, 0)),
        out_shape=jax.ShapeDtypeStruct((t, MOBA_WIDTH), BF16),
        compiler_params=_cparams(1),
        name="moba_k_proj",
    )(x, w_k, cos128, sin128)


def _moba_qv_proj_kernel(x_ref, wt_ref, cost_ref, sint_ref, o_ref, *, scale):
    y_t = _dot_nt(wt_ref[...], x_ref[...])
    hd = MOBA_HEAD_DIM

    @pl.when(pl.program_id(0) == 0)
    def _():
        cos_t = cost_ref[...]
        sin_t = sint_ref[...]
        for h in range(MOBA_HEADS):
            seg = y_t[h * hd:(h + 1) * hd]
            rot = jnp.concatenate([seg[hd // 2:], seg[:hd // 2]], axis=0)
            o_ref[0, h, 0] = ((seg * cos_t + rot * sin_t) * scale).astype(o_ref.dtype)

    @pl.when(pl.program_id(0) == 1)
    def _():
        for h in range(MOBA_HEADS):
            o_ref[0, h, 0] = y_t[h * hd:(h + 1) * hd].astype(o_ref.dtype)


def _moba_qv_proj(x, w_qv_t, cos128_t, sin128_t, seq):
    t = x.shape[0]
    tm = ATTN_BLOCK
    n_pos = seq // tm
    return pl.pallas_call(
        functools.partial(_moba_qv_proj_kernel, scale=MOBA_HEAD_DIM ** -0.5),
        grid=(2, t // tm),
        in_specs=[pl.BlockSpec((tm, D_MODEL), lambda j, i: (i, 0)),
                  pl.BlockSpec((MOBA_WIDTH, D_MODEL), lambda j, i: (j, 0)),
                  pl.BlockSpec((MOBA_HEAD_DIM, tm), lambda j, i: (0, i % n_pos)),
                  pl.BlockSpec((MOBA_HEAD_DIM, tm), lambda j, i: (0, i % n_pos))],
        out_specs=pl.BlockSpec((1, MOBA_HEADS, 1, MOBA_HEAD_DIM, tm),
                               lambda j, i: (i // n_pos, j, i % n_pos, 0, 0)),
        out_shape=jax.ShapeDtypeStruct((t // seq, 2 * MOBA_HEADS, n_pos, MOBA_HEAD_DIM, tm), BF16),
        compiler_params=_cparams(2),
        name="moba_qv_proj",
    )(x, w_qv_t, cos128_t, sin128_t)


def _moba_attn_kernel(qt_ref, k_ref, vt_ref, avg_ref, o_ref, kmean_ref, *, n_heads):
    j = pl.program_id(2)
    blk = MOBA_BLOCK
    hd = MOBA_HEAD_DIM

    @pl.when(j == 0)
    def _():
        kmean_ref[...] = _dot(avg_ref[...], k_ref[...])

    blk_id = lax.broadcasted_iota(I32, (SUBLANES, blk), 0)
    n_blocks = k_ref.shape[0] // blk
    qs, sels = [], []
    for h in range(n_heads):
        q = qt_ref[0, h, 0]
        gate = jnp.dot(kmean_ref[0:SUBLANES, h * hd:(h + 1) * hd], q.astype(F32),
                       precision=lax.Precision.HIGHEST, preferred_element_type=F32)
        rank = jnp.zeros((SUBLANES, blk), I32)
        for m in range(n_blocks):
            gm = gate[m:m + 1, :]
            beats = (gm > gate) | ((gm == gate) & (m < blk_id))
            rank = rank + jnp.where(beats & (m < j), 1, 0)
        qs.append(q)
        sels.append(jnp.where((rank < MOBA_TOPK) & (blk_id < j), 1.0, 0.0))

    def step(kb, carries, masks):
        ks = pl.multiple_of(kb * blk, blk)
        return _softmax_steps(qs, [k_ref[pl.ds(ks, blk), h * hd:(h + 1) * hd] for h in range(n_heads)],
                              [vt_ref[0, h, kb] for h in range(n_heads)], carries, masks)

    key = lax.broadcasted_iota(I32, (blk, blk), 0)
    qry = lax.broadcasted_iota(I32, (blk, blk), 1)
    carries = step(j, _softmax_init(n_heads, blk, hd), [key <= qry] * n_heads)

    def body(n, c):
        masks = [jnp.max(jnp.where(blk_id == n, sels[h], 0.0), axis=0, keepdims=True) > 0.5
                 for h in range(n_heads)]
        return step(n, c, masks)

    carries = lax.fori_loop(0, j, body, carries)
    _softmax_finish(o_ref, carries, hd)


def _moba_attention(qv_t, k, avg, bsz, seq, n_heads=8):
    t = k.shape[0]
    nq = seq // MOBA_BLOCK
    assert nq <= SUBLANES and MOBA_BLOCK == ATTN_BLOCK
    w = n_heads * MOBA_HEAD_DIM
    groups = MOBA_HEADS // n_heads
    return pl.pallas_call(
        functools.partial(_moba_attn_kernel, n_heads=n_heads),
        grid=(bsz, groups, nq),
        in_specs=[pl.BlockSpec((1, n_heads, 1, MOBA_HEAD_DIM, MOBA_BLOCK), lambda b, h, i: (b, h, i, 0, 0)),
                  pl.BlockSpec((seq, w), lambda b, h, i: (b, h)),
                  pl.BlockSpec((1, n_heads, nq, MOBA_HEAD_DIM, MOBA_BLOCK),
                               lambda b, h, i: (b, groups + h, 0, 0, 0)),
                  pl.BlockSpec(avg.shape, lambda b, h, i: (0, 0))],
        out_specs=pl.BlockSpec((MOBA_BLOCK, w), lambda b, h, i: (b * nq + i, h)),
        out_shape=jax.ShapeDtypeStruct((t, MOBA_WIDTH), BF16),
        scratch_shapes=[pltpu.VMEM((LANES, w), F32)],
        compiler_params=_cparams(3),
        name="moba_attn",
    )(qv_t, k, qv_t, avg)


def _mem_attn_kernel(x_ref, wq_ref, kv_ref, o_ref, *, scale):
    q = (_dot(x_ref[...], wq_ref[...]) * scale).astype(BF16)
    for h in range(MEM_HEADS):
        lo, hi = h * MEM_HEAD_DIM, (h + 1) * MEM_HEAD_DIM
        s = _dot_nt(q[:, lo:hi], kv_ref[:, lo:hi])
        p = jnp.exp(s - jnp.max(s, axis=1, keepdims=True))
        o = _dot(p.astype(BF16), kv_ref[:, MEM_WIDTH + lo:MEM_WIDTH + hi])
        o_ref[:, lo:hi] = (o / jnp.sum(p, axis=1, keepdims=True)).astype(o_ref.dtype)


def _mem_attention(x, w_mq, kv_mem, bsz, seq, mem_len, tm=512):
    t = x.shape[0]
    ns = seq // tm
    return pl.pallas_call(
        functools.partial(_mem_attn_kernel, scale=MEM_HEAD_DIM ** -0.5),
        grid=(bsz, ns),
        in_specs=[pl.BlockSpec((tm, D_MODEL), lambda b, i: (b * ns + i, 0)),
                  pl.BlockSpec(w_mq.shape, lambda b, i: (0, 0)),
                  pl.BlockSpec((mem_len, 2 * MEM_WIDTH), lambda b, i: (b, 0))],
        out_specs=pl.BlockSpec((tm, MEM_WIDTH), lambda b, i: (b * ns + i, 0)),
        out_shape=jax.ShapeDtypeStruct((t, MEM_WIDTH), BF16),
        compiler_params=_cparams(2),
        name="mem_attn",
    )(x, w_mq, kv_mem)


def _gated_sum_kernel(x_ref, oa_ref, ob_ref, oc_ref, om_ref, g0_ref, g1_ref, g2_ref, g3_ref,
                      wa_ref, wb_ref, wc_ref, wm_ref, y_ref):
    xb = x_ref[...]
    acc = None
    for o_ref, g_ref, w_ref in ((oa_ref, g0_ref, wa_ref), (ob_ref, g1_ref, wb_ref),
                                (oc_ref, g2_ref, wc_ref), (om_ref, g3_ref, wm_ref)):
        term = jax.nn.sigmoid(_dot(xb, g_ref[...])) * _dot(o_ref[...], w_ref[...])
        acc = term if acc is None else acc + term
    y_ref[...] = acc.astype(y_ref.dtype)


def _gated_sum(x, o_a, o_b, o_c, o_m, w_gate, w_br_a, w_br_b, w_br_c, w_br_m, tm=1024, tn=256):
    t = x.shape[0]
    nj = D_MODEL // tn
    rows = lambda width: pl.BlockSpec((tm, width), lambda j, i: (i, 0))
    gate = lambda b: pl.BlockSpec((D_MODEL, tn), lambda j, i: (0, b * nj + j))
    cols = lambda width: pl.BlockSpec((width, tn), lambda j, i: (0, j))
    return pl.pallas_call(
        _gated_sum_kernel,
        grid=(nj, t // tm),
        in_specs=[rows(D_MODEL), rows(o_a.shape[1]), rows(o_b.shape[1]), rows(o_c.shape[1]), rows(o_m.shape[1]),
                  gate(0), gate(1), gate(2), gate(3),
                  cols(w_br_a.shape[0]), cols(w_br_b.shape[0]), cols(w_br_c.shape[0]), cols(w_br_m.shape[0])],
        out_specs=pl.BlockSpec((tm, tn), lambda j, i: (i, j)),
        out_shape=jax.ShapeDtypeStruct((t, D_MODEL), BF16),
        compiler_params=_cparams(2),
        name="gated_sum",
    )(x, o_a, o_b, o_c, o_m, w_gate, w_gate, w_gate, w_gate, w_br_a, w_br_b, w_br_c, w_br_m)


def _pack_bf16_pairs(v):
    bits = lax.bitcast_convert_type(v.astype(BF16).astype(F32), U32)
    half = v.shape[1] // 2
    return (bits[:, :half] >> 16) | bits[:, half:]


def _unpack_bf16_pairs(words):
    lo = lax.bitcast_convert_type(words << 16, F32)
    hi = lax.bitcast_convert_type(words & jnp.uint32(0xFFFF0000), F32)
    return lo, hi


def _store_row_slabs(ref, words):
    for c in range(ROW_SLAB):
        ref[pl.ds(c, words.shape[0], stride=ROW_SLAB), :] = words[:, c * LANES:(c + 1) * LANES]


def _load_row_slabs(ref):
    n_rows = ref.shape[0] // ROW_SLAB
    return jnp.concatenate([ref[pl.ds(c, n_rows, stride=ROW_SLAB), :] for c in range(ROW_SLAB)], axis=1)


def _proj_norm_kernel(y_ref, w_ref, res_ref, g_ref, b_ref, o32_ref, o16_ref, opk_ref):
    v = DEEPNORM_ALPHA * res_ref[...] + _dot(y_ref[...], w_ref[...])
    out = _layer_norm_rows(v, g_ref[...], b_ref[...])
    o32_ref[...] = out
    o16_ref[...] = out.astype(o16_ref.dtype)
    _store_row_slabs(opk_ref, _pack_bf16_pairs(out))


def _proj_norm(y, w_o, res, g, b, tm=256):
    t = y.shape[0]
    full = lambda arr: pl.BlockSpec(arr.shape, lambda i: (0,) * arr.ndim)
    rows = pl.BlockSpec((tm, D_MODEL), lambda i: (i, 0))
    return pl.pallas_call(
        _proj_norm_kernel,
        grid=(t // tm,),
        in_specs=[rows, full(w_o), rows, full(g), full(b)],
        out_specs=[rows, rows, pl.BlockSpec((tm * ROW_SLAB, LANES), lambda i: (i, 0))],
        out_shape=[jax.ShapeDtypeStruct((t, D_MODEL), F32), jax.ShapeDtypeStruct((t, D_MODEL), BF16),
                   jax.ShapeDtypeStruct((t * ROW_SLAB, LANES), U32)],
        compiler_params=_cparams(1),
        name="proj_norm",
    )(y, w_o, res, g, b)


def _router_kernel(x_ref, w_ref, bias_ref, e_ref, wt_ref, rank_ref, cnt_ref, run_ref, *, tm):
    i = pl.program_id(0)

    @pl.when(i == 0)
    def _():
        run_ref[...] = jnp.zeros_like(run_ref)

    logits = jnp.dot(x_ref[...], w_ref[...], precision=lax.Precision.HIGHEST, preferred_element_type=F32)
    scores = jax.nn.sigmoid(logits)
    lane = lax.broadcasted_iota(I32, (tm, LANES), 1)
    biased = jnp.where(lane < N_EXPERTS, scores + bias_ref[...], NEG_BIG)
    picks = []
    chosen = jnp.zeros((tm, LANES), jnp.bool_)
    for _ in range(TOP_K):
        mx = jnp.max(biased, axis=1, keepdims=True)
        idx = jnp.min(jnp.where(biased == mx, lane, LANES), axis=1, keepdims=True)
        hit = lane == idx
        picks.append((idx, hit))
        chosen = chosen | hit
        biased = jnp.where(hit, 2 * NEG_BIG, biased)
    picked_scores = jnp.where(chosen, scores, 0.0)
    norm = ROUTED_SCALE / jnp.sum(picked_scores, axis=1, keepdims=True)

    r = lax.broadcasted_iota(I32, (tm, tm), 0)
    c = lax.broadcasted_iota(I32, (tm, tm), 1)
    strict_lower = jnp.where(c < r, 1.0, 0.0).astype(BF16)
    chosen_f = jnp.where(chosen, 1.0, 0.0)
    arrival = run_ref[0:1, :] + _dot(strict_lower, chosen_f.astype(BF16))
    run_ref[...] = run_ref[...] + jnp.sum(chosen_f, axis=0, keepdims=True)
    cnt_ref[...] = run_ref[...]

    e_out = jnp.zeros((tm, LANES), I32)
    w_out = jnp.zeros((tm, LANES), F32)
    r_out = jnp.zeros((tm, LANES), F32)
    for slot, (idx, hit) in enumerate(picks):
        here = lane == slot
        e_out = jnp.where(here, idx, e_out)
        w_out = jnp.where(here, jnp.sum(jnp.where(hit, scores, 0.0), axis=1, keepdims=True) * norm, w_out)
        r_out = jnp.where(here, jnp.sum(jnp.where(hit, arrival, 0.0), axis=1, keepdims=True), r_out)
    e_ref[...] = e_out
    wt_ref[...] = w_out
    rank_ref[...] = r_out.astype(I32)


def _router(x32, w_router_pad, bias_pad, tm=256):
    t = x32.shape[0]
    full = lambda arr: pl.BlockSpec(arr.shape, lambda i: (0,) * arr.ndim)
    rows = pl.BlockSpec((tm, LANES), lambda i: (i, 0))
    return pl.pallas_call(
        functools.partial(_router_kernel, tm=tm),
        grid=(t // tm,),
        in_specs=[pl.BlockSpec((tm, D_MODEL), lambda i: (i, 0)), full(w_router_pad), full(bias_pad)],
        out_specs=[rows, rows, rows, pl.BlockSpec((8, LANES), lambda i: (0, 0))],
        out_shape=[jax.ShapeDtypeStruct((t, LANES), I32), jax.ShapeDtypeStruct((t, LANES), F32),
                   jax.ShapeDtypeStruct((t, LANES), I32), jax.ShapeDtypeStruct((8, LANES), F32)],
        scratch_shapes=[pltpu.VMEM((8, LANES), F32)],
        compiler_params=_cparams(1),
        name="moe_router",
    )(x32, w_router_pad, bias_pad)


def _row_tiles(ref, first_row, n_rows):
    return ref.at[pl.ds(pl.multiple_of(first_row * ROW_SLAB, ROW_SLAB), n_rows * ROW_SLAB), :]


def _dispatch_kernel(fill_start_ref, fill_n_ref, n_used_ref, dest_ref, x_ref, out_hbm, zero_ref, sem, fill_sem,
                     *, tm, bm):
    @pl.when(pl.program_id(0) == 0)
    def _():
        zero_ref[...] = jnp.zeros_like(zero_ref)
        n_blk = out_hbm.shape[0] // (bm * ROW_SLAB)

        def fill_expert(e, wait):
            n = fill_n_ref[e]
            row = fill_start_ref[e]
            for bit in range(bm.bit_length() - 1):
                size = 1 << bit

                @pl.when((n & size) != 0)
                def _():
                    copy = pltpu.make_async_copy(_row_tiles(zero_ref, 0, size), _row_tiles(out_hbm, row, size), fill_sem)
                    copy.wait() if wait else copy.start()

                row = row + (n & size)

        def tail_copy(blk):
            return pltpu.make_async_copy(zero_ref, _row_tiles(out_hbm, blk * bm, bm), fill_sem)

        lax.fori_loop(0, N_EXPERTS, lambda e, c: (fill_expert(e, False), c)[1], 0)
        lax.fori_loop(n_used_ref[0], n_blk, lambda b, c: (tail_copy(b).start(), c)[1], 0)
        lax.fori_loop(0, N_EXPERTS, lambda e, c: (fill_expert(e, True), c)[1], 0)
        lax.fori_loop(n_used_ref[0], n_blk, lambda b, c: (tail_copy(0).wait(), c)[1], 0)

    def issue(tok, _):
        for k in range(TOP_K):
            pltpu.make_async_copy(_row_tiles(x_ref, tok, 1),
                                  _row_tiles(out_hbm, dest_ref[tok * SLOT_STRIDE + k], 1), sem).start()
        return 0

    lax.fori_loop(0, tm, issue, 0)
    for _ in range(TOP_K):
        pltpu.make_async_copy(x_ref, _row_tiles(out_hbm, 0, tm), sem).wait()


def _dispatch(x_rows, dest_flat, fill_start, fill_n, n_used, n_rows, tm=256):
    t = x_rows.shape[0] // ROW_SLAB
    bm = MOE_ROW_BLOCK
    grid_spec = pltpu.PrefetchScalarGridSpec(
        num_scalar_prefetch=3,
        grid=(t // tm,),
        in_specs=[pl.BlockSpec((tm * SLOT_STRIDE,), lambda i, fs, fn, nu: (i,), memory_space=pltpu.SMEM),
                  pl.BlockSpec((tm * ROW_SLAB, LANES), lambda i, fs, fn, nu: (i, 0))],
        out_specs=pl.BlockSpec(memory_space=pl.ANY),
        scratch_shapes=[pltpu.VMEM((bm * ROW_SLAB, LANES), x_rows.dtype), pltpu.SemaphoreType.DMA(()),
                        pltpu.SemaphoreType.DMA(())],
    )
    return pl.pallas_call(
        functools.partial(_dispatch_kernel, tm=tm, bm=bm),
        grid_spec=grid_spec,
        out_shape=jax.ShapeDtypeStruct((n_rows * ROW_SLAB, LANES), x_rows.dtype),
        compiler_params=_cparams(1),
        name="moe_dispatch",
    )(fill_start, fill_n, n_used, dest_flat, x_rows)


def _expert_kernel(blk_e_ref, n_used_ref, xs_ref, wg_ref, wu_ref, wd_ref, ys_ref, wg16_ref, wu16_ref, wd16_ref):
    i = pl.program_id(0)
    used = i < n_used_ref[0]
    new_expert = (i == 0) | (blk_e_ref[i] != blk_e_ref[jnp.maximum(i - 1, 0)])

    @pl.when(used & new_expert)
    def _():
        wg16_ref[...] = wg_ref[0, 0].astype(BF16)
        wu16_ref[...] = wu_ref[0, 0].astype(BF16)
        wd16_ref[...] = wd_ref[0, 0].astype(BF16)

    @pl.when(used)
    def _():
        lo, hi = _unpack_bf16_pairs(_load_row_slabs(xs_ref))
        xb = jnp.concatenate([lo.astype(BF16), hi.astype(BF16)], axis=1)
        hid = jax.nn.silu(_dot(xb, wg16_ref[...])) * _dot(xb, wu16_ref[...])
        _store_row_slabs(ys_ref, _pack_bf16_pairs(_dot(hid.astype(BF16), wd16_ref[...])))

    @pl.when(jnp.logical_not(used))
    def _():
        ys_ref[...] = jnp.zeros_like(ys_ref)


def _expert_ffn(xs, blk_e, n_used, w_gate, w_up, w_down, layer):
    n_rows = xs.shape[0] // ROW_SLAB
    bm = MOE_ROW_BLOCK
    grid_spec = pltpu.PrefetchScalarGridSpec(
        num_scalar_prefetch=2,
        grid=(n_rows // bm,),
        in_specs=[pl.BlockSpec((bm * ROW_SLAB, LANES), lambda i, be, nu: (jnp.minimum(i, nu[0] - 1), 0)),
                  pl.BlockSpec((1, 1, D_MODEL, D_EXPERT), lambda i, be, nu: (layer, be[i], 0, 0)),
                  pl.BlockSpec((1, 1, D_MODEL, D_EXPERT), lambda i, be, nu: (layer, be[i], 0, 0)),
                  pl.BlockSpec((1, 1, D_EXPERT, D_MODEL), lambda i, be, nu: (layer, be[i], 0, 0))],
        out_specs=pl.BlockSpec((bm * ROW_SLAB, LANES), lambda i, be, nu: (i, 0)),
        scratch_shapes=[pltpu.VMEM((D_MODEL, D_EXPERT), BF16), pltpu.VMEM((D_MODEL, D_EXPERT), BF16),
                        pltpu.VMEM((D_EXPERT, D_MODEL), BF16)],
    )
    return pl.pallas_call(
        _expert_kernel,
        grid_spec=grid_spec,
        out_shape=jax.ShapeDtypeStruct((n_rows * ROW_SLAB, LANES), xs.dtype),
        compiler_params=_cparams(1),
        name="moe_experts",
    )(blk_e, n_used, xs, w_gate, w_up, w_down)


def _combine_kernel(dest_ref, next_dest_ref, x_ref, wt_ref, ys_hbm, wsg_ref, wsu_ref, wsd_ref, g_ref, b_ref,
                    o32_ref, o16_ref, buf_ref, sem, *, tm):
    i = pl.program_id(0)

    def tile(ref, r):
        return ref.at[pl.ds(pl.multiple_of(r * ROW_SLAB, ROW_SLAB), ROW_SLAB), :]

    def gather(table_ref, half):
        def issue(tok, _):
            for k in range(TOP_K):
                pltpu.make_async_copy(tile(ys_hbm, table_ref[tok * SLOT_STRIDE + k]),
                                      tile(buf_ref.at[half, k], tok), sem.at[half]).start()
            return 0

        lax.fori_loop(0, tm, issue, 0)

    @pl.when(i == 0)
    def _():
        gather(dest_ref, 0)

    for half in range(2):
        @pl.when((i + 1 < pl.num_programs(0)) & ((i + 1) % 2 == half))
        def _():
            gather(next_dest_ref, half)

    x = x_ref[...]
    xb = x.astype(BF16)
    hid = jax.nn.silu(_dot(xb, wsg_ref[...])) * _dot(xb, wsu_ref[...])
    total = DEEPNORM_ALPHA * x + _dot(hid.astype(BF16), wsd_ref[...])
    for k in range(TOP_K):
        pltpu.make_async_copy(ys_hbm.at[pl.ds(0, tm * ROW_SLAB), :], buf_ref.at[0, k], sem.at[i % 2]).wait()
    wt = wt_ref[...]
    routed_lo = routed_hi = None
    for k in range(TOP_K):
        lo, hi = _unpack_bf16_pairs(_load_row_slabs(buf_ref.at[i % 2, k]))
        w_k = wt[:, k:k + 1]
        routed_lo = w_k * lo if routed_lo is None else routed_lo + w_k * lo
        routed_hi = w_k * hi if routed_hi is None else routed_hi + w_k * hi
    total = total + jnp.concatenate([routed_lo, routed_hi], axis=1)
    out = _layer_norm_rows(total, g_ref[...], b_ref[...])
    o32_ref[...] = out
    o16_ref[...] = out.astype(o16_ref.dtype)


def _combine(x32, dest_flat, wts, ys, w_sg, w_su, w_sd, g, b, tm=256):
    t = x32.shape[0]
    n_steps = t // tm
    full = lambda arr: pl.BlockSpec(arr.shape, lambda i: (0,) * arr.ndim)
    rows = pl.BlockSpec((tm, D_MODEL), lambda i: (i, 0))
    return pl.pallas_call(
        functools.partial(_combine_kernel, tm=tm),
        grid=(n_steps,),
        in_specs=[pl.BlockSpec((tm * SLOT_STRIDE,), lambda i: (i,), memory_space=pltpu.SMEM),
                  pl.BlockSpec((tm * SLOT_STRIDE,), lambda i: (jnp.minimum(i + 1, n_steps - 1),),
                               memory_space=pltpu.SMEM),
                  rows,
                  pl.BlockSpec((tm, LANES), lambda i: (i, 0)),
                  pl.BlockSpec(memory_space=pl.ANY),
                  full(w_sg), full(w_su), full(w_sd), full(g), full(b)],
        out_specs=[rows, rows],
        out_shape=[jax.ShapeDtypeStruct((t, D_MODEL), F32), jax.ShapeDtypeStruct((t, D_MODEL), BF16)],
        scratch_shapes=[pltpu.VMEM((2, TOP_K, tm * ROW_SLAB, LANES), ys.dtype), pltpu.SemaphoreType.DMA((2,))],
        compiler_params=_cparams(1),
        name="moe_combine",
    )(dest_flat, dest_flat, x32, wts, ys, w_sg, w_su, w_sd, g, b)


def _rot_half_cols(w):
    half = w.shape[-1] // 2
    return jnp.concatenate([-w[..., half:], w[..., :half]], axis=-1)


def _pad_cols(w, width):
    return jnp.pad(w, [(0, 0)] * (w.ndim - 1) + [(0, width - w.shape[-1])])


def _rope_tables(seq):
    pos = jnp.arange(seq, dtype=F32)[:, None]

    def table(dim):
        half = dim // 2
        inv_freq = ROPE_THETA ** (-jnp.arange(half, dtype=F32) / half)
        ang = pos * inv_freq[None, :]
        return jnp.cos(ang), jnp.sin(ang)

    c64, s64 = table(MLA_ROPE_DIM)
    cos64 = _pad_cols(jnp.concatenate([c64, c64], axis=1), LANES)
    sin64 = _pad_cols(jnp.concatenate([s64, s64], axis=1), LANES)
    c128, s128 = table(MOBA_HEAD_DIM)
    cos128 = jnp.concatenate([c128, c128], axis=1)
    sin128 = jnp.concatenate([-s128, s128], axis=1)
    return cos64, sin64, cos128, sin128


def _layer(x32, x16, mem16, tables, avg, bsz, seq, mem_len, p, expert_weights, layer):
    (w_in, g_qa, w_q_up, g_kva, w_kv_up, sgu_ln_g, sgu_ln_b, w_spatial, b_spatial, w_mem_kv,
     w_br_a, w_br_b, w_br_c, w_br_m, w_o, ln1_g, ln1_b, w_router, router_bias,
     w_sh_gate, w_sh_up, w_sh_down, ln2_g, ln2_b) = p
    cos64, sin64, cos128, sin128 = tables
    t = x32.shape[0]
    row = lambda v: v.reshape(1, -1).astype(F32)

    c0 = MLA_Q_RANK + MLA_KV_RANK
    c1 = c0 + MLA_ROPE_DIM
    c2 = c1 + 2 * SGU_WIDTH
    c3 = c2 + 3 * MOBA_WIDTH
    c4 = c3 + MEM_WIDTH
    w_a = w_in[:, :c0].astype(BF16)
    w_kpe_raw = w_in[:, c0:c1]
    w_kpe = jnp.concatenate([_pad_cols(w_kpe_raw, LANES), _pad_cols(_rot_half_cols(w_kpe_raw), LANES)],
                            axis=1).astype(BF16)
    w_z = w_in[:, c1:c2].astype(BF16)
    w_c = w_in[:, c2:c3].astype(BF16)
    w_mq = w_in[:, c3:c4].astype(BF16)
    w_g = w_in[:, c4:].astype(BF16)

    w_qn = w_q_up[:, :, :MLA_NOPE_DIM].reshape(MLA_Q_RANK, -1).astype(BF16)
    w_qpe = w_q_up[:, :, MLA_NOPE_DIM:]
    w_qp = _pad_cols(w_qpe, LANES).reshape(MLA_Q_RANK, -1).astype(BF16)
    w_qpr = _pad_cols(_rot_half_cols(w_qpe), LANES).reshape(MLA_Q_RANK, -1).astype(BF16)
    w_kn = w_kv_up[:, :, :MLA_NOPE_DIM].reshape(MLA_KV_RANK, -1).astype(BF16)
    w_v = w_kv_up[:, :, MLA_NOPE_DIM:].reshape(MLA_KV_RANK, -1).T.astype(BF16)
    qt_a, k_a, vt_a = _mla_proj(x16, w_a, w_kpe, row(g_qa), row(g_kva), w_qn.T, w_qp.T, w_qpr.T, w_kn, w_v,
                                cos64, sin64, seq)
    o_a = _mla_attention(qt_a, k_a, vt_a, bsz, seq)

    bias_full = jnp.repeat(b_spatial.T.astype(F32), SGU_GROUP_DIM, axis=1)
    o_b = _sgu(x16, w_z, row(sgu_ln_g), row(sgu_ln_b), w_spatial.astype(F32), bias_full)

    k_c = _moba_k_proj(x16, w_c[:, MOBA_WIDTH:2 * MOBA_WIDTH], cos128, sin128, seq)
    w_qv_t = jnp.concatenate([w_c[:, :MOBA_WIDTH], w_c[:, 2 * MOBA_WIDTH:]], axis=1).T
    qvt_c = _moba_qv_proj(x16, w_qv_t, cos128.T, sin128.T, seq)
    o_c = _moba_attention(qvt_c, k_c, avg, bsz, seq)

    kv_mem = _matmul(mem16, w_mem_kv.astype(BF16), mem_len, 2 * MEM_WIDTH, BF16, "mem_kv")
    o_m = _mem_attention(x16, w_mq, kv_mem, bsz, seq, mem_len)

    y = _gated_sum(x16, o_a, o_b, o_c, o_m, w_g, w_br_a.astype(BF16), w_br_b.astype(BF16),
                   w_br_c.astype(BF16), w_br_m.astype(BF16))
    x32, x16, x_packed = _proj_norm(y, w_o.astype(BF16), x32, row(ln1_g), row(ln1_b))

    e_idx, wts, rank, counts = _router(x32, _pad_cols(w_router.astype(F32), LANES),
                                       _pad_cols(row(router_bias), LANES))
    bm = MOE_ROW_BLOCK
    n_blk = (t * TOP_K) // bm + N_EXPERTS
    counts = counts[0, :N_EXPERTS].astype(I32)
    padded = (counts + bm - 1) // bm * bm
    pad_ends = jnp.cumsum(padded)
    pad_starts = pad_ends - padded
    dest = _pad_cols((pad_starts[e_idx[:, :TOP_K]] + rank[:, :TOP_K]).astype(I32), SLOT_STRIDE).reshape(-1)
    blk_first = jnp.arange(n_blk, dtype=I32)[:, None] * bm
    blk_e = jnp.minimum(jnp.sum((pad_ends[None, :] <= blk_first).astype(I32), axis=1), N_EXPERTS - 1)
    n_used = (pad_ends[-1:] // bm).astype(I32)
    xs = _dispatch(x_packed, dest, pad_starts + counts, padded - counts, n_used, n_blk * bm)
    ys = _expert_ffn(xs, blk_e, n_used, *expert_weights, layer)
    return _combine(x32, dest, wts, ys, w_sh_gate.astype(BF16), w_sh_up.astype(BF16),
                    w_sh_down.astype(BF16), row(ln2_g), row(ln2_b))


def kernel(x, mem, w_in, g_qa, w_q_up, g_kva, w_kv_up, sgu_ln_g, sgu_ln_b, w_spatial, b_spatial, w_mem_kv, w_br_a, w_br_b, w_br_c, w_br_m, w_o, ln1_g, ln1_b, w_router, router_bias, w_exp_gate, w_exp_up, w_exp_down, w_sh_gate, w_sh_up, w_sh_down, ln2_g, ln2_b):
    bsz, seq, d = x.shape
    mem_len = mem.shape[1]
    params = (w_in, g_qa, w_q_up, g_kva, w_kv_up, sgu_ln_g, sgu_ln_b, w_spatial, b_spatial, w_mem_kv,
              w_br_a, w_br_b, w_br_c, w_br_m, w_o, ln1_g, ln1_b, w_router, router_bias,
              w_sh_gate, w_sh_up, w_sh_down, ln2_g, ln2_b)
    expert_weights = (w_exp_gate, w_exp_up, w_exp_down)
    tables = _rope_tables(seq)
    n_blocks = seq // MOBA_BLOCK
    blk_of_pos = jnp.arange(seq, dtype=I32)[None, :] // MOBA_BLOCK
    avg = jnp.where(blk_of_pos == jnp.arange(LANES, dtype=I32)[:, None], 1.0 / MOBA_BLOCK, 0.0).astype(BF16)
    assert n_blocks <= LANES
    x32 = x.reshape(bsz * seq, d)
    x16 = x32.astype(BF16)
    mem16 = mem.reshape(bsz * mem_len, d).astype(BF16)
    for l in range(DEPTH):
        x32, x16 = _layer(x32, x16, mem16, tables, avg, bsz, seq, mem_len, tuple(w[l] for w in params),
                          expert_weights, l)
    return x32.reshape(bsz, seq, d)
```

```python
import functools

import jax
import jax.numpy as jnp
from jax import lax
from jax.experimental import pallas as pl
from jax.experimental.pallas import tpu as pltpu

F32 = jnp.float32
BF16 = jnp.bfloat16
I32 = jnp.int32
U32 = jnp.uint32

D_MODEL = 2048
DEPTH = 2
MLA_HEADS = 8
MLA_Q_RANK = 512
MLA_KV_RANK = 256
MLA_NOPE_DIM = 128
MLA_ROPE_DIM = 64
MLA_V_DIM = 128
SGU_GROUPS = 8
SGU_GROUP_DIM = 128
SGU_WIDTH = SGU_GROUPS * SGU_GROUP_DIM
SGU_CHUNK = 128
MOBA_HEADS = 8
MOBA_HEAD_DIM = 128
MOBA_WIDTH = MOBA_HEADS * MOBA_HEAD_DIM
MOBA_BLOCK = 256
MOBA_TOPK = 3
MEM_HEADS = 4
MEM_HEAD_DIM = 128
MEM_WIDTH = MEM_HEADS * MEM_HEAD_DIM
N_BRANCH = 4
N_EXPERTS = 64
TOP_K = 6
D_EXPERT = 512
ROUTED_SCALE = 2.5
ROPE_THETA = 10000.0
DEEPNORM_ALPHA = (2 * DEPTH) ** 0.25

LANES = 128
SUBLANES = 8
ATTN_BLOCK = 256
MOE_ROW_BLOCK = 512
SLOT_STRIDE = 8
ROW_WORDS = D_MODEL // 2
ROW_SLAB = ROW_WORDS // LANES
NEG_BIG = -1e30
VMEM_LIMIT = 56 * 1024 * 1024


def _cparams(n_axes):
    return pltpu.CompilerParams(dimension_semantics=("arbitrary",) * n_axes,
                                vmem_limit_bytes=VMEM_LIMIT)


def _dot(a, b):
    return jnp.dot(a, b, preferred_element_type=F32)


def _dot_nt(a, b):
    return lax.dot_general(a, b, (((1,), (1,)), ((), ())), preferred_element_type=F32)


def _layer_norm_rows(v, g, b, eps=1e-5):
    mu = jnp.mean(v, axis=-1, keepdims=True)
    vc = v - mu
    var = jnp.mean(vc * vc, axis=-1, keepdims=True)
    return vc * lax.rsqrt(var + eps) * g + b


def _rms_norm_rows(v, g, eps=1e-6):
    return v * lax.rsqrt(jnp.mean(v * v, axis=-1, keepdims=True) + eps) * g


def _mm_kernel(x_ref, w_ref, o_ref):
    o_ref[...] = _dot(x_ref[...], w_ref[...]).astype(o_ref.dtype)


def _matmul(x, w, tm, tn, out_dtype, name):
    m, k = x.shape
    n = w.shape[1]
    return pl.pallas_call(
        _mm_kernel,
        grid=(n // tn, m // tm),
        in_specs=[pl.BlockSpec((tm, k), lambda j, i: (i, 0)),
                  pl.BlockSpec((k, tn), lambda j, i: (0, j))],
        out_specs=pl.BlockSpec((tm, tn), lambda j, i: (i, j)),
        out_shape=jax.ShapeDtypeStruct((m, n), out_dtype),
        compiler_params=_cparams(2),
        name=name,
    )(x, w)


def _mla_proj_kernel(x_ref, wa_ref, wkpe_ref, gq_ref, gkv_ref, wqnt_ref, wqpt_ref, wqprt_ref,
                     wkn_ref, wvt_ref, cos_ref, sin_ref, cost_ref, sint_ref, qt_ref, k_ref, vt_ref, *, scale):
    xb = x_ref[...]
    a = _dot(xb, wa_ref[...])
    cq = _rms_norm_rows(a[:, :MLA_Q_RANK], gq_ref[...]).astype(BF16)
    ckv = _rms_norm_rows(a[:, MLA_Q_RANK:], gkv_ref[...]).astype(BF16)
    qn_t = _dot_nt(wqnt_ref[...], cq)
    qp_t = _dot_nt(wqpt_ref[...], cq)
    qpr_t = _dot_nt(wqprt_ref[...], cq)
    v_t = _dot_nt(wvt_ref[...], ckv)
    cos_t = cost_ref[...]
    sin_t = sint_ref[...]
    for h in range(MLA_HEADS):
        lo, hi = h * LANES, (h + 1) * LANES
        qt_ref[0, h, 0, 0:LANES, :] = (qn_t[lo:hi] * scale).astype(qt_ref.dtype)
        qt_ref[0, h, 0, LANES:2 * LANES, :] = ((qp_t[lo:hi] * cos_t + qpr_t[lo:hi] * sin_t) * scale).astype(qt_ref.dtype)
        vt_ref[0, h, 0] = v_t[h * MLA_V_DIM:(h + 1) * MLA_V_DIM, :].astype(vt_ref.dtype)
    kn = _dot(ckv, wkn_ref[...])
    kp = _dot(xb, wkpe_ref[...])
    kpe = (kp[:, :LANES] * cos_ref[...] + kp[:, LANES:] * sin_ref[...]).astype(k_ref.dtype)
    for h in range(MLA_HEADS):
        lo, hi = h * LANES, (h + 1) * LANES
        k_ref[:, 2 * lo:2 * lo + LANES] = kn[:, lo:hi].astype(k_ref.dtype)
        k_ref[:, 2 * lo + LANES:2 * hi] = kpe


def _mla_proj(x, wa, wkpe, gq, gkv, wqnt, wqpt, wqprt, wkn, wvt, cos64, sin64, seq):
    t = x.shape[0]
    tm = ATTN_BLOCK
    full = lambda arr: pl.BlockSpec(arr.shape, lambda i: (0,) * arr.ndim)
    n_pos = seq // tm
    scale = (MLA_NOPE_DIM + MLA_ROPE_DIM) ** -0.5
    dk = 2 * LANES
    feat_major = lambda d: pl.BlockSpec((1, MLA_HEADS, 1, d, tm), lambda i: (i // n_pos, 0, i % n_pos, 0, 0))
    return pl.pallas_call(
        functools.partial(_mla_proj_kernel, scale=scale),
        grid=(t // tm,),
        in_specs=[pl.BlockSpec((tm, D_MODEL), lambda i: (i, 0)),
                  full(wa), full(wkpe), full(gq), full(gkv), full(wqnt), full(wqpt), full(wqprt),
                  full(wkn), full(wvt),
                  pl.BlockSpec((tm, LANES), lambda i: (i % n_pos, 0)),
                  pl.BlockSpec((tm, LANES), lambda i: (i % n_pos, 0)),
                  pl.BlockSpec((LANES, tm), lambda i: (0, i % n_pos)),
                  pl.BlockSpec((LANES, tm), lambda i: (0, i % n_pos))],
        out_specs=[feat_major(dk),
                   pl.BlockSpec((tm, MLA_HEADS * dk), lambda i: (i, 0)),
                   feat_major(MLA_V_DIM)],
        out_shape=[jax.ShapeDtypeStruct((t // seq, MLA_HEADS, n_pos, dk, tm), BF16),
                   jax.ShapeDtypeStruct((t, MLA_HEADS * dk), BF16),
                   jax.ShapeDtypeStruct((t // seq, MLA_HEADS, n_pos, MLA_V_DIM, tm), BF16)],
        compiler_params=_cparams(1),
        name="mla_proj",
    )(x, wa, wkpe, gq, gkv, wqnt, wqpt, wqprt, wkn, wvt, cos64, sin64, cos64.T, sin64.T)


def _softmax_steps(q_ts, ks, v_ts, carries, masks):
    heads = range(len(q_ts))
    s_ts = [_dot(ks[h], q_ts[h]) for h in heads]
    s_ts = [s_ts[h] if masks[h] is None else jnp.where(masks[h], s_ts[h], NEG_BIG) for h in heads]
    m_news = [jnp.maximum(carries[h][0], jnp.max(s_ts[h], axis=0, keepdims=True)) for h in heads]
    p_ts = [jnp.exp(s_ts[h] - m_news[h]) for h in heads]
    pvs = [_dot(v_ts[h], p_ts[h].astype(v_ts[h].dtype)) for h in heads]
    out = []
    for h in heads:
        m_i, l_i, acc_t = carries[h]
        alpha = jnp.exp(m_i - m_news[h])
        l_new = alpha * l_i + jnp.sum(p_ts[h], axis=0, keepdims=True)
        out.append((m_news[h], l_new, alpha * acc_t + pvs[h]))
    return tuple(out)


def _softmax_init(n_heads, tq, dv):
    return tuple((jnp.full((1, tq), NEG_BIG, F32), jnp.zeros((1, tq), F32), jnp.zeros((dv, tq), F32))
                 for _ in range(n_heads))


def _softmax_finish(o_ref, carries, dv):
    for h, (_, l_i, acc_t) in enumerate(carries):
        o_ref[:, h * dv:(h + 1) * dv] = (acc_t / l_i).T.astype(o_ref.dtype)


def _causal_attn_kernel(qt_ref, k_ref, vt_ref, o_ref, *, tq, n_heads, dk, dv):
    qi = pl.program_id(2)
    key = lax.broadcasted_iota(I32, (tq, tq), 0)
    qry = lax.broadcasted_iota(I32, (tq, tq), 1)
    qs = [qt_ref[0, h, 0] for h in range(n_heads)]

    def step(kb, carries, mask):
        ks = pl.multiple_of(kb * tq, tq)
        return _softmax_steps(qs, [k_ref[pl.ds(ks, tq), h * dk:(h + 1) * dk] for h in range(n_heads)],
                              [vt_ref[0, h, kb] for h in range(n_heads)], carries, [mask] * n_heads)

    carries = step(qi, _softmax_init(n_heads, tq, dv), key <= qry)
    carries = lax.fori_loop(0, qi, lambda kb, c: step(kb, c, None), carries)
    _softmax_finish(o_ref, carries, dv)


def _mla_attention(q_t, k_a, v_t, bsz, seq, n_heads=8):
    t = k_a.shape[0]
    tq = ATTN_BLOCK
    nq = seq // tq
    dk = 2 * LANES
    dv = MLA_V_DIM
    return pl.pallas_call(
        functools.partial(_causal_attn_kernel, tq=tq, n_heads=n_heads, dk=dk, dv=dv),
        grid=(bsz, MLA_HEADS // n_heads, nq),
        in_specs=[pl.BlockSpec((1, n_heads, 1, dk, tq), lambda b, h, i: (b, h, i, 0, 0)),
                  pl.BlockSpec((seq, n_heads * dk), lambda b, h, i: (b, h)),
                  pl.BlockSpec((1, n_heads, nq, dv, tq), lambda b, h, i: (b, h, 0, 0, 0))],
        out_specs=pl.BlockSpec((tq, n_heads * dv), lambda b, h, i: (b * nq + i, h)),
        out_shape=jax.ShapeDtypeStruct((t, MLA_HEADS * dv), BF16),
        compiler_params=_cparams(3),
        name="mla_attn",
    )(q_t, k_a, v_t)


def _sgu_kernel(x_ref, wz_ref, g_ref, b_ref, ws_ref, bs_ref, o_ref, *, tm):
    z = jax.nn.gelu(_dot(x_ref[...], wz_ref[...]))
    u = z[:, :SGU_WIDTH]
    v = _layer_norm_rows(z[:, SGU_WIDTH:], g_ref[...], b_ref[...]).astype(BF16)
    row = lax.broadcasted_iota(I32, (SGU_CHUNK, SGU_CHUNK), 0)
    col = lax.broadcasted_iota(I32, (SGU_CHUNK, SGU_CHUNK), 1)
    bias = bs_ref[...]
    for g in range(SGU_GROUPS):
        w = jnp.where(col <= row, ws_ref[g], 0.0).astype(BF16)
        lo, hi = g * SGU_GROUP_DIM, (g + 1) * SGU_GROUP_DIM
        for c in range(tm // SGU_CHUNK):
            r0, r1 = c * SGU_CHUNK, (c + 1) * SGU_CHUNK
            mixed = _dot(w, v[r0:r1, lo:hi]) + bias[:, lo:hi]
            o_ref[r0:r1, lo:hi] = (u[r0:r1, lo:hi] * mixed).astype(o_ref.dtype)


def _sgu(x, wz, ln_g, ln_b, w_s, bias_full, tm=256):
    t = x.shape[0]
    full = lambda arr: pl.BlockSpec(arr.shape, lambda i: (0,) * arr.ndim)
    return pl.pallas_call(
        functools.partial(_sgu_kernel, tm=tm),
        grid=(t // tm,),
        in_specs=[pl.BlockSpec((tm, D_MODEL), lambda i: (i, 0)),
                  full(wz), full(ln_g), full(ln_b), full(w_s), full(bias_full)],
        out_specs=pl.BlockSpec((tm, SGU_WIDTH), lambda i: (i, 0)),
        out_shape=jax.ShapeDtypeStruct((t, SGU_WIDTH), BF16),
        compiler_params=_cparams(1),
        name="sgu",
    )(x, wz, ln_g, ln_b, w_s, bias_full)


def _moba_k_proj_kernel(x_ref, w_ref, cos_ref, sin_ref, o_ref):
    acc = _dot(x_ref[...], w_ref[...])
    cos = cos_ref[...]
    sin = sin_ref[...]
    for h in range(MOBA_HEADS):
        seg = acc[:, h * LANES:(h + 1) * LANES]
        rot = pltpu.roll(seg, MOBA_HEAD_DIM // 2, axis=1)
        o_ref[:, h * LANES:(h + 1) * LANES] = (seg * cos + rot * sin).astype(o_ref.dtype)


def _moba_k_proj(x, w_k, cos128, sin128, seq, tm=512):
    t = x.shape[0]
    n_pos = seq // tm
    return pl.pallas_call(
        _moba_k_proj_kernel,
        grid=(t // tm,),
        in_specs=[pl.BlockSpec((tm, D_MODEL), lambda i: (i, 0)),
                  pl.BlockSpec(w_k.shape, lambda i: (0, 0)),
                  pl.BlockSpec((tm, LANES), lambda i: (i % n_pos, 0)),
                  pl.BlockSpec((tm, LANES), lambda i: (i % n_pos, 0))],
        out_specs=pl.BlockSpec((tm, MOBA_WIDTH), lambda i: (i, 0)),
        out_shape=jax.ShapeDtypeStruct((t, MOBA_WIDTH), BF16),
        compiler_params=_cparams(1),
        name="moba_k_proj",
    )(x, w_k, cos128, sin128)


def _moba_qv_proj_kernel(x_ref, wt_ref, cost_ref, sint_ref, o_ref, *, scale):
    y_t = _dot_nt(wt_ref[...], x_ref[...])
    hd = MOBA_HEAD_DIM

    @pl.when(pl.program_id(0) == 0)
    def _():
        cos_t = cost_ref[...]
        sin_t = sint_ref[...]
        for h in range(MOBA_HEADS):
            seg = y_t[h * hd:(h + 1) * hd]
            rot = jnp.concatenate([seg[hd // 2:], seg[:hd // 2]], axis=0)
            o_ref[0, h, 0] = ((seg * cos_t + rot * sin_t) * scale).astype(o_ref.dtype)

    @pl.when(pl.program_id(0) == 1)
    def _():
        for h in range(MOBA_HEADS):
            o_ref[0, h, 0] = y_t[h * hd:(h + 1) * hd].astype(o_ref.dtype)


def _moba_qv_proj(x, w_qv_t, cos128_t, sin128_t, seq):
    t = x.shape[0]
    tm = ATTN_BLOCK
    n_pos = seq // tm
    return pl.pallas_call(
        functools.partial(_moba_qv_proj_kernel, scale=MOBA_HEAD_DIM ** -0.5),
        grid=(2, t // tm),
        in_specs=[pl.BlockSpec((tm, D_MODEL), lambda j, i: (i, 0)),
                  pl.BlockSpec((MOBA_WIDTH, D_MODEL), lambda j, i: (j, 0)),
                  pl.BlockSpec((MOBA_HEAD_DIM, tm), lambda j, i: (0, i % n_pos)),
                  pl.BlockSpec((MOBA_HEAD_DIM, tm), lambda j, i: (0, i % n_pos))],
        out_specs=pl.BlockSpec((1, MOBA_HEADS, 1, MOBA_HEAD_DIM, tm),
                               lambda j, i: (i // n_pos, j, i % n_pos, 0, 0)),
        out_shape=jax.ShapeDtypeStruct((t // seq, 2 * MOBA_HEADS, n_pos, MOBA_HEAD_DIM, tm), BF16),
        compiler_params=_cparams(2),
        name="moba_qv_proj",
    )(x, w_qv_t, cos128_t, sin128_t)


def _moba_attn_kernel(qt_ref, k_ref, vt_ref, avg_ref, o_ref, kmean_ref, *, n_heads):
    j = pl.program_id(2)
    blk = MOBA_BLOCK
    hd = MOBA_HEAD_DIM

    @pl.when(j == 0)
    def _():
        kmean_ref[...] = _dot(avg_ref[...], k_ref[...])

    blk_id = lax.broadcasted_iota(I32, (SUBLANES, blk), 0)
    n_blocks = k_ref.shape[0] // blk
    qs, sels = [], []
    for h in range(n_heads):
        q = qt_ref[0, h, 0]
        gate = jnp.dot(kmean_ref[0:SUBLANES, h * hd:(h + 1) * hd], q.astype(F32),
                       precision=lax.Precision.HIGHEST, preferred_element_type=F32)
        rank = jnp.zeros((SUBLANES, blk), I32)
        for m in range(n_blocks):
            gm = gate[m:m + 1, :]
            beats = (gm > gate) | ((gm == gate) & (m < blk_id))
            rank = rank + jnp.where(beats & (m < j), 1, 0)
        qs.append(q)
        sels.append(jnp.where((rank < MOBA_TOPK) & (blk_id < j), 1.0, 0.0))

    def step(kb, carries, masks):
        ks = pl.multiple_of(kb * blk, blk)
        return _softmax_steps(qs, [k_ref[pl.ds(ks, blk), h * hd:(h + 1) * hd] for h in range(n_heads)],
                              [vt_ref[0, h, kb] for h in range(n_heads)], carries, masks)

    key = lax.broadcasted_iota(I32, (blk, blk), 0)
    qry = lax.broadcasted_iota(I32, (blk, blk), 1)
    carries = step(j, _softmax_init(n_heads, blk, hd), [key <= qry] * n_heads)

    def body(n, c):
        masks = [jnp.max(jnp.where(blk_id == n, sels[h], 0.0), axis=0, keepdims=True) > 0.5
                 for h in range(n_heads)]
        return step(n, c, masks)

    carries = lax.fori_loop(0, j, body, carries)
    _softmax_finish(o_ref, carries, hd)


def _moba_attention(qv_t, k, avg, bsz, seq, n_heads=8):
    t = k.shape[0]
    nq = seq // MOBA_BLOCK
    assert nq <= SUBLANES and MOBA_BLOCK == ATTN_BLOCK
    w = n_heads * MOBA_HEAD_DIM
    groups = MOBA_HEADS // n_heads
    return pl.pallas_call(
        functools.partial(_moba_attn_kernel, n_heads=n_heads),
        grid=(bsz, groups, nq),
        in_specs=[pl.BlockSpec((1, n_heads, 1, MOBA_HEAD_DIM, MOBA_BLOCK), lambda b, h, i: (b, h, i, 0, 0)),
                  pl.BlockSpec((seq, w), lambda b, h, i: (b, h)),
                  pl.BlockSpec((1, n_heads, nq, MOBA_HEAD_DIM, MOBA_BLOCK),
                               lambda b, h, i: (b, groups + h, 0, 0, 0)),
                  pl.BlockSpec(avg.shape, lambda b, h, i: (0, 0))],
        out_specs=pl.BlockSpec((MOBA_BLOCK, w), lambda b, h, i: (b * nq + i, h)),
        out_shape=jax.ShapeDtypeStruct((t, MOBA_WIDTH), BF16),
        scratch_shapes=[pltpu.VMEM((LANES, w), F32)],
        compiler_params=_cparams(3),
        name="moba_attn",
    )(qv_t, k, qv_t, avg)


def _mem_attn_kernel(x_ref, wq_ref, kv_ref, o_ref, *, scale):
    q = (_dot(x_ref[...], wq_ref[...]) * scale).astype(BF16)
    for h in range(MEM_HEADS):
        lo, hi = h * MEM_HEAD_DIM, (h + 1) * MEM_HEAD_DIM
        s = _dot_nt(q[:, lo:hi], kv_ref[:, lo:hi])
        p = jnp.exp(s - jnp.max(s, axis=1, keepdims=True))
        o = _dot(p.astype(BF16), kv_ref[:, MEM_WIDTH + lo:MEM_WIDTH + hi])
        o_ref[:, lo:hi] = (o / jnp.sum(p, axis=1, keepdims=True)).astype(o_ref.dtype)


def _mem_attention(x, w_mq, kv_mem, bsz, seq, mem_len, tm=512):
    t = x.shape[0]
    ns = seq // tm
    return pl.pallas_call(
        functools.partial(_mem_attn_kernel, scale=MEM_HEAD_DIM ** -0.5),
        grid=(bsz, ns),
        in_specs=[pl.BlockSpec((tm, D_MODEL), lambda b, i: (b * ns + i, 0)),
                  pl.BlockSpec(w_mq.shape, lambda b, i: (0, 0)),
                  pl.BlockSpec((mem_len, 2 * MEM_WIDTH), lambda b, i: (b, 0))],
        out_specs=pl.BlockSpec((tm, MEM_WIDTH), lambda b, i: (b * ns + i, 0)),
        out_shape=jax.ShapeDtypeStruct((t, MEM_WIDTH), BF16),
        compiler_params=_cparams(2),
        name="mem_attn",
    )(x, w_mq, kv_mem)


def _gated_sum_kernel(x_ref, oa_ref, ob_ref, oc_ref, om_ref, g0_ref, g1_ref, g2_ref, g3_ref,
                      wa_ref, wb_ref, wc_ref, wm_ref, y_ref):
    xb = x_ref[...]
    acc = None
    for o_ref, g_ref, w_ref in ((oa_ref, g0_ref, wa_ref), (ob_ref, g1_ref, wb_ref),
                                (oc_ref, g2_ref, wc_ref), (om_ref, g3_ref, wm_ref)):
        term = jax.nn.sigmoid(_dot(xb, g_ref[...])) * _dot(o_ref[...], w_ref[...])
        acc = term if acc is None else acc + term
    y_ref[...] = acc.astype(y_ref.dtype)


def _gated_sum(x, o_a, o_b, o_c, o_m, w_gate, w_br_a, w_br_b, w_br_c, w_br_m, tm=1024, tn=256):
    t = x.shape[0]
    nj = D_MODEL // tn
    rows = lambda width: pl.BlockSpec((tm, width), lambda j, i: (i, 0))
    gate = lambda b: pl.BlockSpec((D_MODEL, tn), lambda j, i: (0, b * nj + j))
    cols = lambda width: pl.BlockSpec((width, tn), lambda j, i: (0, j))
    return pl.pallas_call(
        _gated_sum_kernel,
        grid=(nj, t // tm),
        in_specs=[rows(D_MODEL), rows(o_a.shape[1]), rows(o_b.shape[1]), rows(o_c.shape[1]), rows(o_m.shape[1]),
                  gate(0), gate(1), gate(2), gate(3),
                  cols(w_br_a.shape[0]), cols(w_br_b.shape[0]), cols(w_br_c.shape[0]), cols(w_br_m.shape[0])],
        out_specs=pl.BlockSpec((tm, tn), lambda j, i: (i, j)),
        out_shape=jax.ShapeDtypeStruct((t, D_MODEL), BF16),
        compiler_params=_cparams(2),
        name="gated_sum",
    )(x, o_a, o_b, o_c, o_m, w_gate, w_gate, w_gate, w_gate, w_br_a, w_br_b, w_br_c, w_br_m)


def _pack_bf16_pairs(v):
    bits = lax.bitcast_convert_type(v.astype(BF16).astype(F32), U32)
    half = v.shape[1] // 2
    return (bits[:, :half] >> 16) | bits[:, half:]


def _unpack_bf16_pairs(words):
    lo = lax.bitcast_convert_type(words << 16, F32)
    hi = lax.bitcast_convert_type(words & jnp.uint32(0xFFFF0000), F32)
    return lo, hi


def _store_row_slabs(ref, words):
    for c in range(ROW_SLAB):
        ref[pl.ds(c, words.shape[0], stride=ROW_SLAB), :] = words[:, c * LANES:(c + 1) * LANES]


def _load_row_slabs(ref):
    n_rows = ref.shape[0] // ROW_SLAB
    return jnp.concatenate([ref[pl.ds(c, n_rows, stride=ROW_SLAB), :] for c in range(ROW_SLAB)], axis=1)


def _proj_norm_kernel(y_ref, w_ref, res_ref, g_ref, b_ref, o32_ref, o16_ref, opk_ref):
    v = DEEPNORM_ALPHA * res_ref[...] + _dot(y_ref[...], w_ref[...])
    out = _layer_norm_rows(v, g_ref[...], b_ref[...])
    o32_ref[...] = out
    o16_ref[...] = out.astype(o16_ref.dtype)
    _store_row_slabs(opk_ref, _pack_bf16_pairs(out))


def _proj_norm(y, w_o, res, g, b, tm=256):
    t = y.shape[0]
    full = lambda arr: pl.BlockSpec(arr.shape, lambda i: (0,) * arr.ndim)
    rows = pl.BlockSpec((tm, D_MODEL), lambda i: (i, 0))
    return pl.pallas_call(
        _proj_norm_kernel,
        grid=(t // tm,),
        in_specs=[rows, full(w_o), rows, full(g), full(b)],
        out_specs=[rows, rows, pl.BlockSpec((tm * ROW_SLAB, LANES), lambda i: (i, 0))],
        out_shape=[jax.ShapeDtypeStruct((t, D_MODEL), F32), jax.ShapeDtypeStruct((t, D_MODEL), BF16),
                   jax.ShapeDtypeStruct((t * ROW_SLAB, LANES), U32)],
        compiler_params=_cparams(1),
        name="proj_norm",
    )(y, w_o, res, g, b)


def _router_kernel(x_ref, w_ref, bias_ref, e_ref, wt_ref, rank_ref, cnt_ref, run_ref, *, tm):
    i = pl.program_id(0)

    @pl.when(i == 0)
    def _():
        run_ref[...] = jnp.zeros_like(run_ref)

    logits = jnp.dot(x_ref[...], w_ref[...], precision=lax.Precision.HIGHEST, preferred_element_type=F32)
    scores = jax.nn.sigmoid(logits)
    lane = lax.broadcasted_iota(I32, (tm, LANES), 1)
    biased = jnp.where(lane < N_EXPERTS, scores + bias_ref[...], NEG_BIG)
    picks = []
    chosen = jnp.zeros((tm, LANES), jnp.bool_)
    for _ in range(TOP_K):
        mx = jnp.max(biased, axis=1, keepdims=True)
        idx = jnp.min(jnp.where(biased == mx, lane, LANES), axis=1, keepdims=True)
        hit = lane == idx
        picks.append((idx, hit))
        chosen = chosen | hit
        biased = jnp.where(hit, 2 * NEG_BIG, biased)
    picked_scores = jnp.where(chosen, scores, 0.0)
    norm = ROUTED_SCALE / jnp.sum(picked_scores, axis=1, keepdims=True)

    r = lax.broadcasted_iota(I32, (tm, tm), 0)
    c = lax.broadcasted_iota(I32, (tm, tm), 1)
    strict_lower = jnp.where(c < r, 1.0, 0.0).astype(BF16)
    chosen_f = jnp.where(chosen, 1.0, 0.0)
    arrival = run_ref[0:1, :] + _dot(strict_lower, chosen_f.astype(BF16))
    run_ref[...] = run_ref[...] + jnp.sum(chosen_f, axis=0, keepdims=True)
    cnt_ref[...] = run_ref[...]

    e_out = jnp.zeros((tm, LANES), I32)
    w_out = jnp.zeros((tm, LANES), F32)
    r_out = jnp.zeros((tm, LANES), F32)
    for slot, (idx, hit) in enumerate(picks):
        here = lane == slot
        e_out = jnp.where(here, idx, e_out)
        w_out = jnp.where(here, jnp.sum(jnp.where(hit, scores, 0.0), axis=1, keepdims=True) * norm, w_out)
        r_out = jnp.where(here, jnp.sum(jnp.where(hit, arrival, 0.0), axis=1, keepdims=True), r_out)
    e_ref[...] = e_out
    wt_ref[...] = w_out
    rank_ref[...] = r_out.astype(I32)


def _router(x32, w_router_pad, bias_pad, tm=256):
    t = x32.shape[0]
    full = lambda arr: pl.BlockSpec(arr.shape, lambda i: (0,) * arr.ndim)
    rows = pl.BlockSpec((tm, LANES), lambda i: (i, 0))
    return pl.pallas_call(
        functools.partial(_router_kernel, tm=tm),
        grid=(t // tm,),
        in_specs=[pl.BlockSpec((tm, D_MODEL), lambda i: (i, 0)), full(w_router_pad), full(bias_pad)],
        out_specs=[rows, rows, rows, pl.BlockSpec((8, LANES), lambda i: (0, 0))],
        out_shape=[jax.ShapeDtypeStruct((t, LANES), I32), jax.ShapeDtypeStruct((t, LANES), F32),
                   jax.ShapeDtypeStruct((t, LANES), I32), jax.ShapeDtypeStruct((8, LANES), F32)],
        scratch_shapes=[pltpu.VMEM((8, LANES), F32)],
        compiler_params=_cparams(1),
        name="moe_router",
    )(x32, w_router_pad, bias_pad)


def _row_tiles(ref, first_row, n_rows):
    return ref.at[pl.ds(pl.multiple_of(first_row * ROW_SLAB, ROW_SLAB), n_rows * ROW_SLAB), :]


def _dispatch_kernel(fill_start_ref, fill_n_ref, n_used_ref, dest_ref, x_ref, x16_ref, wsg_ref, wsu_ref, wsd_ref,
                     out_hbm, shared_ref, zero_ref, sem, fill_sem, *, tm, bm):
    @pl.when(pl.program_id(0) == 0)
    def _():
        zero_ref[...] = jnp.zeros_like(zero_ref)
        n_blk = out_hbm.shape[0] // (bm * ROW_SLAB)

        def fill_expert(e, wait):
            n = fill_n_ref[e]
            row = fill_start_ref[e]
            for bit in range(bm.bit_length() - 1):
                size = 1 << bit

                @pl.when((n & size) != 0)
                def _():
                    copy = pltpu.make_async_copy(_row_tiles(zero_ref, 0, size), _row_tiles(out_hbm, row, size), fill_sem)
                    copy.wait() if wait else copy.start()

                row = row + (n & size)

        def tail_copy(blk):
            return pltpu.make_async_copy(zero_ref, _row_tiles(out_hbm, blk * bm, bm), fill_sem)

        lax.fori_loop(0, N_EXPERTS, lambda e, c: (fill_expert(e, False), c)[1], 0)
        lax.fori_loop(n_used_ref[0], n_blk, lambda b, c: (tail_copy(b).start(), c)[1], 0)
        lax.fori_loop(0, N_EXPERTS, lambda e, c: (fill_expert(e, True), c)[1], 0)
        lax.fori_loop(n_used_ref[0], n_blk, lambda b, c: (tail_copy(0).wait(), c)[1], 0)

    def issue(tok, _):
        for k in range(TOP_K):
            pltpu.make_async_copy(_row_tiles(x_ref, tok, 1),
                                  _row_tiles(out_hbm, dest_ref[tok * SLOT_STRIDE + k], 1), sem).start()
        return 0

    lax.fori_loop(0, tm, issue, 0)
    xb = x16_ref[...]
    hid = jax.nn.silu(_dot(xb, wsg_ref[...])) * _dot(xb, wsu_ref[...])
    shared_ref[...] = _dot(hid.astype(BF16), wsd_ref[...]).astype(shared_ref.dtype)
    for _ in range(TOP_K):
        pltpu.make_async_copy(x_ref, _row_tiles(out_hbm, 0, tm), sem).wait()


def _dispatch(x_rows, x16, dest_flat, fill_start, fill_n, n_used, n_rows, w_sg, w_su, w_sd, tm=256):
    t = x16.shape[0]
    bm = MOE_ROW_BLOCK
    full = lambda arr: pl.BlockSpec(arr.shape, lambda i, fs, fn, nu: (0,) * arr.ndim)
    rows = pl.BlockSpec((tm, D_MODEL), lambda i, fs, fn, nu: (i, 0))
    grid_spec = pltpu.PrefetchScalarGridSpec(
        num_scalar_prefetch=3,
        grid=(t // tm,),
        in_specs=[pl.BlockSpec((tm * SLOT_STRIDE,), lambda i, fs, fn, nu: (i,), memory_space=pltpu.SMEM),
                  pl.BlockSpec((tm * ROW_SLAB, LANES), lambda i, fs, fn, nu: (i, 0)),
                  rows, full(w_sg), full(w_su), full(w_sd)],
        out_specs=[pl.BlockSpec(memory_space=pl.ANY), rows],
        scratch_shapes=[pltpu.VMEM((bm * ROW_SLAB, LANES), x_rows.dtype), pltpu.SemaphoreType.DMA(()),
                        pltpu.SemaphoreType.DMA(())],
    )
    return pl.pallas_call(
        functools.partial(_dispatch_kernel, tm=tm, bm=bm),
        grid_spec=grid_spec,
        out_shape=[jax.ShapeDtypeStruct((n_rows * ROW_SLAB, LANES), x_rows.dtype),
                   jax.ShapeDtypeStruct((t, D_MODEL), BF16)],
        compiler_params=_cparams(1),
        name="moe_dispatch",
    )(fill_start, fill_n, n_used, dest_flat, x_rows, x16, w_sg, w_su, w_sd)


def _expert_kernel(blk_e_ref, n_used_ref, xs_ref, wg_ref, wu_ref, wd_ref, ys_ref, wg16_ref, wu16_ref, wd16_ref):
    i = pl.program_id(0)
    used = i < n_used_ref[0]
    new_expert = (i == 0) | (blk_e_ref[i] != blk_e_ref[jnp.maximum(i - 1, 0)])

    @pl.when(used & new_expert)
    def _():
        wg16_ref[...] = wg_ref[0, 0].astype(BF16)
        wu16_ref[...] = wu_ref[0, 0].astype(BF16)
        wd16_ref[...] = wd_ref[0, 0].astype(BF16)

    @pl.when(used)
    def _():
        lo, hi = _unpack_bf16_pairs(_load_row_slabs(xs_ref))
        xb = jnp.concatenate([lo.astype(BF16), hi.astype(BF16)], axis=1)
        hid = jax.nn.silu(_dot(xb, wg16_ref[...])) * _dot(xb, wu16_ref[...])
        _store_row_slabs(ys_ref, _pack_bf16_pairs(_dot(hid.astype(BF16), wd16_ref[...])))

    @pl.when(jnp.logical_not(used))
    def _():
        ys_ref[...] = jnp.zeros_like(ys_ref)


def _expert_ffn(xs, blk_e, n_used, w_gate, w_up, w_down, layer):
    n_rows = xs.shape[0] // ROW_SLAB
    bm = MOE_ROW_BLOCK
    grid_spec = pltpu.PrefetchScalarGridSpec(
        num_scalar_prefetch=2,
        grid=(n_rows // bm,),
        in_specs=[pl.BlockSpec((bm * ROW_SLAB, LANES), lambda i, be, nu: (jnp.minimum(i, nu[0] - 1), 0)),
                  pl.BlockSpec((1, 1, D_MODEL, D_EXPERT), lambda i, be, nu: (layer, be[i], 0, 0)),
                  pl.BlockSpec((1, 1, D_MODEL, D_EXPERT), lambda i, be, nu: (layer, be[i], 0, 0)),
                  pl.BlockSpec((1, 1, D_EXPERT, D_MODEL), lambda i, be, nu: (layer, be[i], 0, 0))],
        out_specs=pl.BlockSpec((bm * ROW_SLAB, LANES), lambda i, be, nu: (i, 0)),
        scratch_shapes=[pltpu.VMEM((D_MODEL, D_EXPERT), BF16), pltpu.VMEM((D_MODEL, D_EXPERT), BF16),
                        pltpu.VMEM((D_EXPERT, D_MODEL), BF16)],
    )
    return pl.pallas_call(
        _expert_kernel,
        grid_spec=grid_spec,
        out_shape=jax.ShapeDtypeStruct((n_rows * ROW_SLAB, LANES), xs.dtype),
        compiler_params=_cparams(1),
        name="moe_experts",
    )(blk_e, n_used, xs, w_gate, w_up, w_down)


def _combine_kernel(dest_ref, next_dest_ref, x_ref, shared_ref, wt_ref, ys_hbm, g_ref, b_ref,
                    o32_ref, o16_ref, buf_ref, sem, *, tm):
    i = pl.program_id(0)

    def tile(ref, r):
        return ref.at[pl.ds(pl.multiple_of(r * ROW_SLAB, ROW_SLAB), ROW_SLAB), :]

    def gather(table_ref, half):
        def issue(tok, _):
            for k in range(TOP_K):
                pltpu.make_async_copy(tile(ys_hbm, table_ref[tok * SLOT_STRIDE + k]),
                                      tile(buf_ref.at[half, k], tok), sem.at[half]).start()
            return 0

        lax.fori_loop(0, tm, issue, 0)

    @pl.when(i == 0)
    def _():
        gather(dest_ref, 0)

    for half in range(2):
        @pl.when((i + 1 < pl.num_programs(0)) & ((i + 1) % 2 == half))
        def _():
            gather(next_dest_ref, half)

    total = DEEPNORM_ALPHA * x_ref[...] + shared_ref[...].astype(F32)
    for k in range(TOP_K):
        pltpu.make_async_copy(ys_hbm.at[pl.ds(0, tm * ROW_SLAB), :], buf_ref.at[0, k], sem.at[i % 2]).wait()
    wt = wt_ref[...]
    routed_lo = routed_hi = None
    for k in range(TOP_K):
        lo, hi = _unpack_bf16_pairs(_load_row_slabs(buf_ref.at[i % 2, k]))
        w_k = wt[:, k:k + 1]
        routed_lo = w_k * lo if routed_lo is None else routed_lo + w_k * lo
        routed_hi = w_k * hi if routed_hi is None else routed_hi + w_k * hi
    total = total + jnp.concatenate([routed_lo, routed_hi], axis=1)
    out = _layer_norm_rows(total, g_ref[...], b_ref[...])
    o32_ref[...] = out
    o16_ref[...] = out.astype(o16_ref.dtype)


def _combine(x32, shared, dest_flat, wts, ys, g, b, tm=256):
    t = x32.shape[0]
    n_steps = t // tm
    full = lambda arr: pl.BlockSpec(arr.shape, lambda i: (0,) * arr.ndim)
    rows = pl.BlockSpec((tm, D_MODEL), lambda i: (i, 0))
    return pl.pallas_call(
        functools.partial(_combine_kernel, tm=tm),
        grid=(n_steps,),
        in_specs=[pl.BlockSpec((tm * SLOT_STRIDE,), lambda i: (i,), memory_space=pltpu.SMEM),
                  pl.BlockSpec((tm * SLOT_STRIDE,), lambda i: (jnp.minimum(i + 1, n_steps - 1),),
                               memory_space=pltpu.SMEM),
                  rows, rows,
                  pl.BlockSpec((tm, LANES), lambda i: (i, 0)),
                  pl.BlockSpec(memory_space=pl.ANY),
                  full(g), full(b)],
        out_specs=[rows, rows],
        out_shape=[jax.ShapeDtypeStruct((t, D_MODEL), F32), jax.ShapeDtypeStruct((t, D_MODEL), BF16)],
        scratch_shapes=[pltpu.VMEM((2, TOP_K, tm * ROW_SLAB, LANES), ys.dtype), pltpu.SemaphoreType.DMA((2,))],
        compiler_params=_cparams(1),
        name="moe_combine",
    )(dest_flat, dest_flat, x32, shared, wts, ys, g, b)


def _rot_half_cols(w):
    half = w.shape[-1] // 2
    return jnp.concatenate([-w[..., half:], w[..., :half]], axis=-1)


def _pad_cols(w, width):
    return jnp.pad(w, [(0, 0)] * (w.ndim - 1) + [(0, width - w.shape[-1])])


def _rope_tables(seq):
    pos = jnp.arange(seq, dtype=F32)[:, None]

    def table(dim):
        half = dim // 2
        inv_freq = ROPE_THETA ** (-jnp.arange(half, dtype=F32) / half)
        ang = pos * inv_freq[None, :]
        return jnp.cos(ang), jnp.sin(ang)

    c64, s64 = table(MLA_ROPE_DIM)
    cos64 = _pad_cols(jnp.concatenate([c64, c64], axis=1), LANES)
    sin64 = _pad_cols(jnp.concatenate([s64, s64], axis=1), LANES)
    c128, s128 = table(MOBA_HEAD_DIM)
    cos128 = jnp.concatenate([c128, c128], axis=1)
    sin128 = jnp.concatenate([-s128, s128], axis=1)
    return cos64, sin64, cos128, sin128


def _layer(x32, x16, mem16, tables, avg, bsz, seq, mem_len, p, expert_weights, layer):
    (w_in, g_qa, w_q_up, g_kva, w_kv_up, sgu_ln_g, sgu_ln_b, w_spatial, b_spatial, w_mem_kv,
     w_br_a, w_br_b, w_br_c, w_br_m, w_o, ln1_g, ln1_b, w_router, router_bias,
     w_sh_gate, w_sh_up, w_sh_down, ln2_g, ln2_b) = p
    cos64, sin64, cos128, sin128 = tables
    t = x32.shape[0]
    row = lambda v: v.reshape(1, -1).astype(F32)

    c0 = MLA_Q_RANK + MLA_KV_RANK
    c1 = c0 + MLA_ROPE_DIM
    c2 = c1 + 2 * SGU_WIDTH
    c3 = c2 + 3 * MOBA_WIDTH
    c4 = c3 + MEM_WIDTH
    w_a = w_in[:, :c0].astype(BF16)
    w_kpe_raw = w_in[:, c0:c1]
    w_kpe = jnp.concatenate([_pad_cols(w_kpe_raw, LANES), _pad_cols(_rot_half_cols(w_kpe_raw), LANES)],
                            axis=1).astype(BF16)
    w_z = w_in[:, c1:c2].astype(BF16)
    w_c = w_in[:, c2:c3].astype(BF16)
    w_mq = w_in[:, c3:c4].astype(BF16)
    w_g = w_in[:, c4:].astype(BF16)

    w_qn = w_q_up[:, :, :MLA_NOPE_DIM].reshape(MLA_Q_RANK, -1).astype(BF16)
    w_qpe = w_q_up[:, :, MLA_NOPE_DIM:]
    w_qp = _pad_cols(w_qpe, LANES).reshape(MLA_Q_RANK, -1).astype(BF16)
    w_qpr = _pad_cols(_rot_half_cols(w_qpe), LANES).reshape(MLA_Q_RANK, -1).astype(BF16)
    w_kn = w_kv_up[:, :, :MLA_NOPE_DIM].reshape(MLA_KV_RANK, -1).astype(BF16)
    w_v = w_kv_up[:, :, MLA_NOPE_DIM:].reshape(MLA_KV_RANK, -1).T.astype(BF16)
    qt_a, k_a, vt_a = _mla_proj(x16, w_a, w_kpe, row(g_qa), row(g_kva), w_qn.T, w_qp.T, w_qpr.T, w_kn, w_v,
                                cos64, sin64, seq)
    o_a = _mla_attention(qt_a, k_a, vt_a, bsz, seq)

    bias_full = jnp.repeat(b_spatial.T.astype(F32), SGU_GROUP_DIM, axis=1)
    o_b = _sgu(x16, w_z, row(sgu_ln_g), row(sgu_ln_b), w_spatial.astype(F32), bias_full)

    k_c = _moba_k_proj(x16, w_c[:, MOBA_WIDTH:2 * MOBA_WIDTH], cos128, sin128, seq)
    w_qv_t = jnp.concatenate([w_c[:, :MOBA_WIDTH], w_c[:, 2 * MOBA_WIDTH:]], axis=1).T
    qvt_c = _moba_qv_proj(x16, w_qv_t, cos128.T, sin128.T, seq)
    o_c = _moba_attention(qvt_c, k_c, avg, bsz, seq)

    kv_mem = _matmul(mem16, w_mem_kv.astype(BF16), mem_len, 2 * MEM_WIDTH, BF16, "mem_kv")
    o_m = _mem_attention(x16, w_mq, kv_mem, bsz, seq, mem_len)

    y = _gated_sum(x16, o_a, o_b, o_c, o_m, w_g, w_br_a.astype(BF16), w_br_b.astype(BF16),
                   w_br_c.astype(BF16), w_br_m.astype(BF16))
    x32, x16, x_packed = _proj_norm(y, w_o.astype(BF16), x32, row(ln1_g), row(ln1_b))

    e_idx, wts, rank, counts = _router(x32, _pad_cols(w_router.astype(F32), LANES),
                                       _pad_cols(row(router_bias), LANES))
    bm = MOE_ROW_BLOCK
    n_blk = (t * TOP_K) // bm + N_EXPERTS
    counts = counts[0, :N_EXPERTS].astype(I32)
    padded = (counts + bm - 1) // bm * bm
    pad_ends = jnp.cumsum(padded)
    pad_starts = pad_ends - padded
    dest = _pad_cols((pad_starts[e_idx[:, :TOP_K]] + rank[:, :TOP_K]).astype(I32), SLOT_STRIDE).reshape(-1)
    blk_first = jnp.arange(n_blk, dtype=I32)[:, None] * bm
    blk_e = jnp.minimum(jnp.sum((pad_ends[None, :] <= blk_first).astype(I32), axis=1), N_EXPERTS - 1)
    n_used = (pad_ends[-1:] // bm).astype(I32)
    xs, shared = _dispatch(x_packed, x16, dest, pad_starts + counts, padded - counts, n_used, n_blk * bm,
                           w_sh_gate.astype(BF16), w_sh_up.astype(BF16), w_sh_down.astype(BF16))
    ys = _expert_ffn(xs, blk_e, n_used, *expert_weights, layer)
    return _combine(x32, shared, dest, wts, ys, row(ln2_g), row(ln2_b))


def kernel(x, mem, w_in, g_qa, w_q_up, g_kva, w_kv_up, sgu_ln_g, sgu_ln_b, w_spatial, b_spatial, w_mem_kv, w_br_a, w_br_b, w_br_c, w_br_m, w_o, ln1_g, ln1_b, w_router, router_bias, w_exp_gate, w_exp_up, w_exp_down, w_sh_gate, w_sh_up, w_sh_down, ln2_g, ln2_b):
    bsz, seq, d = x.shape
    mem_len = mem.shape[1]
    params = (w_in, g_qa, w_q_up, g_kva, w_kv_up, sgu_ln_g, sgu_ln_b, w_spatial, b_spatial, w_mem_kv,
              w_br_a, w_br_b, w_br_c, w_br_m, w_o, ln1_g, ln1_b, w_router, router_bias,
              w_sh_gate, w_sh_up, w_sh_down, ln2_g, ln2_b)
    expert_weights = (w_exp_gate, w_exp_up, w_exp_down)
    tables = _rope_tables(seq)
    n_blocks = seq // MOBA_BLOCK
    blk_of_pos = jnp.arange(seq, dtype=I32)[None, :] // MOBA_BLOCK
    avg = jnp.where(blk_of_pos == jnp.arange(LANES, dtype=I32)[:, None], 1.0 / MOBA_BLOCK, 0.0).astype(BF16)
    assert n_blocks <= LANES
    x32 = x.reshape(bsz * seq, d)
    x16 = x32.astype(BF16)
    mem16 = mem.reshape(bsz * mem_len, d).astype(BF16)
    for l in range(DEPTH):
        x32, x16 = _layer(x32, x16, mem16, tables, avg, bsz, seq, mem_len, tuple(w[l] for w in params),
                          expert_weights, l)
    return x32.reshape(bsz, seq, d)
```

```python
import functools

import jax
import jax.numpy as jnp
from jax import lax
from jax.experimental import pallas as pl
from jax.experimental.pallas import tpu as pltpu

F32 = jnp.float32
BF16 = jnp.bfloat16
I32 = jnp.int32
U32 = jnp.uint32

D_MODEL = 2048
DEPTH = 2
MLA_HEADS = 8
MLA_Q_RANK = 512
MLA_KV_RANK = 256
MLA_NOPE_DIM = 128
MLA_ROPE_DIM = 64
MLA_V_DIM = 128
SGU_GROUPS = 8
SGU_GROUP_DIM = 128
SGU_WIDTH = SGU_GROUPS * SGU_GROUP_DIM
SGU_CHUNK = 128
MOBA_HEADS = 8
MOBA_HEAD_DIM = 128
MOBA_WIDTH = MOBA_HEADS * MOBA_HEAD_DIM
MOBA_BLOCK = 256
MOBA_TOPK = 3
MEM_HEADS = 4
MEM_HEAD_DIM = 128
MEM_WIDTH = MEM_HEADS * MEM_HEAD_DIM
N_BRANCH = 4
N_EXPERTS = 64
TOP_K = 6
D_EXPERT = 512
ROUTED_SCALE = 2.5
ROPE_THETA = 10000.0
DEEPNORM_ALPHA = (2 * DEPTH) ** 0.25

LANES = 128
SUBLANES = 8
ATTN_BLOCK = 256
MOE_ROW_BLOCK = 512
SLOT_STRIDE = 8
ROW_WORDS = D_MODEL // 2
ROW_SLAB = ROW_WORDS // LANES
NEG_BIG = -1e30
VMEM_LIMIT = 56 * 1024 * 1024


def _cparams(n_axes):
    return pltpu.CompilerParams(dimension_semantics=("arbitrary",) * n_axes,
                                vmem_limit_bytes=VMEM_LIMIT)


def _dot(a, b):
    return jnp.dot(a, b, preferred_element_type=F32)


def _dot_nt(a, b):
    return lax.dot_general(a, b, (((1,), (1,)), ((), ())), preferred_element_type=F32)


def _layer_norm_rows(v, g, b, eps=1e-5):
    mu = jnp.mean(v, axis=-1, keepdims=True)
    vc = v - mu
    var = jnp.mean(vc * vc, axis=-1, keepdims=True)
    return vc * lax.rsqrt(var + eps) * g + b


def _rms_norm_rows(v, g, eps=1e-6):
    return v * lax.rsqrt(jnp.mean(v * v, axis=-1, keepdims=True) + eps) * g


def _mm_kernel(x_ref, w_ref, o_ref):
    o_ref[...] = _dot(x_ref[...], w_ref[...]).astype(o_ref.dtype)


def _matmul(x, w, tm, tn, out_dtype, name):
    m, k = x.shape
    n = w.shape[1]
    return pl.pallas_call(
        _mm_kernel,
        grid=(n // tn, m // tm),
        in_specs=[pl.BlockSpec((tm, k), lambda j, i: (i, 0)),
                  pl.BlockSpec((k, tn), lambda j, i: (0, j))],
        out_specs=pl.BlockSpec((tm, tn), lambda j, i: (i, j)),
        out_shape=jax.ShapeDtypeStruct((m, n), out_dtype),
        compiler_params=_cparams(2),
        name=name,
    )(x, w)


def _mla_proj_kernel(x_ref, wa_ref, wkpe_ref, gq_ref, gkv_ref, wqnt_ref, wqpt_ref, wqprt_ref,
                     wkn_ref, wvt_ref, cos_ref, sin_ref, cost_ref, sint_ref, qt_ref, k_ref, vt_ref, *, scale):
    xb = x_ref[...]
    a = _dot(xb, wa_ref[...])
    cq = _rms_norm_rows(a[:, :MLA_Q_RANK], gq_ref[...]).astype(BF16)
    ckv = _rms_norm_rows(a[:, MLA_Q_RANK:], gkv_ref[...]).astype(BF16)
    qn_t = _dot_nt(wqnt_ref[...], cq)
    qp_t = _dot_nt(wqpt_ref[...], cq)
    qpr_t = _dot_nt(wqprt_ref[...], cq)
    v_t = _dot_nt(wvt_ref[...], ckv)
    cos_t = cost_ref[...]
    sin_t = sint_ref[...]
    for h in range(MLA_HEADS):
        lo, hi = h * LANES, (h + 1) * LANES
        qt_ref[0, h, 0, 0:LANES, :] = (qn_t[lo:hi] * scale).astype(qt_ref.dtype)
        qt_ref[0, h, 0, LANES:2 * LANES, :] = ((qp_t[lo:hi] * cos_t + qpr_t[lo:hi] * sin_t) * scale).astype(qt_ref.dtype)
        vt_ref[0, h, 0] = v_t[h * MLA_V_DIM:(h + 1) * MLA_V_DIM, :].astype(vt_ref.dtype)
    kn = _dot(ckv, wkn_ref[...])
    kp = _dot(xb, wkpe_ref[...])
    kpe = (kp[:, :LANES] * cos_ref[...] + kp[:, LANES:] * sin_ref[...]).astype(k_ref.dtype)
    for h in range(MLA_HEADS):
        lo, hi = h * LANES, (h + 1) * LANES
        k_ref[:, 2 * lo:2 * lo + LANES] = kn[:, lo:hi].astype(k_ref.dtype)
        k_ref[:, 2 * lo + LANES:2 * hi] = kpe


def _mla_proj(x, wa, wkpe, gq, gkv, wqnt, wqpt, wqprt, wkn, wvt, cos64, sin64, seq):
    t = x.shape[0]
    tm = ATTN_BLOCK
    full = lambda arr: pl.BlockSpec(arr.shape, lambda i: (0,) * arr.ndim)
    n_pos = seq // tm
    scale = (MLA_NOPE_DIM + MLA_ROPE_DIM) ** -0.5
    dk = 2 * LANES
    feat_major = lambda d: pl.BlockSpec((1, MLA_HEADS, 1, d, tm), lambda i: (i // n_pos, 0, i % n_pos, 0, 0))
    return pl.pallas_call(
        functools.partial(_mla_proj_kernel, scale=scale),
        grid=(t // tm,),
        in_specs=[pl.BlockSpec((tm, D_MODEL), lambda i: (i, 0)),
                  full(wa), full(wkpe), full(gq), full(gkv), full(wqnt), full(wqpt), full(wqprt),
                  full(wkn), full(wvt),
                  pl.BlockSpec((tm, LANES), lambda i: (i % n_pos, 0)),
                  pl.BlockSpec((tm, LANES), lambda i: (i % n_pos, 0)),
                  pl.BlockSpec((LANES, tm), lambda i: (0, i % n_pos)),
                  pl.BlockSpec((LANES, tm), lambda i: (0, i % n_pos))],
        out_specs=[feat_major(dk),
                   pl.BlockSpec((tm, MLA_HEADS * dk), lambda i: (i, 0)),
                   feat_major(MLA_V_DIM)],
        out_shape=[jax.ShapeDtypeStruct((t // seq, MLA_HEADS, n_pos, dk, tm), BF16),
                   jax.ShapeDtypeStruct((t, MLA_HEADS * dk), BF16),
                   jax.ShapeDtypeStruct((t // seq, MLA_HEADS, n_pos, MLA_V_DIM, tm), BF16)],
        compiler_params=_cparams(1),
        name="mla_proj",
    )(x, wa, wkpe, gq, gkv, wqnt, wqpt, wqprt, wkn, wvt, cos64, sin64, cos64.T, sin64.T)


def _softmax_steps(q_ts, ks, v_ts, carries, masks):
    heads = range(len(q_ts))
    s_ts = [_dot(ks[h], q_ts[h]) for h in heads]
    s_ts = [s_ts[h] if masks[h] is None else jnp.where(masks[h], s_ts[h], NEG_BIG) for h in heads]
    m_news = [jnp.maximum(carries[h][0], jnp.max(s_ts[h], axis=0, keepdims=True)) for h in heads]
    p_ts = [jnp.exp(s_ts[h] - m_news[h]) for h in heads]
    pvs = [_dot(v_ts[h], p_ts[h].astype(v_ts[h].dtype)) for h in heads]
    out = []
    for h in heads:
        m_i, l_i, acc_t = carries[h]
        alpha = jnp.exp(m_i - m_news[h])
        l_new = alpha * l_i + jnp.sum(p_ts[h], axis=0, keepdims=True)
        out.append((m_news[h], l_new, alpha * acc_t + pvs[h]))
    return tuple(out)


def _softmax_init(n_heads, tq, dv):
    return tuple((jnp.full((1, tq), NEG_BIG, F32), jnp.zeros((1, tq), F32), jnp.zeros((dv, tq), F32))
                 for _ in range(n_heads))


def _softmax_finish(o_ref, carries, dv):
    for h, (_, l_i, acc_t) in enumerate(carries):
        o_ref[:, h * dv:(h + 1) * dv] = (acc_t / l_i).T.astype(o_ref.dtype)


def _causal_attn_kernel(qt_ref, k_ref, vt_ref, o_ref, *, tq, n_heads, dk, dv):
    qi = pl.program_id(2)
    key = lax.broadcasted_iota(I32, (tq, tq), 0)
    qry = lax.broadcasted_iota(I32, (tq, tq), 1)
    qs = [qt_ref[0, h, 0] for h in range(n_heads)]

    def step(kb, carries, mask):
        ks = pl.multiple_of(kb * tq, tq)
        return _softmax_steps(qs, [k_ref[pl.ds(ks, tq), h * dk:(h + 1) * dk] for h in range(n_heads)],
                              [vt_ref[0, h, kb] for h in range(n_heads)], carries, [mask] * n_heads)

    carries = step(qi, _softmax_init(n_heads, tq, dv), key <= qry)
    carries = lax.fori_loop(0, qi, lambda kb, c: step(kb, c, None), carries)
    _softmax_finish(o_ref, carries, dv)


def _mla_attention(q_t, k_a, v_t, bsz, seq, n_heads=8):
    t = k_a.shape[0]
    tq = ATTN_BLOCK
    nq = seq // tq
    dk = 2 * LANES
    dv = MLA_V_DIM
    return pl.pallas_call(
        functools.partial(_causal_attn_kernel, tq=tq, n_heads=n_heads, dk=dk, dv=dv),
        grid=(bsz, MLA_HEADS // n_heads, nq),
        in_specs=[pl.BlockSpec((1, n_heads, 1, dk, tq), lambda b, h, i: (b, h, i, 0, 0)),
                  pl.BlockSpec((seq, n_heads * dk), lambda b, h, i: (b, h)),
                  pl.BlockSpec((1, n_heads, nq, dv, tq), lambda b, h, i: (b, h, 0, 0, 0))],
        out_specs=pl.BlockSpec((tq, n_heads * dv), lambda b, h, i: (b * nq + i, h)),
        out_shape=jax.ShapeDtypeStruct((t, MLA_HEADS * dv), BF16),
        compiler_params=_cparams(3),
        name="mla_attn",
    )(q_t, k_a, v_t)


def _sgu_kernel(x_ref, wz_ref, g_ref, b_ref, ws_ref, bs_ref, o_ref, *, tm):
    z = jax.nn.gelu(_dot(x_ref[...], wz_ref[...]))
    u = z[:, :SGU_WIDTH]
    v = _layer_norm_rows(z[:, SGU_WIDTH:], g_ref[...], b_ref[...]).astype(BF16)
    row = lax.broadcasted_iota(I32, (SGU_CHUNK, SGU_CHUNK), 0)
    col = lax.broadcasted_iota(I32, (SGU_CHUNK, SGU_CHUNK), 1)
    bias = bs_ref[...]
    for g in range(SGU_GROUPS):
        w = jnp.where(col <= row, ws_ref[g], 0.0).astype(BF16)
        lo, hi = g * SGU_GROUP_DIM, (g + 1) * SGU_GROUP_DIM
        for c in range(tm // SGU_CHUNK):
            r0, r1 = c * SGU_CHUNK, (c + 1) * SGU_CHUNK
            mixed = _dot(w, v[r0:r1, lo:hi]) + bias[:, lo:hi]
            o_ref[r0:r1, lo:hi] = (u[r0:r1, lo:hi] * mixed).astype(o_ref.dtype)


def _sgu(x, wz, ln_g, ln_b, w_s, bias_full, tm=256):
    t = x.shape[0]
    full = lambda arr: pl.BlockSpec(arr.shape, lambda i: (0,) * arr.ndim)
    return pl.pallas_call(
        functools.partial(_sgu_kernel, tm=tm),
        grid=(t // tm,),
        in_specs=[pl.BlockSpec((tm, D_MODEL), lambda i: (i, 0)),
                  full(wz), full(ln_g), full(ln_b), full(w_s), full(bias_full)],
        out_specs=pl.BlockSpec((tm, SGU_WIDTH), lambda i: (i, 0)),
        out_shape=jax.ShapeDtypeStruct((t, SGU_WIDTH), BF16),
        compiler_params=_cparams(1),
        name="sgu",
    )(x, wz, ln_g, ln_b, w_s, bias_full)


def _moba_k_proj_kernel(x_ref, w_ref, cos_ref, sin_ref, o_ref):
    acc = _dot(x_ref[...], w_ref[...])
    cos = cos_ref[...]
    sin = sin_ref[...]
    for h in range(MOBA_HEADS):
        seg = acc[:, h * LANES:(h + 1) * LANES]
        rot = pltpu.roll(seg, MOBA_HEAD_DIM // 2, axis=1)
        o_ref[:, h * LANES:(h + 1) * LANES] = (seg * cos + rot * sin).astype(o_ref.dtype)


def _moba_k_proj(x, w_k, cos128, sin128, seq, tm=512):
    t = x.shape[0]
    n_pos = seq // tm
    return pl.pallas_call(
        _moba_k_proj_kernel,
        grid=(t // tm,),
        in_specs=[pl.BlockSpec((tm, D_MODEL), lambda i: (i, 0)),
                  pl.BlockSpec(w_k.shape, lambda i: (0, 0)),
                  pl.BlockSpec((tm, LANES), lambda i: (i % n_pos, 0)),
                  pl.BlockSpec((tm, LANES), lambda i: (i % n_pos, 0))],
        out_specs=pl.BlockSpec((tm, MOBA_WIDTH), lambda i: (i, 0)),
        out_shape=jax.ShapeDtypeStruct((t, MOBA_WIDTH), BF16),
        compiler_params=_cparams(1),
        name="moba_k_proj",
    )(x, w_k, cos128, sin128)


def _moba_qv_proj_kernel(x_ref, wt_ref, cost_ref, sint_ref, o_ref, *, scale):
    y_t = _dot_nt(wt_ref[...], x_ref[...])
    hd = MOBA_HEAD_DIM

    @pl.when(pl.program_id(0) == 0)
    def _():
        cos_t = cost_ref[...]
        sin_t = sint_ref[...]
        for h in range(MOBA_HEADS):
            seg = y_t[h * hd:(h + 1) * hd]
            rot = jnp.concatenate([seg[hd // 2:], seg[:hd // 2]], axis=0)
            o_ref[0, h, 0] = ((seg * cos_t + rot * sin_t) * scale).astype(o_ref.dtype)

    @pl.when(pl.program_id(0) == 1)
    def _():
        for h in range(MOBA_HEADS):
            o_ref[0, h, 0] = y_t[h * hd:(h + 1) * hd].astype(o_ref.dtype)


def _moba_qv_proj(x, w_qv_t, cos128_t, sin128_t, seq):
    t = x.shape[0]
    tm = ATTN_BLOCK
    n_pos = seq // tm
    return pl.pallas_call(
        functools.partial(_moba_qv_proj_kernel, scale=MOBA_HEAD_DIM ** -0.5),
        grid=(2, t // tm),
        in_specs=[pl.BlockSpec((tm, D_MODEL), lambda j, i: (i, 0)),
                  pl.BlockSpec((MOBA_WIDTH, D_MODEL), lambda j, i: (j, 0)),
                  pl.BlockSpec((MOBA_HEAD_DIM, tm), lambda j, i: (0, i % n_pos)),
                  pl.BlockSpec((MOBA_HEAD_DIM, tm), lambda j, i: (0, i % n_pos))],
        out_specs=pl.BlockSpec((1, MOBA_HEADS, 1, MOBA_HEAD_DIM, tm),
                               lambda j, i: (i // n_pos, j, i % n_pos, 0, 0)),
        out_shape=jax.ShapeDtypeStruct((t // seq, 2 * MOBA_HEADS, n_pos, MOBA_HEAD_DIM, tm), BF16),
        compiler_params=_cparams(2),
        name="moba_qv_proj",
    )(x, w_qv_t, cos128_t, sin128_t)


def _moba_attn_kernel(qt_ref, k_ref, vt_ref, avg_ref, o_ref, kmean_ref, *, n_heads):
    j = pl.program_id(2)
    blk = MOBA_BLOCK
    hd = MOBA_HEAD_DIM

    @pl.when(j == 0)
    def _():
        kmean_ref[...] = _dot(avg_ref[...], k_ref[...])

    blk_id = lax.broadcasted_iota(I32, (SUBLANES, blk), 0)
    n_blocks = k_ref.shape[0] // blk
    qs, sels = [], []
    for h in range(n_heads):
        q = qt_ref[0, h, 0]
        gate = jnp.dot(kmean_ref[0:SUBLANES, h * hd:(h + 1) * hd], q.astype(F32),
                       precision=lax.Precision.HIGHEST, preferred_element_type=F32)
        rank = jnp.zeros((SUBLANES, blk), I32)
        for m in range(n_blocks):
            gm = gate[m:m + 1, :]
            beats = (gm > gate) | ((gm == gate) & (m < blk_id))
            rank = rank + jnp.where(beats & (m < j), 1, 0)
        qs.append(q)
        sels.append(jnp.where((rank < MOBA_TOPK) & (blk_id < j), 1.0, 0.0))

    def step(kb, carries, masks):
        ks = pl.multiple_of(kb * blk, blk)
        return _softmax_steps(qs, [k_ref[pl.ds(ks, blk), h * hd:(h + 1) * hd] for h in range(n_heads)],
                              [vt_ref[0, h, kb] for h in range(n_heads)], carries, masks)

    key = lax.broadcasted_iota(I32, (blk, blk), 0)
    qry = lax.broadcasted_iota(I32, (blk, blk), 1)
    carries = step(j, _softmax_init(n_heads, blk, hd), [key <= qry] * n_heads)

    def body(n, c):
        masks = [jnp.max(jnp.where(blk_id == n, sels[h], 0.0), axis=0, keepdims=True) > 0.5
                 for h in range(n_heads)]
        return step(n, c, masks)

    carries = lax.fori_loop(0, j, body, carries)
    _softmax_finish(o_ref, carries, hd)


def _moba_attention(qv_t, k, avg, bsz, seq, n_heads=8):
    t = k.shape[0]
    nq = seq // MOBA_BLOCK
    assert nq <= SUBLANES and MOBA_BLOCK == ATTN_BLOCK
    w = n_heads * MOBA_HEAD_DIM
    groups = MOBA_HEADS // n_heads
    return pl.pallas_call(
        functools.partial(_moba_attn_kernel, n_heads=n_heads),
        grid=(bsz, groups, nq),
        in_specs=[pl.BlockSpec((1, n_heads, 1, MOBA_HEAD_DIM, MOBA_BLOCK), lambda b, h, i: (b, h, i, 0, 0)),
                  pl.BlockSpec((seq, w), lambda b, h, i: (b, h)),
                  pl.BlockSpec((1, n_heads, nq, MOBA_HEAD_DIM, MOBA_BLOCK),
                               lambda b, h, i: (b, groups + h, 0, 0, 0)),
                  pl.BlockSpec(avg.shape, lambda b, h, i: (0, 0))],
        out_specs=pl.BlockSpec((MOBA_BLOCK, w), lambda b, h, i: (b * nq + i, h)),
        out_shape=jax.ShapeDtypeStruct((t, MOBA_WIDTH), BF16),
        scratch_shapes=[pltpu.VMEM((LANES, w), F32)],
        compiler_params=_cparams(3),
        name="moba_attn",
    )(qv_t, k, qv_t, avg)


def _mem_attn_kernel(x_ref, wq_ref, kv_ref, o_ref, *, scale):
    q = (_dot(x_ref[...], wq_ref[...]) * scale).astype(BF16)
    for h in range(MEM_HEADS):
        lo, hi = h * MEM_HEAD_DIM, (h + 1) * MEM_HEAD_DIM
        s = _dot_nt(q[:, lo:hi], kv_ref[:, lo:hi])
        p = jnp.exp(s - jnp.max(s, axis=1, keepdims=True))
        o = _dot(p.astype(BF16), kv_ref[:, MEM_WIDTH + lo:MEM_WIDTH + hi])
        o_ref[:, lo:hi] = (o / jnp.sum(p, axis=1, keepdims=True)).astype(o_ref.dtype)


def _mem_attention(x, w_mq, kv_mem, bsz, seq, mem_len, tm=512):
    t = x.shape[0]
    ns = seq // tm
    return pl.pallas_call(
        functools.partial(_mem_attn_kernel, scale=MEM_HEAD_DIM ** -0.5),
        grid=(bsz, ns),
        in_specs=[pl.BlockSpec((tm, D_MODEL), lambda b, i: (b * ns + i, 0)),
                  pl.BlockSpec(w_mq.shape, lambda b, i: (0, 0)),
                  pl.BlockSpec((mem_len, 2 * MEM_WIDTH), lambda b, i: (b, 0))],
        out_specs=pl.BlockSpec((tm, MEM_WIDTH), lambda b, i: (b * ns + i, 0)),
        out_shape=jax.ShapeDtypeStruct((t, MEM_WIDTH), BF16),
        compiler_params=_cparams(2),
        name="mem_attn",
    )(x, w_mq, kv_mem)


def _gated_sum_kernel(x_ref, oa_ref, ob_ref, oc_ref, om_ref, g0_ref, g1_ref, g2_ref, g3_ref,
                      wa_ref, wb_ref, wc_ref, wm_ref, y_ref):
    xb = x_ref[...]
    acc = None
    for o_ref, g_ref, w_ref in ((oa_ref, g0_ref, wa_ref), (ob_ref, g1_ref, wb_ref),
                                (oc_ref, g2_ref, wc_ref), (om_ref, g3_ref, wm_ref)):
        term = jax.nn.sigmoid(_dot(xb, g_ref[...])) * _dot(o_ref[...], w_ref[...])
        acc = term if acc is None else acc + term
    y_ref[...] = acc.astype(y_ref.dtype)


def _gated_sum(x, o_a, o_b, o_c, o_m, w_gate, w_br_a, w_br_b, w_br_c, w_br_m, tm=1024, tn=256):
    t = x.shape[0]
    nj = D_MODEL // tn
    rows = lambda width: pl.BlockSpec((tm, width), lambda j, i: (i, 0))
    gate = lambda b: pl.BlockSpec((D_MODEL, tn), lambda j, i: (0, b * nj + j))
    cols = lambda width: pl.BlockSpec((width, tn), lambda j, i: (0, j))
    return pl.pallas_call(
        _gated_sum_kernel,
        grid=(nj, t // tm),
        in_specs=[rows(D_MODEL), rows(o_a.shape[1]), rows(o_b.shape[1]), rows(o_c.shape[1]), rows(o_m.shape[1]),
                  gate(0), gate(1), gate(2), gate(3),
                  cols(w_br_a.shape[0]), cols(w_br_b.shape[0]), cols(w_br_c.shape[0]), cols(w_br_m.shape[0])],
        out_specs=pl.BlockSpec((tm, tn), lambda j, i: (i, j)),
        out_shape=jax.ShapeDtypeStruct((t, D_MODEL), BF16),
        compiler_params=_cparams(2),
        name="gated_sum",
    )(x, o_a, o_b, o_c, o_m, w_gate, w_gate, w_gate, w_gate, w_br_a, w_br_b, w_br_c, w_br_m)


def _pack_bf16_pairs(v):
    bits = lax.bitcast_convert_type(v.astype(BF16).astype(F32), U32)
    half = v.shape[1] // 2
    return (bits[:, :half] >> 16) | bits[:, half:]


def _unpack_bf16_pairs(words):
    lo = lax.bitcast_convert_type(words << 16, F32)
    hi = lax.bitcast_convert_type(words & jnp.uint32(0xFFFF0000), F32)
    return lo, hi


def _store_row_slabs(ref, words):
    for c in range(ROW_SLAB):
        ref[pl.ds(c, words.shape[0], stride=ROW_SLAB), :] = words[:, c * LANES:(c + 1) * LANES]


def _load_row_slabs(ref):
    n_rows = ref.shape[0] // ROW_SLAB
    return jnp.concatenate([ref[pl.ds(c, n_rows, stride=ROW_SLAB), :] for c in range(ROW_SLAB)], axis=1)


def _proj_norm_kernel(y_ref, w_ref, res_ref, g_ref, b_ref, o32_ref, o16_ref, opk_ref):
    v = DEEPNORM_ALPHA * res_ref[...] + _dot(y_ref[...], w_ref[...])
    out = _layer_norm_rows(v, g_ref[...], b_ref[...])
    o32_ref[...] = out
    o16_ref[...] = out.astype(o16_ref.dtype)
    _store_row_slabs(opk_ref, _pack_bf16_pairs(out))


def _proj_norm(y, w_o, res, g, b, tm=256):
    t = y.shape[0]
    full = lambda arr: pl.BlockSpec(arr.shape, lambda i: (0,) * arr.ndim)
    rows = pl.BlockSpec((tm, D_MODEL), lambda i: (i, 0))
    return pl.pallas_call(
        _proj_norm_kernel,
        grid=(t // tm,),
        in_specs=[rows, full(w_o), rows, full(g), full(b)],
        out_specs=[rows, rows, pl.BlockSpec((tm * ROW_SLAB, LANES), lambda i: (i, 0))],
        out_shape=[jax.ShapeDtypeStruct((t, D_MODEL), F32), jax.ShapeDtypeStruct((t, D_MODEL), BF16),
                   jax.ShapeDtypeStruct((t * ROW_SLAB, LANES), U32)],
        compiler_params=_cparams(1),
        name="proj_norm",
    )(y, w_o, res, g, b)


def _router_kernel(x_ref, w_ref, bias_ref, e_ref, wt_ref, rank_ref, cnt_ref, run_ref, *, tm):
    i = pl.program_id(0)

    @pl.when(i == 0)
    def _():
        run_ref[...] = jnp.zeros_like(run_ref)

    x = x_ref[...]
    x_hi = x.astype(BF16)
    x_lo = (x - x_hi.astype(F32)).astype(BF16)
    logits = _dot(x_hi, w_ref[0]) + (_dot(x_hi, w_ref[1]) + _dot(x_lo, w_ref[0]))
    scores = jax.nn.sigmoid(logits)
    lane = lax.broadcasted_iota(I32, (tm, LANES), 1)
    biased = jnp.where(lane < N_EXPERTS, scores + bias_ref[...], NEG_BIG)
    picks = []
    chosen = jnp.zeros((tm, LANES), jnp.bool_)
    for _ in range(TOP_K):
        mx = jnp.max(biased, axis=1, keepdims=True)
        idx = jnp.min(jnp.where(biased == mx, lane, LANES), axis=1, keepdims=True)
        hit = lane == idx
        picks.append((idx, hit))
        chosen = chosen | hit
        biased = jnp.where(hit, 2 * NEG_BIG, biased)
    picked_scores = jnp.where(chosen, scores, 0.0)
    norm = ROUTED_SCALE / jnp.sum(picked_scores, axis=1, keepdims=True)

    r = lax.broadcasted_iota(I32, (tm, tm), 0)
    c = lax.broadcasted_iota(I32, (tm, tm), 1)
    strict_lower = jnp.where(c < r, 1.0, 0.0).astype(BF16)
    chosen_f = jnp.where(chosen, 1.0, 0.0)
    arrival = run_ref[0:1, :] + _dot(strict_lower, chosen_f.astype(BF16))
    run_ref[...] = run_ref[...] + jnp.sum(chosen_f, axis=0, keepdims=True)
    cnt_ref[...] = run_ref[...]

    e_out = jnp.zeros((tm, LANES), I32)
    w_out = jnp.zeros((tm, LANES), F32)
    r_out = jnp.zeros((tm, LANES), F32)
    for slot, (idx, hit) in enumerate(picks):
        here = lane == slot
        e_out = jnp.where(here, idx, e_out)
        w_out = jnp.where(here, jnp.sum(jnp.where(hit, scores, 0.0), axis=1, keepdims=True) * norm, w_out)
        r_out = jnp.where(here, jnp.sum(jnp.where(hit, arrival, 0.0), axis=1, keepdims=True), r_out)
    e_ref[...] = e_out
    wt_ref[...] = w_out
    rank_ref[...] = r_out.astype(I32)


def _router(x32, w_router_pad, bias_pad, tm=256):
    t = x32.shape[0]
    full = lambda arr: pl.BlockSpec(arr.shape, lambda i: (0,) * arr.ndim)
    rows = pl.BlockSpec((tm, LANES), lambda i: (i, 0))
    return pl.pallas_call(
        functools.partial(_router_kernel, tm=tm),
        grid=(t // tm,),
        in_specs=[pl.BlockSpec((tm, D_MODEL), lambda i: (i, 0)), full(w_router_pad), full(bias_pad)],
        out_specs=[rows, rows, rows, pl.BlockSpec((8, LANES), lambda i: (0, 0))],
        out_shape=[jax.ShapeDtypeStruct((t, LANES), I32), jax.ShapeDtypeStruct((t, LANES), F32),
                   jax.ShapeDtypeStruct((t, LANES), I32), jax.ShapeDtypeStruct((8, LANES), F32)],
        scratch_shapes=[pltpu.VMEM((8, LANES), F32)],
        compiler_params=_cparams(1),
        name="moe_router",
    )(x32, w_router_pad, bias_pad)


def _row_tiles(ref, first_row, n_rows):
    return ref.at[pl.ds(pl.multiple_of(first_row * ROW_SLAB, ROW_SLAB), n_rows * ROW_SLAB), :]


def _dispatch_kernel(fill_start_ref, fill_n_ref, n_used_ref, dest_ref, x_ref, x16_ref, wsg_ref, wsu_ref, wsd_ref,
                     out_hbm, shared_ref, zero_ref, sem, fill_sem, *, tm, bm):
    @pl.when(pl.program_id(0) == 0)
    def _():
        zero_ref[...] = jnp.zeros_like(zero_ref)
        n_blk = out_hbm.shape[0] // (bm * ROW_SLAB)

        def fill_expert(e, wait):
            n = fill_n_ref[e]
            row = fill_start_ref[e]
            for bit in range(bm.bit_length() - 1):
                size = 1 << bit

                @pl.when((n & size) != 0)
                def _():
                    copy = pltpu.make_async_copy(_row_tiles(zero_ref, 0, size), _row_tiles(out_hbm, row, size), fill_sem)
                    copy.wait() if wait else copy.start()

                row = row + (n & size)

        def tail_copy(blk):
            return pltpu.make_async_copy(zero_ref, _row_tiles(out_hbm, blk * bm, bm), fill_sem)

        lax.fori_loop(0, N_EXPERTS, lambda e, c: (fill_expert(e, False), c)[1], 0)
        lax.fori_loop(n_used_ref[0], n_blk, lambda b, c: (tail_copy(b).start(), c)[1], 0)
        lax.fori_loop(0, N_EXPERTS, lambda e, c: (fill_expert(e, True), c)[1], 0)
        lax.fori_loop(n_used_ref[0], n_blk, lambda b, c: (tail_copy(0).wait(), c)[1], 0)

    def issue(tok, _):
        for k in range(TOP_K):
            pltpu.make_async_copy(_row_tiles(x_ref, tok, 1),
                                  _row_tiles(out_hbm, dest_ref[tok * SLOT_STRIDE + k], 1), sem).start()
        return 0

    lax.fori_loop(0, tm, issue, 0)
    xb = x16_ref[...]
    hid = jax.nn.silu(_dot(xb, wsg_ref[...])) * _dot(xb, wsu_ref[...])
    shared_ref[...] = _dot(hid.astype(BF16), wsd_ref[...]).astype(shared_ref.dtype)
    for _ in range(TOP_K):
        pltpu.make_async_copy(x_ref, _row_tiles(out_hbm, 0, tm), sem).wait()


def _dispatch(x_rows, x16, dest_flat, fill_start, fill_n, n_used, n_rows, w_sg, w_su, w_sd, tm=256):
    t = x16.shape[0]
    bm = MOE_ROW_BLOCK
    full = lambda arr: pl.BlockSpec(arr.shape, lambda i, fs, fn, nu: (0,) * arr.ndim)
    rows = pl.BlockSpec((tm, D_MODEL), lambda i, fs, fn, nu: (i, 0))
    grid_spec = pltpu.PrefetchScalarGridSpec(
        num_scalar_prefetch=3,
        grid=(t // tm,),
        in_specs=[pl.BlockSpec((tm * SLOT_STRIDE,), lambda i, fs, fn, nu: (i,), memory_space=pltpu.SMEM),
                  pl.BlockSpec((tm * ROW_SLAB, LANES), lambda i, fs, fn, nu: (i, 0)),
                  rows, full(w_sg), full(w_su), full(w_sd)],
        out_specs=[pl.BlockSpec(memory_space=pl.ANY), rows],
        scratch_shapes=[pltpu.VMEM((bm * ROW_SLAB, LANES), x_rows.dtype), pltpu.SemaphoreType.DMA(()),
                        pltpu.SemaphoreType.DMA(())],
    )
    return pl.pallas_call(
        functools.partial(_dispatch_kernel, tm=tm, bm=bm),
        grid_spec=grid_spec,
        out_shape=[jax.ShapeDtypeStruct((n_rows * ROW_SLAB, LANES), x_rows.dtype),
                   jax.ShapeDtypeStruct((t, D_MODEL), BF16)],
        compiler_params=_cparams(1),
        name="moe_dispatch",
    )(fill_start, fill_n, n_used, dest_flat, x_rows, x16, w_sg, w_su, w_sd)


def _expert_kernel(blk_e_ref, n_used_ref, xs_ref, wg_ref, wu_ref, wd_ref, ys_ref, wg16_ref, wu16_ref, wd16_ref):
    i = pl.program_id(0)
    used = i < n_used_ref[0]
    new_expert = (i == 0) | (blk_e_ref[i] != blk_e_ref[jnp.maximum(i - 1, 0)])

    @pl.when(used & new_expert)
    def _():
        wg16_ref[...] = wg_ref[0, 0].astype(BF16)
        wu16_ref[...] = wu_ref[0, 0].astype(BF16)
        wd16_ref[...] = wd_ref[0, 0].astype(BF16)

    @pl.when(used)
    def _():
        lo, hi = _unpack_bf16_pairs(_load_row_slabs(xs_ref))
        xb = jnp.concatenate([lo.astype(BF16), hi.astype(BF16)], axis=1)
        hid = jax.nn.silu(_dot(xb, wg16_ref[...])) * _dot(xb, wu16_ref[...])
        _store_row_slabs(ys_ref, _pack_bf16_pairs(_dot(hid.astype(BF16), wd16_ref[...])))

    @pl.when(jnp.logical_not(used))
    def _():
        ys_ref[...] = jnp.zeros_like(ys_ref)


def _expert_ffn(xs, blk_e, n_used, w_gate, w_up, w_down, layer):
    n_rows = xs.shape[0] // ROW_SLAB
    bm = MOE_ROW_BLOCK
    grid_spec = pltpu.PrefetchScalarGridSpec(
        num_scalar_prefetch=2,
        grid=(n_rows // bm,),
        in_specs=[pl.BlockSpec((bm * ROW_SLAB, LANES), lambda i, be, nu: (jnp.minimum(i, nu[0] - 1), 0)),
                  pl.BlockSpec((1, 1, D_MODEL, D_EXPERT), lambda i, be, nu: (layer, be[i], 0, 0)),
                  pl.BlockSpec((1, 1, D_MODEL, D_EXPERT), lambda i, be, nu: (layer, be[i], 0, 0)),
                  pl.BlockSpec((1, 1, D_EXPERT, D_MODEL), lambda i, be, nu: (layer, be[i], 0, 0))],
        out_specs=pl.BlockSpec((bm * ROW_SLAB, LANES), lambda i, be, nu: (i, 0)),
        scratch_shapes=[pltpu.VMEM((D_MODEL, D_EXPERT), BF16), pltpu.VMEM((D_MODEL, D_EXPERT), BF16),
                        pltpu.VMEM((D_EXPERT, D_MODEL), BF16)],
    )
    return pl.pallas_call(
        _expert_kernel,
        grid_spec=grid_spec,
        out_shape=jax.ShapeDtypeStruct((n_rows * ROW_SLAB, LANES), xs.dtype),
        compiler_params=_cparams(1),
        name="moe_experts",
    )(blk_e, n_used, xs, w_gate, w_up, w_down)


def _combine_kernel(dest_ref, next_dest_ref, x_ref, shared_ref, wt_ref, ys_hbm, g_ref, b_ref,
                    o32_ref, o16_ref, buf_ref, sem, *, tm):
    i = pl.program_id(0)

    def tile(ref, r):
        return ref.at[pl.ds(pl.multiple_of(r * ROW_SLAB, ROW_SLAB), ROW_SLAB), :]

    def gather(table_ref, half):
        def issue(tok, _):
            for k in range(TOP_K):
                pltpu.make_async_copy(tile(ys_hbm, table_ref[tok * SLOT_STRIDE + k]),
                                      tile(buf_ref.at[half, k], tok), sem.at[half]).start()
            return 0

        lax.fori_loop(0, tm, issue, 0)

    @pl.when(i == 0)
    def _():
        gather(dest_ref, 0)

    for half in range(2):
        @pl.when((i + 1 < pl.num_programs(0)) & ((i + 1) % 2 == half))
        def _():
            gather(next_dest_ref, half)

    total = DEEPNORM_ALPHA * x_ref[...] + shared_ref[...].astype(F32)
    for k in range(TOP_K):
        pltpu.make_async_copy(ys_hbm.at[pl.ds(0, tm * ROW_SLAB), :], buf_ref.at[0, k], sem.at[i % 2]).wait()
    wt = wt_ref[...]
    routed_lo = routed_hi = None
    for k in range(TOP_K):
        lo, hi = _unpack_bf16_pairs(_load_row_slabs(buf_ref.at[i % 2, k]))
        w_k = wt[:, k:k + 1]
        routed_lo = w_k * lo if routed_lo is None else routed_lo + w_k * lo
        routed_hi = w_k * hi if routed_hi is None else routed_hi + w_k * hi
    total = total + jnp.concatenate([routed_lo, routed_hi], axis=1)
    out = _layer_norm_rows(total, g_ref[...], b_ref[...])
    o32_ref[...] = out
    o16_ref[...] = out.astype(o16_ref.dtype)


def _combine(x32, shared, dest_flat, wts, ys, g, b, tm=256):
    t = x32.shape[0]
    n_steps = t // tm
    full = lambda arr: pl.BlockSpec(arr.shape, lambda i: (0,) * arr.ndim)
    rows = pl.BlockSpec((tm, D_MODEL), lambda i: (i, 0))
    return pl.pallas_call(
        functools.partial(_combine_kernel, tm=tm),
        grid=(n_steps,),
        in_specs=[pl.BlockSpec((tm * SLOT_STRIDE,), lambda i: (i,), memory_space=pltpu.SMEM),
                  pl.BlockSpec((tm * SLOT_STRIDE,), lambda i: (jnp.minimum(i + 1, n_steps - 1),),
                               memory_space=pltpu.SMEM),
                  rows, rows,
                  pl.BlockSpec((tm, LANES), lambda i: (i, 0)),
                  pl.BlockSpec(memory_space=pl.ANY),
                  full(g), full(b)],
        out_specs=[rows, rows],
        out_shape=[jax.ShapeDtypeStruct((t, D_MODEL), F32), jax.ShapeDtypeStruct((t, D_MODEL), BF16)],
        scratch_shapes=[pltpu.VMEM((2, TOP_K, tm * ROW_SLAB, LANES), ys.dtype), pltpu.SemaphoreType.DMA((2,))],
        compiler_params=_cparams(1),
        name="moe_combine",
    )(dest_flat, dest_flat, x32, shared, wts, ys, g, b)


def _rot_half_cols(w):
    half = w.shape[-1] // 2
    return jnp.concatenate([-w[..., half:], w[..., :half]], axis=-1)


def _pad_cols(w, width):
    return jnp.pad(w, [(0, 0)] * (w.ndim - 1) + [(0, width - w.shape[-1])])


def _rope_tables(seq):
    pos = jnp.arange(seq, dtype=F32)[:, None]

    def table(dim):
        half = dim // 2
        inv_freq = ROPE_THETA ** (-jnp.arange(half, dtype=F32) / half)
        ang = pos * inv_freq[None, :]
        return jnp.cos(ang), jnp.sin(ang)

    c64, s64 = table(MLA_ROPE_DIM)
    cos64 = _pad_cols(jnp.concatenate([c64, c64], axis=1), LANES)
    sin64 = _pad_cols(jnp.concatenate([s64, s64], axis=1), LANES)
    c128, s128 = table(MOBA_HEAD_DIM)
    cos128 = jnp.concatenate([c128, c128], axis=1)
    sin128 = jnp.concatenate([-s128, s128], axis=1)
    return cos64, sin64, cos128, sin128


def _layer(x32, x16, mem16, tables, avg, bsz, seq, mem_len, p, expert_weights, layer):
    (w_in, g_qa, w_q_up, g_kva, w_kv_up, sgu_ln_g, sgu_ln_b, w_spatial, b_spatial, w_mem_kv,
     w_br_a, w_br_b, w_br_c, w_br_m, w_o, ln1_g, ln1_b, w_router, router_bias,
     w_sh_gate, w_sh_up, w_sh_down, ln2_g, ln2_b) = p
    cos64, sin64, cos128, sin128 = tables
    t = x32.shape[0]
    row = lambda v: v.reshape(1, -1).astype(F32)

    c0 = MLA_Q_RANK + MLA_KV_RANK
    c1 = c0 + MLA_ROPE_DIM
    c2 = c1 + 2 * SGU_WIDTH
    c3 = c2 + 3 * MOBA_WIDTH
    c4 = c3 + MEM_WIDTH
    w_a = w_in[:, :c0].astype(BF16)
    w_kpe_raw = w_in[:, c0:c1]
    w_kpe = jnp.concatenate([_pad_cols(w_kpe_raw, LANES), _pad_cols(_rot_half_cols(w_kpe_raw), LANES)],
                            axis=1).astype(BF16)
    w_z = w_in[:, c1:c2].astype(BF16)
    w_c = w_in[:, c2:c3].astype(BF16)
    w_mq = w_in[:, c3:c4].astype(BF16)
    w_g = w_in[:, c4:].astype(BF16)

    w_qn = w_q_up[:, :, :MLA_NOPE_DIM].reshape(MLA_Q_RANK, -1).astype(BF16)
    w_qpe = w_q_up[:, :, MLA_NOPE_DIM:]
    w_qp = _pad_cols(w_qpe, LANES).reshape(MLA_Q_RANK, -1).astype(BF16)
    w_qpr = _pad_cols(_rot_half_cols(w_qpe), LANES).reshape(MLA_Q_RANK, -1).astype(BF16)
    w_kn = w_kv_up[:, :, :MLA_NOPE_DIM].reshape(MLA_KV_RANK, -1).astype(BF16)
    w_v = w_kv_up[:, :, MLA_NOPE_DIM:].reshape(MLA_KV_RANK, -1).T.astype(BF16)
    qt_a, k_a, vt_a = _mla_proj(x16, w_a, w_kpe, row(g_qa), row(g_kva), w_qn.T, w_qp.T, w_qpr.T, w_kn, w_v,
                                cos64, sin64, seq)
    o_a = _mla_attention(qt_a, k_a, vt_a, bsz, seq)

    bias_full = jnp.repeat(b_spatial.T.astype(F32), SGU_GROUP_DIM, axis=1)
    o_b = _sgu(x16, w_z, row(sgu_ln_g), row(sgu_ln_b), w_spatial.astype(F32), bias_full)

    k_c = _moba_k_proj(x16, w_c[:, MOBA_WIDTH:2 * MOBA_WIDTH], cos128, sin128, seq)
    w_qv_t = jnp.concatenate([w_c[:, :MOBA_WIDTH], w_c[:, 2 * MOBA_WIDTH:]], axis=1).T
    qvt_c = _moba_qv_proj(x16, w_qv_t, cos128.T, sin128.T, seq)
    o_c = _moba_attention(qvt_c, k_c, avg, bsz, seq)

    kv_mem = _matmul(mem16, w_mem_kv.astype(BF16), mem_len, 2 * MEM_WIDTH, BF16, "mem_kv")
    o_m = _mem_attention(x16, w_mq, kv_mem, bsz, seq, mem_len)

    y = _gated_sum(x16, o_a, o_b, o_c, o_m, w_g, w_br_a.astype(BF16), w_br_b.astype(BF16),
                   w_br_c.astype(BF16), w_br_m.astype(BF16))
    x32, x16, x_packed = _proj_norm(y, w_o.astype(BF16), x32, row(ln1_g), row(ln1_b))

    w_r = _pad_cols(w_router.astype(F32), LANES)
    w_r_hi = w_r.astype(BF16)
    w_r_split = jnp.stack([w_r_hi, (w_r - w_r_hi.astype(F32)).astype(BF16)])
    e_idx, wts, rank, counts = _router(x32, w_r_split, _pad_cols(row(router_bias), LANES))
    bm = MOE_ROW_BLOCK
    n_blk = (t * TOP_K) // bm + N_EXPERTS
    counts = counts[0, :N_EXPERTS].astype(I32)
    padded = (counts + bm - 1) // bm * bm
    pad_ends = jnp.cumsum(padded)
    pad_starts = pad_ends - padded
    dest = _pad_cols((pad_starts[e_idx[:, :TOP_K]] + rank[:, :TOP_K]).astype(I32), SLOT_STRIDE).reshape(-1)
    blk_first = jnp.arange(n_blk, dtype=I32)[:, None] * bm
    blk_e = jnp.minimum(jnp.sum((pad_ends[None, :] <= blk_first).astype(I32), axis=1), N_EXPERTS - 1)
    n_used = (pad_ends[-1:] // bm).astype(I32)
    xs, shared = _dispatch(x_packed, x16, dest, pad_starts + counts, padded - counts, n_used, n_blk * bm,
                           w_sh_gate.astype(BF16), w_sh_up.astype(BF16), w_sh_down.astype(BF16))
    ys = _expert_ffn(xs, blk_e, n_used, *expert_weights, layer)
    return _combine(x32, shared, dest, wts, ys, row(ln2_g), row(ln2_b))


def kernel(x, mem, w_in, g_qa, w_q_up, g_kva, w_kv_up, sgu_ln_g, sgu_ln_b, w_spatial, b_spatial, w_mem_kv, w_br_a, w_br_b, w_br_c, w_br_m, w_o, ln1_g, ln1_b, w_router, router_bias, w_exp_gate, w_exp_up, w_exp_down, w_sh_gate, w_sh_up, w_sh_down, ln2_g, ln2_b):
    bsz, seq, d = x.shape
    mem_len = mem.shape[1]
    params = (w_in, g_qa, w_q_up, g_kva, w_kv_up, sgu_ln_g, sgu_ln_b, w_spatial, b_spatial, w_mem_kv,
              w_br_a, w_br_b, w_br_c, w_br_m, w_o, ln1_g, ln1_b, w_router, router_bias,
              w_sh_gate, w_sh_up, w_sh_down, ln2_g, ln2_b)
    expert_weights = (w_exp_gate, w_exp_up, w_exp_down)
    tables = _rope_tables(seq)
    n_blocks = seq // MOBA_BLOCK
    blk_of_pos = jnp.arange(seq, dtype=I32)[None, :] // MOBA_BLOCK
    avg = jnp.where(blk_of_pos == jnp.arange(LANES, dtype=I32)[:, None], 1.0 / MOBA_BLOCK, 0.0).astype(BF16)
    assert n_blocks <= LANES
    x32 = x.reshape(bsz * seq, d)
    x16 = x32.astype(BF16)
    mem16 = mem.reshape(bsz * mem_len, d).astype(BF16)
    for l in range(DEPTH):
        x32, x16 = _layer(x32, x16, mem16, tables, avg, bsz, seq, mem_len, tuple(w[l] for w in params),
                          expert_weights, l)
    return x32.reshape(bsz, seq, d)
```

```python
import functools

import jax
import jax.numpy as jnp
from jax import lax
from jax.experimental import pallas as pl
from jax.experimental.pallas import tpu as pltpu

F32 = jnp.float32
BF16 = jnp.bfloat16
I32 = jnp.int32
U32 = jnp.uint32

D_MODEL = 2048
DEPTH = 2
MLA_HEADS = 8
MLA_Q_RANK = 512
MLA_KV_RANK = 256
MLA_NOPE_DIM = 128
MLA_ROPE_DIM = 64
MLA_V_DIM = 128
SGU_GROUPS = 8
SGU_GROUP_DIM = 128
SGU_WIDTH = SGU_GROUPS * SGU_GROUP_DIM
SGU_CHUNK = 128
MOBA_HEADS = 8
MOBA_HEAD_DIM = 128
MOBA_WIDTH = MOBA_HEADS * MOBA_HEAD_DIM
MOBA_BLOCK = 256
MOBA_TOPK = 3
MEM_HEADS = 4
MEM_HEAD_DIM = 128
MEM_WIDTH = MEM_HEADS * MEM_HEAD_DIM
N_BRANCH = 4
N_EXPERTS = 64
TOP_K = 6
D_EXPERT = 512
ROUTED_SCALE = 2.5
ROPE_THETA = 10000.0
DEEPNORM_ALPHA = (2 * DEPTH) ** 0.25

LANES = 128
SUBLANES = 8
ATTN_BLOCK = 256
MOE_ROW_BLOCK = 512
SLOT_STRIDE = 8
ROW_WORDS = D_MODEL // 2
ROW_SLAB = ROW_WORDS // LANES
NEG_BIG = -1e30
VMEM_LIMIT = 56 * 1024 * 1024


def _cparams(n_axes):
    return pltpu.CompilerParams(dimension_semantics=("arbitrary",) * n_axes,
                                vmem_limit_bytes=VMEM_LIMIT)


def _dot(a, b):
    return jnp.dot(a, b, preferred_element_type=F32)


def _dot_nt(a, b):
    return lax.dot_general(a, b, (((1,), (1,)), ((), ())), preferred_element_type=F32)


def _layer_norm_rows(v, g, b, eps=1e-5):
    mu = jnp.mean(v, axis=-1, keepdims=True)
    vc = v - mu
    var = jnp.mean(vc * vc, axis=-1, keepdims=True)
    return vc * lax.rsqrt(var + eps) * g + b


def _rms_norm_rows(v, g, eps=1e-6):
    return v * lax.rsqrt(jnp.mean(v * v, axis=-1, keepdims=True) + eps) * g


def _mm_kernel(x_ref, w_ref, o_ref):
    o_ref[...] = _dot(x_ref[...], w_ref[...]).astype(o_ref.dtype)


def _matmul(x, w, tm, tn, out_dtype, name):
    m, k = x.shape
    n = w.shape[1]
    return pl.pallas_call(
        _mm_kernel,
        grid=(n // tn, m // tm),
        in_specs=[pl.BlockSpec((tm, k), lambda j, i: (i, 0)),
                  pl.BlockSpec((k, tn), lambda j, i: (0, j))],
        out_specs=pl.BlockSpec((tm, tn), lambda j, i: (i, j)),
        out_shape=jax.ShapeDtypeStruct((m, n), out_dtype),
        compiler_params=_cparams(2),
        name=name,
    )(x, w)


def _mla_proj_kernel(x_ref, wa_ref, wkpe_ref, gq_ref, gkv_ref, wqnt_ref, wqpt_ref, wqprt_ref,
                     wkn_ref, wvt_ref, cos_ref, sin_ref, cost_ref, sint_ref, qt_ref, k_ref, vt_ref, *, scale):
    xb = x_ref[...]
    a = _dot(xb, wa_ref[...])
    cq = _rms_norm_rows(a[:, :MLA_Q_RANK], gq_ref[...]).astype(BF16)
    ckv = _rms_norm_rows(a[:, MLA_Q_RANK:], gkv_ref[...]).astype(BF16)
    qn_t = _dot_nt(wqnt_ref[...], cq)
    qp_t = _dot_nt(wqpt_ref[...], cq)
    qpr_t = _dot_nt(wqprt_ref[...], cq)
    v_t = _dot_nt(wvt_ref[...], ckv)
    cos_t = cost_ref[...]
    sin_t = sint_ref[...]
    for h in range(MLA_HEADS):
        lo, hi = h * LANES, (h + 1) * LANES
        qt_ref[0, h, 0, 0:LANES, :] = (qn_t[lo:hi] * scale).astype(qt_ref.dtype)
        qt_ref[0, h, 0, LANES:2 * LANES, :] = ((qp_t[lo:hi] * cos_t + qpr_t[lo:hi] * sin_t) * scale).astype(qt_ref.dtype)
        vt_ref[0, h, 0] = v_t[h * MLA_V_DIM:(h + 1) * MLA_V_DIM, :].astype(vt_ref.dtype)
    kn = _dot(ckv, wkn_ref[...])
    kp = _dot(xb, wkpe_ref[...])
    kpe = (kp[:, :LANES] * cos_ref[...] + kp[:, LANES:] * sin_ref[...]).astype(k_ref.dtype)
    for h in range(MLA_HEADS):
        lo, hi = h * LANES, (h + 1) * LANES
        k_ref[:, 2 * lo:2 * lo + LANES] = kn[:, lo:hi].astype(k_ref.dtype)
        k_ref[:, 2 * lo + LANES:2 * hi] = kpe


def _mla_proj(x, wa, wkpe, gq, gkv, wqnt, wqpt, wqprt, wkn, wvt, cos64, sin64, seq):
    t = x.shape[0]
    tm = ATTN_BLOCK
    full = lambda arr: pl.BlockSpec(arr.shape, lambda i: (0,) * arr.ndim)
    n_pos = seq // tm
    scale = (MLA_NOPE_DIM + MLA_ROPE_DIM) ** -0.5
    dk = 2 * LANES
    feat_major = lambda d: pl.BlockSpec((1, MLA_HEADS, 1, d, tm), lambda i: (i // n_pos, 0, i % n_pos, 0, 0))
    return pl.pallas_call(
        functools.partial(_mla_proj_kernel, scale=scale),
        grid=(t // tm,),
        in_specs=[pl.BlockSpec((tm, D_MODEL), lambda i: (i, 0)),
                  full(wa), full(wkpe), full(gq), full(gkv), full(wqnt), full(wqpt), full(wqprt),
                  full(wkn), full(wvt),
                  pl.BlockSpec((tm, LANES), lambda i: (i % n_pos, 0)),
                  pl.BlockSpec((tm, LANES), lambda i: (i % n_pos, 0)),
                  pl.BlockSpec((LANES, tm), lambda i: (0, i % n_pos)),
                  pl.BlockSpec((LANES, tm), lambda i: (0, i % n_pos))],
        out_specs=[feat_major(dk),
                   pl.BlockSpec((tm, MLA_HEADS * dk), lambda i: (i, 0)),
                   feat_major(MLA_V_DIM)],
        out_shape=[jax.ShapeDtypeStruct((t // seq, MLA_HEADS, n_pos, dk, tm), BF16),
                   jax.ShapeDtypeStruct((t, MLA_HEADS * dk), BF16),
                   jax.ShapeDtypeStruct((t // seq, MLA_HEADS, n_pos, MLA_V_DIM, tm), BF16)],
        compiler_params=_cparams(1),
        name="mla_proj",
    )(x, wa, wkpe, gq, gkv, wqnt, wqpt, wqprt, wkn, wvt, cos64, sin64, cos64.T, sin64.T)


def _softmax_steps(q_ts, ks, v_ts, carries, masks):
    heads = range(len(q_ts))
    s_ts = [_dot(ks[h], q_ts[h]) for h in heads]
    s_ts = [s_ts[h] if masks[h] is None else jnp.where(masks[h], s_ts[h], NEG_BIG) for h in heads]
    m_news = [jnp.maximum(carries[h][0], jnp.max(s_ts[h], axis=0, keepdims=True)) for h in heads]
    p_ts = [jnp.exp(s_ts[h] - m_news[h]) for h in heads]
    pvs = [_dot(v_ts[h], p_ts[h].astype(v_ts[h].dtype)) for h in heads]
    out = []
    for h in heads:
        m_i, l_i, acc_t = carries[h]
        alpha = jnp.exp(m_i - m_news[h])
        l_new = alpha * l_i + jnp.sum(p_ts[h], axis=0, keepdims=True)
        out.append((m_news[h], l_new, alpha * acc_t + pvs[h]))
    return tuple(out)


def _softmax_init(n_heads, tq, dv):
    return tuple((jnp.full((1, tq), NEG_BIG, F32), jnp.zeros((1, tq), F32), jnp.zeros((dv, tq), F32))
                 for _ in range(n_heads))


def _softmax_finish(o_ref, carries, dv):
    for h, (_, l_i, acc_t) in enumerate(carries):
        o_ref[:, h * dv:(h + 1) * dv] = (acc_t / l_i).T.astype(o_ref.dtype)


def _causal_attn_kernel(qt_ref, k_ref, vt_ref, o_ref, *, tq, n_heads, dk, dv):
    qi = pl.program_id(2)
    key = lax.broadcasted_iota(I32, (tq, tq), 0)
    qry = lax.broadcasted_iota(I32, (tq, tq), 1)
    qs = [qt_ref[0, h, 0] for h in range(n_heads)]

    def step(kb, carries, mask):
        ks = pl.multiple_of(kb * tq, tq)
        return _softmax_steps(qs, [k_ref[pl.ds(ks, tq), h * dk:(h + 1) * dk] for h in range(n_heads)],
                              [vt_ref[0, h, kb] for h in range(n_heads)], carries, [mask] * n_heads)

    carries = step(qi, _softmax_init(n_heads, tq, dv), key <= qry)
    carries = lax.fori_loop(0, qi, lambda kb, c: step(kb, c, None), carries)
    _softmax_finish(o_ref, carries, dv)


def _mla_attention(q_t, k_a, v_t, bsz, seq, n_heads=8):
    t = k_a.shape[0]
    tq = ATTN_BLOCK
    nq = seq // tq
    dk = 2 * LANES
    dv = MLA_V_DIM
    return pl.pallas_call(
        functools.partial(_causal_attn_kernel, tq=tq, n_heads=n_heads, dk=dk, dv=dv),
        grid=(bsz, MLA_HEADS // n_heads, nq),
        in_specs=[pl.BlockSpec((1, n_heads, 1, dk, tq), lambda b, h, i: (b, h, i, 0, 0)),
                  pl.BlockSpec((seq, n_heads * dk), lambda b, h, i: (b, h)),
                  pl.BlockSpec((1, n_heads, nq, dv, tq), lambda b, h, i: (b, h, 0, 0, 0))],
        out_specs=pl.BlockSpec((tq, n_heads * dv), lambda b, h, i: (b * nq + i, h)),
        out_shape=jax.ShapeDtypeStruct((t, MLA_HEADS * dv), BF16),
        compiler_params=_cparams(3),
        name="mla_attn",
    )(q_t, k_a, v_t)


def _sgu_kernel(x_ref, wz_ref, g_ref, b_ref, ws_ref, bs_ref, o_ref, *, tm):
    z = jax.nn.gelu(_dot(x_ref[...], wz_ref[...]))
    u = z[:, :SGU_WIDTH]
    v = _layer_norm_rows(z[:, SGU_WIDTH:], g_ref[...], b_ref[...]).astype(BF16)
    row = lax.broadcasted_iota(I32, (SGU_CHUNK, SGU_CHUNK), 0)
    col = lax.broadcasted_iota(I32, (SGU_CHUNK, SGU_CHUNK), 1)
    bias = bs_ref[...]
    for g in range(SGU_GROUPS):
        w = jnp.where(col <= row, ws_ref[g], 0.0).astype(BF16)
        lo, hi = g * SGU_GROUP_DIM, (g + 1) * SGU_GROUP_DIM
        for c in range(tm // SGU_CHUNK):
            r0, r1 = c * SGU_CHUNK, (c + 1) * SGU_CHUNK
            mixed = _dot(w, v[r0:r1, lo:hi]) + bias[:, lo:hi]
            o_ref[r0:r1, lo:hi] = (u[r0:r1, lo:hi] * mixed).astype(o_ref.dtype)


def _sgu(x, wz, ln_g, ln_b, w_s, bias_full, tm=256):
    t = x.shape[0]
    full = lambda arr: pl.BlockSpec(arr.shape, lambda i: (0,) * arr.ndim)
    return pl.pallas_call(
        functools.partial(_sgu_kernel, tm=tm),
        grid=(t // tm,),
        in_specs=[pl.BlockSpec((tm, D_MODEL), lambda i: (i, 0)),
                  full(wz), full(ln_g), full(ln_b), full(w_s), full(bias_full)],
        out_specs=pl.BlockSpec((tm, SGU_WIDTH), lambda i: (i, 0)),
        out_shape=jax.ShapeDtypeStruct((t, SGU_WIDTH), BF16),
        compiler_params=_cparams(1),
        name="sgu",
    )(x, wz, ln_g, ln_b, w_s, bias_full)


def _moba_k_proj_kernel(x_ref, w_ref, cos_ref, sin_ref, o_ref):
    acc = _dot(x_ref[...], w_ref[...])
    cos = cos_ref[...]
    sin = sin_ref[...]
    for h in range(MOBA_HEADS):
        seg = acc[:, h * LANES:(h + 1) * LANES]
        rot = pltpu.roll(seg, MOBA_HEAD_DIM // 2, axis=1)
        o_ref[:, h * LANES:(h + 1) * LANES] = (seg * cos + rot * sin).astype(o_ref.dtype)


def _moba_k_proj(x, w_k, cos128, sin128, seq, tm=512):
    t = x.shape[0]
    n_pos = seq // tm
    return pl.pallas_call(
        _moba_k_proj_kernel,
        grid=(t // tm,),
        in_specs=[pl.BlockSpec((tm, D_MODEL), lambda i: (i, 0)),
                  pl.BlockSpec(w_k.shape, lambda i: (0, 0)),
                  pl.BlockSpec((tm, LANES), lambda i: (i % n_pos, 0)),
                  pl.BlockSpec((tm, LANES), lambda i: (i % n_pos, 0))],
        out_specs=pl.BlockSpec((tm, MOBA_WIDTH), lambda i: (i, 0)),
        out_shape=jax.ShapeDtypeStruct((t, MOBA_WIDTH), BF16),
        compiler_params=_cparams(1),
        name="moba_k_proj",
    )(x, w_k, cos128, sin128)


def _moba_qv_proj_kernel(x_ref, wt_ref, cost_ref, sint_ref, o_ref, *, scale):
    y_t = _dot_nt(wt_ref[...], x_ref[...])
    hd = MOBA_HEAD_DIM

    @pl.when(pl.program_id(0) == 0)
    def _():
        cos_t = cost_ref[...]
        sin_t = sint_ref[...]
        for h in range(MOBA_HEADS):
            seg = y_t[h * hd:(h + 1) * hd]
            rot = jnp.concatenate([seg[hd // 2:], seg[:hd // 2]], axis=0)
            o_ref[0, h, 0] = ((seg * cos_t + rot * sin_t) * scale).astype(o_ref.dtype)

    @pl.when(pl.program_id(0) == 1)
    def _():
        for h in range(MOBA_HEADS):
            o_ref[0, h, 0] = y_t[h * hd:(h + 1) * hd].astype(o_ref.dtype)


def _moba_qv_proj(x, w_qv_t, cos128_t, sin128_t, seq):
    t = x.shape[0]
    tm = ATTN_BLOCK
    n_pos = seq // tm
    return pl.pallas_call(
        functools.partial(_moba_qv_proj_kernel, scale=MOBA_HEAD_DIM ** -0.5),
        grid=(2, t // tm),
        in_specs=[pl.BlockSpec((tm, D_MODEL), lambda j, i: (i, 0)),
                  pl.BlockSpec((MOBA_WIDTH, D_MODEL), lambda j, i: (j, 0)),
                  pl.BlockSpec((MOBA_HEAD_DIM, tm), lambda j, i: (0, i % n_pos)),
                  pl.BlockSpec((MOBA_HEAD_DIM, tm), lambda j, i: (0, i % n_pos))],
        out_specs=pl.BlockSpec((1, MOBA_HEADS, 1, MOBA_HEAD_DIM, tm),
                               lambda j, i: (i // n_pos, j, i % n_pos, 0, 0)),
        out_shape=jax.ShapeDtypeStruct((t // seq, 2 * MOBA_HEADS, n_pos, MOBA_HEAD_DIM, tm), BF16),
        compiler_params=_cparams(2),
        name="moba_qv_proj",
    )(x, w_qv_t, cos128_t, sin128_t)


def _moba_attn_kernel(qt_ref, k_ref, vt_ref, avg_ref, o_ref, kmean_ref, *, n_heads):
    j = pl.program_id(2)
    blk = MOBA_BLOCK
    hd = MOBA_HEAD_DIM

    @pl.when(j == 0)
    def _():
        kmean_ref[...] = _dot(avg_ref[...], k_ref[...])

    blk_id = lax.broadcasted_iota(I32, (SUBLANES, blk), 0)
    n_blocks = k_ref.shape[0] // blk
    qs, sels = [], []
    for h in range(n_heads):
        q = qt_ref[0, h, 0]
        gate = jnp.dot(kmean_ref[0:SUBLANES, h * hd:(h + 1) * hd], q.astype(F32),
                       precision=lax.Precision.HIGHEST, preferred_element_type=F32)
        rank = jnp.zeros((SUBLANES, blk), I32)
        for m in range(n_blocks):
            gm = gate[m:m + 1, :]
            beats = (gm > gate) | ((gm == gate) & (m < blk_id))
            rank = rank + jnp.where(beats & (m < j), 1, 0)
        qs.append(q)
        sels.append(jnp.where((rank < MOBA_TOPK) & (blk_id < j), 1.0, 0.0))

    def step(kb, carries, masks):
        ks = pl.multiple_of(kb * blk, blk)
        return _softmax_steps(qs, [k_ref[pl.ds(ks, blk), h * hd:(h + 1) * hd] for h in range(n_heads)],
                              [vt_ref[0, h, kb] for h in range(n_heads)], carries, masks)

    key = lax.broadcasted_iota(I32, (blk, blk), 0)
    qry = lax.broadcasted_iota(I32, (blk, blk), 1)
    carries = step(j, _softmax_init(n_heads, blk, hd), [key <= qry] * n_heads)

    def body(n, c):
        masks = [jnp.max(jnp.where(blk_id == n, sels[h], 0.0), axis=0, keepdims=True) > 0.5
                 for h in range(n_heads)]
        return step(n, c, masks)

    carries = lax.fori_loop(0, j, body, carries)
    _softmax_finish(o_ref, carries, hd)


def _moba_attention(qv_t, k, avg, bsz, seq, n_heads=8):
    t = k.shape[0]
    nq = seq // MOBA_BLOCK
    assert nq <= SUBLANES and MOBA_BLOCK == ATTN_BLOCK
    w = n_heads * MOBA_HEAD_DIM
    groups = MOBA_HEADS // n_heads
    return pl.pallas_call(
        functools.partial(_moba_attn_kernel, n_heads=n_heads),
        grid=(bsz, groups, nq),
        in_specs=[pl.BlockSpec((1, n_heads, 1, MOBA_HEAD_DIM, MOBA_BLOCK), lambda b, h, i: (b, h, i, 0, 0)),
                  pl.BlockSpec((seq, w), lambda b, h, i: (b, h)),
                  pl.BlockSpec((1, n_heads, nq, MOBA_HEAD_DIM, MOBA_BLOCK),
                               lambda b, h, i: (b, groups + h, 0, 0, 0)),
                  pl.BlockSpec(avg.shape, lambda b, h, i: (0, 0))],
        out_specs=pl.BlockSpec((MOBA_BLOCK, w), lambda b, h, i: (b * nq + i, h)),
        out_shape=jax.ShapeDtypeStruct((t, MOBA_WIDTH), BF16),
        scratch_shapes=[pltpu.VMEM((LANES, w), F32)],
        compiler_params=_cparams(3),
        name="moba_attn",
    )(qv_t, k, qv_t, avg)


def _mem_attn_kernel(x_ref, wq_ref, kv_ref, o_ref, *, scale):
    q = (_dot(x_ref[...], wq_ref[...]) * scale).astype(BF16)
    for h in range(MEM_HEADS):
        lo, hi = h * MEM_HEAD_DIM, (h + 1) * MEM_HEAD_DIM
        s = _dot_nt(q[:, lo:hi], kv_ref[:, lo:hi])
        p = jnp.exp(s - jnp.max(s, axis=1, keepdims=True))
        o = _dot(p.astype(BF16), kv_ref[:, MEM_WIDTH + lo:MEM_WIDTH + hi])
        o_ref[:, lo:hi] = (o / jnp.sum(p, axis=1, keepdims=True)).astype(o_ref.dtype)


def _mem_attention(x, w_mq, kv_mem, bsz, seq, mem_len, tm=512):
    t = x.shape[0]
    ns = seq // tm
    return pl.pallas_call(
        functools.partial(_mem_attn_kernel, scale=MEM_HEAD_DIM ** -0.5),
        grid=(bsz, ns),
        in_specs=[pl.BlockSpec((tm, D_MODEL), lambda b, i: (b * ns + i, 0)),
                  pl.BlockSpec(w_mq.shape, lambda b, i: (0, 0)),
                  pl.BlockSpec((mem_len, 2 * MEM_WIDTH), lambda b, i: (b, 0))],
        out_specs=pl.BlockSpec((tm, MEM_WIDTH), lambda b, i: (b * ns + i, 0)),
        out_shape=jax.ShapeDtypeStruct((t, MEM_WIDTH), BF16),
        compiler_params=_cparams(2),
        name="mem_attn",
    )(x, w_mq, kv_mem)


def _gated_sum_kernel(x_ref, oa_ref, ob_ref, oc_ref, om_ref, g0_ref, g1_ref, g2_ref, g3_ref,
                      wa_ref, wb_ref, wc_ref, wm_ref, y_ref):
    xb = x_ref[...]
    acc = None
    for o_ref, g_ref, w_ref in ((oa_ref, g0_ref, wa_ref), (ob_ref, g1_ref, wb_ref),
                                (oc_ref, g2_ref, wc_ref), (om_ref, g3_ref, wm_ref)):
        term = jax.nn.sigmoid(_dot(xb, g_ref[...])) * _dot(o_ref[...], w_ref[...])
        acc = term if acc is None else acc + term
    y_ref[...] = acc.astype(y_ref.dtype)


def _gated_sum(x, o_a, o_b, o_c, o_m, w_gate, w_br_a, w_br_b, w_br_c, w_br_m, tm=1024, tn=256):
    t = x.shape[0]
    nj = D_MODEL // tn
    rows = lambda width: pl.BlockSpec((tm, width), lambda j, i: (i, 0))
    gate = lambda b: pl.BlockSpec((D_MODEL, tn), lambda j, i: (0, b * nj + j))
    cols = lambda width: pl.BlockSpec((width, tn), lambda j, i: (0, j))
    return pl.pallas_call(
        _gated_sum_kernel,
        grid=(nj, t // tm),
        in_specs=[rows(D_MODEL), rows(o_a.shape[1]), rows(o_b.shape[1]), rows(o_c.shape[1]), rows(o_m.shape[1]),
                  gate(0), gate(1), gate(2), gate(3),
                  cols(w_br_a.shape[0]), cols(w_br_b.shape[0]), cols(w_br_c.shape[0]), cols(w_br_m.shape[0])],
        out_specs=pl.BlockSpec((tm, tn), lambda j, i: (i, j)),
        out_shape=jax.ShapeDtypeStruct((t, D_MODEL), BF16),
        compiler_params=_cparams(2),
        name="gated_sum",
    )(x, o_a, o_b, o_c, o_m, w_gate, w_gate, w_gate, w_gate, w_br_a, w_br_b, w_br_c, w_br_m)


def _pack_bf16_pairs(v):
    bits = lax.bitcast_convert_type(v.astype(BF16).astype(F32), U32)
    half = v.shape[1] // 2
    return (bits[:, :half] >> 16) | bits[:, half:]


def _unpack_bf16_pairs(words):
    lo = lax.bitcast_convert_type(words << 16, F32)
    hi = lax.bitcast_convert_type(words & jnp.uint32(0xFFFF0000), F32)
    return lo, hi


def _store_row_slabs(ref, words):
    for c in range(ROW_SLAB):
        ref[pl.ds(c, words.shape[0], stride=ROW_SLAB), :] = words[:, c * LANES:(c + 1) * LANES]


def _load_row_slabs(ref):
    n_rows = ref.shape[0] // ROW_SLAB
    return jnp.concatenate([ref[pl.ds(c, n_rows, stride=ROW_SLAB), :] for c in range(ROW_SLAB)], axis=1)


def _proj_norm_kernel(y_ref, w_ref, res_ref, g_ref, b_ref, o32_ref, o16_ref, opk_ref):
    v = DEEPNORM_ALPHA * res_ref[...] + _dot(y_ref[...], w_ref[...])
    out = _layer_norm_rows(v, g_ref[...], b_ref[...])
    o32_ref[...] = out
    o16_ref[...] = out.astype(o16_ref.dtype)
    _store_row_slabs(opk_ref, _pack_bf16_pairs(out))


def _proj_norm(y, w_o, res, g, b, tm=256):
    t = y.shape[0]
    full = lambda arr: pl.BlockSpec(arr.shape, lambda i: (0,) * arr.ndim)
    rows = pl.BlockSpec((tm, D_MODEL), lambda i: (i, 0))
    return pl.pallas_call(
        _proj_norm_kernel,
        grid=(t // tm,),
        in_specs=[rows, full(w_o), rows, full(g), full(b)],
        out_specs=[rows, rows, pl.BlockSpec((tm * ROW_SLAB, LANES), lambda i: (i, 0))],
        out_shape=[jax.ShapeDtypeStruct((t, D_MODEL), F32), jax.ShapeDtypeStruct((t, D_MODEL), BF16),
                   jax.ShapeDtypeStruct((t * ROW_SLAB, LANES), U32)],
        compiler_params=_cparams(1),
        name="proj_norm",
    )(y, w_o, res, g, b)


def _router_kernel(x_ref, w_ref, bias_ref, e_ref, wt_ref, rank_ref, cnt_ref, run_ref, *, tm):
    i = pl.program_id(0)

    @pl.when(i == 0)
    def _():
        run_ref[...] = jnp.zeros_like(run_ref)

    x = x_ref[...]
    x_hi = x.astype(BF16)
    x_lo = (x - x_hi.astype(F32)).astype(BF16)
    logits = _dot(x_hi, w_ref[0]) + (_dot(x_hi, w_ref[1]) + _dot(x_lo, w_ref[0]))
    scores = jax.nn.sigmoid(logits)
    lane = lax.broadcasted_iota(I32, (tm, LANES), 1)
    biased = jnp.where(lane < N_EXPERTS, scores + bias_ref[...], NEG_BIG)
    picks = []
    chosen = jnp.zeros((tm, LANES), jnp.bool_)
    for _ in range(TOP_K):
        mx = jnp.max(biased, axis=1, keepdims=True)
        idx = jnp.min(jnp.where(biased == mx, lane, LANES), axis=1, keepdims=True)
        hit = lane == idx
        picks.append((idx, hit))
        chosen = chosen | hit
        biased = jnp.where(hit, 2 * NEG_BIG, biased)
    picked_scores = jnp.where(chosen, scores, 0.0)
    norm = ROUTED_SCALE / jnp.sum(picked_scores, axis=1, keepdims=True)

    r = lax.broadcasted_iota(I32, (tm, tm), 0)
    c = lax.broadcasted_iota(I32, (tm, tm), 1)
    strict_lower = jnp.where(c < r, 1.0, 0.0).astype(BF16)
    chosen_f = jnp.where(chosen, 1.0, 0.0)
    arrival = run_ref[0:1, :] + _dot(strict_lower, chosen_f.astype(BF16))
    run_ref[...] = run_ref[...] + jnp.sum(chosen_f, axis=0, keepdims=True)
    cnt_ref[...] = run_ref[...]

    e_out = jnp.zeros((tm, LANES), I32)
    w_out = jnp.zeros((tm, LANES), F32)
    r_out = jnp.zeros((tm, LANES), F32)
    for slot, (idx, hit) in enumerate(picks):
        here = lane == slot
        e_out = jnp.where(here, idx, e_out)
        w_out = jnp.where(here, jnp.sum(jnp.where(hit, scores, 0.0), axis=1, keepdims=True) * norm, w_out)
        r_out = jnp.where(here, jnp.sum(jnp.where(hit, arrival, 0.0), axis=1, keepdims=True), r_out)
    e_ref[...] = e_out
    wt_ref[...] = w_out
    rank_ref[...] = r_out.astype(I32)


def _router(x32, w_router_pad, bias_pad, tm=256):
    t = x32.shape[0]
    full = lambda arr: pl.BlockSpec(arr.shape, lambda i: (0,) * arr.ndim)
    rows = pl.BlockSpec((tm, LANES), lambda i: (i, 0))
    return pl.pallas_call(
        functools.partial(_router_kernel, tm=tm),
        grid=(t // tm,),
        in_specs=[pl.BlockSpec((tm, D_MODEL), lambda i: (i, 0)), full(w_router_pad), full(bias_pad)],
        out_specs=[rows, rows, rows, pl.BlockSpec((8, LANES), lambda i: (0, 0))],
        out_shape=[jax.ShapeDtypeStruct((t, LANES), I32), jax.ShapeDtypeStruct((t, LANES), F32),
                   jax.ShapeDtypeStruct((t, LANES), I32), jax.ShapeDtypeStruct((8, LANES), F32)],
        scratch_shapes=[pltpu.VMEM((8, LANES), F32)],
        compiler_params=_cparams(1),
        name="moe_router",
    )(x32, w_router_pad, bias_pad)


def _row_tiles(ref, first_row, n_rows):
    return ref.at[pl.ds(pl.multiple_of(first_row * ROW_SLAB, ROW_SLAB), n_rows * ROW_SLAB), :]


def _dispatch_kernel(fill_start_ref, fill_n_ref, n_used_ref, pad_start_ref, expert_ref, rank_ref, x_ref, x16_ref,
                     wsg_ref, wsu_ref, wsd_ref, out_hbm, shared_ref, dest_ref, zero_ref, sem, fill_sem, *, tm, bm):
    @pl.when(pl.program_id(0) == 0)
    def _():
        zero_ref[...] = jnp.zeros_like(zero_ref)
        n_blk = out_hbm.shape[0] // (bm * ROW_SLAB)

        def fill_expert(e, wait):
            n = fill_n_ref[e]
            row = fill_start_ref[e]
            for bit in range(bm.bit_length() - 1):
                size = 1 << bit

                @pl.when((n & size) != 0)
                def _():
                    copy = pltpu.make_async_copy(_row_tiles(zero_ref, 0, size), _row_tiles(out_hbm, row, size), fill_sem)
                    copy.wait() if wait else copy.start()

                row = row + (n & size)

        def tail_copy(blk):
            return pltpu.make_async_copy(zero_ref, _row_tiles(out_hbm, blk * bm, bm), fill_sem)

        lax.fori_loop(0, N_EXPERTS, lambda e, c: (fill_expert(e, False), c)[1], 0)
        lax.fori_loop(n_used_ref[0], n_blk, lambda b, c: (tail_copy(b).start(), c)[1], 0)
        lax.fori_loop(0, N_EXPERTS, lambda e, c: (fill_expert(e, True), c)[1], 0)
        lax.fori_loop(n_used_ref[0], n_blk, lambda b, c: (tail_copy(0).wait(), c)[1], 0)

    def issue(tok, _):
        for k in range(SLOT_STRIDE):
            slot = tok * SLOT_STRIDE + k
            if k < TOP_K:
                dest = pad_start_ref[expert_ref[slot]] + rank_ref[slot]
                dest_ref[slot] = dest
                pltpu.make_async_copy(_row_tiles(x_ref, tok, 1), _row_tiles(out_hbm, dest, 1), sem).start()
            else:
                dest_ref[slot] = 0
        return 0

    lax.fori_loop(0, tm, issue, 0)
    xb = x16_ref[...]
    hid = jax.nn.silu(_dot(xb, wsg_ref[...])) * _dot(xb, wsu_ref[...])
    shared_ref[...] = _dot(hid.astype(BF16), wsd_ref[...]).astype(shared_ref.dtype)
    for _ in range(TOP_K):
        pltpu.make_async_copy(x_ref, _row_tiles(out_hbm, 0, tm), sem).wait()


def _dispatch(x_rows, x16, expert_flat, rank_flat, pad_starts, fill_start, fill_n, n_used, n_rows,
              w_sg, w_su, w_sd, tm=256):
    t = x16.shape[0]
    bm = MOE_ROW_BLOCK
    full = lambda arr: pl.BlockSpec(arr.shape, lambda i, *_: (0,) * arr.ndim)
    rows = pl.BlockSpec((tm, D_MODEL), lambda i, *_: (i, 0))
    slots = pl.BlockSpec((tm * SLOT_STRIDE,), lambda i, *_: (i,), memory_space=pltpu.SMEM)
    grid_spec = pltpu.PrefetchScalarGridSpec(
        num_scalar_prefetch=4,
        grid=(t // tm,),
        in_specs=[slots, slots,
                  pl.BlockSpec((tm * ROW_SLAB, LANES), lambda i, *_: (i, 0)),
                  rows, full(w_sg), full(w_su), full(w_sd)],
        out_specs=[pl.BlockSpec(memory_space=pl.ANY), rows, slots],
        scratch_shapes=[pltpu.VMEM((bm * ROW_SLAB, LANES), x_rows.dtype), pltpu.SemaphoreType.DMA(()),
                        pltpu.SemaphoreType.DMA(())],
    )
    return pl.pallas_call(
        functools.partial(_dispatch_kernel, tm=tm, bm=bm),
        grid_spec=grid_spec,
        out_shape=[jax.ShapeDtypeStruct((n_rows * ROW_SLAB, LANES), x_rows.dtype),
                   jax.ShapeDtypeStruct((t, D_MODEL), BF16),
                   jax.ShapeDtypeStruct((t * SLOT_STRIDE,), I32)],
        compiler_params=_cparams(1),
        name="moe_dispatch",
    )(fill_start, fill_n, n_used, pad_starts, expert_flat, rank_flat, x_rows, x16, w_sg, w_su, w_sd)


def _expert_kernel(blk_e_ref, n_used_ref, xs_ref, wg_ref, wu_ref, wd_ref, ys_ref, wg16_ref, wu16_ref, wd16_ref):
    i = pl.program_id(0)
    used = i < n_used_ref[0]
    new_expert = (i == 0) | (blk_e_ref[i] != blk_e_ref[jnp.maximum(i - 1, 0)])

    @pl.when(used & new_expert)
    def _():
        wg16_ref[...] = wg_ref[0, 0].astype(BF16)
        wu16_ref[...] = wu_ref[0, 0].astype(BF16)
        wd16_ref[...] = wd_ref[0, 0].astype(BF16)

    @pl.when(used)
    def _():
        lo, hi = _unpack_bf16_pairs(_load_row_slabs(xs_ref))
        xb = jnp.concatenate([lo.astype(BF16), hi.astype(BF16)], axis=1)
        hid = jax.nn.silu(_dot(xb, wg16_ref[...])) * _dot(xb, wu16_ref[...])
        _store_row_slabs(ys_ref, _pack_bf16_pairs(_dot(hid.astype(BF16), wd16_ref[...])))

    @pl.when(jnp.logical_not(used))
    def _():
        ys_ref[...] = jnp.zeros_like(ys_ref)


def _expert_ffn(xs, blk_e, n_used, w_gate, w_up, w_down, layer):
    n_rows = xs.shape[0] // ROW_SLAB
    bm = MOE_ROW_BLOCK
    grid_spec = pltpu.PrefetchScalarGridSpec(
        num_scalar_prefetch=2,
        grid=(n_rows // bm,),
        in_specs=[pl.BlockSpec((bm * ROW_SLAB, LANES), lambda i, be, nu: (jnp.minimum(i, nu[0] - 1), 0)),
                  pl.BlockSpec((1, 1, D_MODEL, D_EXPERT), lambda i, be, nu: (layer, be[i], 0, 0)),
                  pl.BlockSpec((1, 1, D_MODEL, D_EXPERT), lambda i, be, nu: (layer, be[i], 0, 0)),
                  pl.BlockSpec((1, 1, D_EXPERT, D_MODEL), lambda i, be, nu: (layer, be[i], 0, 0))],
        out_specs=pl.BlockSpec((bm * ROW_SLAB, LANES), lambda i, be, nu: (i, 0)),
        scratch_shapes=[pltpu.VMEM((D_MODEL, D_EXPERT), BF16), pltpu.VMEM((D_MODEL, D_EXPERT), BF16),
                        pltpu.VMEM((D_EXPERT, D_MODEL), BF16)],
    )
    return pl.pallas_call(
        _expert_kernel,
        grid_spec=grid_spec,
        out_shape=jax.ShapeDtypeStruct((n_rows * ROW_SLAB, LANES), xs.dtype),
        compiler_params=_cparams(1),
        name="moe_experts",
    )(blk_e, n_used, xs, w_gate, w_up, w_down)


def _combine_kernel(dest_ref, next_dest_ref, x_ref, shared_ref, wt_ref, ys_hbm, g_ref, b_ref,
                    o32_ref, o16_ref, buf_ref, sem, *, tm):
    i = pl.program_id(0)

    def tile(ref, r):
        return ref.at[pl.ds(pl.multiple_of(r * ROW_SLAB, ROW_SLAB), ROW_SLAB), :]

    def gather(table_ref, half):
        def issue(tok, _):
            for k in range(TOP_K):
                pltpu.make_async_copy(tile(ys_hbm, table_ref[tok * SLOT_STRIDE + k]),
                                      tile(buf_ref.at[half, k], tok), sem.at[half]).start()
            return 0

        lax.fori_loop(0, tm, issue, 0)

    @pl.when(i == 0)
    def _():
        gather(dest_ref, 0)

    for half in range(2):
        @pl.when((i + 1 < pl.num_programs(0)) & ((i + 1) % 2 == half))
        def _():
            gather(next_dest_ref, half)

    total = DEEPNORM_ALPHA * x_ref[...] + shared_ref[...].astype(F32)
    for k in range(TOP_K):
        pltpu.make_async_copy(ys_hbm.at[pl.ds(0, tm * ROW_SLAB), :], buf_ref.at[0, k], sem.at[i % 2]).wait()
    wt = wt_ref[...]
    routed_lo = routed_hi = None
    for k in range(TOP_K):
        lo, hi = _unpack_bf16_pairs(_load_row_slabs(buf_ref.at[i % 2, k]))
        w_k = wt[:, k:k + 1]
        routed_lo = w_k * lo if routed_lo is None else routed_lo + w_k * lo
        routed_hi = w_k * hi if routed_hi is None else routed_hi + w_k * hi
    total = total + jnp.concatenate([routed_lo, routed_hi], axis=1)
    out = _layer_norm_rows(total, g_ref[...], b_ref[...])
    o32_ref[...] = out
    o16_ref[...] = out.astype(o16_ref.dtype)


def _combine(x32, shared, dest_flat, wts, ys, g, b, tm=256):
    t = x32.shape[0]
    n_steps = t // tm
    full = lambda arr: pl.BlockSpec(arr.shape, lambda i: (0,) * arr.ndim)
    rows = pl.BlockSpec((tm, D_MODEL), lambda i: (i, 0))
    return pl.pallas_call(
        functools.partial(_combine_kernel, tm=tm),
        grid=(n_steps,),
        in_specs=[pl.BlockSpec((tm * SLOT_STRIDE,), lambda i: (i,), memory_space=pltpu.SMEM),
                  pl.BlockSpec((tm * SLOT_STRIDE,), lambda i: (jnp.minimum(i + 1, n_steps - 1),),
                               memory_space=pltpu.SMEM),
                  rows, rows,
                  pl.BlockSpec((tm, LANES), lambda i: (i, 0)),
                  pl.BlockSpec(memory_space=pl.ANY),
                  full(g), full(b)],
        out_specs=[rows, rows],
        out_shape=[jax.ShapeDtypeStruct((t, D_MODEL), F32), jax.ShapeDtypeStruct((t, D_MODEL), BF16)],
        scratch_shapes=[pltpu.VMEM((2, TOP_K, tm * ROW_SLAB, LANES), ys.dtype), pltpu.SemaphoreType.DMA((2,))],
        compiler_params=_cparams(1),
        name="moe_combine",
    )(dest_flat, dest_flat, x32, shared, wts, ys, g, b)


def _rot_half_cols(w):
    half = w.shape[-1] // 2
    return jnp.concatenate([-w[..., half:], w[..., :half]], axis=-1)


def _pad_cols(w, width):
    return jnp.pad(w, [(0, 0)] * (w.ndim - 1) + [(0, width - w.shape[-1])])


def _rope_tables(seq):
    pos = jnp.arange(seq, dtype=F32)[:, None]

    def table(dim):
        half = dim // 2
        inv_freq = ROPE_THETA ** (-jnp.arange(half, dtype=F32) / half)
        ang = pos * inv_freq[None, :]
        return jnp.cos(ang), jnp.sin(ang)

    c64, s64 = table(MLA_ROPE_DIM)
    cos64 = _pad_cols(jnp.concatenate([c64, c64], axis=1), LANES)
    sin64 = _pad_cols(jnp.concatenate([s64, s64], axis=1), LANES)
    c128, s128 = table(MOBA_HEAD_DIM)
    cos128 = jnp.concatenate([c128, c128], axis=1)
    sin128 = jnp.concatenate([-s128, s128], axis=1)
    return cos64, sin64, cos128, sin128


def _layer(x32, x16, mem16, tables, avg, bsz, seq, mem_len, p, expert_weights, layer):
    (w_in, g_qa, w_q_up, g_kva, w_kv_up, sgu_ln_g, sgu_ln_b, w_spatial, b_spatial, w_mem_kv,
     w_br_a, w_br_b, w_br_c, w_br_m, w_o, ln1_g, ln1_b, w_router, router_bias,
     w_sh_gate, w_sh_up, w_sh_down, ln2_g, ln2_b) = p
    cos64, sin64, cos128, sin128 = tables
    t = x32.shape[0]
    row = lambda v: v.reshape(1, -1).astype(F32)

    c0 = MLA_Q_RANK + MLA_KV_RANK
    c1 = c0 + MLA_ROPE_DIM
    c2 = c1 + 2 * SGU_WIDTH
    c3 = c2 + 3 * MOBA_WIDTH
    c4 = c3 + MEM_WIDTH
    w_a = w_in[:, :c0].astype(BF16)
    w_kpe_raw = w_in[:, c0:c1]
    w_kpe = jnp.concatenate([_pad_cols(w_kpe_raw, LANES), _pad_cols(_rot_half_cols(w_kpe_raw), LANES)],
                            axis=1).astype(BF16)
    w_z = w_in[:, c1:c2].astype(BF16)
    w_c = w_in[:, c2:c3].astype(BF16)
    w_mq = w_in[:, c3:c4].astype(BF16)
    w_g = w_in[:, c4:].astype(BF16)

    w_qn = w_q_up[:, :, :MLA_NOPE_DIM].reshape(MLA_Q_RANK, -1).astype(BF16)
    w_qpe = w_q_up[:, :, MLA_NOPE_DIM:]
    w_qp = _pad_cols(w_qpe, LANES).reshape(MLA_Q_RANK, -1).astype(BF16)
    w_qpr = _pad_cols(_rot_half_cols(w_qpe), LANES).reshape(MLA_Q_RANK, -1).astype(BF16)
    w_kn = w_kv_up[:, :, :MLA_NOPE_DIM].reshape(MLA_KV_RANK, -1).astype(BF16)
    w_v = w_kv_up[:, :, MLA_NOPE_DIM:].reshape(MLA_KV_RANK, -1).T.astype(BF16)
    qt_a, k_a, vt_a = _mla_proj(x16, w_a, w_kpe, row(g_qa), row(g_kva), w_qn.T, w_qp.T, w_qpr.T, w_kn, w_v,
                                cos64, sin64, seq)
    o_a = _mla_attention(qt_a, k_a, vt_a, bsz, seq)

    bias_full = jnp.repeat(b_spatial.T.astype(F32), SGU_GROUP_DIM, axis=1)
    o_b = _sgu(x16, w_z, row(sgu_ln_g), row(sgu_ln_b), w_spatial.astype(F32), bias_full)

    k_c = _moba_k_proj(x16, w_c[:, MOBA_WIDTH:2 * MOBA_WIDTH], cos128, sin128, seq)
    w_qv_t = jnp.concatenate([w_c[:, :MOBA_WIDTH], w_c[:, 2 * MOBA_WIDTH:]], axis=1).T
    qvt_c = _moba_qv_proj(x16, w_qv_t, cos128.T, sin128.T, seq)
    o_c = _moba_attention(qvt_c, k_c, avg, bsz, seq)

    kv_mem = _matmul(mem16, w_mem_kv.astype(BF16), mem_len, 2 * MEM_WIDTH, BF16, "mem_kv")
    o_m = _mem_attention(x16, w_mq, kv_mem, bsz, seq, mem_len)

    y = _gated_sum(x16, o_a, o_b, o_c, o_m, w_g, w_br_a.astype(BF16), w_br_b.astype(BF16),
                   w_br_c.astype(BF16), w_br_m.astype(BF16))
    x32, x16, x_packed = _proj_norm(y, w_o.astype(BF16), x32, row(ln1_g), row(ln1_b))

    w_r = _pad_cols(w_router.astype(F32), LANES)
    w_r_hi = w_r.astype(BF16)
    w_r_split = jnp.stack([w_r_hi, (w_r - w_r_hi.astype(F32)).astype(BF16)])
    e_idx, wts, rank, counts = _router(x32, w_r_split, _pad_cols(row(router_bias), LANES))
    bm = MOE_ROW_BLOCK
    n_blk = (t * TOP_K) // bm + N_EXPERTS
    counts = counts[0, :N_EXPERTS].astype(I32)
    padded = (counts + bm - 1) // bm * bm
    pad_ends = jnp.cumsum(padded)
    pad_starts = pad_ends - padded
    blk_first = jnp.arange(n_blk, dtype=I32)[:, None] * bm
    blk_e = jnp.minimum(jnp.sum((pad_ends[None, :] <= blk_first).astype(I32), axis=1), N_EXPERTS - 1)
    n_used = (pad_ends[-1:] // bm).astype(I32)
    xs, shared, dest = _dispatch(x_packed, x16, e_idx[:, :SLOT_STRIDE].reshape(-1), rank[:, :SLOT_STRIDE].reshape(-1),
                                 pad_starts.astype(I32), pad_starts + counts, padded - counts, n_used, n_blk * bm,
                                 w_sh_gate.astype(BF16), w_sh_up.astype(BF16), w_sh_down.astype(BF16))
    ys = _expert_ffn(xs, blk_e, n_used, *expert_weights, layer)
    return _combine(x32, shared, dest, wts, ys, row(ln2_g), row(ln2_b))


def kernel(x, mem, w_in, g_qa, w_q_up, g_kva, w_kv_up, sgu_ln_g, sgu_ln_b, w_spatial, b_spatial, w_mem_kv, w_br_a, w_br_b, w_br_c, w_br_m, w_o, ln1_g, ln1_b, w_router, router_bias, w_exp_gate, w_exp_up, w_exp_down, w_sh_gate, w_sh_up, w_sh_down, ln2_g, ln2_b):
    bsz, seq, d = x.shape
    mem_len = mem.shape[1]
    params = (w_in, g_qa, w_q_up, g_kva, w_kv_up, sgu_ln_g, sgu_ln_b, w_spatial, b_spatial, w_mem_kv,
              w_br_a, w_br_b, w_br_c, w_br_m, w_o, ln1_g, ln1_b, w_router, router_bias,
              w_sh_gate, w_sh_up, w_sh_down, ln2_g, ln2_b)
    expert_weights = (w_exp_gate, w_exp_up, w_exp_down)
    tables = _rope_tables(seq)
    n_blocks = seq // MOBA_BLOCK
    blk_of_pos = jnp.arange(seq, dtype=I32)[None, :] // MOBA_BLOCK
    avg = jnp.where(blk_of_pos == jnp.arange(LANES, dtype=I32)[:, None], 1.0 / MOBA_BLOCK, 0.0).astype(BF16)
    assert n_blocks <= LANES
    x32 = x.reshape(bsz * seq, d)
    x16 = x32.astype(BF16)
    mem16 = mem.reshape(bsz * mem_len, d).astype(BF16)
    for l in range(DEPTH):
        x32, x16 = _layer(x32, x16, mem16, tables, avg, bsz, seq, mem_len, tuple(w[l] for w in params),
                          expert_weights, l)
    return x32.reshape(bsz, seq, d)
```

```python
import functools

import jax
import jax.numpy as jnp
from jax import lax
from jax.experimental import pallas as pl
from jax.experimental.pallas import tpu as pltpu

F32 = jnp.float32
BF16 = jnp.bfloat16
I32 = jnp.int32
U32 = jnp.uint32

D_MODEL = 2048
DEPTH = 2
MLA_HEADS = 8
MLA_Q_RANK = 512
MLA_KV_RANK = 256
MLA_NOPE_DIM = 128
MLA_ROPE_DIM = 64
MLA_V_DIM = 128
SGU_GROUPS = 8
SGU_GROUP_DIM = 128
SGU_WIDTH = SGU_GROUPS * SGU_GROUP_DIM
SGU_CHUNK = 128
MOBA_HEADS = 8
MOBA_HEAD_DIM = 128
MOBA_WIDTH = MOBA_HEADS * MOBA_HEAD_DIM
MOBA_BLOCK = 256
MOBA_TOPK = 3
MEM_HEADS = 4
MEM_HEAD_DIM = 128
MEM_WIDTH = MEM_HEADS * MEM_HEAD_DIM
N_BRANCH = 4
N_EXPERTS = 64
TOP_K = 6
D_EXPERT = 512
ROUTED_SCALE = 2.5
ROPE_THETA = 10000.0
DEEPNORM_ALPHA = (2 * DEPTH) ** 0.25

LANES = 128
SUBLANES = 8
ATTN_BLOCK = 256
MOE_ROW_BLOCK = 512
SLOT_STRIDE = 8
ROW_WORDS = D_MODEL // 2
ROW_SLAB = ROW_WORDS // LANES
NEG_BIG = -1e30
VMEM_LIMIT = 56 * 1024 * 1024


def _cparams(n_axes):
    return pltpu.CompilerParams(dimension_semantics=("arbitrary",) * n_axes,
                                vmem_limit_bytes=VMEM_LIMIT)


def _dot(a, b):
    return jnp.dot(a, b, preferred_element_type=F32)


def _dot_nt(a, b):
    return lax.dot_general(a, b, (((1,), (1,)), ((), ())), preferred_element_type=F32)


def _layer_norm_rows(v, g, b, eps=1e-5):
    mu = jnp.mean(v, axis=-1, keepdims=True)
    vc = v - mu
    var = jnp.mean(vc * vc, axis=-1, keepdims=True)
    return vc * lax.rsqrt(var + eps) * g + b


def _rms_norm_rows(v, g, eps=1e-6):
    return v * lax.rsqrt(jnp.mean(v * v, axis=-1, keepdims=True) + eps) * g


def _mm_kernel(x_ref, w_ref, o_ref):
    o_ref[...] = _dot(x_ref[...], w_ref[...]).astype(o_ref.dtype)


def _matmul(x, w, tm, tn, out_dtype, name):
    m, k = x.shape
    n = w.shape[1]
    return pl.pallas_call(
        _mm_kernel,
        grid=(n // tn, m // tm),
        in_specs=[pl.BlockSpec((tm, k), lambda j, i: (i, 0)),
                  pl.BlockSpec((k, tn), lambda j, i: (0, j))],
        out_specs=pl.BlockSpec((tm, tn), lambda j, i: (i, j)),
        out_shape=jax.ShapeDtypeStruct((m, n), out_dtype),
        compiler_params=_cparams(2),
        name=name,
    )(x, w)


def _mla_proj_kernel(x_ref, wa_ref, wkpe_ref, gq_ref, gkv_ref, wqnt_ref, wqpt_ref, wqprt_ref,
                     wkn_ref, wvt_ref, cos_ref, sin_ref, cost_ref, sint_ref, qt_ref, k_ref, vt_ref, *, scale):
    xb = x_ref[...]
    a = _dot(xb, wa_ref[...])
    cq = _rms_norm_rows(a[:, :MLA_Q_RANK], gq_ref[...]).astype(BF16)
    ckv = _rms_norm_rows(a[:, MLA_Q_RANK:], gkv_ref[...]).astype(BF16)
    qn_t = _dot_nt(wqnt_ref[...], cq)
    qp_t = _dot_nt(wqpt_ref[...], cq)
    qpr_t = _dot_nt(wqprt_ref[...], cq)
    v_t = _dot_nt(wvt_ref[...], ckv)
    cos_t = cost_ref[...]
    sin_t = sint_ref[...]
    for h in range(MLA_HEADS):
        lo, hi = h * LANES, (h + 1) * LANES
        qt_ref[0, h, 0, 0:LANES, :] = (qn_t[lo:hi] * scale).astype(qt_ref.dtype)
        qt_ref[0, h, 0, LANES:2 * LANES, :] = ((qp_t[lo:hi] * cos_t + qpr_t[lo:hi] * sin_t) * scale).astype(qt_ref.dtype)
        vt_ref[0, h, 0] = v_t[h * MLA_V_DIM:(h + 1) * MLA_V_DIM, :].astype(vt_ref.dtype)
    kn = _dot(ckv, wkn_ref[...])
    kp = _dot(xb, wkpe_ref[...])
    kpe = (kp[:, :LANES] * cos_ref[...] + kp[:, LANES:] * sin_ref[...]).astype(k_ref.dtype)
    for h in range(MLA_HEADS):
        lo, hi = h * LANES, (h + 1) * LANES
        k_ref[:, 2 * lo:2 * lo + LANES] = kn[:, lo:hi].astype(k_ref.dtype)
        k_ref[:, 2 * lo + LANES:2 * hi] = kpe


def _mla_proj(x, wa, wkpe, gq, gkv, wqnt, wqpt, wqprt, wkn, wvt, cos64, sin64, seq):
    t = x.shape[0]
    tm = ATTN_BLOCK
    full = lambda arr: pl.BlockSpec(arr.shape, lambda i: (0,) * arr.ndim)
    n_pos = seq // tm
    scale = (MLA_NOPE_DIM + MLA_ROPE_DIM) ** -0.5
    dk = 2 * LANES
    feat_major = lambda d: pl.BlockSpec((1, MLA_HEADS, 1, d, tm), lambda i: (i // n_pos, 0, i % n_pos, 0, 0))
    return pl.pallas_call(
        functools.partial(_mla_proj_kernel, scale=scale),
        grid=(t // tm,),
        in_specs=[pl.BlockSpec((tm, D_MODEL), lambda i: (i, 0)),
                  full(wa), full(wkpe), full(gq), full(gkv), full(wqnt), full(wqpt), full(wqprt),
                  full(wkn), full(wvt),
                  pl.BlockSpec((tm, LANES), lambda i: (i % n_pos, 0)),
                  pl.BlockSpec((tm, LANES), lambda i: (i % n_pos, 0)),
                  pl.BlockSpec((LANES, tm), lambda i: (0, i % n_pos)),
                  pl.BlockSpec((LANES, tm), lambda i: (0, i % n_pos))],
        out_specs=[feat_major(dk),
                   pl.BlockSpec((tm, MLA_HEADS * dk), lambda i: (i, 0)),
                   feat_major(MLA_V_DIM)],
        out_shape=[jax.ShapeDtypeStruct((t // seq, MLA_HEADS, n_pos, dk, tm), BF16),
                   jax.ShapeDtypeStruct((t, MLA_HEADS * dk), BF16),
                   jax.ShapeDtypeStruct((t // seq, MLA_HEADS, n_pos, MLA_V_DIM, tm), BF16)],
        compiler_params=_cparams(1),
        name="mla_proj",
    )(x, wa, wkpe, gq, gkv, wqnt, wqpt, wqprt, wkn, wvt, cos64, sin64, cos64.T, sin64.T)


def _softmax_steps(q_ts, ks, v_ts, carries, masks):
    heads = range(len(q_ts))
    s_ts = [_dot(ks[h], q_ts[h]) for h in heads]
    s_ts = [s_ts[h] if masks[h] is None else jnp.where(masks[h], s_ts[h], NEG_BIG) for h in heads]
    m_news = [jnp.maximum(carries[h][0], jnp.max(s_ts[h], axis=0, keepdims=True)) for h in heads]
    p_ts = [jnp.exp(s_ts[h] - m_news[h]) for h in heads]
    pvs = [_dot(v_ts[h], p_ts[h].astype(v_ts[h].dtype)) for h in heads]
    out = []
    for h in heads:
        m_i, l_i, acc_t = carries[h]
        alpha = jnp.exp(m_i - m_news[h])
        l_new = alpha * l_i + jnp.sum(p_ts[h], axis=0, keepdims=True)
        out.append((m_news[h], l_new, alpha * acc_t + pvs[h]))
    return tuple(out)


def _softmax_init(n_heads, tq, dv):
    return tuple((jnp.full((1, tq), NEG_BIG, F32), jnp.zeros((1, tq), F32), jnp.zeros((dv, tq), F32))
                 for _ in range(n_heads))


def _softmax_finish(o_ref, carries, dv):
    for h, (_, l_i, acc_t) in enumerate(carries):
        o_ref[:, h * dv:(h + 1) * dv] = (acc_t / l_i).T.astype(o_ref.dtype)


def _causal_attn_kernel(qt_ref, k_ref, vt_ref, o_ref, *, tq, n_heads, dk, dv):
    qi = pl.program_id(2)
    key = lax.broadcasted_iota(I32, (tq, tq), 0)
    qry = lax.broadcasted_iota(I32, (tq, tq), 1)
    qs = [qt_ref[0, h, 0] for h in range(n_heads)]

    def step(kb, carries, mask):
        ks = pl.multiple_of(kb * tq, tq)
        return _softmax_steps(qs, [k_ref[pl.ds(ks, tq), h * dk:(h + 1) * dk] for h in range(n_heads)],
                              [vt_ref[0, h, kb] for h in range(n_heads)], carries, [mask] * n_heads)

    carries = step(qi, _softmax_init(n_heads, tq, dv), key <= qry)
    carries = lax.fori_loop(0, qi, lambda kb, c: step(kb, c, None), carries)
    _softmax_finish(o_ref, carries, dv)


def _mla_attention(q_t, k_a, v_t, bsz, seq, n_heads=8):
    t = k_a.shape[0]
    tq = ATTN_BLOCK
    nq = seq // tq
    dk = 2 * LANES
    dv = MLA_V_DIM
    return pl.pallas_call(
        functools.partial(_causal_attn_kernel, tq=tq, n_heads=n_heads, dk=dk, dv=dv),
        grid=(bsz, MLA_HEADS // n_heads, nq),
        in_specs=[pl.BlockSpec((1, n_heads, 1, dk, tq), lambda b, h, i: (b, h, i, 0, 0)),
                  pl.BlockSpec((seq, n_heads * dk), lambda b, h, i: (b, h)),
                  pl.BlockSpec((1, n_heads, nq, dv, tq), lambda b, h, i: (b, h, 0, 0, 0))],
        out_specs=pl.BlockSpec((tq, n_heads * dv), lambda b, h, i: (b * nq + i, h)),
        out_shape=jax.ShapeDtypeStruct((t, MLA_HEADS * dv), BF16),
        compiler_params=_cparams(3),
        name="mla_attn",
    )(q_t, k_a, v_t)


def _sgu_kernel(x_ref, wz_ref, g_ref, b_ref, ws_ref, bs_ref, o_ref, *, tm):
    z = jax.nn.gelu(_dot(x_ref[...], wz_ref[...]))
    u = z[:, :SGU_WIDTH]
    v = _layer_norm_rows(z[:, SGU_WIDTH:], g_ref[...], b_ref[...]).astype(BF16)
    row = lax.broadcasted_iota(I32, (SGU_CHUNK, SGU_CHUNK), 0)
    col = lax.broadcasted_iota(I32, (SGU_CHUNK, SGU_CHUNK), 1)
    bias = bs_ref[...]
    for g in range(SGU_GROUPS):
        w = jnp.where(col <= row, ws_ref[g], 0.0).astype(BF16)
        lo, hi = g * SGU_GROUP_DIM, (g + 1) * SGU_GROUP_DIM
        for c in range(tm // SGU_CHUNK):
            r0, r1 = c * SGU_CHUNK, (c + 1) * SGU_CHUNK
            mixed = _dot(w, v[r0:r1, lo:hi]) + bias[:, lo:hi]
            o_ref[r0:r1, lo:hi] = (u[r0:r1, lo:hi] * mixed).astype(o_ref.dtype)


def _sgu(x, wz, ln_g, ln_b, w_s, bias_full, tm=512):
    t = x.shape[0]
    full = lambda arr: pl.BlockSpec(arr.shape, lambda i: (0,) * arr.ndim)
    return pl.pallas_call(
        functools.partial(_sgu_kernel, tm=tm),
        grid=(t // tm,),
        in_specs=[pl.BlockSpec((tm, D_MODEL), lambda i: (i, 0)),
                  full(wz), full(ln_g), full(ln_b), full(w_s), full(bias_full)],
        out_specs=pl.BlockSpec((tm, SGU_WIDTH), lambda i: (i, 0)),
        out_shape=jax.ShapeDtypeStruct((t, SGU_WIDTH), BF16),
        compiler_params=_cparams(1),
        name="sgu",
    )(x, wz, ln_g, ln_b, w_s, bias_full)


def _moba_k_proj_kernel(x_ref, w_ref, cos_ref, sin_ref, o_ref):
    acc = _dot(x_ref[...], w_ref[...])
    cos = cos_ref[...]
    sin = sin_ref[...]
    for h in range(MOBA_HEADS):
        seg = acc[:, h * LANES:(h + 1) * LANES]
        rot = pltpu.roll(seg, MOBA_HEAD_DIM // 2, axis=1)
        o_ref[:, h * LANES:(h + 1) * LANES] = (seg * cos + rot * sin).astype(o_ref.dtype)


def _moba_k_proj(x, w_k, cos128, sin128, seq, tm=512):
    t = x.shape[0]
    n_pos = seq // tm
    return pl.pallas_call(
        _moba_k_proj_kernel,
        grid=(t // tm,),
        in_specs=[pl.BlockSpec((tm, D_MODEL), lambda i: (i, 0)),
                  pl.BlockSpec(w_k.shape, lambda i: (0, 0)),
                  pl.BlockSpec((tm, LANES), lambda i: (i % n_pos, 0)),
                  pl.BlockSpec((tm, LANES), lambda i: (i % n_pos, 0))],
        out_specs=pl.BlockSpec((tm, MOBA_WIDTH), lambda i: (i, 0)),
        out_shape=jax.ShapeDtypeStruct((t, MOBA_WIDTH), BF16),
        compiler_params=_cparams(1),
        name="moba_k_proj",
    )(x, w_k, cos128, sin128)


def _moba_qv_proj_kernel(x_ref, wt_ref, cost_ref, sint_ref, o_ref, *, scale):
    y_t = _dot_nt(wt_ref[...], x_ref[...])
    hd = MOBA_HEAD_DIM

    @pl.when(pl.program_id(0) == 0)
    def _():
        cos_t = cost_ref[...]
        sin_t = sint_ref[...]
        for h in range(MOBA_HEADS):
            seg = y_t[h * hd:(h + 1) * hd]
            rot = jnp.concatenate([seg[hd // 2:], seg[:hd // 2]], axis=0)
            o_ref[0, h, 0] = ((seg * cos_t + rot * sin_t) * scale).astype(o_ref.dtype)

    @pl.when(pl.program_id(0) == 1)
    def _():
        for h in range(MOBA_HEADS):
            o_ref[0, h, 0] = y_t[h * hd:(h + 1) * hd].astype(o_ref.dtype)


def _moba_qv_proj(x, w_qv_t, cos128_t, sin128_t, seq):
    t = x.shape[0]
    tm = ATTN_BLOCK
    n_pos = seq // tm
    return pl.pallas_call(
        functools.partial(_moba_qv_proj_kernel, scale=MOBA_HEAD_DIM ** -0.5),
        grid=(2, t // tm),
        in_specs=[pl.BlockSpec((tm, D_MODEL), lambda j, i: (i, 0)),
                  pl.BlockSpec((MOBA_WIDTH, D_MODEL), lambda j, i: (j, 0)),
                  pl.BlockSpec((MOBA_HEAD_DIM, tm), lambda j, i: (0, i % n_pos)),
                  pl.BlockSpec((MOBA_HEAD_DIM, tm), lambda j, i: (0, i % n_pos))],
        out_specs=pl.BlockSpec((1, MOBA_HEADS, 1, MOBA_HEAD_DIM, tm),
                               lambda j, i: (i // n_pos, j, i % n_pos, 0, 0)),
        out_shape=jax.ShapeDtypeStruct((t // seq, 2 * MOBA_HEADS, n_pos, MOBA_HEAD_DIM, tm), BF16),
        compiler_params=_cparams(2),
        name="moba_qv_proj",
    )(x, w_qv_t, cos128_t, sin128_t)


def _moba_attn_kernel(qt_ref, k_ref, vt_ref, avg_ref, o_ref, kmean_ref, *, n_heads):
    j = pl.program_id(2)
    blk = MOBA_BLOCK
    hd = MOBA_HEAD_DIM

    @pl.when(j == 0)
    def _():
        kmean_ref[...] = _dot(avg_ref[...], k_ref[...])

    blk_id = lax.broadcasted_iota(I32, (SUBLANES, blk), 0)
    n_blocks = k_ref.shape[0] // blk
    qs, sels = [], []
    for h in range(n_heads):
        q = qt_ref[0, h, 0]
        gate = jnp.dot(kmean_ref[0:SUBLANES, h * hd:(h + 1) * hd], q.astype(F32),
                       precision=lax.Precision.HIGHEST, preferred_element_type=F32)
        rank = jnp.zeros((SUBLANES, blk), I32)
        for m in range(n_blocks):
            gm = gate[m:m + 1, :]
            beats = (gm > gate) | ((gm == gate) & (m < blk_id))
            rank = rank + jnp.where(beats & (m < j), 1, 0)
        qs.append(q)
        sels.append(jnp.where((rank < MOBA_TOPK) & (blk_id < j), 1.0, 0.0))

    def step(kb, carries, masks):
        ks = pl.multiple_of(kb * blk, blk)
        return _softmax_steps(qs, [k_ref[pl.ds(ks, blk), h * hd:(h + 1) * hd] for h in range(n_heads)],
                              [vt_ref[0, h, kb] for h in range(n_heads)], carries, masks)

    key = lax.broadcasted_iota(I32, (blk, blk), 0)
    qry = lax.broadcasted_iota(I32, (blk, blk), 1)
    carries = step(j, _softmax_init(n_heads, blk, hd), [key <= qry] * n_heads)

    def body(n, c):
        masks = [jnp.max(jnp.where(blk_id == n, sels[h], 0.0), axis=0, keepdims=True) > 0.5
                 for h in range(n_heads)]
        return step(n, c, masks)

    carries = lax.fori_loop(0, j, body, carries)
    _softmax_finish(o_ref, carries, hd)


def _moba_attention(qv_t, k, avg, bsz, seq, n_heads=8):
    t = k.shape[0]
    nq = seq // MOBA_BLOCK
    assert nq <= SUBLANES and MOBA_BLOCK == ATTN_BLOCK
    w = n_heads * MOBA_HEAD_DIM
    groups = MOBA_HEADS // n_heads
    return pl.pallas_call(
        functools.partial(_moba_attn_kernel, n_heads=n_heads),
        grid=(bsz, groups, nq),
        in_specs=[pl.BlockSpec((1, n_heads, 1, MOBA_HEAD_DIM, MOBA_BLOCK), lambda b, h, i: (b, h, i, 0, 0)),
                  pl.BlockSpec((seq, w), lambda b, h, i: (b, h)),
                  pl.BlockSpec((1, n_heads, nq, MOBA_HEAD_DIM, MOBA_BLOCK),
                               lambda b, h, i: (b, groups + h, 0, 0, 0)),
                  pl.BlockSpec(avg.shape, lambda b, h, i: (0, 0))],
        out_specs=pl.BlockSpec((MOBA_BLOCK, w), lambda b, h, i: (b * nq + i, h)),
        out_shape=jax.ShapeDtypeStruct((t, MOBA_WIDTH), BF16),
        scratch_shapes=[pltpu.VMEM((LANES, w), F32)],
        compiler_params=_cparams(3),
        name="moba_attn",
    )(qv_t, k, qv_t, avg)


def _mem_attn_kernel(x_ref, wq_ref, kv_ref, o_ref, *, scale):
    q = (_dot(x_ref[...], wq_ref[...]) * scale).astype(BF16)
    for h in range(MEM_HEADS):
        lo, hi = h * MEM_HEAD_DIM, (h + 1) * MEM_HEAD_DIM
        s = _dot_nt(q[:, lo:hi], kv_ref[:, lo:hi])
        p = jnp.exp(s - jnp.max(s, axis=1, keepdims=True))
        o = _dot(p.astype(BF16), kv_ref[:, MEM_WIDTH + lo:MEM_WIDTH + hi])
        o_ref[:, lo:hi] = (o / jnp.sum(p, axis=1, keepdims=True)).astype(o_ref.dtype)


def _mem_attention(x, w_mq, kv_mem, bsz, seq, mem_len, tm=512):
    t = x.shape[0]
    ns = seq // tm
    return pl.pallas_call(
        functools.partial(_mem_attn_kernel, scale=MEM_HEAD_DIM ** -0.5),
        grid=(bsz, ns),
        in_specs=[pl.BlockSpec((tm, D_MODEL), lambda b, i: (b * ns + i, 0)),
                  pl.BlockSpec(w_mq.shape, lambda b, i: (0, 0)),
                  pl.BlockSpec((mem_len, 2 * MEM_WIDTH), lambda b, i: (b, 0))],
        out_specs=pl.BlockSpec((tm, MEM_WIDTH), lambda b, i: (b * ns + i, 0)),
        out_shape=jax.ShapeDtypeStruct((t, MEM_WIDTH), BF16),
        compiler_params=_cparams(2),
        name="mem_attn",
    )(x, w_mq, kv_mem)


def _gated_sum_kernel(x_ref, oa_ref, ob_ref, oc_ref, om_ref, g0_ref, g1_ref, g2_ref, g3_ref,
                      wa_ref, wb_ref, wc_ref, wm_ref, y_ref):
    xb = x_ref[...]
    acc = None
    for o_ref, g_ref, w_ref in ((oa_ref, g0_ref, wa_ref), (ob_ref, g1_ref, wb_ref),
                                (oc_ref, g2_ref, wc_ref), (om_ref, g3_ref, wm_ref)):
        term = jax.nn.sigmoid(_dot(xb, g_ref[...])) * _dot(o_ref[...], w_ref[...])
        acc = term if acc is None else acc + term
    y_ref[...] = acc.astype(y_ref.dtype)


def _gated_sum(x, o_a, o_b, o_c, o_m, w_gate, w_br_a, w_br_b, w_br_c, w_br_m, tm=1024, tn=256):
    t = x.shape[0]
    nj = D_MODEL // tn
    rows = lambda width: pl.BlockSpec((tm, width), lambda j, i: (i, 0))
    gate = lambda b: pl.BlockSpec((D_MODEL, tn), lambda j, i: (0, b * nj + j))
    cols = lambda width: pl.BlockSpec((width, tn), lambda j, i: (0, j))
    return pl.pallas_call(
        _gated_sum_kernel,
        grid=(nj, t // tm),
        in_specs=[rows(D_MODEL), rows(o_a.shape[1]), rows(o_b.shape[1]), rows(o_c.shape[1]), rows(o_m.shape[1]),
                  gate(0), gate(1), gate(2), gate(3),
                  cols(w_br_a.shape[0]), cols(w_br_b.shape[0]), cols(w_br_c.shape[0]), cols(w_br_m.shape[0])],
        out_specs=pl.BlockSpec((tm, tn), lambda j, i: (i, j)),
        out_shape=jax.ShapeDtypeStruct((t, D_MODEL), BF16),
        compiler_params=_cparams(2),
        name="gated_sum",
    )(x, o_a, o_b, o_c, o_m, w_gate, w_gate, w_gate, w_gate, w_br_a, w_br_b, w_br_c, w_br_m)


def _pack_bf16_pairs(v):
    bits = lax.bitcast_convert_type(v.astype(BF16).astype(F32), U32)
    half = v.shape[1] // 2
    return (bits[:, :half] >> 16) | bits[:, half:]


def _unpack_bf16_pairs(words):
    lo = lax.bitcast_convert_type(words << 16, F32)
    hi = lax.bitcast_convert_type(words & jnp.uint32(0xFFFF0000), F32)
    return lo, hi


def _store_row_slabs(ref, words):
    for c in range(ROW_SLAB):
        ref[pl.ds(c, words.shape[0], stride=ROW_SLAB), :] = words[:, c * LANES:(c + 1) * LANES]


def _load_row_slabs(ref):
    n_rows = ref.shape[0] // ROW_SLAB
    return jnp.concatenate([ref[pl.ds(c, n_rows, stride=ROW_SLAB), :] for c in range(ROW_SLAB)], axis=1)


def _proj_norm_kernel(y_ref, w_ref, res_ref, g_ref, b_ref, o32_ref, o16_ref, opk_ref):
    v = DEEPNORM_ALPHA * res_ref[...] + _dot(y_ref[...], w_ref[...])
    out = _layer_norm_rows(v, g_ref[...], b_ref[...])
    o32_ref[...] = out
    o16_ref[...] = out.astype(o16_ref.dtype)
    _store_row_slabs(opk_ref, _pack_bf16_pairs(out))


def _proj_norm(y, w_o, res, g, b, tm=512):
    t = y.shape[0]
    full = lambda arr: pl.BlockSpec(arr.shape, lambda i: (0,) * arr.ndim)
    rows = pl.BlockSpec((tm, D_MODEL), lambda i: (i, 0))
    return pl.pallas_call(
        _proj_norm_kernel,
        grid=(t // tm,),
        in_specs=[rows, full(w_o), rows, full(g), full(b)],
        out_specs=[rows, rows, pl.BlockSpec((tm * ROW_SLAB, LANES), lambda i: (i, 0))],
        out_shape=[jax.ShapeDtypeStruct((t, D_MODEL), F32), jax.ShapeDtypeStruct((t, D_MODEL), BF16),
                   jax.ShapeDtypeStruct((t * ROW_SLAB, LANES), U32)],
        compiler_params=_cparams(1),
        name="proj_norm",
    )(y, w_o, res, g, b)


def _router_kernel(x_ref, w_ref, bias_ref, e_ref, wt_ref, rank_ref, cnt_ref, run_ref, *, tm):
    i = pl.program_id(0)

    @pl.when(i == 0)
    def _():
        run_ref[...] = jnp.zeros_like(run_ref)

    x = x_ref[...]
    x_hi = x.astype(BF16)
    x_lo = (x - x_hi.astype(F32)).astype(BF16)
    logits = _dot(x_hi, w_ref[0]) + (_dot(x_hi, w_ref[1]) + _dot(x_lo, w_ref[0]))
    scores = jax.nn.sigmoid(logits)
    lane = lax.broadcasted_iota(I32, (tm, LANES), 1)
    biased = jnp.where(lane < N_EXPERTS, scores + bias_ref[...], NEG_BIG)
    picks = []
    chosen = jnp.zeros((tm, LANES), jnp.bool_)
    for _ in range(TOP_K):
        mx = jnp.max(biased, axis=1, keepdims=True)
        idx = jnp.min(jnp.where(biased == mx, lane, LANES), axis=1, keepdims=True)
        hit = lane == idx
        picks.append((idx, hit))
        chosen = chosen | hit
        biased = jnp.where(hit, 2 * NEG_BIG, biased)
    picked_scores = jnp.where(chosen, scores, 0.0)
    norm = ROUTED_SCALE / jnp.sum(picked_scores, axis=1, keepdims=True)

    r = lax.broadcasted_iota(I32, (tm, tm), 0)
    c = lax.broadcasted_iota(I32, (tm, tm), 1)
    strict_lower = jnp.where(c < r, 1.0, 0.0).astype(BF16)
    chosen_f = jnp.where(chosen, 1.0, 0.0)
    arrival = run_ref[0:1, :] + _dot(strict_lower, chosen_f.astype(BF16))
    run_ref[...] = run_ref[...] + jnp.sum(chosen_f, axis=0, keepdims=True)
    cnt_ref[...] = run_ref[...]

    e_out = jnp.zeros((tm, LANES), I32)
    w_out = jnp.zeros((tm, LANES), F32)
    r_out = jnp.zeros((tm, LANES), F32)
    for slot, (idx, hit) in enumerate(picks):
        here = lane == slot
        e_out = jnp.where(here, idx, e_out)
        w_out = jnp.where(here, jnp.sum(jnp.where(hit, scores, 0.0), axis=1, keepdims=True) * norm, w_out)
        r_out = jnp.where(here, jnp.sum(jnp.where(hit, arrival, 0.0), axis=1, keepdims=True), r_out)
    e_ref[...] = e_out
    wt_ref[...] = w_out
    rank_ref[...] = r_out.astype(I32)


def _router(x32, w_router_pad, bias_pad, tm=256):
    t = x32.shape[0]
    full = lambda arr: pl.BlockSpec(arr.shape, lambda i: (0,) * arr.ndim)
    rows = pl.BlockSpec((tm, LANES), lambda i: (i, 0))
    return pl.pallas_call(
        functools.partial(_router_kernel, tm=tm),
        grid=(t // tm,),
        in_specs=[pl.BlockSpec((tm, D_MODEL), lambda i: (i, 0)), full(w_router_pad), full(bias_pad)],
        out_specs=[rows, rows, rows, pl.BlockSpec((8, LANES), lambda i: (0, 0))],
        out_shape=[jax.ShapeDtypeStruct((t, LANES), I32), jax.ShapeDtypeStruct((t, LANES), F32),
                   jax.ShapeDtypeStruct((t, LANES), I32), jax.ShapeDtypeStruct((8, LANES), F32)],
        scratch_shapes=[pltpu.VMEM((8, LANES), F32)],
        compiler_params=_cparams(1),
        name="moe_router",
    )(x32, w_router_pad, bias_pad)


def _row_tiles(ref, first_row, n_rows):
    return ref.at[pl.ds(pl.multiple_of(first_row * ROW_SLAB, ROW_SLAB), n_rows * ROW_SLAB), :]


def _dispatch_kernel(fill_start_ref, fill_n_ref, n_used_ref, dest_ref, x_ref, x16_ref, wsg_ref, wsu_ref, wsd_ref,
                     out_hbm, shared_ref, zero_ref, sem, fill_sem, *, tm, bm):
    @pl.when(pl.program_id(0) == 0)
    def _():
        zero_ref[...] = jnp.zeros_like(zero_ref)
        n_blk = out_hbm.shape[0] // (bm * ROW_SLAB)

        def fill_expert(e, wait):
            n = fill_n_ref[e]
            row = fill_start_ref[e]
            for bit in range(bm.bit_length() - 1):
                size = 1 << bit

                @pl.when((n & size) != 0)
                def _():
                    copy = pltpu.make_async_copy(_row_tiles(zero_ref, 0, size), _row_tiles(out_hbm, row, size), fill_sem)
                    copy.wait() if wait else copy.start()

                row = row + (n & size)

        def tail_copy(blk):
            return pltpu.make_async_copy(zero_ref, _row_tiles(out_hbm, blk * bm, bm), fill_sem)

        lax.fori_loop(0, N_EXPERTS, lambda e, c: (fill_expert(e, False), c)[1], 0)
        lax.fori_loop(n_used_ref[0], n_blk, lambda b, c: (tail_copy(b).start(), c)[1], 0)
        lax.fori_loop(0, N_EXPERTS, lambda e, c: (fill_expert(e, True), c)[1], 0)
        lax.fori_loop(n_used_ref[0], n_blk, lambda b, c: (tail_copy(0).wait(), c)[1], 0)

    def issue(tok, _):
        for k in range(TOP_K):
            pltpu.make_async_copy(_row_tiles(x_ref, tok, 1),
                                  _row_tiles(out_hbm, dest_ref[tok * SLOT_STRIDE + k], 1), sem).start()
        return 0

    lax.fori_loop(0, tm, issue, 0)
    xb = x16_ref[...]
    hid = jax.nn.silu(_dot(xb, wsg_ref[...])) * _dot(xb, wsu_ref[...])
    shared_ref[...] = _dot(hid.astype(BF16), wsd_ref[...]).astype(shared_ref.dtype)
    for _ in range(TOP_K):
        pltpu.make_async_copy(x_ref, _row_tiles(out_hbm, 0, tm), sem).wait()


def _dispatch(x_rows, x16, dest_flat, fill_start, fill_n, n_used, n_rows, w_sg, w_su, w_sd, tm=256):
    t = x16.shape[0]
    bm = MOE_ROW_BLOCK
    full = lambda arr: pl.BlockSpec(arr.shape, lambda i, fs, fn, nu: (0,) * arr.ndim)
    rows = pl.BlockSpec((tm, D_MODEL), lambda i, fs, fn, nu: (i, 0))
    grid_spec = pltpu.PrefetchScalarGridSpec(
        num_scalar_prefetch=3,
        grid=(t // tm,),
        in_specs=[pl.BlockSpec((tm * SLOT_STRIDE,), lambda i, fs, fn, nu: (i,), memory_space=pltpu.SMEM),
                  pl.BlockSpec((tm * ROW_SLAB, LANES), lambda i, fs, fn, nu: (i, 0)),
                  rows, full(w_sg), full(w_su), full(w_sd)],
        out_specs=[pl.BlockSpec(memory_space=pl.ANY), rows],
        scratch_shapes=[pltpu.VMEM((bm * ROW_SLAB, LANES), x_rows.dtype), pltpu.SemaphoreType.DMA(()),
                        pltpu.SemaphoreType.DMA(())],
    )
    return pl.pallas_call(
        functools.partial(_dispatch_kernel, tm=tm, bm=bm),
        grid_spec=grid_spec,
        out_shape=[jax.ShapeDtypeStruct((n_rows * ROW_SLAB, LANES), x_rows.dtype),
                   jax.ShapeDtypeStruct((t, D_MODEL), BF16)],
        compiler_params=_cparams(1),
        name="moe_dispatch",
    )(fill_start, fill_n, n_used, dest_flat, x_rows, x16, w_sg, w_su, w_sd)


def _expert_kernel(blk_e_ref, n_used_ref, xs_ref, wg_ref, wu_ref, wd_ref, ys_ref, wg16_ref, wu16_ref, wd16_ref):
    i = pl.program_id(0)
    used = i < n_used_ref[0]
    new_expert = (i == 0) | (blk_e_ref[i] != blk_e_ref[jnp.maximum(i - 1, 0)])

    @pl.when(used & new_expert)
    def _():
        wg16_ref[...] = wg_ref[0, 0].astype(BF16)
        wu16_ref[...] = wu_ref[0, 0].astype(BF16)
        wd16_ref[...] = wd_ref[0, 0].astype(BF16)

    @pl.when(used)
    def _():
        lo, hi = _unpack_bf16_pairs(_load_row_slabs(xs_ref))
        xb = jnp.concatenate([lo.astype(BF16), hi.astype(BF16)], axis=1)
        hid = jax.nn.silu(_dot(xb, wg16_ref[...])) * _dot(xb, wu16_ref[...])
        _store_row_slabs(ys_ref, _pack_bf16_pairs(_dot(hid.astype(BF16), wd16_ref[...])))

    @pl.when(jnp.logical_not(used))
    def _():
        ys_ref[...] = jnp.zeros_like(ys_ref)


def _expert_ffn(xs, blk_e, n_used, w_gate, w_up, w_down, layer):
    n_rows = xs.shape[0] // ROW_SLAB
    bm = MOE_ROW_BLOCK
    grid_spec = pltpu.PrefetchScalarGridSpec(
        num_scalar_prefetch=2,
        grid=(n_rows // bm,),
        in_specs=[pl.BlockSpec((bm * ROW_SLAB, LANES), lambda i, be, nu: (jnp.minimum(i, nu[0] - 1), 0)),
                  pl.BlockSpec((1, 1, D_MODEL, D_EXPERT), lambda i, be, nu: (layer, be[i], 0, 0)),
                  pl.BlockSpec((1, 1, D_MODEL, D_EXPERT), lambda i, be, nu: (layer, be[i], 0, 0)),
                  pl.BlockSpec((1, 1, D_EXPERT, D_MODEL), lambda i, be, nu: (layer, be[i], 0, 0))],
        out_specs=pl.BlockSpec((bm * ROW_SLAB, LANES), lambda i, be, nu: (i, 0)),
        scratch_shapes=[pltpu.VMEM((D_MODEL, D_EXPERT), BF16), pltpu.VMEM((D_MODEL, D_EXPERT), BF16),
                        pltpu.VMEM((D_EXPERT, D_MODEL), BF16)],
    )
    return pl.pallas_call(
        _expert_kernel,
        grid_spec=grid_spec,
        out_shape=jax.ShapeDtypeStruct((n_rows * ROW_SLAB, LANES), xs.dtype),
        compiler_params=_cparams(1),
        name="moe_experts",
    )(blk_e, n_used, xs, w_gate, w_up, w_down)


def _combine_kernel(dest_ref, next_dest_ref, x_ref, shared_ref, wt_ref, ys_hbm, g_ref, b_ref,
                    o32_ref, o16_ref, buf_ref, sem, *, tm):
    i = pl.program_id(0)

    def tile(ref, r):
        return ref.at[pl.ds(pl.multiple_of(r * ROW_SLAB, ROW_SLAB), ROW_SLAB), :]

    def gather(table_ref, half):
        def issue(tok, _):
            for k in range(TOP_K):
                pltpu.make_async_copy(tile(ys_hbm, table_ref[tok * SLOT_STRIDE + k]),
                                      tile(buf_ref.at[half, k], tok), sem.at[half]).start()
            return 0

        lax.fori_loop(0, tm, issue, 0)

    @pl.when(i == 0)
    def _():
        gather(dest_ref, 0)

    for half in range(2):
        @pl.when((i + 1 < pl.num_programs(0)) & ((i + 1) % 2 == half))
        def _():
            gather(next_dest_ref, half)

    total = DEEPNORM_ALPHA * x_ref[...] + shared_ref[...].astype(F32)
    for k in range(TOP_K):
        pltpu.make_async_copy(ys_hbm.at[pl.ds(0, tm * ROW_SLAB), :], buf_ref.at[0, k], sem.at[i % 2]).wait()
    wt = wt_ref[...]
    routed_lo = routed_hi = None
    for k in range(TOP_K):
        lo, hi = _unpack_bf16_pairs(_load_row_slabs(buf_ref.at[i % 2, k]))
        w_k = wt[:, k:k + 1]
        routed_lo = w_k * lo if routed_lo is None else routed_lo + w_k * lo
        routed_hi = w_k * hi if routed_hi is None else routed_hi + w_k * hi
    total = total + jnp.concatenate([routed_lo, routed_hi], axis=1)
    out = _layer_norm_rows(total, g_ref[...], b_ref[...])
    o32_ref[...] = out
    o16_ref[...] = out.astype(o16_ref.dtype)


def _combine(x32, shared, dest_flat, wts, ys, g, b, tm=256):
    t = x32.shape[0]
    n_steps = t // tm
    full = lambda arr: pl.BlockSpec(arr.shape, lambda i: (0,) * arr.ndim)
    rows = pl.BlockSpec((tm, D_MODEL), lambda i: (i, 0))
    return pl.pallas_call(
        functools.partial(_combine_kernel, tm=tm),
        grid=(n_steps,),
        in_specs=[pl.BlockSpec((tm * SLOT_STRIDE,), lambda i: (i,), memory_space=pltpu.SMEM),
                  pl.BlockSpec((tm * SLOT_STRIDE,), lambda i: (jnp.minimum(i + 1, n_steps - 1),),
                               memory_space=pltpu.SMEM),
                  rows, rows,
                  pl.BlockSpec((tm, LANES), lambda i: (i, 0)),
                  pl.BlockSpec(memory_space=pl.ANY),
                  full(g), full(b)],
        out_specs=[rows, rows],
        out_shape=[jax.ShapeDtypeStruct((t, D_MODEL), F32), jax.ShapeDtypeStruct((t, D_MODEL), BF16)],
        scratch_shapes=[pltpu.VMEM((2, TOP_K, tm * ROW_SLAB, LANES), ys.dtype), pltpu.SemaphoreType.DMA((2,))],
        compiler_params=_cparams(1),
        name="moe_combine",
    )(dest_flat, dest_flat, x32, shared, wts, ys, g, b)


def _rot_half_cols(w):
    half = w.shape[-1] // 2
    return jnp.concatenate([-w[..., half:], w[..., :half]], axis=-1)


def _pad_cols(w, width):
    return jnp.pad(w, [(0, 0)] * (w.ndim - 1) + [(0, width - w.shape[-1])])


def _rope_tables(seq):
    pos = jnp.arange(seq, dtype=F32)[:, None]

    def table(dim):
        half = dim // 2
        inv_freq = ROPE_THETA ** (-jnp.arange(half, dtype=F32) / half)
        ang = pos * inv_freq[None, :]
        return jnp.cos(ang), jnp.sin(ang)

    c64, s64 = table(MLA_ROPE_DIM)
    cos64 = _pad_cols(jnp.concatenate([c64, c64], axis=1), LANES)
    sin64 = _pad_cols(jnp.concatenate([s64, s64], axis=1), LANES)
    c128, s128 = table(MOBA_HEAD_DIM)
    cos128 = jnp.concatenate([c128, c128], axis=1)
    sin128 = jnp.concatenate([-s128, s128], axis=1)
    return cos64, sin64, cos128, sin128


def _layer(x32, x16, mem16, tables, avg, bsz, seq, mem_len, p, expert_weights, layer):
    (w_in, g_qa, w_q_up, g_kva, w_kv_up, sgu_ln_g, sgu_ln_b, w_spatial, b_spatial, w_mem_kv,
     w_br_a, w_br_b, w_br_c, w_br_m, w_o, ln1_g, ln1_b, w_router, router_bias,
     w_sh_gate, w_sh_up, w_sh_down, ln2_g, ln2_b) = p
    cos64, sin64, cos128, sin128 = tables
    t = x32.shape[0]
    row = lambda v: v.reshape(1, -1).astype(F32)

    c0 = MLA_Q_RANK + MLA_KV_RANK
    c1 = c0 + MLA_ROPE_DIM
    c2 = c1 + 2 * SGU_WIDTH
    c3 = c2 + 3 * MOBA_WIDTH
    c4 = c3 + MEM_WIDTH
    w_a = w_in[:, :c0].astype(BF16)
    w_kpe_raw = w_in[:, c0:c1]
    w_kpe = jnp.concatenate([_pad_cols(w_kpe_raw, LANES), _pad_cols(_rot_half_cols(w_kpe_raw), LANES)],
                            axis=1).astype(BF16)
    w_z = w_in[:, c1:c2].astype(BF16)
    w_c = w_in[:, c2:c3].astype(BF16)
    w_mq = w_in[:, c3:c4].astype(BF16)
    w_g = w_in[:, c4:].astype(BF16)

    w_qn = w_q_up[:, :, :MLA_NOPE_DIM].reshape(MLA_Q_RANK, -1).astype(BF16)
    w_qpe = w_q_up[:, :, MLA_NOPE_DIM:]
    w_qp = _pad_cols(w_qpe, LANES).reshape(MLA_Q_RANK, -1).astype(BF16)
    w_qpr = _pad_cols(_rot_half_cols(w_qpe), LANES).reshape(MLA_Q_RANK, -1).astype(BF16)
    w_kn = w_kv_up[:, :, :MLA_NOPE_DIM].reshape(MLA_KV_RANK, -1).astype(BF16)
    w_v = w_kv_up[:, :, MLA_NOPE_DIM:].reshape(MLA_KV_RANK, -1).T.astype(BF16)
    qt_a, k_a, vt_a = _mla_proj(x16, w_a, w_kpe, row(g_qa), row(g_kva), w_qn.T, w_qp.T, w_qpr.T, w_kn, w_v,
                                cos64, sin64, seq)
    o_a = _mla_attention(qt_a, k_a, vt_a, bsz, seq)

    bias_full = jnp.repeat(b_spatial.T.astype(F32), SGU_GROUP_DIM, axis=1)
    o_b = _sgu(x16, w_z, row(sgu_ln_g), row(sgu_ln_b), w_spatial.astype(F32), bias_full)

    k_c = _moba_k_proj(x16, w_c[:, MOBA_WIDTH:2 * MOBA_WIDTH], cos128, sin128, seq)
    w_qv_t = jnp.concatenate([w_c[:, :MOBA_WIDTH], w_c[:, 2 * MOBA_WIDTH:]], axis=1).T
    qvt_c = _moba_qv_proj(x16, w_qv_t, cos128.T, sin128.T, seq)
    o_c = _moba_attention(qvt_c, k_c, avg, bsz, seq)

    kv_mem = _matmul(mem16, w_mem_kv.astype(BF16), mem_len, 2 * MEM_WIDTH, BF16, "mem_kv")
    o_m = _mem_attention(x16, w_mq, kv_mem, bsz, seq, mem_len)

    y = _gated_sum(x16, o_a, o_b, o_c, o_m, w_g, w_br_a.astype(BF16), w_br_b.astype(BF16),
                   w_br_c.astype(BF16), w_br_m.astype(BF16))
    x32, x16, x_packed = _proj_norm(y, w_o.astype(BF16), x32, row(ln1_g), row(ln1_b))

    w_r = _pad_cols(w_router.astype(F32), LANES)
    w_r_hi = w_r.astype(BF16)
    w_r_split = jnp.stack([w_r_hi, (w_r - w_r_hi.astype(F32)).astype(BF16)])
    e_idx, wts, rank, counts = _router(x32, w_r_split, _pad_cols(row(router_bias), LANES))
    bm = MOE_ROW_BLOCK
    n_blk = (t * TOP_K) // bm + N_EXPERTS
    counts = counts[0, :N_EXPERTS].astype(I32)
    padded = (counts + bm - 1) // bm * bm
    pad_ends = jnp.cumsum(padded)
    pad_starts = pad_ends - padded
    dest = _pad_cols((pad_starts[e_idx[:, :TOP_K]] + rank[:, :TOP_K]).astype(I32), SLOT_STRIDE).reshape(-1)
    blk_first = jnp.arange(n_blk, dtype=I32)[:, None] * bm
    blk_e = jnp.minimum(jnp.sum((pad_ends[None, :] <= blk_first).astype(I32), axis=1), N_EXPERTS - 1)
    n_used = (pad_ends[-1:] // bm).astype(I32)
    xs, shared = _dispatch(x_packed, x16, dest, pad_starts + counts, padded - counts, n_used, n_blk * bm,
                           w_sh_gate.astype(BF16), w_sh_up.astype(BF16), w_sh_down.astype(BF16))
    ys = _expert_ffn(xs, blk_e, n_used, *expert_weights, layer)
    return _combine(x32, shared, dest, wts, ys, row(ln2_g), row(ln2_b))


def kernel(x, mem, w_in, g_qa, w_q_up, g_kva, w_kv_up, sgu_ln_g, sgu_ln_b, w_spatial, b_spatial, w_mem_kv, w_br_a, w_br_b, w_br_c, w_br_m, w_o, ln1_g, ln1_b, w_router, router_bias, w_exp_gate, w_exp_up, w_exp_down, w_sh_gate, w_sh_up, w_sh_down, ln2_g, ln2_b):
    bsz, seq, d = x.shape
    mem_len = mem.shape[1]
    params = (w_in, g_qa, w_q_up, g_kva, w_kv_up, sgu_ln_g, sgu_ln_b, w_spatial, b_spatial, w_mem_kv,
              w_br_a, w_br_b, w_br_c, w_br_m, w_o, ln1_g, ln1_b, w_router, router_bias,
              w_sh_gate, w_sh_up, w_sh_down, ln2_g, ln2_b)
    expert_weights = (w_exp_gate, w_exp_up, w_exp_down)
    tables = _rope_tables(seq)
    n_blocks = seq // MOBA_BLOCK
    blk_of_pos = jnp.arange(seq, dtype=I32)[None, :] // MOBA_BLOCK
    avg = jnp.where(blk_of_pos == jnp.arange(LANES, dtype=I32)[:, None], 1.0 / MOBA_BLOCK, 0.0).astype(BF16)
    assert n_blocks <= LANES
    x32 = x.reshape(bsz * seq, d)
    x16 = x32.astype(BF16)
    mem16 = mem.reshape(bsz * mem_len, d).astype(BF16)
    for l in range(DEPTH):
        x32, x16 = _layer(x32, x16, mem16, tables, avg, bsz, seq, mem_len, tuple(w[l] for w in params),
                          expert_weights, l)
    return x32.reshape(bsz, seq, d)
```
